```python
import jax
import jax.numpy as jnp
from jax import lax
import numpy as np

D_MODEL = 2048
BATCH = 4
SEQ = 2048
DEPTH = 4

MEM_LEN = 256
GRID_W = 64
N_MIXERS = 4
GROUP_W = D_MODEL // N_MIXERS
ML_HEADS = 4
ML_DH = GROUP_W // ML_HEADS
ML_CHUNK = 64
ML_CONV = 3
HG_HEADS = 4
HG_DK = GROUP_W // HG_HEADS
HG_CHUNK = 16
FN_GROUPS = 4
FN_CH = GROUP_W // FN_GROUPS
NA_HEADS = 8
NA_DH = GROUP_W // NA_HEADS
NA_KH = 8
NA_KW = 16
XA_HEADS = 4
XA_DH = D_MODEL // XA_HEADS
MOE_GROUPS = 4
MOE_PER_GROUP = 8
MOE_EXPERTS = MOE_GROUPS * MOE_PER_GROUP
MOE_TOPK = 2
MOE_FF = D_MODEL // 4
MOE_BLOCK = 128
EPS = 1e-6
F32 = jnp.float32
IN_SIZES = (GROUP_W, GROUP_W, GROUP_W, GROUP_W, 4 * ML_HEADS,
            GROUP_W, GROUP_W, GROUP_W, GROUP_W, GROUP_W,
            GROUP_W,
            GROUP_W, GROUP_W, GROUP_W)
IN_W = 13 * GROUP_W + 4 * ML_HEADS

kernel_name = 'hybrid_parallel_heads_hmoe_encoder'


def rms_norm(x, g):
    xf = x.astype(F32)
    y = xf * lax.rsqrt(jnp.mean(xf * xf, axis=-1, keepdims=True) + EPS)
    return (y * g.astype(F32)).astype(x.dtype)


def to_heads(t, n_heads):
    b, s, _ = t.shape
    return t.reshape(b, s, n_heads, -1).transpose(0, 2, 1, 3).astype(F32)


def merge_heads(t):
    b, h, s, d = t.shape
    return t.transpose(0, 2, 1, 3).reshape(b, s, h * d)


def centred_dwconv(x, w):
    k, c = w.shape
    return lax.conv_general_dilated(x, w[:, None, :].astype(x.dtype), window_strides=(1,),
                                    padding=[(k // 2, k // 2)],
                                    dimension_numbers=('NWC', 'WIO', 'NWC'),
                                    feature_group_count=c)


def mlstm_chunkwise(q, k, v, ig, fg):
    b_, h_, s_, d_ = q.shape
    L = ML_CHUNK
    nc = s_ // L
    q = q.reshape(b_, h_, nc, L, d_)
    k = k.reshape(b_, h_, nc, L, d_)
    v = v.reshape(b_, h_, nc, L, d_)
    ig = ig.reshape(b_, h_, nc, L)
    bcum = jnp.cumsum(jax.nn.log_sigmoid(fg).reshape(b_, h_, nc, L), axis=-1)
    b_end = bcum[..., -1]
    causal = jnp.tril(jnp.ones((L, L), dtype=bool))
    dmat = jnp.where(causal, bcum[..., :, None] - bcum[..., None, :] + ig[..., None, :], -jnp.inf)
    a = b_end[..., None] - bcum + ig
    m_loc = jnp.max(a, axis=-1)
    wa = jnp.exp(a - m_loc[..., None])
    c_loc = jnp.einsum('bhcs,bhcsd,bhcse->bhcde', wa, k, v)
    n_loc = jnp.einsum('bhcs,bhcsd->bhcd', wa, k)

    def step(carry, xs):
        c, n, m = carry
        cl, nl, ml, bl = xs
        m_new = jnp.maximum(bl + m, ml)
        s_old = jnp.exp(bl + m - m_new)
        s_new = jnp.exp(ml - m_new)
        c_next = s_old[..., None, None] * c + s_new[..., None, None] * cl
        n_next = s_old[..., None] * n + s_new[..., None] * nl
        return (c_next, n_next, m_new), (c, n, m)

    init = (jnp.zeros((b_, h_, d_, d_), F32), jnp.zeros((b_, h_, d_), F32), jnp.zeros((b_, h_), F32))
    xs = (jnp.moveaxis(c_loc, 2, 0), jnp.moveaxis(n_loc, 2, 0), jnp.moveaxis(m_loc, 2, 0), jnp.moveaxis(b_end, 2, 0))
    _, (c_prev, n_prev, m_prev) = lax.scan(step, init, xs)
    c_prev = jnp.moveaxis(c_prev, 0, 2)
    n_prev = jnp.moveaxis(n_prev, 0, 2)
    m_prev = jnp.moveaxis(m_prev, 0, 2)
    inter = bcum + m_prev[..., None]
    m_t = jnp.maximum(inter, jnp.max(dmat, axis=-1))
    w = jnp.exp(dmat - m_t[..., None])
    s_inter = jnp.exp(inter - m_t)
    qk = jnp.einsum('bhctd,bhcsd->bhcts', q, k) * w
    num = jnp.einsum('bhcts,bhcse->bhcte', qk, v) + s_inter[..., None] * jnp.einsum('bhctd,bhcde->bhcte', q, c_prev)
    den = qk.sum(-1) + s_inter * jnp.einsum('bhctd,bhcd->bhct', q, n_prev)
    h = num / jnp.maximum(jnp.abs(den), jnp.exp(-m_t))[..., None]
    return h.reshape(b_, h_, s_, d_)


def hgrn2_chunkwise(q, k, v, logf):
    b_, h_, s_, dk = q.shape
    dv = v.shape[-1]
    L = HG_CHUNK
    nc = s_ // L
    q = q.reshape(b_, h_, nc, L, dk)
    k = k.reshape(b_, h_, nc, L, dk)
    v = v.reshape(b_, h_, nc, L, dv)
    bcum = jnp.cumsum(logf.reshape(b_, h_, nc, L, dk), axis=-2)
    b_end = bcum[..., -1, :]
    causal = jnp.tril(jnp.ones((L, L), dtype=bool))[..., None]
    decay = jnp.exp(jnp.where(causal, bcum[..., :, None, :] - bcum[..., None, :, :], -jnp.inf))
    attn = jnp.einsum('bhctd,bhctsd,bhcsd->bhcts', q, decay, k)
    o_intra = jnp.einsum('bhcts,bhcse->bhcte', attn, v)
    s_loc = jnp.einsum('bhcsd,bhcse->bhcde', k * jnp.exp(b_end[..., None, :] - bcum), v)

    def step(s, xs):
        sl, dl = xs
        return jnp.exp(dl)[..., None] * s + sl, s

    _, s_prev = lax.scan(step, jnp.zeros((b_, h_, dk, dv), F32),
                         (jnp.moveaxis(s_loc, 2, 0), jnp.moveaxis(b_end, 2, 0)))
    s_prev = jnp.moveaxis(s_prev, 0, 2)
    o_inter = jnp.einsum('bhctd,bhcde->bhcte', q * jnp.exp(bcum), s_prev)
    return (o_intra + o_inter).reshape(b_, h_, s_, dv)


def neighbourhood_attention(q, k, v, rpb):
    b_, s_, _ = q.shape
    rows = s_ // GRID_W
    kh = min(NA_KH, rows)
    grid = lambda t: t.reshape(b_, rows, GRID_W, NA_HEADS, NA_DH).astype(F32)
    qg = grid(q) * (NA_DH ** -0.5)
    kg = grid(k)
    vg = grid(v)
    r = jnp.arange(rows)
    row_idx = jnp.clip(r - kh // 2, 0, rows - kh)[:, None] + jnp.arange(kh)[None, :]
    c = jnp.arange(GRID_W)
    col_start = jnp.clip(c - NA_KW // 2, 0, GRID_W - NA_KW)
    col_ok = (c[None, :] >= col_start[:, None]) & (c[None, :] < col_start[:, None] + NA_KW)
    k_band = jnp.take(kg, row_idx, axis=1)
    v_band = jnp.take(vg, row_idx, axis=1)
    scores = jnp.einsum('brchd,brawhd->bhrcaw', qg, k_band)
    dr = row_idx - r[:, None] + NA_KH - 1
    dc = jnp.clip(c[None, :] - c[:, None] + NA_KW - 1, 0, 2 * NA_KW - 2)
    bias = rpb.astype(F32)[:, dr[:, None, :, None], dc[None, :, None, :]]
    scores = jnp.where(col_ok[:, None, :], scores + bias[None], -jnp.inf)
    shp = scores.shape
    p = jax.nn.softmax(scores.reshape(shp[:4] + (kh * GRID_W,)), axis=-1).reshape(shp)
    out = jnp.einsum('bhrcaw,brawhd->brchd', p, v_band)
    return out.reshape(b_, s_, GROUP_W)


def hybrid_mixer(hn, w_in, conv_w, gate_b, lb, rpb, group_gain, w_out):
    b_, s_, _ = hn.shape
    split_at = np.cumsum(IN_SIZES)[:-1].tolist()
    (mq, mk, mv, mo, mg, hq, hff, hfb, hi, hg, fu, nq, nk, nv) = jnp.split(hn @ w_in, split_at, axis=-1)
    flip = lambda t: jnp.flip(t, axis=2)

    qk = jax.nn.silu(centred_dwconv(jnp.concatenate([mq, mk], axis=-1), conv_w))
    q = to_heads(qk[..., :GROUP_W], ML_HEADS)
    k = to_heads(qk[..., GROUP_W:], ML_HEADS) * (ML_DH ** -0.5)
    v = to_heads(mv, ML_HEADS)
    gates = (mg.astype(F32) + gate_b.astype(F32)).reshape(b_, s_, 4, ML_HEADS).transpose(2, 0, 3, 1)
    h_fwd = mlstm_chunkwise(q, k, v, gates[0], gates[1])
    h_bwd = flip(mlstm_chunkwise(flip(q), flip(k), flip(v), flip(gates[2]), flip(gates[3])))
    y_ml = jax.nn.sigmoid(mo.astype(F32)) * merge_heads(h_fwd + h_bwd)

    lb = lb.astype(F32)
    hq_ = to_heads(jax.nn.silu(hq.astype(F32)), HG_HEADS)
    hi_ = to_heads(hi, HG_HEADS)

    def forget(fp):
        fp = fp.astype(F32)
        logf = jnp.logaddexp(jnp.log(lb), jnp.log1p(-lb) + jax.nn.log_sigmoid(fp))
        kk = (1.0 - lb) * jax.nn.sigmoid(-fp)
        return to_heads(logf, HG_HEADS), to_heads(kk, HG_HEADS)

    lf_f, k_f = forget(hff)
    lf_b, k_b = forget(hfb)
    o = hgrn2_chunkwise(hq_, k_f, hi_, lf_f) + flip(hgrn2_chunkwise(flip(hq_), flip(k_b), flip(hi_), flip(lf_b)))
    o = o * lax.rsqrt(jnp.mean(o * o, axis=-1, keepdims=True) + EPS)
    y_hg = merge_heads(o) * jax.nn.silu(hg.astype(F32))

    u = fu.astype(F32).reshape(b_, s_, FN_GROUPS, FN_CH)
    y_fn = jnp.fft.fft2(u, axes=(1, 3), norm='ortho').real.reshape(b_, s_, GROUP_W)

    y_na = neighbourhood_attention(nq, nk, nv, rpb)

    y = jnp.concatenate([y_ml, y_hg, y_fn, y_na], axis=-1).reshape(b_, s_, N_MIXERS, GROUP_W)
    y = rms_norm(y, group_gain.reshape(N_MIXERS, GROUP_W)).reshape(b_, s_, D_MODEL)
    return y.astype(hn.dtype) @ w_out


def memory_cross_attention(hn, mem_n, wq, wk, wv, wo):
    b_, s_, _ = hn.shape
    m_ = mem_n.shape[1]
    q = (hn @ wq).reshape(b_, s_, XA_HEADS, XA_DH)
    k = (mem_n @ wk).reshape(b_, m_, XA_HEADS, XA_DH)
    v = (mem_n @ wv).reshape(b_, m_, XA_HEADS, XA_DH)
    sc = jnp.einsum('bshd,bmhd->bhsm', q, k).astype(F32) * (XA_DH ** -0.5)
    p = jax.nn.softmax(sc, axis=-1).astype(v.dtype)
    o = jnp.einsum('bhsm,bmhd->bshd', p, v).reshape(b_, s_, D_MODEL)
    return o @ wo


def hierarchical_moe(hn, w_rg, w_re, w1, w3, w2):
    b_, s_, d_ = hn.shape
    n = b_ * s_
    xf = hn.reshape(n, d_)
    p_group = jax.nn.softmax((xf @ w_rg).astype(F32), axis=-1)
    g_idx = jnp.argmax(p_group, axis=-1)
    g_gate = jnp.take_along_axis(p_group, g_idx[:, None], axis=-1)
    fine = (xf @ w_re).astype(F32).reshape(n, MOE_GROUPS, MOE_PER_GROUP)
    fine = jnp.take_along_axis(fine, g_idx[:, None, None], axis=1)[:, 0]
    top_v, top_i = lax.top_k(fine, MOE_TOPK)
    gate = g_gate * jax.nn.softmax(top_v, axis=-1)
    expert = g_idx[:, None].astype(jnp.int32) * MOE_PER_GROUP + top_i.astype(jnp.int32)
    a_ = n * MOE_TOPK
    e_flat = expert.reshape(a_)
    tok = jnp.repeat(jnp.arange(n, dtype=jnp.int32), MOE_TOPK)
    w_flat = gate.reshape(a_)
    order = jnp.argsort(e_flat)
    e_s = e_flat[order]
    tok_s = tok[order]
    w_s = w_flat[order]
    counts = jnp.bincount(e_flat, length=MOE_EXPERTS)
    padded = (counts + MOE_BLOCK - 1) // MOE_BLOCK * MOE_BLOCK
    pad_end = jnp.cumsum(padded)
    pad_start = pad_end - padded
    start = jnp.cumsum(counts) - counts
    dest = pad_start[e_s] + jnp.arange(a_, dtype=jnp.int32) - start[e_s]
    nb = -(-a_ // MOE_BLOCK) + MOE_EXPERTS
    p_ = nb * MOE_BLOCK
    slot_tok = jnp.full((p_,), n, dtype=jnp.int32).at[dest].set(tok_s)
    slot_w = jnp.zeros((p_,), F32).at[dest].set(w_s)
    block_e = jnp.minimum(jnp.searchsorted(pad_end, jnp.arange(nb) * MOE_BLOCK, side='right'), MOE_EXPERTS - 1)
    xb = jnp.concatenate([xf, jnp.zeros((1, d_), xf.dtype)], axis=0)[slot_tok].reshape(nb, MOE_BLOCK, d_)

    def expert_block(args):
        xblk, e = args
        return (jax.nn.silu(xblk @ w1[e]) * (xblk @ w3[e])) @ w2[e]

    yb = lax.map(expert_block, (xb, block_e)).reshape(p_, d_)
    y = jnp.zeros((n + 1, d_), F32).at[slot_tok].add(yb.astype(F32) * slot_w[:, None])[:n]
    return y.astype(hn.dtype).reshape(b_, s_, d_)


def setup_inputs(seed: int = 0) -> dict:
    key = jax.random.key(seed)
    ks = jax.random.split(key, 24)
    nrm = lambda k, shape, scale: jax.random.normal(k, shape, F32) * scale
    gain = lambda k, shape: 1.0 + 0.02 * jax.random.normal(k, shape, F32)
    gate_base = jnp.tile(jnp.concatenate([jnp.zeros((ML_HEADS,), F32), jnp.linspace(3.0, 6.0, ML_HEADS, dtype=F32)]), 2)
    dsc = D_MODEL ** -0.5
    return {
        'x': nrm(ks[0], (BATCH, SEQ, D_MODEL), 1.0),
        'mem': nrm(ks[1], (BATCH, MEM_LEN, D_MODEL), 1.0),
        'norm_mix': gain(ks[2], (DEPTH, D_MODEL)),
        'norm_cross': gain(ks[3], (DEPTH, D_MODEL)),
        'norm_ffn': gain(ks[4], (DEPTH, D_MODEL)),
        'norm_final': gain(ks[5], (D_MODEL,)),
        'norm_mem': gain(ks[6], (D_MODEL,)),
        'w_in': nrm(ks[7], (DEPTH, D_MODEL, IN_W), dsc),
        'mlstm_conv': nrm(ks[8], (DEPTH, ML_CONV, 2 * GROUP_W), ML_CONV ** -0.5),
        'mlstm_gate_bias': gate_base[None, :] + nrm(ks[9], (DEPTH, 4 * ML_HEADS), 0.1),
        'hgrn_lower_bound': nrm(ks[10], (DEPTH, GROUP_W), 0.1),
        'na_rpb': nrm(ks[11], (DEPTH, NA_HEADS, 2 * NA_KH - 1, 2 * NA_KW - 1), 0.05),
        'group_gain': gain(ks[12], (DEPTH, D_MODEL)),
        'w_out': nrm(ks[13], (DEPTH, D_MODEL, D_MODEL), dsc),
        'xa_wq': nrm(ks[14], (DEPTH, D_MODEL, D_MODEL), dsc),
        'xa_wk': nrm(ks[15], (DEPTH, D_MODEL, D_MODEL), dsc),
        'xa_wv': nrm(ks[16], (DEPTH, D_MODEL, D_MODEL), dsc),
        'xa_wo': nrm(ks[17], (DEPTH, D_MODEL, D_MODEL), dsc),
        'moe_router_group': nrm(ks[18], (DEPTH, D_MODEL, MOE_GROUPS), dsc),
        'moe_router_expert': nrm(ks[19], (DEPTH, D_MODEL, MOE_EXPERTS), dsc),
        'moe_w1': nrm(ks[20], (DEPTH, MOE_EXPERTS, D_MODEL, MOE_FF), dsc),
        'moe_w3': nrm(ks[21], (DEPTH, MOE_EXPERTS, D_MODEL, MOE_FF), dsc),
        'moe_w2': nrm(ks[22], (DEPTH, MOE_EXPERTS, MOE_FF, D_MODEL), MOE_FF ** -0.5),
    }


def reference(x, mem, norm_mix, norm_cross, norm_ffn, norm_final, norm_mem, w_in, mlstm_conv,
              mlstm_gate_bias, hgrn_lower_bound, na_rpb, group_gain, w_out, xa_wq, xa_wk, xa_wv, xa_wo,
              moe_router_group, moe_router_expert, moe_w1, moe_w3, moe_w2):
    mem_n = rms_norm(mem, norm_mem)
    lbs = jnp.cumsum(jax.nn.softmax(hgrn_lower_bound.astype(F32), axis=0), axis=0)
    lbs = lbs - lbs[0]
    h = x
    for l in range(DEPTH):
        h = h + hybrid_mixer(rms_norm(h, norm_mix[l]), w_in[l], mlstm_conv[l], mlstm_gate_bias[l],
                             lbs[l], na_rpb[l], group_gain[l], w_out[l])
        h = h + memory_cross_attention(rms_norm(h, norm_cross[l]), mem_n, xa_wq[l], xa_wk[l], xa_wv[l], xa_wo[l])
        h = h + hierarchical_moe(rms_norm(h, norm_ffn[l]), moe_router_group[l], moe_router_expert[l],
                                 moe_w1[l], moe_w3[l], moe_w2[l])
    return rms_norm(h, norm_final)
```

```python
import functools

import jax
import jax.numpy as jnp
import numpy as np
from jax import lax
from jax.experimental import pallas as pl
from jax.experimental.pallas import tpu as pltpu

D_MODEL = 2048
DEPTH = 4
GRID_W = 64
N_MIXERS = 4
GROUP_W = D_MODEL // N_MIXERS
ML_HEADS = 4
ML_DH = GROUP_W // ML_HEADS
ML_CHUNK = 64
HG_HEADS = 4
HG_CHUNK = 16
FN_GROUPS = 4
FN_CH = GROUP_W // FN_GROUPS
NA_HEADS = 8
NA_DH = GROUP_W // NA_HEADS
NA_KH = 8
NA_KW = 16
XA_HEADS = 4
XA_DH = D_MODEL // XA_HEADS
MOE_GROUPS = 4
MOE_PER_GROUP = 8
MOE_EXPERTS = MOE_GROUPS * MOE_PER_GROUP
MOE_TOPK = 2
MOE_BLOCK = 128
EPS = 1e-6
F32 = jnp.float32
BF16 = jnp.bfloat16

VMEM_LIMIT_BYTES = 56 * 1024 * 1024


def _mm_kernel(*refs, norm, residual):
    refs = list(refs)
    x_ref = refs.pop(0)
    g_ref = refs.pop(0) if norm else None
    w_ref = refs.pop(0)
    r_ref = refs.pop(0) if residual else None
    o_ref = refs.pop(0)
    xn_ref = refs.pop(0)

    @pl.when(pl.program_id(1) == 0)
    def _():
        x = x_ref[...].astype(F32)
        if norm:
            ms = jnp.mean(x * x, axis=-1, keepdims=True)
            x = x * lax.rsqrt(ms + EPS) * g_ref[...]
        xn_ref[...] = x.astype(BF16)

    acc = jnp.dot(xn_ref[...], w_ref[...].astype(BF16), preferred_element_type=F32)
    if residual:
        acc = acc + r_ref[...]
    o_ref[...] = acc.astype(o_ref.dtype)


def _matmul(x, w, *, gain=None, residual=None, tm=1024, tn=512, out_dtype=F32):
    m, k = x.shape
    n = w.shape[1]
    tm = min(tm, m)
    tn = min(tn, n)
    assert m % tm == 0 and n % tn == 0, (m, n, tm, tn)
    norm = gain is not None
    has_res = residual is not None
    in_specs = [pl.BlockSpec((tm, k), lambda i, j: (i, 0))]
    args = [x]
    if norm:
        in_specs.append(pl.BlockSpec((1, k), lambda i, j: (0, 0)))
        args.append(gain.reshape(1, k).astype(F32))
    in_specs.append(pl.BlockSpec((k, tn), lambda i, j: (0, j)))
    args.append(w)
    if has_res:
        in_specs.append(pl.BlockSpec((tm, tn), lambda i, j: (i, j)))
        args.append(residual)
    return pl.pallas_call(
        functools.partial(_mm_kernel, norm=norm, residual=has_res),
        grid=(m // tm, n // tn),
        in_specs=in_specs,
        out_specs=pl.BlockSpec((tm, tn), lambda i, j: (i, j)),
        out_shape=jax.ShapeDtypeStruct((m, n), out_dtype),
        scratch_shapes=[pltpu.VMEM((tm, k), BF16)],
        compiler_params=pltpu.CompilerParams(
            dimension_semantics=("parallel", "arbitrary"),
            vmem_limit_bytes=VMEM_LIMIT_BYTES),
    )(*args)


def _rms_norm(x, g):
    xf = x.astype(F32)
    y = xf * lax.rsqrt(jnp.mean(xf * xf, axis=-1, keepdims=True) + EPS)
    return (y * g.astype(F32)).astype(x.dtype)


def _to_heads(t, n_heads):
    b, s, _ = t.shape
    return t.reshape(b, s, n_heads, -1).transpose(0, 2, 1, 3).astype(F32)


def _merge_heads(t):
    b, h, s, d = t.shape
    return t.transpose(0, 2, 1, 3).reshape(b, s, h * d)


def _centred_dwconv(x, w):
    k, c = w.shape
    return lax.conv_general_dilated(x, w[:, None, :].astype(x.dtype), window_strides=(1,),
                                    padding=[(k // 2, k // 2)],
                                    dimension_numbers=('NWC', 'WIO', 'NWC'),
                                    feature_group_count=c)


def _mlstm_chunkwise(q, k, v, ig, fg):
    b_, h_, s_, d_ = q.shape
    L = ML_CHUNK
    nc = s_ // L
    q = q.reshape(b_, h_, nc, L, d_)
    k = k.reshape(b_, h_, nc, L, d_)
    v = v.reshape(b_, h_, nc, L, d_)
    ig = ig.reshape(b_, h_, nc, L)
    bcum = jnp.cumsum(jax.nn.log_sigmoid(fg).reshape(b_, h_, nc, L), axis=-1)
    b_end = bcum[..., -1]
    causal = jnp.tril(jnp.ones((L, L), dtype=bool))
    dmat = jnp.where(causal, bcum[..., :, None] - bcum[..., None, :] + ig[..., None, :], -jnp.inf)
    a = b_end[..., None] - bcum + ig
    m_loc = jnp.max(a, axis=-1)
    wa = jnp.exp(a - m_loc[..., None])
    c_loc = jnp.einsum('bhcs,bhcsd,bhcse->bhcde', wa, k, v)
    n_loc = jnp.einsum('bhcs,bhcsd->bhcd', wa, k)

    def step(carry, xs):
        c, n, m = carry
        cl, nl, ml, bl = xs
        m_new = jnp.maximum(bl + m, ml)
        s_old = jnp.exp(bl + m - m_new)
        s_new = jnp.exp(ml - m_new)
        c_next = s_old[..., None, None] * c + s_new[..., None, None] * cl
        n_next = s_old[..., None] * n + s_new[..., None] * nl
        return (c_next, n_next, m_new), (c, n, m)

    init = (jnp.zeros((b_, h_, d_, d_), F32), jnp.zeros((b_, h_, d_), F32), jnp.zeros((b_, h_), F32))
    xs = (jnp.moveaxis(c_loc, 2, 0), jnp.moveaxis(n_loc, 2, 0), jnp.moveaxis(m_loc, 2, 0), jnp.moveaxis(b_end, 2, 0))
    _, (c_prev, n_prev, m_prev) = lax.scan(step, init, xs)
    c_prev = jnp.moveaxis(c_prev, 0, 2)
    n_prev = jnp.moveaxis(n_prev, 0, 2)
    m_prev = jnp.moveaxis(m_prev, 0, 2)
    inter = bcum + m_prev[..., None]
    m_t = jnp.maximum(inter, jnp.max(dmat, axis=-1))
    w = jnp.exp(dmat - m_t[..., None])
    s_inter = jnp.exp(inter - m_t)
    qk = jnp.einsum('bhctd,bhcsd->bhcts', q, k) * w
    num = jnp.einsum('bhcts,bhcse->bhcte', qk, v) + s_inter[..., None] * jnp.einsum('bhctd,bhcde->bhcte', q, c_prev)
    den = qk.sum(-1) + s_inter * jnp.einsum('bhctd,bhcd->bhct', q, n_prev)
    h = num / jnp.maximum(jnp.abs(den), jnp.exp(-m_t))[..., None]
    return h.reshape(b_, h_, s_, d_)


def _hgrn2_chunkwise(q, k, v, logf):
    b_, h_, s_, dk = q.shape
    dv = v.shape[-1]
    L = HG_CHUNK
    nc = s_ // L
    q = q.reshape(b_, h_, nc, L, dk)
    k = k.reshape(b_, h_, nc, L, dk)
    v = v.reshape(b_, h_, nc, L, dv)
    bcum = jnp.cumsum(logf.reshape(b_, h_, nc, L, dk), axis=-2)
    b_end = bcum[..., -1, :]
    causal = jnp.tril(jnp.ones((L, L), dtype=bool))[..., None]
    decay = jnp.exp(jnp.where(causal, bcum[..., :, None, :] - bcum[..., None, :, :], -jnp.inf))
    attn = jnp.einsum('bhctd,bhctsd,bhcsd->bhcts', q, decay, k)
    o_intra = jnp.einsum('bhcts,bhcse->bhcte', attn, v)
    s_loc = jnp.einsum('bhcsd,bhcse->bhcde', k * jnp.exp(b_end[..., None, :] - bcum), v)

    def step(s, xs):
        sl, dl = xs
        return jnp.exp(dl)[..., None] * s + sl, s

    _, s_prev = lax.scan(step, jnp.zeros((b_, h_, dk, dv), F32),
                         (jnp.moveaxis(s_loc, 2, 0), jnp.moveaxis(b_end, 2, 0)))
    s_prev = jnp.moveaxis(s_prev, 0, 2)
    o_inter = jnp.einsum('bhctd,bhcde->bhcte', q * jnp.exp(bcum), s_prev)
    return (o_intra + o_inter).reshape(b_, h_, s_, dv)


def _neighbourhood_attention(q, k, v, rpb):
    b_, s_, _ = q.shape
    rows = s_ // GRID_W
    kh = min(NA_KH, rows)
    grid = lambda t: t.reshape(b_, rows, GRID_W, NA_HEADS, NA_DH).astype(F32)
    qg = grid(q) * (NA_DH ** -0.5)
    kg = grid(k)
    vg = grid(v)
    r = jnp.arange(rows)
    row_idx = jnp.clip(r - kh // 2, 0, rows - kh)[:, None] + jnp.arange(kh)[None, :]
    c = jnp.arange(GRID_W)
    col_start = jnp.clip(c - NA_KW // 2, 0, GRID_W - NA_KW)
    col_ok = (c[None, :] >= col_start[:, None]) & (c[None, :] < col_start[:, None] + NA_KW)
    k_band = jnp.take(kg, row_idx, axis=1)
    v_band = jnp.take(vg, row_idx, axis=1)
    scores = jnp.einsum('brchd,brawhd->bhrcaw', qg, k_band)
    dr = row_idx - r[:, None] + NA_KH - 1
    dc = jnp.clip(c[None, :] - c[:, None] + NA_KW - 1, 0, 2 * NA_KW - 2)
    bias = rpb.astype(F32)[:, dr[:, None, :, None], dc[None, :, None, :]]
    scores = jnp.where(col_ok[:, None, :], scores + bias[None], -jnp.inf)
    shp = scores.shape
    p = jax.nn.softmax(scores.reshape(shp[:4] + (kh * GRID_W,)), axis=-1).reshape(shp)
    out = jnp.einsum('bhrcaw,brawhd->brchd', p, v_band)
    return out.reshape(b_, s_, GROUP_W)


def _hybrid_mixer(h, norm_g, w_in, conv_w, gate_b, lb, rpb, group_gain, w_out):
    b_, s_, d_ = h.shape
    n = b_ * s_
    w_main = jnp.concatenate([w_in[:, :4 * GROUP_W], w_in[:, 4 * GROUP_W + 4 * ML_HEADS:]], axis=1)
    w_gate = jnp.pad(w_in[:, 4 * GROUP_W:4 * GROUP_W + 4 * ML_HEADS], ((0, 0), (0, 128 - 4 * ML_HEADS)))
    hf = h.reshape(n, d_)
    proj = _matmul(hf, w_main, gain=norm_g).reshape(b_, s_, 13 * GROUP_W)
    mg = _matmul(hf, w_gate, gain=norm_g, tn=128)[:, :4 * ML_HEADS].reshape(b_, s_, 4 * ML_HEADS)
    (mq, mk, mv, mo, hq, hff, hfb, hi, hg, fu, nq, nk, nv) = jnp.split(proj, 13, axis=-1)
    flip = lambda t: jnp.flip(t, axis=2)

    qk = jax.nn.silu(_centred_dwconv(jnp.concatenate([mq, mk], axis=-1), conv_w))
    q = _to_heads(qk[..., :GROUP_W], ML_HEADS)
    k = _to_heads(qk[..., GROUP_W:], ML_HEADS) * (ML_DH ** -0.5)
    v = _to_heads(mv, ML_HEADS)
    gates = (mg.astype(F32) + gate_b.astype(F32)).reshape(b_, s_, 4, ML_HEADS).transpose(2, 0, 3, 1)
    h_fwd = _mlstm_chunkwise(q, k, v, gates[0], gates[1])
    h_bwd = flip(_mlstm_chunkwise(flip(q), flip(k), flip(v), flip(gates[2]), flip(gates[3])))
    y_ml = jax.nn.sigmoid(mo.astype(F32)) * _merge_heads(h_fwd + h_bwd)

    lb = lb.astype(F32)
    hq_ = _to_heads(jax.nn.silu(hq.astype(F32)), HG_HEADS)
    hi_ = _to_heads(hi, HG_HEADS)

    def forget(fp):
        fp = fp.astype(F32)
        logf = jnp.logaddexp(jnp.log(lb), jnp.log1p(-lb) + jax.nn.log_sigmoid(fp))
        kk = (1.0 - lb) * jax.nn.sigmoid(-fp)
        return _to_heads(logf, HG_HEADS), _to_heads(kk, HG_HEADS)

    lf_f, k_f = forget(hff)
    lf_b, k_b = forget(hfb)
    o = _hgrn2_chunkwise(hq_, k_f, hi_, lf_f) + flip(_hgrn2_chunkwise(flip(hq_), flip(k_b), flip(hi_), flip(lf_b)))
    o = o * lax.rsqrt(jnp.mean(o * o, axis=-1, keepdims=True) + EPS)
    y_hg = _merge_heads(o) * jax.nn.silu(hg.astype(F32))

    u = fu.astype(F32).reshape(b_, s_, FN_GROUPS, FN_CH)
    y_fn = jnp.fft.fft2(u, axes=(1, 3), norm='ortho').real.reshape(b_, s_, GROUP_W)

    y_na = _neighbourhood_attention(nq, nk, nv, rpb)

    y = jnp.concatenate([y_ml, y_hg, y_fn, y_na], axis=-1).reshape(b_, s_, N_MIXERS, GROUP_W)
    y = _rms_norm(y, group_gain.reshape(N_MIXERS, GROUP_W)).reshape(n, d_)
    return _matmul(y, w_out, residual=hf).reshape(b_, s_, d_)


def _memory_cross_attention(h, norm_g, mem_n, wq, wk, wv, wo):
    b_, s_, d_ = h.shape
    m_ = mem_n.shape[1]
    hf = h.reshape(b_ * s_, d_)
    mf = mem_n.reshape(b_ * m_, d_)
    q = _matmul(hf, wq, gain=norm_g).reshape(b_, s_, XA_HEADS, XA_DH)
    k = _matmul(mf, wk).reshape(b_, m_, XA_HEADS, XA_DH)
    v = _matmul(mf, wv).reshape(b_, m_, XA_HEADS, XA_DH)
    sc = jnp.einsum('bshd,bmhd->bhsm', q, k).astype(F32) * (XA_DH ** -0.5)
    p = jax.nn.softmax(sc, axis=-1).astype(v.dtype)
    o = jnp.einsum('bhsm,bmhd->bshd', p, v).reshape(b_ * s_, d_)
    return _matmul(o, wo, residual=hf).reshape(b_, s_, d_)


def _hierarchical_moe(hn, w_rg, w_re, w1, w3, w2):
    b_, s_, d_ = hn.shape
    n = b_ * s_
    xf = hn.reshape(n, d_)
    p_group = jax.nn.softmax((xf @ w_rg).astype(F32), axis=-1)
    g_idx = jnp.argmax(p_group, axis=-1)
    g_gate = jnp.take_along_axis(p_group, g_idx[:, None], axis=-1)
    fine = (xf @ w_re).astype(F32).reshape(n, MOE_GROUPS, MOE_PER_GROUP)
    fine = jnp.take_along_axis(fine, g_idx[:, None, None], axis=1)[:, 0]
    top_v, top_i = lax.top_k(fine, MOE_TOPK)
    gate = g_gate * jax.nn.softmax(top_v, axis=-1)
    expert = g_idx[:, None].astype(jnp.int32) * MOE_PER_GROUP + top_i.astype(jnp.int32)
    a_ = n * MOE_TOPK
    e_flat = expert.reshape(a_)
    tok = jnp.repeat(jnp.arange(n, dtype=jnp.int32), MOE_TOPK)
    w_flat = gate.reshape(a_)
    order = jnp.argsort(e_flat)
    e_s = e_flat[order]
    tok_s = tok[order]
    w_s = w_flat[order]
    counts = jnp.bincount(e_flat, length=MOE_EXPERTS)
    padded = (counts + MOE_BLOCK - 1) // MOE_BLOCK * MOE_BLOCK
    pad_end = jnp.cumsum(padded)
    pad_start = pad_end - padded
    start = jnp.cumsum(counts) - counts
    dest = pad_start[e_s] + jnp.arange(a_, dtype=jnp.int32) - start[e_s]
    nb = -(-a_ // MOE_BLOCK) + MOE_EXPERTS
    p_ = nb * MOE_BLOCK
    slot_tok = jnp.full((p_,), n, dtype=jnp.int32).at[dest].set(tok_s)
    slot_w = jnp.zeros((p_,), F32).at[dest].set(w_s)
    block_e = jnp.minimum(jnp.searchsorted(pad_end, jnp.arange(nb) * MOE_BLOCK, side='right'), MOE_EXPERTS - 1)
    xb = jnp.concatenate([xf, jnp.zeros((1, d_), xf.dtype)], axis=0)[slot_tok].reshape(nb, MOE_BLOCK, d_)

    def expert_block(args):
        xblk, e = args
        return (jax.nn.silu(xblk @ w1[e]) * (xblk @ w3[e])) @ w2[e]

    yb = lax.map(expert_block, (xb, block_e)).reshape(p_, d_)
    y = jnp.zeros((n + 1, d_), F32).at[slot_tok].add(yb.astype(F32) * slot_w[:, None])[:n]
    return y.astype(hn.dtype).reshape(b_, s_, d_)


def kernel(x, mem, norm_mix, norm_cross, norm_ffn, norm_final, norm_mem, w_in, mlstm_conv,
           mlstm_gate_bias, hgrn_lower_bound, na_rpb, group_gain, w_out, xa_wq, xa_wk, xa_wv, xa_wo,
           moe_router_group, moe_router_expert, moe_w1, moe_w3, moe_w2):
    mem_n = _rms_norm(mem, norm_mem)
    lbs = jnp.cumsum(jax.nn.softmax(hgrn_lower_bound.astype(F32), axis=0), axis=0)
    lbs = lbs - lbs[0]
    h = x
    for l in range(DEPTH):
        h = _hybrid_mixer(h, norm_mix[l], w_in[l], mlstm_conv[l], mlstm_gate_bias[l],
                          lbs[l], na_rpb[l], group_gain[l], w_out[l])
        h = _memory_cross_attention(h, norm_cross[l], mem_n, xa_wq[l], xa_wk[l], xa_wv[l], xa_wo[l])
        h = h + _hierarchical_moe(_rms_norm(h, norm_ffn[l]), moe_router_group[l], moe_router_expert[l],
                                  moe_w1[l], moe_w3[l], moe_w2[l])
    return _rms_norm(h, norm_final)
```

```python
import functools

import jax
import jax.numpy as jnp
import numpy as np
from jax import lax
from jax.experimental import pallas as pl
from jax.experimental.pallas import tpu as pltpu

D_MODEL = 2048
DEPTH = 4
GRID_W = 64
N_MIXERS = 4
GROUP_W = D_MODEL // N_MIXERS
ML_HEADS = 4
ML_DH = GROUP_W // ML_HEADS
ML_CHUNK = 64
HG_HEADS = 4
HG_CHUNK = 16
FN_GROUPS = 4
FN_CH = GROUP_W // FN_GROUPS
NA_HEADS = 8
NA_DH = GROUP_W // NA_HEADS
NA_KH = 8
NA_KW = 16
XA_HEADS = 4
XA_DH = D_MODEL // XA_HEADS
MOE_GROUPS = 4
MOE_PER_GROUP = 8
MOE_EXPERTS = MOE_GROUPS * MOE_PER_GROUP
MOE_TOPK = 2
MOE_BLOCK = 128
EPS = 1e-6
F32 = jnp.float32
BF16 = jnp.bfloat16

VMEM_LIMIT_BYTES = 56 * 1024 * 1024


def _mm_kernel(*refs, n_x, norm, residual, side):
    refs = list(refs)
    x_refs = [refs.pop(0) for _ in range(n_x)]
    g_ref = refs.pop(0) if norm else None
    w_ref = refs.pop(0)
    ws_ref = refs.pop(0) if side else None
    r_ref = refs.pop(0) if residual else None
    o_ref = refs.pop(0)
    os_ref = refs.pop(0) if side else None
    xn_ref = refs.pop(0)

    @pl.when(pl.program_id(1) == 0)
    def _():
        col = 0
        for x_ref in x_refs:
            x = x_ref[...].astype(F32)
            kx = x.shape[1]
            if norm:
                ms = jnp.mean(x * x, axis=-1, keepdims=True)
                x = x * lax.rsqrt(ms + EPS) * g_ref[:, col:col + kx]
            xn_ref[:, col:col + kx] = x.astype(BF16)
            col += kx
        if side:
            os_ref[...] = jnp.dot(xn_ref[...], ws_ref[...].astype(BF16),
                                  preferred_element_type=F32).astype(os_ref.dtype)

    acc = jnp.dot(xn_ref[...], w_ref[...].astype(BF16), preferred_element_type=F32)
    if residual:
        acc = acc + r_ref[...]
    o_ref[...] = acc.astype(o_ref.dtype)


def _matmul(xs, w, *, gain=None, residual=None, w_side=None, tm=1024, tn=512, out_dtype=F32):
    if not isinstance(xs, (list, tuple)):
        xs = [xs]
    m = xs[0].shape[0]
    k = sum(x.shape[1] for x in xs)
    n = w.shape[1]
    tm = min(tm, m)
    tn = min(tn, n)
    assert m % tm == 0 and n % tn == 0 and w.shape[0] == k, (m, n, k, tm, tn)
    norm = gain is not None
    has_res = residual is not None
    side = w_side is not None
    in_specs = [pl.BlockSpec((tm, x.shape[1]), lambda i, j: (i, 0)) for x in xs]
    args = list(xs)
    if norm:
        in_specs.append(pl.BlockSpec((1, k), lambda i, j: (0, 0)))
        args.append(gain.reshape(1, k).astype(F32))
    in_specs.append(pl.BlockSpec((k, tn), lambda i, j: (0, j)))
    args.append(w)
    if side:
        in_specs.append(pl.BlockSpec((k, w_side.shape[1]), lambda i, j: (0, 0)))
        args.append(w_side)
    if has_res:
        in_specs.append(pl.BlockSpec((tm, tn), lambda i, j: (i, j)))
        args.append(residual)
    out_specs = pl.BlockSpec((tm, tn), lambda i, j: (i, j))
    out_shape = jax.ShapeDtypeStruct((m, n), out_dtype)
    if side:
        out_specs = [out_specs, pl.BlockSpec((tm, w_side.shape[1]), lambda i, j: (i, 0))]
        out_shape = [out_shape, jax.ShapeDtypeStruct((m, w_side.shape[1]), F32)]
    return pl.pallas_call(
        functools.partial(_mm_kernel, n_x=len(xs), norm=norm, residual=has_res, side=side),
        grid=(m // tm, n // tn),
        in_specs=in_specs,
        out_specs=out_specs,
        out_shape=out_shape,
        scratch_shapes=[pltpu.VMEM((tm, k), BF16)],
        compiler_params=pltpu.CompilerParams(
            dimension_semantics=("parallel", "arbitrary"),
            vmem_limit_bytes=VMEM_LIMIT_BYTES),
    )(*args)


NA_BAND = NA_KH * GRID_W


def _na_bias_tables(rpb):
    c = np.arange(GRID_W)
    dc = np.clip(c[None, :] - c[:, None] + NA_KW - 1, 0, 2 * NA_KW - 2)
    onehot = (dc[None] == np.arange(2 * NA_KW - 1)[:, None, None]).astype(np.float32)
    col_start = np.clip(c - NA_KW // 2, 0, GRID_W - NA_KW)
    col_ok = (c[None, :] >= col_start[:, None]) & (c[None, :] < col_start[:, None] + NA_KW)
    t = jnp.einsum('hrd,dqk->hrqk', rpb.astype(F32), onehot, precision=lax.Precision.HIGHEST)
    t = jnp.where(col_ok[None, None], t, -jnp.inf)
    win = jnp.stack([t[:, d0:d0 + NA_KH] for d0 in range(NA_KH)], axis=1)
    return win.transpose(0, 1, 3, 2, 4).reshape(NA_HEADS, NA_KH, GRID_W, NA_BAND)


def _na_kernel(q_ref, k_ref, v_ref, bias_ref, o_ref, *, rows):
    def row(r, carry):
        rs = jnp.clip(r - NA_KH // 2, 0, rows - NA_KH)
        d0 = rs - r + NA_KH - 1
        q0 = pl.multiple_of(r * GRID_W, GRID_W)
        k0 = pl.multiple_of(rs * GRID_W, GRID_W)
        qrow = q_ref[pl.ds(q0, GRID_W), :].astype(F32) * (NA_DH ** -0.5)
        kb = k_ref[pl.ds(k0, NA_BAND), :]
        vb = v_ref[pl.ds(k0, NA_BAND), :]
        outs = []
        for hh in range(2):
            sl = slice(hh * NA_DH, (hh + 1) * NA_DH)
            s = lax.dot_general(qrow[:, sl].astype(BF16), kb[:, sl].astype(BF16),
                                (((1,), (1,)), ((), ())), preferred_element_type=F32)
            s = s + bias_ref[hh, d0]
            m = jnp.max(s, axis=-1, keepdims=True)
            p = jnp.exp(s - m)
            l = jnp.sum(p, axis=-1, keepdims=True)
            o = jnp.dot(p.astype(BF16), vb[:, sl].astype(BF16), preferred_element_type=F32)
            outs.append(o / l)
        o_ref[pl.ds(q0, GRID_W), :] = jnp.concatenate(outs, axis=-1).astype(o_ref.dtype)
        return carry

    lax.fori_loop(0, rows, row, 0)


def _na_pallas(proj, rpb, *, batch, seq, col0, out_dtype=F32):
    rows = seq // GRID_W
    pair_w = 2 * NA_DH
    cb = col0 // pair_w
    gb = GROUP_W // pair_w
    bias = _na_bias_tables(rpb)
    spec = lambda g: pl.BlockSpec((seq, pair_w), lambda b, p, g=g: (b, cb + g * gb + p))
    return pl.pallas_call(
        functools.partial(_na_kernel, rows=rows),
        grid=(batch, NA_HEADS // 2),
        in_specs=[spec(0), spec(1), spec(2),
                  pl.BlockSpec((2, NA_KH, GRID_W, NA_BAND), lambda b, p: (p, 0, 0, 0))],
        out_specs=pl.BlockSpec((seq, pair_w), lambda b, p: (b, p)),
        out_shape=jax.ShapeDtypeStruct((batch * seq, GROUP_W), out_dtype),
        compiler_params=pltpu.CompilerParams(
            dimension_semantics=("parallel", "parallel"),
            vmem_limit_bytes=VMEM_LIMIT_BYTES),
    )(proj, proj, proj, bias)


ML_L = 128
ML_GATES = 8
ML_GATE_LANES = 128


def _log_sigmoid(x):
    return jnp.minimum(x, 0.0) - jnp.log1p(jnp.exp(-jnp.abs(x)))


def _silu(x):
    return x * jax.nn.sigmoid(x)


def _mlstm_direction(q, k, v, b_col, ig_col, b_row, ig_row, mask, c_ref, n_ref, m_ref, end_row):
    b_end = b_col[end_row:end_row + 1, :]
    m_prev = m_ref[...]
    dmat = jnp.where(mask, b_col - b_row + ig_row, -jnp.inf)
    a_col = b_end - b_col + ig_col
    m_loc = jnp.max(a_col, axis=0, keepdims=True)
    inter = b_col + m_prev
    m_t = jnp.maximum(inter, jnp.max(dmat, axis=-1, keepdims=True))
    w = jnp.exp(dmat - m_t)
    s_inter = jnp.exp(inter - m_t)
    qb = q.astype(BF16)
    vb = v.astype(BF16)
    qk = lax.dot_general(qb, k.astype(BF16), (((1,), (1,)), ((), ())), preferred_element_type=F32) * w
    num = (jnp.dot(qk.astype(BF16), vb, preferred_element_type=F32)
           + s_inter * jnp.dot(qb, c_ref[...].astype(BF16), preferred_element_type=F32))
    den = (jnp.sum(qk, axis=-1, keepdims=True)
           + s_inter * jnp.sum(q * n_ref[...], axis=-1, keepdims=True))
    h = num / jnp.maximum(jnp.abs(den), jnp.exp(-m_t))
    m_new = jnp.maximum(b_end + m_prev, m_loc)
    s_old = jnp.exp(b_end + m_prev - m_new)
    s_new = jnp.exp(m_loc - m_new)
    kw = k * jnp.exp(a_col - m_loc)
    c_loc = lax.dot_general(kw.astype(BF16), vb, (((0,), (0,)), ((), ())), preferred_element_type=F32)
    c_ref[...] = s_old * c_ref[...] + s_new * c_loc
    n_ref[...] = s_old * n_ref[...] + s_new * jnp.sum(kw, axis=0, keepdims=True)
    m_ref[...] = m_new
    return h


def _mlstm_kernel(q_ref, k_ref, v_ref, o_ref, cwq_ref, cwk_ref, gcol_ref, grow_ref, bcol_ref, brow_ref,
                  y_ref, qs_ref, ks_ref, hf_ref, hb_ref, cf_ref, nf_ref, mf_ref, cb_ref, nb_ref, mb_ref, *, seq):
    L = ML_L
    nc = seq // L
    t_idx = lax.broadcasted_iota(jnp.int32, (seq, 1), 0)

    def conv_silu(x, w):
        prev = jnp.where(t_idx == 0, 0.0, pltpu.roll(x, 1, axis=0))
        nxt = jnp.where(t_idx == seq - 1, 0.0, pltpu.roll(x, seq - 1, axis=0))
        return _silu(prev * w[0:1, :] + x * w[1:2, :] + nxt * w[2:3, :])

    qs_ref[...] = conv_silu(q_ref[...].astype(F32), cwq_ref[...])
    ks_ref[...] = conv_silu(k_ref[...].astype(F32), cwk_ref[...]) * (ML_DH ** -0.5)

    for r in (cf_ref, nf_ref, mf_ref, cb_ref, nb_ref, mb_ref):
        r[...] = jnp.zeros_like(r)

    ti = lax.broadcasted_iota(jnp.int32, (L, L), 0)
    si = lax.broadcasted_iota(jnp.int32, (L, L), 1)
    lower = si <= ti
    upper = si >= ti
    tril = lower.astype(F32)
    triu = upper.astype(F32)

    def step(i, carry):
        for direction in (0, 1):
            c = i if direction == 0 else nc - 1 - i
            t0 = pl.multiple_of(c * L, L)
            gcol = gcol_ref[pl.ds(t0, L), :] + bcol_ref[...]
            grow = grow_ref[:, pl.ds(t0, L)] + brow_ref[...]
            tri = tril if direction == 0 else triu
            cum_col = jnp.dot(tri, _log_sigmoid(gcol), precision=lax.Precision.HIGHEST,
                              preferred_element_type=F32)
            cum_row = lax.dot_general(_log_sigmoid(grow), tri, (((1,), (1,)), ((), ())),
                                      precision=lax.Precision.HIGHEST, preferred_element_type=F32)
            gi = 2 * direction
            ig_col = gcol[:, gi:gi + 1]
            b_col = cum_col[:, gi + 1:gi + 2]
            ig_row = grow[gi:gi + 1, :]
            b_row = cum_row[gi + 1:gi + 2, :]
            q = qs_ref[pl.ds(t0, L), :]
            k = ks_ref[pl.ds(t0, L), :]
            v = v_ref[pl.ds(t0, L), :].astype(F32)
            if direction == 0:
                h = _mlstm_direction(q, k, v, b_col, ig_col, b_row, ig_row, lower,
                                     cf_ref, nf_ref, mf_ref, L - 1)
                hf_ref[pl.ds(t0, L), :] = h
            else:
                h = _mlstm_direction(q, k, v, b_col, ig_col, b_row, ig_row, upper,
                                     cb_ref, nb_ref, mb_ref, 0)
                hb_ref[pl.ds(t0, L), :] = h
        return carry

    lax.fori_loop(0, nc, step, 0)
    y_ref[...] = (jax.nn.sigmoid(o_ref[...].astype(F32)) * (hf_ref[...] + hb_ref[...])).astype(y_ref.dtype)


def _mlstm_pallas(proj, gates, conv_w, gate_b, *, batch, seq, out_dtype=F32):
    d = ML_DH
    hb = GROUP_W // d
    g4 = gates.reshape(batch, seq, 4, ML_HEADS).transpose(0, 3, 1, 2).astype(F32)
    g = jnp.pad(g4, ((0, 0), (0, 0), (0, 0), (0, ML_GATE_LANES - 4)))
    g_row = jnp.pad(g4, ((0, 0), (0, 0), (0, 0), (0, ML_GATES - 4))).transpose(0, 1, 3, 2)
    gb4 = gate_b.astype(F32).reshape(4, ML_HEADS).T
    gb_col = jnp.pad(gb4, ((0, 0), (0, ML_GATE_LANES - 4))).reshape(ML_HEADS, 1, ML_GATE_LANES)
    gb_row = jnp.pad(gb4, ((0, 0), (0, ML_GATES - 4))).reshape(ML_HEADS, ML_GATES, 1)
    spec = lambda grp: pl.BlockSpec((seq, d), lambda b, h, grp=grp: (b, grp * hb + h))
    f32 = lambda *shape: pltpu.VMEM(shape, F32)
    return pl.pallas_call(
        functools.partial(_mlstm_kernel, seq=seq),
        grid=(batch, ML_HEADS),
        in_specs=[spec(0), spec(1), spec(2), spec(3),
                  pl.BlockSpec((3, d), lambda b, h: (0, h)),
                  pl.BlockSpec((3, d), lambda b, h: (0, hb + h)),
                  pl.BlockSpec((None, None, seq, ML_GATE_LANES), lambda b, h: (b, h, 0, 0)),
                  pl.BlockSpec((None, None, ML_GATES, seq), lambda b, h: (b, h, 0, 0)),
                  pl.BlockSpec((None, 1, ML_GATE_LANES), lambda b, h: (h, 0, 0)),
                  pl.BlockSpec((None, ML_GATES, 1), lambda b, h: (h, 0, 0))],
        out_specs=pl.BlockSpec((seq, d), lambda b, h: (b, h)),
        out_shape=jax.ShapeDtypeStruct((batch * seq, GROUP_W), out_dtype),
        scratch_shapes=[f32(seq, d), f32(seq, d), f32(seq, d), f32(seq, d),
                        f32(d, d), f32(1, d), f32(1, 1), f32(d, d), f32(1, d), f32(1, 1)],
        compiler_params=pltpu.CompilerParams(
            dimension_semantics=("parallel", "parallel"),
            vmem_limit_bytes=VMEM_LIMIT_BYTES),
    )(proj, proj, proj, proj, conv_w.astype(F32), conv_w.astype(F32), g, g_row, gb_col, gb_row)


HG_L = 128
HG_LEAF = 8


def _hgrn_level_ref(p, block, row):
    nb = p.shape[0] // block
    pb = p.reshape(nb, block, p.shape[1])
    return jnp.broadcast_to(pb[:, row:row + 1, :], pb.shape).reshape(p.shape)


def _hgrn_chunk(q, k, v, logf, tri, masks, st_ref, backward):
    L = HG_L
    p = jnp.dot(tri, logf, precision=lax.Precision.HIGHEST, preferred_element_type=F32)
    nt = (((1,), (1,)), ((), ()))
    attn = jnp.zeros((L, L), F32)
    m = L // 2
    li = 0
    while m >= HG_LEAF:
        ref = _hgrn_level_ref(p, 2 * m, m if backward else m - 1)
        qh = q * jnp.exp(jnp.minimum(p - ref, 0.0))
        kh = k * jnp.exp(jnp.minimum(ref - p, 0.0))
        s = lax.dot_general(qh.astype(BF16), kh.astype(BF16), nt, preferred_element_type=F32)
        attn = attn + jnp.where(masks[li] != 0.0, s, 0.0)
        m //= 2
        li += 1
    ref = _hgrn_level_ref(p, HG_LEAF, HG_LEAF // 2 if backward else HG_LEAF // 2 - 1)
    qh = q * jnp.exp(p - ref)
    kh = k * jnp.exp(ref - p)
    s = lax.dot_general(qh.astype(BF16), kh.astype(BF16), nt, preferred_element_type=F32)
    attn = attn + jnp.where(masks[li] != 0.0, s, 0.0)
    vb = v.astype(BF16)
    o = jnp.dot(attn.astype(BF16), vb, preferred_element_type=F32)
    o = o + lax.dot_general((q * jnp.exp(p)).astype(BF16), st_ref[...].astype(BF16), nt,
                            preferred_element_type=F32)
    p_end = p[0:1, :] if backward else p[L - 1:L, :]
    kd = k * jnp.exp(p_end - p)
    st_ref[...] = (st_ref[...] * jnp.exp(p_end)
                   + lax.dot_general(vb, kd.astype(BF16), (((0,), (0,)), ((), ())), preferred_element_type=F32))
    return o


def _hgrn_masks(backward):
    L = HG_L
    ti = lax.broadcasted_iota(jnp.int32, (L, L), 0)
    si = lax.broadcasted_iota(jnp.int32, (L, L), 1)
    if backward:
        ti, si = si, ti
    one = lambda cond: jnp.where(cond, 1.0, 0.0)
    masks = []
    m = L // 2
    while m >= HG_LEAF:
        same = one((ti // (2 * m)) == (si // (2 * m)))
        masks.append(same * one((ti % (2 * m)) >= m) * one((si % (2 * m)) < m))
        m //= 2
    masks.append(one((ti // HG_LEAF) == (si // HG_LEAF)) * one(si <= ti))
    return masks


def _hgrn_kernel(q_ref, ff_ref, fb_ref, i_ref, g_ref, lb_ref, y_ref,
                 qs_ref, lf_ref, kf_ref, lbk_ref, kb_ref, of_ref, ob_ref, sf_ref, sb_ref, *, seq):
    L = HG_L
    nc = seq // L
    lb = lb_ref[...]
    log_lb = jnp.log(lb)
    log1m_lb = jnp.log1p(-lb)

    def forget(fp):
        a = log_lb
        c = log1m_lb + _log_sigmoid(fp)
        logf = jnp.maximum(a, c) + jnp.log1p(jnp.exp(-jnp.abs(a - c)))
        return logf, (1.0 - lb) * jax.nn.sigmoid(-fp)

    qs_ref[...] = _silu(q_ref[...].astype(F32))
    lf_ref[...], kf_ref[...] = forget(ff_ref[...].astype(F32))
    lbk_ref[...], kb_ref[...] = forget(fb_ref[...].astype(F32))
    sf_ref[...] = jnp.zeros_like(sf_ref)
    sb_ref[...] = jnp.zeros_like(sb_ref)

    ti = lax.broadcasted_iota(jnp.int32, (L, L), 0)
    si = lax.broadcasted_iota(jnp.int32, (L, L), 1)
    tril = (si <= ti).astype(F32)
    triu = (si >= ti).astype(F32)
    masks_f = _hgrn_masks(False)
    masks_b = _hgrn_masks(True)

    def step(i, carry):
        t0 = pl.multiple_of(i * L, L)
        sl = pl.ds(t0, L)
        of_ref[sl, :] = _hgrn_chunk(qs_ref[sl, :], kf_ref[sl, :], i_ref[sl, :].astype(F32), lf_ref[sl, :],
                                    tril, masks_f, sf_ref, False)
        t1 = pl.multiple_of((nc - 1 - i) * L, L)
        sl = pl.ds(t1, L)
        ob_ref[sl, :] = _hgrn_chunk(qs_ref[sl, :], kb_ref[sl, :], i_ref[sl, :].astype(F32), lbk_ref[sl, :],
                                    triu, masks_b, sb_ref, True)
        return carry

    lax.fori_loop(0, nc, step, 0)
    o = of_ref[...] + ob_ref[...]
    o = o * lax.rsqrt(jnp.mean(o * o, axis=-1, keepdims=True) + EPS)
    y_ref[...] = (o * _silu(g_ref[...].astype(F32))).astype(y_ref.dtype)


def _hgrn_pallas(proj, lb, *, batch, seq, col0, out_dtype=F32):
    d = GROUP_W // HG_HEADS
    hb = GROUP_W // d
    cb = col0 // d
    spec = lambda grp: pl.BlockSpec((seq, d), lambda b, h, grp=grp: (b, cb + grp * hb + h))
    f32 = lambda *shape: pltpu.VMEM(shape, F32)
    return pl.pallas_call(
        functools.partial(_hgrn_kernel, seq=seq),
        grid=(batch, HG_HEADS),
        in_specs=[spec(0), spec(1), spec(2), spec(3), spec(4),
                  pl.BlockSpec((1, d), lambda b, h: (0, h))],
        out_specs=pl.BlockSpec((seq, d), lambda b, h: (b, h)),
        out_shape=jax.ShapeDtypeStruct((batch * seq, GROUP_W), out_dtype),
        scratch_shapes=[f32(seq, d)] * 7 + [f32(d, d), f32(d, d)],
        compiler_params=pltpu.CompilerParams(
            dimension_semantics=("parallel", "parallel"),
            vmem_limit_bytes=VMEM_LIMIT_BYTES),
    )(proj, proj, proj, proj, proj, lb.astype(F32).reshape(1, GROUP_W))


def _dft_cos_sin(n):
    k = (jnp.arange(n, dtype=jnp.int32)[:, None] * jnp.arange(n, dtype=jnp.int32)[None, :]) % n
    ang = k.astype(F32) * (2.0 * np.pi / n)
    return jnp.cos(ang), jnp.sin(ang)


def _fnet_tables(seq):
    cc, sc = _dft_cos_sin(FN_CH)
    eye = jnp.eye(FN_GROUPS, dtype=F32)
    chan = jnp.concatenate([jnp.kron(eye, cc), jnp.kron(eye, sc)], axis=1)
    cs, ss = _dft_cos_sin(seq)
    pos = jnp.concatenate([cs, -ss], axis=1) * ((seq * FN_CH) ** -0.5)
    return chan.astype(BF16), pos.astype(BF16)


def _fnet_chan_kernel(u_ref, dft_ref, v_ref):
    r = jnp.dot(u_ref[...].astype(BF16), dft_ref[...], preferred_element_type=F32)
    v_ref[0] = r[:, :GROUP_W].astype(v_ref.dtype)
    v_ref[1] = r[:, GROUP_W:].astype(v_ref.dtype)


def _fnet_pos_kernel(w_ref, v_ref, o_ref):
    o_ref[...] = jnp.dot(w_ref[...], v_ref[...], preferred_element_type=F32).astype(o_ref.dtype)


def _fnet_pallas(proj, tables, *, batch, seq, col0, out_dtype=F32, ts=512):
    chan, pos = tables
    cb = col0 // GROUP_W
    nt = seq // ts
    v = pl.pallas_call(
        _fnet_chan_kernel,
        grid=(batch, nt),
        in_specs=[pl.BlockSpec((ts, GROUP_W), lambda b, i: (b * nt + i, cb)),
                  pl.BlockSpec((GROUP_W, 2 * GROUP_W), lambda b, i: (0, 0))],
        out_specs=pl.BlockSpec((2, ts, GROUP_W), lambda b, i: (0, i, b)),
        out_shape=jax.ShapeDtypeStruct((2, seq, batch * GROUP_W), BF16),
        compiler_params=pltpu.CompilerParams(
            dimension_semantics=("parallel", "parallel"),
            vmem_limit_bytes=VMEM_LIMIT_BYTES),
    )(proj, chan)
    v = v.reshape(2 * seq, batch * GROUP_W)
    return pl.pallas_call(
        _fnet_pos_kernel,
        grid=(nt, batch),
        in_specs=[pl.BlockSpec((ts, 2 * seq), lambda i, b: (i, 0)),
                  pl.BlockSpec((2 * seq, GROUP_W), lambda i, b: (0, b))],
        out_specs=pl.BlockSpec((ts, GROUP_W), lambda i, b: (b * nt + i, 0)),
        out_shape=jax.ShapeDtypeStruct((batch * seq, GROUP_W), out_dtype),
        compiler_params=pltpu.CompilerParams(
            dimension_semantics=("parallel", "parallel"),
            vmem_limit_bytes=VMEM_LIMIT_BYTES),
    )(pos, v)


def _xattn_kernel(q_ref, k_ref, v_ref, o_ref):
    for h in range(XA_HEADS):
        sl = slice(h * XA_DH, (h + 1) * XA_DH)
        s = lax.dot_general(q_ref[:, sl].astype(BF16), k_ref[:, sl].astype(BF16),
                            (((1,), (1,)), ((), ())), preferred_element_type=F32) * (XA_DH ** -0.5)
        p = jnp.exp(s - jnp.max(s, axis=-1, keepdims=True))
        l = jnp.sum(p, axis=-1, keepdims=True)
        o = jnp.dot(p.astype(BF16), v_ref[:, sl].astype(BF16), preferred_element_type=F32)
        o_ref[:, sl] = (o / l).astype(o_ref.dtype)


def _xattn_pallas(q, k, v, *, batch, seq, mem_len, ts=512, out_dtype=BF16):
    nt = seq // ts
    return pl.pallas_call(
        _xattn_kernel,
        grid=(batch, nt),
        in_specs=[pl.BlockSpec((ts, D_MODEL), lambda b, i: (b * nt + i, 0)),
                  pl.BlockSpec((mem_len, D_MODEL), lambda b, i: (b, 0)),
                  pl.BlockSpec((mem_len, D_MODEL), lambda b, i: (b, 0))],
        out_specs=pl.BlockSpec((ts, D_MODEL), lambda b, i: (b * nt + i, 0)),
        out_shape=jax.ShapeDtypeStruct((batch * seq, D_MODEL), out_dtype),
        compiler_params=pltpu.CompilerParams(
            dimension_semantics=("parallel", "parallel"),
            vmem_limit_bytes=VMEM_LIMIT_BYTES),
    )(q, k, v)


MOE_FF = D_MODEL // 4
MOE_TB = 256
MOE_LANE0 = MOE_GROUPS
ROUTER_LANES = 128
META_E, META_RANK, META_GATE = 0, 2, 4
HALF_D = D_MODEL // 2


def _router_kernel(x_ref, g_ref, wr_ref, hp_ref, meta_ref, cnt_ref, carry_ref):
    tm = x_ref.shape[0]

    @pl.when(pl.program_id(0) == 0)
    def _():
        carry_ref[...] = jnp.zeros_like(carry_ref)

    x = x_ref[...].astype(F32)
    hn = x * lax.rsqrt(jnp.mean(x * x, axis=-1, keepdims=True) + EPS) * g_ref[...]
    hb = hn.astype(BF16)
    logits = jnp.dot(hb, wr_ref[...].astype(BF16), preferred_element_type=F32)
    bits = lax.bitcast_convert_type(hb.astype(F32), jnp.uint32)
    hp_ref[...] = (bits[:, :HALF_D] >> 16) | (bits[:, HALF_D:] & jnp.uint32(0xFFFF0000))

    lane = lax.broadcasted_iota(jnp.int32, (tm, ROUTER_LANES), 1).astype(F32)
    ninf = -jnp.inf
    first = lambda hit: jnp.min(jnp.where(hit, lane, float(ROUTER_LANES)), axis=-1, keepdims=True)
    gl = jnp.where(lane < MOE_GROUPS, logits, ninf)
    gmax = jnp.max(gl, axis=-1, keepdims=True)
    gidx = first(gl == gmax)
    g_gate = 1.0 / jnp.sum(jnp.exp(gl - gmax), axis=-1, keepdims=True)
    off = lane - (MOE_LANE0 + MOE_PER_GROUP * gidx)
    el = jnp.where(jnp.abs(2.0 * off - (MOE_PER_GROUP - 1)) < MOE_PER_GROUP, logits, ninf)
    v1 = jnp.max(el, axis=-1, keepdims=True)
    l1 = first(el == v1)
    el2 = jnp.where(lane == l1, ninf, el)
    v2 = jnp.max(el2, axis=-1, keepdims=True)
    l2 = first(el2 == v2)
    t = jnp.exp(v2 - v1)
    gate1 = g_gate / (1.0 + t)
    gate2 = g_gate * t / (1.0 + t)

    oh = jnp.where(lane == l1, 1.0, 0.0) + jnp.where(lane == l2, 1.0, 0.0)
    ti = lax.broadcasted_iota(jnp.int32, (tm, tm), 0)
    si = lax.broadcasted_iota(jnp.int32, (tm, tm), 1)
    before = jnp.where(si < ti, 1.0, 0.0).astype(BF16)
    base = jnp.dot(before, oh.astype(BF16), preferred_element_type=F32) + carry_ref[0:1, :]
    rank1 = jnp.sum(jnp.where(lane == l1, base, 0.0), axis=-1, keepdims=True)
    rank2 = jnp.sum(jnp.where(lane == l2, base, 0.0), axis=-1, keepdims=True)
    carry_ref[...] = carry_ref[...] + jnp.sum(oh, axis=0, keepdims=True)
    cnt_ref[...] = carry_ref[...]

    meta = jnp.zeros((tm, ROUTER_LANES), F32)
    for ln, val in ((META_E, l1 - MOE_LANE0), (META_E + 1, l2 - MOE_LANE0), (META_RANK, rank1),
                    (META_RANK + 1, rank2), (META_GATE, gate1), (META_GATE + 1, gate2)):
        meta = jnp.where(lane == ln, val, meta)
    meta_ref[...] = meta


def _router_pallas(h, gain, w_rg, w_re, *, tm=512):
    n = h.shape[0]
    wr = jnp.concatenate([w_rg, w_re], axis=1)
    wr = jnp.pad(wr, ((0, 0), (0, ROUTER_LANES - wr.shape[1])))
    return pl.pallas_call(
        _router_kernel,
        grid=(n // tm,),
        in_specs=[pl.BlockSpec((tm, D_MODEL), lambda i: (i, 0)),
                  pl.BlockSpec((1, D_MODEL), lambda i: (0, 0)),
                  pl.BlockSpec((D_MODEL, ROUTER_LANES), lambda i: (0, 0))],
        out_specs=[pl.BlockSpec((tm, HALF_D), lambda i: (i, 0)),
                   pl.BlockSpec((tm, ROUTER_LANES), lambda i: (i, 0)),
                   pl.BlockSpec((8, ROUTER_LANES), lambda i: (0, 0))],
        out_shape=[jax.ShapeDtypeStruct((n, HALF_D), jnp.uint32),
                   jax.ShapeDtypeStruct((n, ROUTER_LANES), F32),
                   jax.ShapeDtypeStruct((8, ROUTER_LANES), F32)],
        scratch_shapes=[pltpu.VMEM((8, ROUTER_LANES), F32)],
        compiler_params=pltpu.CompilerParams(
            dimension_semantics=("arbitrary",),
            vmem_limit_bytes=VMEM_LIMIT_BYTES),
    )(h, gain.reshape(1, D_MODEL).astype(F32), wr)


def _dispatch_kernel(dest_ref, hp_ref, xs_in_ref, xs_ref, sem):
    del xs_in_ref
    tc = hp_ref.shape[0]
    base = pl.program_id(0) * tc * MOE_TOPK

    def row_copy(r, k):
        return pltpu.make_async_copy(hp_ref.at[pl.ds(r, 1)],
                                     xs_ref.at[pl.ds(dest_ref[base + MOE_TOPK * r + k], 1)], sem)

    def start(r, c):
        for k in range(MOE_TOPK):
            row_copy(r, k).start()
        return c

    def wait(r, c):
        for k in range(MOE_TOPK):
            row_copy(r, k).wait()
        return c

    lax.fori_loop(0, tc, start, 0)
    lax.fori_loop(0, tc, wait, 0)


def _dispatch_pallas(hp, dest, n_slots, *, tc=256):
    n = hp.shape[0]
    xs0 = jnp.zeros((n_slots, HALF_D), jnp.uint32)
    return pl.pallas_call(
        _dispatch_kernel,
        grid_spec=pltpu.PrefetchScalarGridSpec(
            num_scalar_prefetch=1,
            grid=(n // tc,),
            in_specs=[pl.BlockSpec((tc, HALF_D), lambda i, d: (i, 0)),
                      pl.BlockSpec(memory_space=pl.ANY)],
            out_specs=pl.BlockSpec(memory_space=pl.ANY),
            scratch_shapes=[pltpu.SemaphoreType.DMA(())]),
        out_shape=jax.ShapeDtypeStruct((n_slots, HALF_D), jnp.uint32),
        input_output_aliases={2: 0},
        compiler_params=pltpu.CompilerParams(
            dimension_semantics=("arbitrary",),
            vmem_limit_bytes=VMEM_LIMIT_BYTES),
    )(dest, hp, xs0)


def _expert_kernel(be_ref, nu_ref, x_ref, w1_ref, w3_ref, w2_ref, y_ref):
    del be_ref

    @pl.when(pl.program_id(0) < nu_ref[0])
    def _():
        xp = x_ref[...]
        x_lo = lax.bitcast_convert_type(xp << 16, F32).astype(BF16)
        x_hi = lax.bitcast_convert_type(xp & jnp.uint32(0xFFFF0000), F32).astype(BF16)

        def up(w_ref):
            return (jnp.dot(x_lo, w_ref[:HALF_D, :].astype(BF16), preferred_element_type=F32)
                    + jnp.dot(x_hi, w_ref[HALF_D:, :].astype(BF16), preferred_element_type=F32))

        a = _silu(up(w1_ref)) * up(w3_ref)
        y_ref[...] = jnp.dot(a.astype(BF16), w2_ref[...].astype(BF16),
                             preferred_element_type=F32).astype(y_ref.dtype)

    @pl.when(pl.program_id(0) >= nu_ref[0])
    def _():
        y_ref[...] = jnp.zeros_like(y_ref)


def _expert_pallas(xs, block_e, n_used, w1, w3, w2):
    n_slots = xs.shape[0]
    nb = n_slots // MOE_TB
    blk = lambda i, be, nu: jnp.minimum(i, nu[0] - 1)
    wmap = lambda i, be, nu: (be[blk(i, be, nu)], 0, 0)
    return pl.pallas_call(
        _expert_kernel,
        grid_spec=pltpu.PrefetchScalarGridSpec(
            num_scalar_prefetch=2,
            grid=(nb,),
            in_specs=[pl.BlockSpec((MOE_TB, HALF_D), lambda i, be, nu: (blk(i, be, nu), 0)),
                      pl.BlockSpec((None, D_MODEL, MOE_FF), wmap),
                      pl.BlockSpec((None, D_MODEL, MOE_FF), wmap),
                      pl.BlockSpec((None, MOE_FF, D_MODEL), wmap)],
            out_specs=pl.BlockSpec((MOE_TB, D_MODEL), lambda i, be, nu: (i, 0))),
        out_shape=jax.ShapeDtypeStruct((n_slots, D_MODEL), F32),
        compiler_params=pltpu.CompilerParams(
            dimension_semantics=("arbitrary",),
            vmem_limit_bytes=VMEM_LIMIT_BYTES),
    )(block_e, n_used, xs, w1, w3, w2)


def _combine_kernel(dest_ref, h_ref, meta_ref, g_ref, yb_ref, o_ref, buf_ref, sem, *, final_norm):
    tc = h_ref.shape[0]
    base = pl.program_id(0) * tc * MOE_TOPK

    def row_copy(r, k):
        return pltpu.make_async_copy(yb_ref.at[pl.ds(dest_ref[base + MOE_TOPK * r + k], 1)],
                                     buf_ref.at[k, pl.ds(r, 1)], sem)

    def start(r, c):
        for k in range(MOE_TOPK):
            row_copy(r, k).start()
        return c

    def wait(r, c):
        for k in range(MOE_TOPK):
            row_copy(r, k).wait()
        return c

    lax.fori_loop(0, tc, start, 0)
    lax.fori_loop(0, tc, wait, 0)
    meta = meta_ref[...]
    out = h_ref[...]
    y = jnp.zeros_like(out)
    for k in range(MOE_TOPK):
        y = y + buf_ref[k] * meta[:, META_GATE + k:META_GATE + k + 1]
    out = out + y
    if final_norm:
        out = out * lax.rsqrt(jnp.mean(out * out, axis=-1, keepdims=True) + EPS) * g_ref[...]
    o_ref[...] = out


def _combine_pallas(h, meta, dest, yb, final_gain, *, final_norm, tc=256):
    n = h.shape[0]
    return pl.pallas_call(
        functools.partial(_combine_kernel, final_norm=final_norm),
        grid_spec=pltpu.PrefetchScalarGridSpec(
            num_scalar_prefetch=1,
            grid=(n // tc,),
            in_specs=[pl.BlockSpec((tc, D_MODEL), lambda i, d: (i, 0)),
                      pl.BlockSpec((tc, ROUTER_LANES), lambda i, d: (i, 0)),
                      pl.BlockSpec((1, D_MODEL), lambda i, d: (0, 0)),
                      pl.BlockSpec(memory_space=pl.ANY)],
            out_specs=pl.BlockSpec((tc, D_MODEL), lambda i, d: (i, 0)),
            scratch_shapes=[pltpu.VMEM((MOE_TOPK, tc, D_MODEL), F32), pltpu.SemaphoreType.DMA(())]),
        out_shape=jax.ShapeDtypeStruct((n, D_MODEL), F32),
        compiler_params=pltpu.CompilerParams(
            dimension_semantics=("arbitrary",),
            vmem_limit_bytes=VMEM_LIMIT_BYTES),
    )(dest, h, meta, final_gain.reshape(1, D_MODEL).astype(F32), yb)


def _moe_pallas(h, gain, w_rg, w_re, w1, w3, w2, final_gain, *, final_norm):
    n = h.shape[0]
    nb = (n * MOE_TOPK) // MOE_TB + MOE_EXPERTS
    hp, meta, cnt = _router_pallas(h, gain, w_rg, w_re)
    expert = meta[:, META_E:META_E + MOE_TOPK].astype(jnp.int32)
    rank = meta[:, META_RANK:META_RANK + MOE_TOPK].astype(jnp.int32)
    counts = cnt[0, MOE_LANE0:MOE_LANE0 + MOE_EXPERTS].astype(jnp.int32)
    nblk = (counts + MOE_TB - 1) // MOE_TB
    bend = jnp.cumsum(nblk)
    dest = ((bend - nblk)[expert] * MOE_TB + rank).reshape(n * MOE_TOPK)
    n_used = bend[-1:].astype(jnp.int32)
    block_e = jnp.minimum(jnp.searchsorted(bend, jnp.arange(nb, dtype=jnp.int32), side='right'),
                          MOE_EXPERTS - 1).astype(jnp.int32)
    xs = _dispatch_pallas(hp, dest, nb * MOE_TB)
    yb = _expert_pallas(xs, block_e, n_used, w1, w3, w2)
    return _combine_pallas(h, meta, dest, yb, final_gain, final_norm=final_norm)


COL_MLSTM, COL_HGRN, COL_FNET, COL_NA = 0, 4 * GROUP_W, 9 * GROUP_W, 10 * GROUP_W
N_GATES = 4 * ML_HEADS


def kernel(x, mem, norm_mix, norm_cross, norm_ffn, norm_final, norm_mem, w_in, mlstm_conv,
           mlstm_gate_bias, hgrn_lower_bound, na_rpb, group_gain, w_out, xa_wq, xa_wk, xa_wv, xa_wo,
           moe_router_group, moe_router_expert, moe_w1, moe_w3, moe_w2):
    b, s, d = x.shape
    mem_len = mem.shape[1]
    h = x.reshape(b * s, d)
    mem_f = mem.reshape(b * mem_len, d)
    lbs = jnp.cumsum(jax.nn.softmax(hgrn_lower_bound.astype(F32), axis=0), axis=0)
    lbs = lbs - lbs[0]
    fnet_tables = _fnet_tables(s)
    g0 = 4 * GROUP_W
    for l in range(DEPTH):
        w_main = jnp.concatenate([w_in[l][:, :g0], w_in[l][:, g0 + N_GATES:]], axis=1)
        w_gate = jnp.pad(w_in[l][:, g0:g0 + N_GATES], ((0, 0), (0, ML_GATE_LANES - N_GATES)))
        proj, gates = _matmul(h, w_main, gain=norm_mix[l], w_side=w_gate)
        y_ml = _mlstm_pallas(proj, gates[:, :N_GATES], mlstm_conv[l], mlstm_gate_bias[l], batch=b, seq=s)
        y_hg = _hgrn_pallas(proj, lbs[l], batch=b, seq=s, col0=COL_HGRN)
        y_fn = _fnet_pallas(proj, fnet_tables, batch=b, seq=s, col0=COL_FNET)
        y_na = _na_pallas(proj, na_rpb[l], batch=b, seq=s, col0=COL_NA)
        h = _matmul([y_ml, y_hg, y_fn, y_na], w_out[l], gain=group_gain[l], residual=h)

        q = _matmul(h, xa_wq[l], gain=norm_cross[l])
        k = _matmul(mem_f, xa_wk[l], gain=norm_mem)
        v = _matmul(mem_f, xa_wv[l], gain=norm_mem)
        o = _xattn_pallas(q, k, v, batch=b, seq=s, mem_len=mem_len)
        h = _matmul(o, xa_wo[l], residual=h)

        h = _moe_pallas(h, norm_ffn[l], moe_router_group[l], moe_router_expert[l],
                        moe_w1[l], moe_w3[l], moe_w2[l], norm_final, final_norm=(l == DEPTH - 1))
    return h.reshape(b, s, d)
```

```python
import functools

import jax
import jax.numpy as jnp
import numpy as np
from jax import lax
from jax.experimental import pallas as pl
from jax.experimental.pallas import tpu as pltpu

D_MODEL = 2048
DEPTH = 4
GRID_W = 64
N_MIXERS = 4
GROUP_W = D_MODEL // N_MIXERS
ML_HEADS = 4
ML_DH = GROUP_W // ML_HEADS
ML_CHUNK = 64
HG_HEADS = 4
HG_CHUNK = 16
FN_GROUPS = 4
FN_CH = GROUP_W // FN_GROUPS
NA_HEADS = 8
NA_DH = GROUP_W // NA_HEADS
NA_KH = 8
NA_KW = 16
XA_HEADS = 4
XA_DH = D_MODEL // XA_HEADS
MOE_GROUPS = 4
MOE_PER_GROUP = 8
MOE_EXPERTS = MOE_GROUPS * MOE_PER_GROUP
MOE_TOPK = 2
MOE_BLOCK = 128
EPS = 1e-6
F32 = jnp.float32
BF16 = jnp.bfloat16

VMEM_LIMIT_BYTES = 56 * 1024 * 1024


def _mm_kernel(*refs, n_x, norm, residual, side):
    refs = list(refs)
    x_refs = [refs.pop(0) for _ in range(n_x)]
    g_ref = refs.pop(0) if norm else None
    w_ref = refs.pop(0)
    ws_ref = refs.pop(0) if side else None
    r_ref = refs.pop(0) if residual else None
    o_ref = refs.pop(0)
    os_ref = refs.pop(0) if side else None
    xn_ref = refs.pop(0)

    @pl.when(pl.program_id(1) == 0)
    def _():
        col = 0
        for x_ref in x_refs:
            x = x_ref[...].astype(F32)
            kx = x.shape[1]
            if norm:
                ms = jnp.mean(x * x, axis=-1, keepdims=True)
                x = x * lax.rsqrt(ms + EPS) * g_ref[:, col:col + kx]
            xn_ref[:, col:col + kx] = x.astype(BF16)
            col += kx
        if side:
            os_ref[...] = jnp.dot(xn_ref[...], ws_ref[...].astype(BF16),
                                  preferred_element_type=F32).astype(os_ref.dtype)

    acc = jnp.dot(xn_ref[...], w_ref[...].astype(BF16), preferred_element_type=F32)
    if residual:
        acc = acc + r_ref[...]
    o_ref[...] = acc.astype(o_ref.dtype)


def _matmul(xs, w, layer, *, name, gain=None, residual=None, w_side=None, tm=1024, tn=512, out_dtype=F32):
    if not isinstance(xs, (list, tuple)):
        xs = [xs]
    m = xs[0].shape[0]
    k = sum(x.shape[1] for x in xs)
    n = w.shape[2]
    tm = min(tm, m)
    tn = min(tn, n)
    assert m % tm == 0 and n % tn == 0 and w.shape[1] == k, (m, n, k, tm, tn)
    norm = gain is not None
    has_res = residual is not None
    side = w_side is not None
    in_specs = [pl.BlockSpec((tm, x.shape[1]), lambda i, j: (i, 0)) for x in xs]
    args = list(xs)
    if norm:
        in_specs.append(pl.BlockSpec((1, k), lambda i, j: (0, 0)))
        args.append(gain.reshape(1, k).astype(F32))
    in_specs.append(pl.BlockSpec((None, k, tn), lambda i, j: (layer, 0, j)))
    args.append(w)
    if side:
        in_specs.append(pl.BlockSpec((None, k, w_side.shape[2]), lambda i, j: (layer, 0, 0)))
        args.append(w_side)
    if has_res:
        in_specs.append(pl.BlockSpec((tm, tn), lambda i, j: (i, j)))
        args.append(residual)
    out_specs = pl.BlockSpec((tm, tn), lambda i, j: (i, j))
    out_shape = jax.ShapeDtypeStruct((m, n), out_dtype)
    if side:
        out_specs = [out_specs, pl.BlockSpec((tm, w_side.shape[2]), lambda i, j: (i, 0))]
        out_shape = [out_shape, jax.ShapeDtypeStruct((m, w_side.shape[2]), F32)]
    return pl.pallas_call(
        functools.partial(_mm_kernel, n_x=len(xs), norm=norm, residual=has_res, side=side),
        name=name,
        grid=(m // tm, n // tn),
        in_specs=in_specs,
        out_specs=out_specs,
        out_shape=out_shape,
        scratch_shapes=[pltpu.VMEM((tm, k), BF16)],
        compiler_params=pltpu.CompilerParams(
            dimension_semantics=("parallel", "arbitrary"),
            vmem_limit_bytes=VMEM_LIMIT_BYTES),
    )(*args)


NA_QROWS = 4
NA_WROWS = 12


def _na_groups(rows):
    tables, plan = [], []
    for r0 in range(0, rows, NA_QROWS):
        band0 = lambda r: min(max(r - NA_KH // 2, 0), rows - NA_KH)
        kr0 = min(band0(r0), rows - NA_WROWS)
        assert band0(r0 + NA_QROWS - 1) + NA_KH <= kr0 + NA_WROWS
        dr = np.full((NA_QROWS, NA_WROWS), -1, np.int64)
        for i in range(NA_QROWS):
            for j in range(NA_WROWS):
                if 0 <= kr0 + j - band0(r0 + i) < NA_KH:
                    dr[i, j] = kr0 + j - (r0 + i) + NA_KH - 1
        key = dr.tobytes()
        if key not in [t.tobytes() for t in tables]:
            tables.append(dr)
        plan.append((r0, kr0, [t.tobytes() for t in tables].index(key)))
    return plan, np.stack(tables)


def _na_bias_tables(rpb, dr_tables):
    c = np.arange(GRID_W)
    dc = np.clip(c[None, :] - c[:, None] + NA_KW - 1, 0, 2 * NA_KW - 2)
    onehot = (dc[None] == np.arange(2 * NA_KW - 1)[:, None, None]).astype(np.float32)
    col_start = np.clip(c - NA_KW // 2, 0, GRID_W - NA_KW)
    col_ok = (c[None, :] >= col_start[:, None]) & (c[None, :] < col_start[:, None] + NA_KW)
    t = jnp.einsum('hrd,dqk->hrqk', rpb.astype(F32), onehot, precision=lax.Precision.HIGHEST)
    t = jnp.where(col_ok[None, None], t, -jnp.inf)
    t = jnp.concatenate([t, jnp.full_like(t[:, :1], -jnp.inf)], axis=1)
    idx = np.where(dr_tables < 0, t.shape[1] - 1, dr_tables)
    win = t[:, idx]
    n_t = dr_tables.shape[0]
    return win.transpose(0, 1, 2, 4, 3, 5).reshape(NA_HEADS, n_t, NA_QROWS * GRID_W, NA_WROWS * GRID_W)


def _na_kernel(q_ref, k_ref, v_ref, bias_ref, o_ref, *, plan):
    nq = NA_QROWS * GRID_W
    nk = NA_WROWS * GRID_W
    for r0, kr0, typ in plan:
        q = q_ref[r0 * GRID_W:r0 * GRID_W + nq, :].astype(F32) * (NA_DH ** -0.5)
        kb = k_ref[kr0 * GRID_W:kr0 * GRID_W + nk, :]
        vb = v_ref[kr0 * GRID_W:kr0 * GRID_W + nk, :]
        outs = []
        for hh in range(2):
            sl = slice(hh * NA_DH, (hh + 1) * NA_DH)
            s = lax.dot_general(q[:, sl].astype(BF16), kb[:, sl].astype(BF16),
                                (((1,), (1,)), ((), ())), preferred_element_type=F32)
            s = s + bias_ref[hh, typ]
            m = jnp.max(s, axis=-1, keepdims=True)
            p = jnp.exp(s - m)
            l = jnp.sum(p, axis=-1, keepdims=True)
            o = jnp.dot(p.astype(BF16), vb[:, sl].astype(BF16), preferred_element_type=F32)
            outs.append(o / l)
        o_ref[r0 * GRID_W:r0 * GRID_W + nq, :] = jnp.concatenate(outs, axis=-1).astype(o_ref.dtype)


def _na_pallas(proj, rpb, *, batch, seq, col0, out_dtype=F32):
    rows = seq // GRID_W
    pair_w = 2 * NA_DH
    cb = col0 // pair_w
    gb = GROUP_W // pair_w
    plan, dr_tables = _na_groups(rows)
    bias = _na_bias_tables(rpb, dr_tables)
    spec = lambda g: pl.BlockSpec((seq, pair_w), lambda b, p, g=g: (b, cb + g * gb + p))
    return pl.pallas_call(
        functools.partial(_na_kernel, plan=plan),
        name="na_attention",
        grid=(batch, NA_HEADS // 2),
        in_specs=[spec(0), spec(1), spec(2),
                  pl.BlockSpec((2,) + bias.shape[1:], lambda b, p: (p, 0, 0, 0))],
        out_specs=pl.BlockSpec((seq, pair_w), lambda b, p: (b, p)),
        out_shape=jax.ShapeDtypeStruct((batch * seq, GROUP_W), out_dtype),
        compiler_params=pltpu.CompilerParams(
            dimension_semantics=("parallel", "parallel"),
            vmem_limit_bytes=VMEM_LIMIT_BYTES),
    )(proj, proj, proj, bias)


ML_L = 128
ML_GATES = 8
ML_GATE_LANES = 128


def _log_sigmoid(x):
    return jnp.minimum(x, 0.0) - jnp.log(1.0 + jnp.exp(-jnp.abs(x)))


def _silu(x):
    return x * jax.nn.sigmoid(x)


def _mlstm_direction(q, k, v, b_col, ig_col, b_row, ig_row, mask, c_ref, n_ref, m_ref, end_row):
    b_end = b_col[end_row:end_row + 1, :]
    m_prev = m_ref[...]
    dmat = jnp.where(mask, b_col - b_row + ig_row, -jnp.inf)
    a_col = b_end - b_col + ig_col
    m_loc = jnp.max(a_col, axis=0, keepdims=True)
    inter = b_col + m_prev
    m_t = jnp.maximum(inter, jnp.max(dmat, axis=-1, keepdims=True))
    w = jnp.exp(dmat - m_t)
    s_inter = jnp.exp(inter - m_t)
    qb = q.astype(BF16)
    vb = v.astype(BF16)
    qk = lax.dot_general(qb, k.astype(BF16), (((1,), (1,)), ((), ())), preferred_element_type=F32) * w
    num = (jnp.dot(qk.astype(BF16), vb, preferred_element_type=F32)
           + s_inter * jnp.dot(qb, c_ref[...].astype(BF16), preferred_element_type=F32))
    den = (jnp.sum(qk, axis=-1, keepdims=True)
           + s_inter * jnp.sum(q * n_ref[...], axis=-1, keepdims=True))
    h = num / jnp.maximum(jnp.abs(den), jnp.exp(-m_t))
    m_new = jnp.maximum(b_end + m_prev, m_loc)
    s_old = jnp.exp(b_end + m_prev - m_new)
    s_new = jnp.exp(m_loc - m_new)
    kw = k * jnp.exp(a_col - m_loc)
    c_loc = lax.dot_general(kw.astype(BF16), vb, (((0,), (0,)), ((), ())), preferred_element_type=F32)
    c_ref[...] = s_old * c_ref[...] + s_new * c_loc
    n_ref[...] = s_old * n_ref[...] + s_new * jnp.sum(kw, axis=0, keepdims=True)
    m_ref[...] = m_new
    return h


def _mlstm_kernel(q_ref, k_ref, v_ref, o_ref, cwq_ref, cwk_ref, gcol_ref, grow_ref, bcol_ref, brow_ref,
                  y_ref, qs_ref, ks_ref, hf_ref, hb_ref, cf_ref, nf_ref, mf_ref, cb_ref, nb_ref, mb_ref, *, seq):
    L = ML_L
    nc = seq // L
    t_idx = lax.broadcasted_iota(jnp.int32, (seq, 1), 0)

    def conv_silu(x, w):
        prev = jnp.where(t_idx == 0, 0.0, pltpu.roll(x, 1, axis=0))
        nxt = jnp.where(t_idx == seq - 1, 0.0, pltpu.roll(x, seq - 1, axis=0))
        return _silu(prev * w[0:1, :] + x * w[1:2, :] + nxt * w[2:3, :])

    qs_ref[...] = conv_silu(q_ref[...].astype(F32), cwq_ref[...])
    ks_ref[...] = conv_silu(k_ref[...].astype(F32), cwk_ref[...]) * (ML_DH ** -0.5)

    for r in (cf_ref, nf_ref, mf_ref, cb_ref, nb_ref, mb_ref):
        r[...] = jnp.zeros_like(r)

    ti = lax.broadcasted_iota(jnp.int32, (L, L), 0)
    si = lax.broadcasted_iota(jnp.int32, (L, L), 1)
    lower = si <= ti
    upper = si >= ti
    tril = lower.astype(F32)
    triu = upper.astype(F32)

    def step(i, carry):
        for direction in (0, 1):
            c = i if direction == 0 else nc - 1 - i
            t0 = pl.multiple_of(c * L, L)
            gcol = gcol_ref[pl.ds(t0, L), :] + bcol_ref[...]
            grow = grow_ref[:, pl.ds(t0, L)] + brow_ref[...]
            tri = tril if direction == 0 else triu
            cum_col = jnp.dot(tri, _log_sigmoid(gcol), precision=lax.Precision.HIGHEST,
                              preferred_element_type=F32)
            cum_row = lax.dot_general(_log_sigmoid(grow), tri, (((1,), (1,)), ((), ())),
                                      precision=lax.Precision.HIGHEST, preferred_element_type=F32)
            gi = 2 * direction
            ig_col = gcol[:, gi:gi + 1]
            b_col = cum_col[:, gi + 1:gi + 2]
            ig_row = grow[gi:gi + 1, :]
            b_row = cum_row[gi + 1:gi + 2, :]
            q = qs_ref[pl.ds(t0, L), :]
            k = ks_ref[pl.ds(t0, L), :]
            v = v_ref[pl.ds(t0, L), :].astype(F32)
            if direction == 0:
                h = _mlstm_direction(q, k, v, b_col, ig_col, b_row, ig_row, lower,
                                     cf_ref, nf_ref, mf_ref, L - 1)
                hf_ref[pl.ds(t0, L), :] = h
            else:
                h = _mlstm_direction(q, k, v, b_col, ig_col, b_row, ig_row, upper,
                                     cb_ref, nb_ref, mb_ref, 0)
                hb_ref[pl.ds(t0, L), :] = h
        return carry

    lax.fori_loop(0, nc, step, 0, unroll=2)
    y_ref[...] = (jax.nn.sigmoid(o_ref[...].astype(F32)) * (hf_ref[...] + hb_ref[...])).astype(y_ref.dtype)


def _mlstm_pallas(proj, gates, conv_w, gate_b, *, batch, seq, out_dtype=F32):
    d = ML_DH
    hb = GROUP_W // d
    g4 = gates.reshape(batch, seq, 4, ML_HEADS).transpose(0, 3, 1, 2).astype(F32)
    g = jnp.pad(g4, ((0, 0), (0, 0), (0, 0), (0, ML_GATE_LANES - 4)))
    g_row = jnp.pad(g4, ((0, 0), (0, 0), (0, 0), (0, ML_GATES - 4))).transpose(0, 1, 3, 2)
    gb4 = gate_b.astype(F32).reshape(4, ML_HEADS).T
    gb_col = jnp.pad(gb4, ((0, 0), (0, ML_GATE_LANES - 4))).reshape(ML_HEADS, 1, ML_GATE_LANES)
    gb_row = jnp.pad(gb4, ((0, 0), (0, ML_GATES - 4))).reshape(ML_HEADS, ML_GATES, 1)
    spec = lambda grp: pl.BlockSpec((seq, d), lambda b, h, grp=grp: (b, grp * hb + h))
    f32 = lambda *shape: pltpu.VMEM(shape, F32)
    return pl.pallas_call(
        functools.partial(_mlstm_kernel, seq=seq),
        name="mlstm",
        grid=(batch, ML_HEADS),
        in_specs=[spec(0), spec(1), spec(2), spec(3),
                  pl.BlockSpec((3, d), lambda b, h: (0, h)),
                  pl.BlockSpec((3, d), lambda b, h: (0, hb + h)),
                  pl.BlockSpec((None, None, seq, ML_GATE_LANES), lambda b, h: (b, h, 0, 0)),
                  pl.BlockSpec((None, None, ML_GATES, seq), lambda b, h: (b, h, 0, 0)),
                  pl.BlockSpec((None, 1, ML_GATE_LANES), lambda b, h: (h, 0, 0)),
                  pl.BlockSpec((None, ML_GATES, 1), lambda b, h: (h, 0, 0))],
        out_specs=pl.BlockSpec((seq, d), lambda b, h: (b, h)),
        out_shape=jax.ShapeDtypeStruct((batch * seq, GROUP_W), out_dtype),
        scratch_shapes=[f32(seq, d), f32(seq, d), f32(seq, d), f32(seq, d),
                        f32(d, d), f32(1, d), f32(1, 1), f32(d, d), f32(1, d), f32(1, 1)],
        compiler_params=pltpu.CompilerParams(
            dimension_semantics=("parallel", "parallel"),
            vmem_limit_bytes=VMEM_LIMIT_BYTES),
    )(proj, proj, proj, proj, conv_w.astype(F32), conv_w.astype(F32), g, g_row, gb_col, gb_row)


HG_L = 128
HG_LEAF = 8


def _hgrn_level_ref(p, block, row):
    nb = p.shape[0] // block
    pb = p.reshape(nb, block, p.shape[1])
    return jnp.broadcast_to(pb[:, row:row + 1, :], pb.shape).reshape(p.shape)


def _hgrn_chunk(q, k, v, logf, tri, masks, st_ref, backward):
    L = HG_L
    p = jnp.dot(tri, logf, precision=lax.Precision.HIGHEST, preferred_element_type=F32)
    nt = (((1,), (1,)), ((), ()))
    attn = jnp.zeros((L, L), F32)
    m = L // 2
    li = 0
    while m >= HG_LEAF:
        ref = _hgrn_level_ref(p, 2 * m, m if backward else m - 1)
        qh = q * jnp.exp(jnp.minimum(p - ref, 0.0))
        kh = k * jnp.exp(jnp.minimum(ref - p, 0.0))
        s = lax.dot_general(qh.astype(BF16), kh.astype(BF16), nt, preferred_element_type=F32)
        attn = attn + jnp.where(masks[li] != 0.0, s, 0.0)
        m //= 2
        li += 1
    ref = _hgrn_level_ref(p, HG_LEAF, HG_LEAF // 2 if backward else HG_LEAF // 2 - 1)
    qh = q * jnp.exp(p - ref)
    kh = k * jnp.exp(ref - p)
    s = lax.dot_general(qh.astype(BF16), kh.astype(BF16), nt, preferred_element_type=F32)
    attn = attn + jnp.where(masks[li] != 0.0, s, 0.0)
    vb = v.astype(BF16)
    o = jnp.dot(attn.astype(BF16), vb, preferred_element_type=F32)
    o = o + lax.dot_general((q * jnp.exp(p)).astype(BF16), st_ref[...].astype(BF16), nt,
                            preferred_element_type=F32)
    p_end = p[0:1, :] if backward else p[L - 1:L, :]
    kd = k * jnp.exp(p_end - p)
    st_ref[...] = (st_ref[...] * jnp.exp(p_end)
                   + lax.dot_general(vb, kd.astype(BF16), (((0,), (0,)), ((), ())), preferred_element_type=F32))
    return o


def _hgrn_masks(backward):
    L = HG_L
    ti = lax.broadcasted_iota(jnp.int32, (L, L), 0)
    si = lax.broadcasted_iota(jnp.int32, (L, L), 1)
    if backward:
        ti, si = si, ti
    one = lambda cond: jnp.where(cond, 1.0, 0.0)
    masks = []
    m = L // 2
    while m >= HG_LEAF:
        same = one((ti // (2 * m)) == (si // (2 * m)))
        masks.append(same * one((ti % (2 * m)) >= m) * one((si % (2 * m)) < m))
        m //= 2
    masks.append(one((ti // HG_LEAF) == (si // HG_LEAF)) * one(si <= ti))
    return masks


def _hgrn_kernel(q_ref, ff_ref, fb_ref, i_ref, g_ref, lb_ref, y_ref,
                 qs_ref, lf_ref, kf_ref, lbk_ref, kb_ref, of_ref, ob_ref, sf_ref, sb_ref, *, seq):
    L = HG_L
    nc = seq // L
    lb = lb_ref[...]
    log_lb = jnp.log(lb)
    log1m_lb = jnp.log1p(-lb)

    def forget(fp):
        ls = _log_sigmoid(fp)
        a = log_lb
        c = log1m_lb + ls
        logf = jnp.maximum(a, c) + jnp.log(1.0 + jnp.exp(-jnp.abs(a - c)))
        return logf, (1.0 - lb) * jnp.exp(ls - fp)

    qs_ref[...] = _silu(q_ref[...].astype(F32))
    lf_ref[...], kf_ref[...] = forget(ff_ref[...].astype(F32))
    lbk_ref[...], kb_ref[...] = forget(fb_ref[...].astype(F32))
    sf_ref[...] = jnp.zeros_like(sf_ref)
    sb_ref[...] = jnp.zeros_like(sb_ref)

    ti = lax.broadcasted_iota(jnp.int32, (L, L), 0)
    si = lax.broadcasted_iota(jnp.int32, (L, L), 1)
    tril = (si <= ti).astype(F32)
    triu = (si >= ti).astype(F32)
    masks_f = _hgrn_masks(False)
    masks_b = _hgrn_masks(True)

    def step(i, carry):
        t0 = pl.multiple_of(i * L, L)
        sl = pl.ds(t0, L)
        of_ref[sl, :] = _hgrn_chunk(qs_ref[sl, :], kf_ref[sl, :], i_ref[sl, :].astype(F32), lf_ref[sl, :],
                                    tril, masks_f, sf_ref, False)
        t1 = pl.multiple_of((nc - 1 - i) * L, L)
        sl = pl.ds(t1, L)
        ob_ref[sl, :] = _hgrn_chunk(qs_ref[sl, :], kb_ref[sl, :], i_ref[sl, :].astype(F32), lbk_ref[sl, :],
                                    triu, masks_b, sb_ref, True)
        return carry

    lax.fori_loop(0, nc, step, 0, unroll=2)
    o = of_ref[...] + ob_ref[...]
    o = o * lax.rsqrt(jnp.mean(o * o, axis=-1, keepdims=True) + EPS)
    y_ref[...] = (o * _silu(g_ref[...].astype(F32))).astype(y_ref.dtype)


def _hgrn_pallas(proj, lb, *, batch, seq, col0, out_dtype=F32):
    d = GROUP_W // HG_HEADS
    hb = GROUP_W // d
    cb = col0 // d
    spec = lambda grp: pl.BlockSpec((seq, d), lambda b, h, grp=grp: (b, cb + grp * hb + h))
    f32 = lambda *shape: pltpu.VMEM(shape, F32)
    return pl.pallas_call(
        functools.partial(_hgrn_kernel, seq=seq),
        name="hgrn2",
        grid=(batch, HG_HEADS),
        in_specs=[spec(0), spec(1), spec(2), spec(3), spec(4),
                  pl.BlockSpec((1, d), lambda b, h: (0, h))],
        out_specs=pl.BlockSpec((seq, d), lambda b, h: (b, h)),
        out_shape=jax.ShapeDtypeStruct((batch * seq, GROUP_W), out_dtype),
        scratch_shapes=[f32(seq, d)] * 7 + [f32(d, d), f32(d, d)],
        compiler_params=pltpu.CompilerParams(
            dimension_semantics=("parallel", "parallel"),
            vmem_limit_bytes=VMEM_LIMIT_BYTES),
    )(proj, proj, proj, proj, proj, lb.astype(F32).reshape(1, GROUP_W))


def _dft_cos_sin(n):
    k = (jnp.arange(n, dtype=jnp.int32)[:, None] * jnp.arange(n, dtype=jnp.int32)[None, :]) % n
    ang = k.astype(F32) * (2.0 * np.pi / n)
    return jnp.cos(ang), jnp.sin(ang)


def _fnet_tables(seq):
    cc, sc = _dft_cos_sin(FN_CH)
    eye = jnp.eye(FN_GROUPS, dtype=F32)
    chan = jnp.concatenate([jnp.kron(eye, cc), jnp.kron(eye, sc)], axis=1)
    cs, ss = _dft_cos_sin(seq)
    pos = jnp.concatenate([cs, -ss], axis=1) * ((seq * FN_CH) ** -0.5)
    return chan.astype(BF16), pos.astype(BF16)


def _fnet_chan_kernel(u_ref, dft_ref, v_ref):
    r = jnp.dot(u_ref[...].astype(BF16), dft_ref[...], preferred_element_type=F32)
    v_ref[0] = r[:, :GROUP_W].astype(v_ref.dtype)
    v_ref[1] = r[:, GROUP_W:].astype(v_ref.dtype)


def _fnet_pos_kernel(w_ref, v_ref, o_ref):
    o_ref[...] = jnp.dot(w_ref[...], v_ref[...], preferred_element_type=F32).astype(o_ref.dtype)


def _fnet_pallas(proj, tables, *, batch, seq, col0, out_dtype=F32, ts=512):
    chan, pos = tables
    cb = col0 // GROUP_W
    nt = seq // ts
    v = pl.pallas_call(
        _fnet_chan_kernel,
        name="fnet_channel_dft",
        grid=(batch, nt),
        in_specs=[pl.BlockSpec((ts, GROUP_W), lambda b, i: (b * nt + i, cb)),
                  pl.BlockSpec((GROUP_W, 2 * GROUP_W), lambda b, i: (0, 0))],
        out_specs=pl.BlockSpec((2, ts, GROUP_W), lambda b, i: (0, i, b)),
        out_shape=jax.ShapeDtypeStruct((2, seq, batch * GROUP_W), BF16),
        compiler_params=pltpu.CompilerParams(
            dimension_semantics=("parallel", "parallel"),
            vmem_limit_bytes=VMEM_LIMIT_BYTES),
    )(proj, chan)
    v = v.reshape(2 * seq, batch * GROUP_W)
    return pl.pallas_call(
        _fnet_pos_kernel,
        name="fnet_position_dft",
        grid=(nt, batch),
        in_specs=[pl.BlockSpec((ts, 2 * seq), lambda i, b: (i, 0)),
                  pl.BlockSpec((2 * seq, GROUP_W), lambda i, b: (0, b))],
        out_specs=pl.BlockSpec((ts, GROUP_W), lambda i, b: (b * nt + i, 0)),
        out_shape=jax.ShapeDtypeStruct((batch * seq, GROUP_W), out_dtype),
        compiler_params=pltpu.CompilerParams(
            dimension_semantics=("parallel", "parallel"),
            vmem_limit_bytes=VMEM_LIMIT_BYTES),
    )(pos, v)


def _xattn_kernel(q_ref, k_ref, v_ref, o_ref):
    for h in range(XA_HEADS):
        sl = slice(h * XA_DH, (h + 1) * XA_DH)
        s = lax.dot_general(q_ref[:, sl].astype(BF16), k_ref[:, sl].astype(BF16),
                            (((1,), (1,)), ((), ())), preferred_element_type=F32) * (XA_DH ** -0.5)
        p = jnp.exp(s - jnp.max(s, axis=-1, keepdims=True))
        l = jnp.sum(p, axis=-1, keepdims=True)
        o = jnp.dot(p.astype(BF16), v_ref[:, sl].astype(BF16), preferred_element_type=F32)
        o_ref[:, sl] = (o / l).astype(o_ref.dtype)


def _xattn_pallas(q, k, v, *, batch, seq, mem_len, ts=512, out_dtype=BF16):
    nt = seq // ts
    return pl.pallas_call(
        _xattn_kernel,
        name="cross_attention",
        grid=(batch, nt),
        in_specs=[pl.BlockSpec((ts, D_MODEL), lambda b, i: (b * nt + i, 0)),
                  pl.BlockSpec((mem_len, D_MODEL), lambda b, i: (b, 0)),
                  pl.BlockSpec((mem_len, D_MODEL), lambda b, i: (b, 0))],
        out_specs=pl.BlockSpec((ts, D_MODEL), lambda b, i: (b * nt + i, 0)),
        out_shape=jax.ShapeDtypeStruct((batch * seq, D_MODEL), out_dtype),
        compiler_params=pltpu.CompilerParams(
            dimension_semantics=("parallel", "parallel"),
            vmem_limit_bytes=VMEM_LIMIT_BYTES),
    )(q, k, v)


MOE_FF = D_MODEL // 4
MOE_TB = 256
MOE_LANE0 = MOE_GROUPS
ROUTER_LANES = 128
META_E, META_RANK, META_GATE = 0, 2, 4
HALF_D = D_MODEL // 2
ROW_DMA_UNROLL = 8


def _router_kernel(x_ref, g_ref, wr_ref, hp_ref, meta_ref, cnt_ref, carry_ref):
    tm = x_ref.shape[0]

    @pl.when(pl.program_id(0) == 0)
    def _():
        carry_ref[...] = jnp.zeros_like(carry_ref)

    x = x_ref[...].astype(F32)
    hn = x * lax.rsqrt(jnp.mean(x * x, axis=-1, keepdims=True) + EPS) * g_ref[...]
    hb = hn.astype(BF16)
    logits = jnp.dot(hb, wr_ref[...].astype(BF16), preferred_element_type=F32)
    bits = lax.bitcast_convert_type(hb.astype(F32), jnp.uint32)
    hp_ref[...] = (bits[:, :HALF_D] >> 16) | (bits[:, HALF_D:] & jnp.uint32(0xFFFF0000))

    lane = lax.broadcasted_iota(jnp.int32, (tm, ROUTER_LANES), 1).astype(F32)
    ninf = -jnp.inf
    first = lambda hit: jnp.min(jnp.where(hit, lane, float(ROUTER_LANES)), axis=-1, keepdims=True)
    gl = jnp.where(lane < MOE_GROUPS, logits, ninf)
    gmax = jnp.max(gl, axis=-1, keepdims=True)
    gidx = first(gl == gmax)
    g_gate = 1.0 / jnp.sum(jnp.exp(gl - gmax), axis=-1, keepdims=True)
    off = lane - (MOE_LANE0 + MOE_PER_GROUP * gidx)
    el = jnp.where(jnp.abs(2.0 * off - (MOE_PER_GROUP - 1)) < MOE_PER_GROUP, logits, ninf)
    v1 = jnp.max(el, axis=-1, keepdims=True)
    l1 = first(el == v1)
    el2 = jnp.where(lane == l1, ninf, el)
    v2 = jnp.max(el2, axis=-1, keepdims=True)
    l2 = first(el2 == v2)
    t = jnp.exp(v2 - v1)
    gate1 = g_gate / (1.0 + t)
    gate2 = g_gate * t / (1.0 + t)

    oh = jnp.where(lane == l1, 1.0, 0.0) + jnp.where(lane == l2, 1.0, 0.0)
    ti = lax.broadcasted_iota(jnp.int32, (tm, tm), 0)
    si = lax.broadcasted_iota(jnp.int32, (tm, tm), 1)
    before = jnp.where(si < ti, 1.0, 0.0).astype(BF16)
    base = jnp.dot(before, oh.astype(BF16), preferred_element_type=F32) + carry_ref[0:1, :]
    rank1 = jnp.sum(jnp.where(lane == l1, base, 0.0), axis=-1, keepdims=True)
    rank2 = jnp.sum(jnp.where(lane == l2, base, 0.0), axis=-1, keepdims=True)
    carry_ref[...] = carry_ref[...] + jnp.sum(oh, axis=0, keepdims=True)
    cnt_ref[...] = carry_ref[...]

    meta = jnp.zeros((tm, ROUTER_LANES), F32)
    for ln, val in ((META_E, l1 - MOE_LANE0), (META_E + 1, l2 - MOE_LANE0), (META_RANK, rank1),
                    (META_RANK + 1, rank2), (META_GATE, gate1), (META_GATE + 1, gate2)):
        meta = jnp.where(lane == ln, val, meta)
    meta_ref[...] = meta


def _router_pallas(h, gain, w_rg, w_re, *, tm=512):
    n = h.shape[0]
    wr = jnp.concatenate([w_rg, w_re], axis=1)
    wr = jnp.pad(wr, ((0, 0), (0, ROUTER_LANES - wr.shape[1])))
    return pl.pallas_call(
        _router_kernel,
        name="moe_router",
        grid=(n // tm,),
        in_specs=[pl.BlockSpec((tm, D_MODEL), lambda i: (i, 0)),
                  pl.BlockSpec((1, D_MODEL), lambda i: (0, 0)),
                  pl.BlockSpec((D_MODEL, ROUTER_LANES), lambda i: (0, 0))],
        out_specs=[pl.BlockSpec((tm, HALF_D), lambda i: (i, 0)),
                   pl.BlockSpec((tm, ROUTER_LANES), lambda i: (i, 0)),
                   pl.BlockSpec((8, ROUTER_LANES), lambda i: (0, 0))],
        out_shape=[jax.ShapeDtypeStruct((n, HALF_D), jnp.uint32),
                   jax.ShapeDtypeStruct((n, ROUTER_LANES), F32),
                   jax.ShapeDtypeStruct((8, ROUTER_LANES), F32)],
        scratch_shapes=[pltpu.VMEM((8, ROUTER_LANES), F32)],
        compiler_params=pltpu.CompilerParams(
            dimension_semantics=("arbitrary",),
            vmem_limit_bytes=VMEM_LIMIT_BYTES),
    )(h, gain.reshape(1, D_MODEL).astype(F32), wr)


def _dispatch_kernel(dest_ref, hp_ref, xs_in_ref, xs_ref, sem):
    del xs_in_ref
    tc = hp_ref.shape[0]
    base = pl.program_id(0) * tc * MOE_TOPK

    def row_copy(r, k):
        return pltpu.make_async_copy(hp_ref.at[pl.ds(r, 1)],
                                     xs_ref.at[pl.ds(dest_ref[base + MOE_TOPK * r + k], 1)], sem)

    def start(r, c):
        for k in range(MOE_TOPK):
            row_copy(r, k).start()
        return c

    def wait(r, c):
        for k in range(MOE_TOPK):
            row_copy(r, k).wait()
        return c

    lax.fori_loop(0, tc, start, 0, unroll=ROW_DMA_UNROLL)
    lax.fori_loop(0, tc, wait, 0, unroll=ROW_DMA_UNROLL)


def _dispatch_pallas(hp, dest, n_slots, *, tc=256):
    n = hp.shape[0]
    xs0 = jnp.zeros((n_slots, HALF_D), jnp.uint32)
    return pl.pallas_call(
        _dispatch_kernel,
        name="moe_dispatch",
        grid_spec=pltpu.PrefetchScalarGridSpec(
            num_scalar_prefetch=1,
            grid=(n // tc,),
            in_specs=[pl.BlockSpec((tc, HALF_D), lambda i, d: (i, 0)),
                      pl.BlockSpec(memory_space=pl.ANY)],
            out_specs=pl.BlockSpec(memory_space=pl.ANY),
            scratch_shapes=[pltpu.SemaphoreType.DMA(())]),
        out_shape=jax.ShapeDtypeStruct((n_slots, HALF_D), jnp.uint32),
        input_output_aliases={2: 0},
        compiler_params=pltpu.CompilerParams(
            dimension_semantics=("arbitrary",),
            vmem_limit_bytes=VMEM_LIMIT_BYTES),
    )(dest, hp, xs0)


def _expert_kernel(be_ref, first_ref, slot_ref, nxt_ref, nu_ref, x_ref, w1_hbm, w3_hbm, w2_hbm, y_ref,
                   f1_ref, f3_ref, f2_ref, b1_ref, b3_ref, b2_ref, sem, *, layer):
    i = pl.program_id(0)

    def weight_copies(e, slot):
        return [pltpu.make_async_copy(src.at[layer, e], dst.at[slot], sem.at[slot, n])
                for n, (src, dst) in enumerate(((w1_hbm, f1_ref), (w3_hbm, f3_ref), (w2_hbm, f2_ref)))]

    @pl.when(i < nu_ref[0])
    def _():
        slot = slot_ref[i]

        @pl.when(first_ref[i] == 1)
        def _():
            @pl.when(i == 0)
            def _():
                for c in weight_copies(be_ref[i], slot):
                    c.start()

            for c in weight_copies(be_ref[i], slot):
                c.wait()

            @pl.when(nxt_ref[i] >= 0)
            def _():
                for c in weight_copies(nxt_ref[i], 1 - slot):
                    c.start()

            b1_ref[...] = f1_ref[slot].astype(BF16)
            b3_ref[...] = f3_ref[slot].astype(BF16)
            b2_ref[...] = f2_ref[slot].astype(BF16)

        xp = x_ref[...]
        x_lo = lax.bitcast_convert_type(xp << 16, F32).astype(BF16)
        x_hi = lax.bitcast_convert_type(xp & jnp.uint32(0xFFFF0000), F32).astype(BF16)

        def up(w_ref):
            return (jnp.dot(x_lo, w_ref[:HALF_D, :], preferred_element_type=F32)
                    + jnp.dot(x_hi, w_ref[HALF_D:, :], preferred_element_type=F32))

        a = _silu(up(b1_ref)) * up(b3_ref)
        y_ref[...] = jnp.dot(a.astype(BF16), b2_ref[...], preferred_element_type=F32).astype(y_ref.dtype)

    @pl.when(i >= nu_ref[0])
    def _():
        y_ref[...] = jnp.zeros_like(y_ref)


def _expert_pallas(xs, nblk, w1, w3, w2, layer):
    n_slots = xs.shape[0]
    nb = n_slots // MOE_TB
    bend = jnp.cumsum(nblk)
    blocks = jnp.arange(nb, dtype=jnp.int32)
    block_e = jnp.minimum(jnp.searchsorted(bend, blocks, side='right'), MOE_EXPERTS - 1).astype(jnp.int32)
    first = (blocks == (bend - nblk)[block_e]).astype(jnp.int32)
    nonempty = nblk > 0
    slot_e = (jnp.cumsum(nonempty) - 1) % 2
    ids = jnp.where(nonempty, jnp.arange(MOE_EXPERTS), MOE_EXPERTS)
    after = jnp.concatenate([lax.cummin(ids, reverse=True)[1:], jnp.array([MOE_EXPERTS])])
    next_e = jnp.where(after < MOE_EXPERTS, after, -1)
    n_used = bend[-1:].astype(jnp.int32)
    blk = lambda i, be, fi, sl, nx, nu: jnp.minimum(i, nu[0] - 1)
    hbm = pl.BlockSpec(memory_space=pl.ANY)
    return pl.pallas_call(
        functools.partial(_expert_kernel, layer=layer),
        name="moe_experts",
        grid_spec=pltpu.PrefetchScalarGridSpec(
            num_scalar_prefetch=5,
            grid=(nb,),
            in_specs=[pl.BlockSpec((MOE_TB, HALF_D), lambda i, *s: (blk(i, *s), 0)), hbm, hbm, hbm],
            out_specs=pl.BlockSpec((MOE_TB, D_MODEL), lambda i, *s: (i, 0)),
            scratch_shapes=[pltpu.VMEM((2, D_MODEL, MOE_FF), F32), pltpu.VMEM((2, D_MODEL, MOE_FF), F32),
                            pltpu.VMEM((2, MOE_FF, D_MODEL), F32),
                            pltpu.VMEM((D_MODEL, MOE_FF), BF16), pltpu.VMEM((D_MODEL, MOE_FF), BF16),
                            pltpu.VMEM((MOE_FF, D_MODEL), BF16),
                            pltpu.SemaphoreType.DMA((2, 3))]),
        out_shape=jax.ShapeDtypeStruct((n_slots, D_MODEL), F32),
        compiler_params=pltpu.CompilerParams(
            dimension_semantics=("arbitrary",),
            vmem_limit_bytes=VMEM_LIMIT_BYTES),
    )(block_e, first, slot_e[block_e].astype(jnp.int32), next_e[block_e].astype(jnp.int32), n_used,
      xs, w1, w3, w2)


def _combine_kernel(dest_ref, h_ref, meta_ref, g_ref, yb_ref, o_ref, buf_ref, sem, *, final_norm):
    tc = h_ref.shape[0]
    base = pl.program_id(0) * tc * MOE_TOPK

    def row_copy(r, k):
        return pltpu.make_async_copy(yb_ref.at[pl.ds(dest_ref[base + MOE_TOPK * r + k], 1)],
                                     buf_ref.at[k, pl.ds(r, 1)], sem)

    def start(r, c):
        for k in range(MOE_TOPK):
            row_copy(r, k).start()
        return c

    def wait(r, c):
        for k in range(MOE_TOPK):
            row_copy(r, k).wait()
        return c

    lax.fori_loop(0, tc, start, 0, unroll=ROW_DMA_UNROLL)
    lax.fori_loop(0, tc, wait, 0, unroll=ROW_DMA_UNROLL)
    meta = meta_ref[...]
    out = h_ref[...]
    y = jnp.zeros_like(out)
    for k in range(MOE_TOPK):
        y = y + buf_ref[k] * meta[:, META_GATE + k:META_GATE + k + 1]
    out = out + y
    if final_norm:
        out = out * lax.rsqrt(jnp.mean(out * out, axis=-1, keepdims=True) + EPS) * g_ref[...]
    o_ref[...] = out


def _combine_pallas(h, meta, dest, yb, final_gain, *, final_norm, tc=256):
    n = h.shape[0]
    return pl.pallas_call(
        functools.partial(_combine_kernel, final_norm=final_norm),
        name="moe_combine",
        grid_spec=pltpu.PrefetchScalarGridSpec(
            num_scalar_prefetch=1,
            grid=(n // tc,),
            in_specs=[pl.BlockSpec((tc, D_MODEL), lambda i, d: (i, 0)),
                      pl.BlockSpec((tc, ROUTER_LANES), lambda i, d: (i, 0)),
                      pl.BlockSpec((1, D_MODEL), lambda i, d: (0, 0)),
                      pl.BlockSpec(memory_space=pl.ANY)],
            out_specs=pl.BlockSpec((tc, D_MODEL), lambda i, d: (i, 0)),
            scratch_shapes=[pltpu.VMEM((MOE_TOPK, tc, D_MODEL), F32), pltpu.SemaphoreType.DMA(())]),
        out_shape=jax.ShapeDtypeStruct((n, D_MODEL), F32),
        compiler_params=pltpu.CompilerParams(
            dimension_semantics=("arbitrary",),
            vmem_limit_bytes=VMEM_LIMIT_BYTES),
    )(dest, h, meta, final_gain.reshape(1, D_MODEL).astype(F32), yb)


def _moe_pallas(h, gain, w_rg, w_re, w1, w3, w2, layer, final_gain, *, final_norm):
    n = h.shape[0]
    nb = (n * MOE_TOPK) // MOE_TB + MOE_EXPERTS
    hp, meta, cnt = _router_pallas(h, gain, w_rg, w_re)
    expert = meta[:, META_E:META_E + MOE_TOPK].astype(jnp.int32)
    rank = meta[:, META_RANK:META_RANK + MOE_TOPK].astype(jnp.int32)
    counts = cnt[0, MOE_LANE0:MOE_LANE0 + MOE_EXPERTS].astype(jnp.int32)
    nblk = (counts + MOE_TB - 1) // MOE_TB
    dest = ((jnp.cumsum(nblk) - nblk)[expert] * MOE_TB + rank).reshape(n * MOE_TOPK)
    xs = _dispatch_pallas(hp, dest, nb * MOE_TB)
    yb = _expert_pallas(xs, nblk, w1, w3, w2, layer)
    return _combine_pallas(h, meta, dest, yb, final_gain, final_norm=final_norm)


COL_MLSTM, COL_HGRN, COL_FNET, COL_NA = 0, 4 * GROUP_W, 9 * GROUP_W, 10 * GROUP_W
N_GATES = 4 * ML_HEADS


def kernel(x, mem, norm_mix, norm_cross, norm_ffn, norm_final, norm_mem, w_in, mlstm_conv,
           mlstm_gate_bias, hgrn_lower_bound, na_rpb, group_gain, w_out, xa_wq, xa_wk, xa_wv, xa_wo,
           moe_router_group, moe_router_expert, moe_w1, moe_w3, moe_w2):
    b, s, d = x.shape
    mem_len = mem.shape[1]
    h = x.reshape(b * s, d)
    mem_f = mem.reshape(b * mem_len, d)
    lbs = jnp.cumsum(jax.nn.softmax(hgrn_lower_bound.astype(F32), axis=0), axis=0)
    lbs = lbs - lbs[0]
    fnet_tables = _fnet_tables(s)
    g0 = 4 * GROUP_W
    w_main = jnp.concatenate([w_in[:, :, :g0], w_in[:, :, g0 + N_GATES:]], axis=2).astype(BF16)
    w_gate = jnp.pad(w_in[:, :, g0:g0 + N_GATES], ((0, 0), (0, 0), (0, ML_GATE_LANES - N_GATES)))
    w_out_b, wq_b, wk_b, wv_b, wo_b = (w.astype(BF16) for w in (w_out, xa_wq, xa_wk, xa_wv, xa_wo))
    for l in range(DEPTH):
        proj, gates = _matmul(h, w_main, l, name="in_proj", gain=norm_mix[l], w_side=w_gate, out_dtype=BF16)
        y_ml = _mlstm_pallas(proj, gates[:, :N_GATES], mlstm_conv[l], mlstm_gate_bias[l], batch=b, seq=s,
                             out_dtype=BF16)
        y_hg = _hgrn_pallas(proj, lbs[l], batch=b, seq=s, col0=COL_HGRN, out_dtype=BF16)
        y_fn = _fnet_pallas(proj, fnet_tables, batch=b, seq=s, col0=COL_FNET, out_dtype=BF16)
        y_na = _na_pallas(proj, na_rpb[l], batch=b, seq=s, col0=COL_NA, out_dtype=BF16)
        h = _matmul([y_ml, y_hg, y_fn, y_na], w_out_b, l, name="out_proj", gain=group_gain[l], residual=h)

        q = _matmul(h, wq_b, l, name="xa_q_proj", gain=norm_cross[l], out_dtype=BF16)
        k = _matmul(mem_f, wk_b, l, name="xa_k_proj", gain=norm_mem, out_dtype=BF16)
        v = _matmul(mem_f, wv_b, l, name="xa_v_proj", gain=norm_mem, out_dtype=BF16)
        o = _xattn_pallas(q, k, v, batch=b, seq=s, mem_len=mem_len)
        h = _matmul(o, wo_b, l, name="xa_o_proj", residual=h)

        h = _moe_pallas(h, norm_ffn[l], moe_router_group[l], moe_router_expert[l],
                        moe_w1, moe_w3, moe_w2, l, norm_final, final_norm=(l == DEPTH - 1))
    return h.reshape(b, s, d)
```

```python
import functools

import jax
import jax.numpy as jnp
import numpy as np
from jax import lax
from jax.experimental import pallas as pl
from jax.experimental.pallas import tpu as pltpu

D_MODEL = 2048
DEPTH = 4
GRID_W = 64
N_MIXERS = 4
GROUP_W = D_MODEL // N_MIXERS
ML_HEADS = 4
ML_DH = GROUP_W // ML_HEADS
ML_CHUNK = 64
HG_HEADS = 4
HG_CHUNK = 16
FN_GROUPS = 4
FN_CH = GROUP_W // FN_GROUPS
NA_HEADS = 8
NA_DH = GROUP_W // NA_HEADS
NA_KH = 8
NA_KW = 16
XA_HEADS = 4
XA_DH = D_MODEL // XA_HEADS
MOE_GROUPS = 4
MOE_PER_GROUP = 8
MOE_EXPERTS = MOE_GROUPS * MOE_PER_GROUP
MOE_TOPK = 2
MOE_BLOCK = 128
EPS = 1e-6
F32 = jnp.float32
BF16 = jnp.bfloat16

VMEM_LIMIT_BYTES = 56 * 1024 * 1024


def _mm_kernel(*refs, n_x, norm, residual, side):
    refs = list(refs)
    x_refs = [refs.pop(0) for _ in range(n_x)]
    g_ref = refs.pop(0) if norm else None
    w_ref = refs.pop(0)
    ws_ref = refs.pop(0) if side else None
    r_ref = refs.pop(0) if residual else None
    o_ref = refs.pop(0)
    os_ref = refs.pop(0) if side else None
    xn_ref = refs.pop(0)

    @pl.when(pl.program_id(1) == 0)
    def _():
        col = 0
        for x_ref in x_refs:
            x = x_ref[...].astype(F32)
            kx = x.shape[1]
            if norm:
                ms = jnp.mean(x * x, axis=-1, keepdims=True)
                x = x * lax.rsqrt(ms + EPS) * g_ref[:, col:col + kx]
            xn_ref[:, col:col + kx] = x.astype(BF16)
            col += kx
        if side:
            os_ref[...] = jnp.dot(xn_ref[...], ws_ref[...].astype(BF16),
                                  preferred_element_type=F32).astype(os_ref.dtype)

    acc = jnp.dot(xn_ref[...], w_ref[...].astype(BF16), preferred_element_type=F32)
    if residual:
        acc = acc + r_ref[...]
    o_ref[...] = acc.astype(o_ref.dtype)


def _matmul(xs, w, layer, *, name, gain=None, residual=None, w_side=None, tm=1024, tn=512, out_dtype=F32):
    if not isinstance(xs, (list, tuple)):
        xs = [xs]
    m = xs[0].shape[0]
    k = sum(x.shape[1] for x in xs)
    n = w.shape[2]
    tm = min(tm, m)
    tn = min(tn, n)
    assert m % tm == 0 and n % tn == 0 and w.shape[1] == k, (m, n, k, tm, tn)
    norm = gain is not None
    has_res = residual is not None
    side = w_side is not None
    in_specs = [pl.BlockSpec((tm, x.shape[1]), lambda i, j: (i, 0)) for x in xs]
    args = list(xs)
    if norm:
        in_specs.append(pl.BlockSpec((1, k), lambda i, j: (0, 0)))
        args.append(gain.reshape(1, k).astype(F32))
    in_specs.append(pl.BlockSpec((None, k, tn), lambda i, j: (layer, 0, j)))
    args.append(w)
    if side:
        in_specs.append(pl.BlockSpec((None, k, w_side.shape[2]), lambda i, j: (layer, 0, 0)))
        args.append(w_side)
    if has_res:
        in_specs.append(pl.BlockSpec((tm, tn), lambda i, j: (i, j)))
        args.append(residual)
    out_specs = pl.BlockSpec((tm, tn), lambda i, j: (i, j))
    out_shape = jax.ShapeDtypeStruct((m, n), out_dtype)
    if side:
        out_specs = [out_specs, pl.BlockSpec((tm, w_side.shape[2]), lambda i, j: (i, 0))]
        out_shape = [out_shape, jax.ShapeDtypeStruct((m, w_side.shape[2]), F32)]
    return pl.pallas_call(
        functools.partial(_mm_kernel, n_x=len(xs), norm=norm, residual=has_res, side=side),
        name=name,
        grid=(m // tm, n // tn),
        in_specs=in_specs,
        out_specs=out_specs,
        out_shape=out_shape,
        scratch_shapes=[pltpu.VMEM((tm, k), BF16)],
        compiler_params=pltpu.CompilerParams(
            dimension_semantics=("parallel", "arbitrary"),
            vmem_limit_bytes=VMEM_LIMIT_BYTES),
    )(*args)


NA_QROWS = 4
NA_WROWS = 12


def _na_groups(rows):
    tables, plan = [], []
    for r0 in range(0, rows, NA_QROWS):
        band0 = lambda r: min(max(r - NA_KH // 2, 0), rows - NA_KH)
        kr0 = min(band0(r0), rows - NA_WROWS)
        assert band0(r0 + NA_QROWS - 1) + NA_KH <= kr0 + NA_WROWS
        dr = np.full((NA_QROWS, NA_WROWS), -1, np.int64)
        for i in range(NA_QROWS):
            for j in range(NA_WROWS):
                if 0 <= kr0 + j - band0(r0 + i) < NA_KH:
                    dr[i, j] = kr0 + j - (r0 + i) + NA_KH - 1
        key = dr.tobytes()
        if key not in [t.tobytes() for t in tables]:
            tables.append(dr)
        plan.append((r0, kr0, [t.tobytes() for t in tables].index(key)))
    return plan, np.stack(tables)


def _na_bias_tables(rpb, dr_tables):
    c = np.arange(GRID_W)
    dc = np.clip(c[None, :] - c[:, None] + NA_KW - 1, 0, 2 * NA_KW - 2)
    onehot = (dc[None] == np.arange(2 * NA_KW - 1)[:, None, None]).astype(np.float32)
    col_start = np.clip(c - NA_KW // 2, 0, GRID_W - NA_KW)
    col_ok = (c[None, :] >= col_start[:, None]) & (c[None, :] < col_start[:, None] + NA_KW)
    t = jnp.einsum('hrd,dqk->hrqk', rpb.astype(F32), onehot, precision=lax.Precision.HIGHEST)
    t = jnp.where(col_ok[None, None], t, -jnp.inf)
    t = jnp.concatenate([t, jnp.full_like(t[:, :1], -jnp.inf)], axis=1)
    idx = np.where(dr_tables < 0, t.shape[1] - 1, dr_tables)
    win = t[:, idx]
    n_t = dr_tables.shape[0]
    return win.transpose(0, 1, 2, 4, 3, 5).reshape(NA_HEADS, n_t, NA_QROWS * GRID_W, NA_WROWS * GRID_W)


def _na_kernel(q_ref, k_ref, v_ref, bias_ref, o_ref, *, plan):
    nq = NA_QROWS * GRID_W
    nk = NA_WROWS * GRID_W
    for r0, kr0, typ in plan:
        q = q_ref[r0 * GRID_W:r0 * GRID_W + nq, :].astype(F32) * (NA_DH ** -0.5)
        kb = k_ref[kr0 * GRID_W:kr0 * GRID_W + nk, :]
        vb = v_ref[kr0 * GRID_W:kr0 * GRID_W + nk, :]
        outs = []
        for hh in range(2):
            sl = slice(hh * NA_DH, (hh + 1) * NA_DH)
            s = lax.dot_general(q[:, sl].astype(BF16), kb[:, sl].astype(BF16),
                                (((1,), (1,)), ((), ())), preferred_element_type=F32)
            s = s + bias_ref[hh, typ]
            m = jnp.max(s, axis=-1, keepdims=True)
            p = jnp.exp(s - m)
            l = jnp.sum(p, axis=-1, keepdims=True)
            o = jnp.dot(p.astype(BF16), vb[:, sl].astype(BF16), preferred_element_type=F32)
            outs.append(o / l)
        o_ref[r0 * GRID_W:r0 * GRID_W + nq, :] = jnp.concatenate(outs, axis=-1).astype(o_ref.dtype)


def _na_pallas(proj, rpb, *, batch, seq, col0, out_dtype=F32):
    rows = seq // GRID_W
    pair_w = 2 * NA_DH
    cb = col0 // pair_w
    gb = GROUP_W // pair_w
    plan, dr_tables = _na_groups(rows)
    bias = _na_bias_tables(rpb, dr_tables)
    spec = lambda g: pl.BlockSpec((seq, pair_w), lambda b, p, g=g: (b, cb + g * gb + p))
    return pl.pallas_call(
        functools.partial(_na_kernel, plan=plan),
        name="na_attention",
        grid=(batch, NA_HEADS // 2),
        in_specs=[spec(0), spec(1), spec(2),
                  pl.BlockSpec((2,) + bias.shape[1:], lambda b, p: (p, 0, 0, 0))],
        out_specs=pl.BlockSpec((seq, pair_w), lambda b, p: (b, p)),
        out_shape=jax.ShapeDtypeStruct((batch * seq, GROUP_W), out_dtype),
        compiler_params=pltpu.CompilerParams(
            dimension_semantics=("parallel", "parallel"),
            vmem_limit_bytes=VMEM_LIMIT_BYTES),
    )(proj, proj, proj, bias)


ML_L = 128
ML_GATES = 8
ML_GATE_LANES = 128


def _log_sigmoid(x):
    return jnp.minimum(x, 0.0) - jnp.log(1.0 + jnp.exp(-jnp.abs(x)))


def _silu(x):
    return x * jax.nn.sigmoid(x)


_NT = (((1,), (1,)), ((), ()))
_TN = (((0,), (0,)), ((), ()))


def _split3(x):
    hi = x.astype(BF16)
    r1 = x - hi.astype(F32)
    mid = r1.astype(BF16)
    lo = (r1 - mid.astype(F32)).astype(BF16)
    return hi, mid, lo


def _dot_exact_lhs(a, x, dims=None):
    a = a.astype(BF16)
    if dims is None:
        return sum(jnp.dot(a, p, preferred_element_type=F32) for p in _split3(x))
    return sum(lax.dot_general(a, p, dims, preferred_element_type=F32) for p in _split3(x))


def _dot_exact_rhs(x, a, dims):
    a = a.astype(BF16)
    return sum(lax.dot_general(p, a, dims, preferred_element_type=F32) for p in _split3(x))


ML_GROUP = 4
ML_PAD = 16


def _mlstm_local(q, k, v, gcol, grow, tri, mask_w, gi, end_row):
    L = ML_L
    cum_col = _dot_exact_lhs(tri, _log_sigmoid(gcol))
    cum_row = _dot_exact_rhs(_log_sigmoid(grow), tri, _NT)
    wide = lambda col: jnp.broadcast_to(col, (L, L))
    b_cols, ig_cols = [], []
    for g in range(len(q)):
        b_cols.append(cum_col[:, g * ML_GATE_LANES + gi + 1:g * ML_GATE_LANES + gi + 2])
        ig_cols.append(gcol[:, g * ML_GATE_LANES + gi:g * ML_GATE_LANES + gi + 1])
    b_w = jnp.concatenate([wide(b) for b in b_cols], axis=1)
    row_w = jnp.concatenate([grow[g * ML_GATES + gi:g * ML_GATES + gi + 1, :]
                             - cum_row[g * ML_GATES + gi + 1:g * ML_GATES + gi + 2, :]
                             for g in range(len(q))], axis=1)
    dmat = jnp.where(mask_w != 0.0, b_w + row_w, -jnp.inf)
    m_in_w = jnp.concatenate([wide(jnp.max(dmat[:, g * L:(g + 1) * L], axis=-1, keepdims=True))
                              for g in range(len(q))], axis=1)
    w = jnp.exp(dmat - m_in_w)
    ones = jnp.ones((L, ML_DH), BF16)
    out = []
    for g in range(len(q)):
        lanes = slice(g * L, (g + 1) * L)
        kb = k[g].astype(BF16)
        v_aug = jnp.concatenate([v[g].astype(BF16), ones], axis=1)
        qk = lax.dot_general(q[g].astype(BF16), kb, _NT, preferred_element_type=F32) * w[:, lanes]
        res = jnp.dot(qk.astype(BF16), v_aug, preferred_element_type=F32)
        b_end = b_cols[g][end_row:end_row + 1, :]
        a_col = b_end - b_cols[g] + ig_cols[g]
        m_loc = jnp.max(a_col, axis=0, keepdims=True)
        kw = k[g] * jnp.exp(a_col - m_loc)
        st = lax.dot_general(v_aug, kw.astype(BF16), _TN, preferred_element_type=F32)
        out.append(dict(m_in=m_in_w[:, lanes], num=res[:, :ML_DH], den=res[:, ML_DH:], b=b_w[:, lanes],
                        ct=st[:ML_DH], n=st[ML_DH:ML_DH + 1], m_loc=m_loc, b_end=b_end))
    return out


def _mlstm_carry(q, m_in, num_in, den_in, b, ct, n, m_prev):
    inter = b + m_prev
    m_t = jnp.maximum(inter, m_in)
    s_in = jnp.exp(m_in - m_t)
    s_inter = jnp.exp(inter - m_t)
    rhs = jnp.concatenate([ct.astype(BF16), jnp.broadcast_to(n, (ML_PAD, ML_DH)).astype(BF16)], axis=0)
    both = lax.dot_general(q.astype(BF16), rhs, _NT, preferred_element_type=F32)
    num = s_in * num_in + s_inter * both[:, :ML_DH]
    den = s_in * den_in + s_inter * both[:, ML_DH:ML_DH + 1]
    return num / jnp.maximum(jnp.abs(den), jnp.exp(-m_t))


def _mlstm_kernel(q_ref, k_ref, v_ref, o_ref, cwq_ref, cwk_ref, gcol_ref, grow_ref, bcol_ref, brow_ref,
                  y_ref, qs_ref, ks_ref, num_ref, den_ref, min_ref, b_ref, ctl_ref, stat_ref, *, seq):
    L = ML_L
    nc = seq // L
    t_idx = lax.broadcasted_iota(jnp.int32, (seq, 1), 0)

    def conv_silu(x, w):
        prev = jnp.where(t_idx == 0, 0.0, pltpu.roll(x, 1, axis=0))
        nxt = jnp.where(t_idx == seq - 1, 0.0, pltpu.roll(x, seq - 1, axis=0))
        return _silu(prev * w[0:1, :] + x * w[1:2, :] + nxt * w[2:3, :])

    qs_ref[...] = conv_silu(q_ref[...].astype(F32), cwq_ref[...])
    ks_ref[...] = conv_silu(k_ref[...].astype(F32), cwk_ref[...]) * (ML_DH ** -0.5)

    ti = lax.broadcasted_iota(jnp.int32, (L, L), 0)
    si = lax.broadcasted_iota(jnp.int32, (L, L), 1)
    tris = (jnp.where(si <= ti, 1.0, 0.0), jnp.where(si >= ti, 1.0, 0.0))
    masks_w = [jnp.concatenate([t] * ML_GROUP, axis=1) for t in tris]

    for direction in (0, 1):
        for c0 in range(0, nc, ML_GROUP):
            rows = [pl.ds((c0 + g) * L, L) for g in range(ML_GROUP)]
            gcol = jnp.concatenate([gcol_ref[r, :] + bcol_ref[...] for r in rows], axis=1)
            grow = jnp.concatenate([grow_ref[:, r] + brow_ref[...] for r in rows], axis=0)
            local = _mlstm_local([qs_ref[r, :] for r in rows], [ks_ref[r, :] for r in rows],
                                 [v_ref[r, :].astype(F32) for r in rows], gcol, grow,
                                 tris[direction], masks_w[direction], 2 * direction,
                                 L - 1 if direction == 0 else 0)
            for g, (r, loc) in enumerate(zip(rows, local)):
                idx = direction * nc + c0 + g
                num_ref[direction, r, :] = loc["num"]
                den_ref[direction, r, :] = loc["den"]
                min_ref[direction, r, :] = loc["m_in"]
                b_ref[direction, r, :] = loc["b"]
                ctl_ref[idx] = loc["ct"]
                stat_ref[idx, 0:1, :] = loc["n"]
                stat_ref[idx, 1:2, :] = jnp.broadcast_to(loc["m_loc"], (1, ML_DH))
                stat_ref[idx, 2:3, :] = jnp.broadcast_to(loc["b_end"], (1, ML_DH))

    state = [(jnp.zeros((ML_DH, ML_DH), F32), jnp.zeros((1, ML_DH), F32), jnp.zeros((1, ML_DH), F32))] * 2
    for i in range(nc):
        for direction in (0, 1):
            c = i if direction == 0 else nc - 1 - i
            r = pl.ds(c * L, L)
            idx = direction * nc + c
            ct, n, m = state[direction]
            num_ref[direction, r, :] = _mlstm_carry(qs_ref[r, :], min_ref[direction, r, :],
                                                    num_ref[direction, r, :], den_ref[direction, r, :],
                                                    b_ref[direction, r, :], ct, n, m)
            n_loc, m_loc, b_end = stat_ref[idx, 0:1, :], stat_ref[idx, 1:2, :], stat_ref[idx, 2:3, :]
            m_new = jnp.maximum(b_end + m, m_loc)
            s_old = jnp.exp(b_end + m - m_new)
            s_new = jnp.exp(m_loc - m_new)
            state[direction] = (s_old * ct + s_new * ctl_ref[idx], s_old * n + s_new * n_loc, m_new)
    y_ref[...] = (jax.nn.sigmoid(o_ref[...].astype(F32)) * (num_ref[0] + num_ref[1])).astype(y_ref.dtype)


def _mlstm_pallas(proj, gates, conv_w, gate_b, *, batch, seq, out_dtype=F32):
    d = ML_DH
    hb = GROUP_W // d
    g4 = gates.reshape(batch, seq, 4, ML_HEADS).transpose(0, 3, 1, 2).astype(F32)
    g = jnp.pad(g4, ((0, 0), (0, 0), (0, 0), (0, ML_GATE_LANES - 4)))
    g_row = jnp.pad(g4, ((0, 0), (0, 0), (0, 0), (0, ML_GATES - 4))).transpose(0, 1, 3, 2)
    gb4 = gate_b.astype(F32).reshape(4, ML_HEADS).T
    gb_col = jnp.pad(gb4, ((0, 0), (0, ML_GATE_LANES - 4))).reshape(ML_HEADS, 1, ML_GATE_LANES)
    gb_row = jnp.pad(gb4, ((0, 0), (0, ML_GATES - 4))).reshape(ML_HEADS, ML_GATES, 1)
    spec = lambda grp: pl.BlockSpec((seq, d), lambda b, h, grp=grp: (b, grp * hb + h))
    f32 = lambda *shape: pltpu.VMEM(shape, F32)
    return pl.pallas_call(
        functools.partial(_mlstm_kernel, seq=seq),
        name="mlstm",
        grid=(batch, ML_HEADS),
        in_specs=[spec(0), spec(1), spec(2), spec(3),
                  pl.BlockSpec((3, d), lambda b, h: (0, h)),
                  pl.BlockSpec((3, d), lambda b, h: (0, hb + h)),
                  pl.BlockSpec((None, None, seq, ML_GATE_LANES), lambda b, h: (b, h, 0, 0)),
                  pl.BlockSpec((None, None, ML_GATES, seq), lambda b, h: (b, h, 0, 0)),
                  pl.BlockSpec((None, 1, ML_GATE_LANES), lambda b, h: (h, 0, 0)),
                  pl.BlockSpec((None, ML_GATES, 1), lambda b, h: (h, 0, 0))],
        out_specs=pl.BlockSpec((seq, d), lambda b, h: (b, h)),
        out_shape=jax.ShapeDtypeStruct((batch * seq, GROUP_W), out_dtype),
        scratch_shapes=[f32(seq, d), f32(seq, d), f32(2, seq, d), f32(2, seq, d), f32(2, seq, d),
                        f32(2, seq, d), f32(2 * (seq // ML_L), d, d), f32(2 * (seq // ML_L), 8, d)],
        compiler_params=pltpu.CompilerParams(
            dimension_semantics=("parallel", "parallel"),
            vmem_limit_bytes=VMEM_LIMIT_BYTES),
    )(proj, proj, proj, proj, conv_w.astype(F32), conv_w.astype(F32), g, g_row, gb_col, gb_row)


HG_L = 128
HG_LEAF = 8


HG_GROUP = 4
HG_NLEAF = HG_L // HG_LEAF
HG_LEVELS = 4


def _hgrn_select(backward):
    shape = ((1 + HG_LEVELS) * HG_NLEAF, HG_L)
    r = lax.broadcasted_iota(jnp.int32, shape, 0)
    c = lax.broadcasted_iota(jnp.int32, shape, 1)
    start = (r % HG_NLEAF) * HG_LEAF
    blk = r // HG_NLEAF
    sel = jnp.where(blk == 0, jnp.where(c == start + (HG_LEAF // 2 if backward else HG_LEAF // 2 - 1), 1.0, 0.0),
                    0.0)
    m = HG_L // 2
    for li in range(HG_LEVELS):
        target = start - start % (2 * m) + (m if backward else m - 1)
        sel = sel + jnp.where(blk == li + 1, jnp.where(c == target, 1.0, 0.0), 0.0)
        m //= 2
    return sel.astype(BF16)


def _hgrn_chunks(q, k, v, logf, tri, sel, masks):
    L = HG_L
    dk = HG_L
    groups = q.shape[1] // dk
    p = _dot_exact_lhs(tri, logf)
    refs = _dot_exact_lhs(sel, p)
    spread = lambda e: jnp.broadcast_to(e[:, None, :], (HG_NLEAF, HG_LEAF, e.shape[1])).reshape(L, e.shape[1])
    ref_leaf = refs[:HG_NLEAF]
    d = p - spread(ref_leaf)
    q_leaf = q * jnp.exp(d)
    k_leaf = k * jnp.exp(-d)
    pairs = [(q_leaf.astype(BF16), k_leaf.astype(BF16), masks[HG_LEVELS])]
    for li in range(HG_LEVELS):
        gap = ref_leaf - refs[(li + 1) * HG_NLEAF:(li + 2) * HG_NLEAF]
        qh = q_leaf * spread(jnp.exp(jnp.minimum(gap, 0.0)))
        kh = k_leaf * spread(jnp.exp(jnp.minimum(-gap, 0.0)))
        pairs.append((qh.astype(BF16), kh.astype(BF16), masks[li]))
    vb = v.astype(BF16)
    outs = []
    for g in range(groups):
        lanes = slice(g * dk, (g + 1) * dk)
        attn = jnp.zeros((L, L), F32)
        for qh, kh, mask in pairs:
            s = lax.dot_general(qh[:, lanes], kh[:, lanes], _NT, preferred_element_type=F32)
            attn = attn + jnp.where(mask != 0.0, s, 0.0)
        outs.append(jnp.dot(attn.astype(BF16), vb[:, lanes], preferred_element_type=F32))
    return p, jnp.concatenate(outs, axis=1)


def _hgrn_carry(q, k, v, p, st, backward):
    L = HG_L
    o = lax.dot_general((q * jnp.exp(p)).astype(BF16), st.astype(BF16), _NT, preferred_element_type=F32)
    p_end = p[0:1, :] if backward else p[L - 1:L, :]
    kd = k * jnp.exp(p_end - p)
    st = st * jnp.exp(p_end) + lax.dot_general(v.astype(BF16), kd.astype(BF16), _TN,
                                               preferred_element_type=F32)
    return o, st


def _hgrn_masks(backward):
    L = HG_L
    ti = lax.broadcasted_iota(jnp.int32, (L, L), 0)
    si = lax.broadcasted_iota(jnp.int32, (L, L), 1)
    if backward:
        ti, si = si, ti
    one = lambda cond: jnp.where(cond, 1.0, 0.0)
    masks = []
    m = L // 2
    while m >= HG_LEAF:
        same = one((ti // (2 * m)) == (si // (2 * m)))
        masks.append(same * one((ti % (2 * m)) >= m) * one((si % (2 * m)) < m))
        m //= 2
    masks.append(one((ti // HG_LEAF) == (si // HG_LEAF)) * one(si <= ti))
    return masks


def _hgrn_kernel(q_ref, ff_ref, fb_ref, i_ref, g_ref, lb_ref, y_ref,
                 qs_ref, lf_ref, kf_ref, lbk_ref, kb_ref, of_ref, ob_ref, *, seq):
    L = HG_L
    nc = seq // L
    lb = lb_ref[...]
    log_lb = jnp.log(lb)
    log1m_lb = jnp.log1p(-lb)

    def forget(fp):
        ls = _log_sigmoid(fp)
        a = log_lb
        c = log1m_lb + ls
        logf = jnp.maximum(a, c) + jnp.log(1.0 + jnp.exp(-jnp.abs(a - c)))
        return logf, (1.0 - lb) * jnp.exp(ls - fp)

    qs_ref[...] = _silu(q_ref[...].astype(F32))
    lf_ref[...], kf_ref[...] = forget(ff_ref[...].astype(F32))
    lbk_ref[...], kb_ref[...] = forget(fb_ref[...].astype(F32))

    ti = lax.broadcasted_iota(jnp.int32, (L, L), 0)
    si = lax.broadcasted_iota(jnp.int32, (L, L), 1)
    tril = (si <= ti).astype(F32)
    triu = (si >= ti).astype(F32)
    masks_f = _hgrn_masks(False)
    masks_b = _hgrn_masks(True)
    sel_f = _hgrn_select(False)
    sel_b = _hgrn_select(True)

    def side_by_side(ref, c0):
        return jnp.concatenate([ref[pl.ds((c0 + g) * L, L), :].astype(F32) for g in range(HG_GROUP)], axis=1)

    def scatter_back(ref, c0, wide):
        for g in range(HG_GROUP):
            ref[pl.ds((c0 + g) * L, L), :] = wide[:, g * L:(g + 1) * L]

    for c0 in range(0, nc, HG_GROUP):
        q = side_by_side(qs_ref, c0)
        v = side_by_side(i_ref, c0)
        for k_ref, f_ref, o_ref, tri, sel, masks in ((kf_ref, lf_ref, of_ref, tril, sel_f, masks_f),
                                                     (kb_ref, lbk_ref, ob_ref, triu, sel_b, masks_b)):
            p, o = _hgrn_chunks(q, side_by_side(k_ref, c0), v, side_by_side(f_ref, c0), tri, sel, masks)
            scatter_back(f_ref, c0, p)
            scatter_back(o_ref, c0, o)
    st_f = jnp.zeros((L, L), F32)
    st_b = jnp.zeros((L, L), F32)
    for i in range(nc):
        sl = pl.ds(i * L, L)
        o, st_f = _hgrn_carry(qs_ref[sl, :], kf_ref[sl, :], i_ref[sl, :].astype(F32), lf_ref[sl, :], st_f, False)
        of_ref[sl, :] += o
        sl = pl.ds((nc - 1 - i) * L, L)
        o, st_b = _hgrn_carry(qs_ref[sl, :], kb_ref[sl, :], i_ref[sl, :].astype(F32), lbk_ref[sl, :], st_b, True)
        ob_ref[sl, :] += o
    o = of_ref[...] + ob_ref[...]
    o = o * lax.rsqrt(jnp.mean(o * o, axis=-1, keepdims=True) + EPS)
    y_ref[...] = (o * _silu(g_ref[...].astype(F32))).astype(y_ref.dtype)


def _hgrn_pallas(proj, lb, *, batch, seq, col0, out_dtype=F32):
    d = GROUP_W // HG_HEADS
    hb = GROUP_W // d
    cb = col0 // d
    spec = lambda grp: pl.BlockSpec((seq, d), lambda b, h, grp=grp: (b, cb + grp * hb + h))
    f32 = lambda *shape: pltpu.VMEM(shape, F32)
    return pl.pallas_call(
        functools.partial(_hgrn_kernel, seq=seq),
        name="hgrn2",
        grid=(batch, HG_HEADS),
        in_specs=[spec(0), spec(1), spec(2), spec(3), spec(4),
                  pl.BlockSpec((1, d), lambda b, h: (0, h))],
        out_specs=pl.BlockSpec((seq, d), lambda b, h: (b, h)),
        out_shape=jax.ShapeDtypeStruct((batch * seq, GROUP_W), out_dtype),
        scratch_shapes=[f32(seq, d)] * 7,
        compiler_params=pltpu.CompilerParams(
            dimension_semantics=("parallel", "parallel"),
            vmem_limit_bytes=VMEM_LIMIT_BYTES),
    )(proj, proj, proj, proj, proj, lb.astype(F32).reshape(1, GROUP_W))


def _dft_cos_sin(n):
    k = (jnp.arange(n, dtype=jnp.int32)[:, None] * jnp.arange(n, dtype=jnp.int32)[None, :]) % n
    ang = k.astype(F32) * (2.0 * np.pi / n)
    return jnp.cos(ang), jnp.sin(ang)


def _fnet_tables(seq):
    cc, sc = _dft_cos_sin(FN_CH)
    eye = jnp.eye(FN_GROUPS, dtype=F32)
    chan = jnp.concatenate([jnp.kron(eye, cc), jnp.kron(eye, sc)], axis=1)
    cs, ss = _dft_cos_sin(seq)
    pos = jnp.concatenate([cs, -ss], axis=1) * ((seq * FN_CH) ** -0.5)
    return chan.astype(BF16), pos.astype(BF16)


def _fnet_chan_kernel(u_ref, dft_ref, v_ref):
    r = jnp.dot(u_ref[...].astype(BF16), dft_ref[...], preferred_element_type=F32)
    v_ref[0] = r[:, :GROUP_W].astype(v_ref.dtype)
    v_ref[1] = r[:, GROUP_W:].astype(v_ref.dtype)


def _fnet_pos_kernel(w_ref, v_ref, o_ref):
    o_ref[...] = jnp.dot(w_ref[...], v_ref[...], preferred_element_type=F32).astype(o_ref.dtype)


def _fnet_pallas(proj, tables, *, batch, seq, col0, out_dtype=F32, ts=512):
    chan, pos = tables
    cb = col0 // GROUP_W
    nt = seq // ts
    v = pl.pallas_call(
        _fnet_chan_kernel,
        name="fnet_channel_dft",
        grid=(batch, nt),
        in_specs=[pl.BlockSpec((ts, GROUP_W), lambda b, i: (b * nt + i, cb)),
                  pl.BlockSpec((GROUP_W, 2 * GROUP_W), lambda b, i: (0, 0))],
        out_specs=pl.BlockSpec((2, ts, GROUP_W), lambda b, i: (0, i, b)),
        out_shape=jax.ShapeDtypeStruct((2, seq, batch * GROUP_W), BF16),
        compiler_params=pltpu.CompilerParams(
            dimension_semantics=("parallel", "parallel"),
            vmem_limit_bytes=VMEM_LIMIT_BYTES),
    )(proj, chan)
    v = v.reshape(2 * seq, batch * GROUP_W)
    return pl.pallas_call(
        _fnet_pos_kernel,
        name="fnet_position_dft",
        grid=(nt, batch),
        in_specs=[pl.BlockSpec((ts, 2 * seq), lambda i, b: (i, 0)),
                  pl.BlockSpec((2 * seq, GROUP_W), lambda i, b: (0, b))],
        out_specs=pl.BlockSpec((ts, GROUP_W), lambda i, b: (b * nt + i, 0)),
        out_shape=jax.ShapeDtypeStruct((batch * seq, GROUP_W), out_dtype),
        compiler_params=pltpu.CompilerParams(
            dimension_semantics=("parallel", "parallel"),
            vmem_limit_bytes=VMEM_LIMIT_BYTES),
    )(pos, v)


def _xattn_kernel(q_ref, k_ref, v_ref, o_ref):
    for h in range(XA_HEADS):
        sl = slice(h * XA_DH, (h + 1) * XA_DH)
        s = lax.dot_general(q_ref[:, sl].astype(BF16), k_ref[:, sl].astype(BF16),
                            (((1,), (1,)), ((), ())), preferred_element_type=F32) * (XA_DH ** -0.5)
        p = jnp.exp(s - jnp.max(s, axis=-1, keepdims=True))
        l = jnp.sum(p, axis=-1, keepdims=True)
        o = jnp.dot(p.astype(BF16), v_ref[:, sl].astype(BF16), preferred_element_type=F32)
        o_ref[:, sl] = (o / l).astype(o_ref.dtype)


def _xattn_pallas(q, k, v, *, batch, seq, mem_len, ts=512, out_dtype=BF16):
    nt = seq // ts
    return pl.pallas_call(
        _xattn_kernel,
        name="cross_attention",
        grid=(batch, nt),
        in_specs=[pl.BlockSpec((ts, D_MODEL), lambda b, i: (b * nt + i, 0)),
                  pl.BlockSpec((mem_len, D_MODEL), lambda b, i: (b, 0)),
                  pl.BlockSpec((mem_len, D_MODEL), lambda b, i: (b, 0))],
        out_specs=pl.BlockSpec((ts, D_MODEL), lambda b, i: (b * nt + i, 0)),
        out_shape=jax.ShapeDtypeStruct((batch * seq, D_MODEL), out_dtype),
        compiler_params=pltpu.CompilerParams(
            dimension_semantics=("parallel", "parallel"),
            vmem_limit_bytes=VMEM_LIMIT_BYTES),
    )(q, k, v)


MOE_FF = D_MODEL // 4
MOE_TB = 256
MOE_LANE0 = MOE_GROUPS
ROUTER_LANES = 128
META_E, META_RANK, META_GATE = 0, 2, 4
HALF_D = D_MODEL // 2
ROW_DMA_UNROLL = 8


def _router_kernel(x_ref, g_ref, wr_ref, hp_ref, meta_ref, cnt_ref, carry_ref):
    tm = x_ref.shape[0]

    @pl.when(pl.program_id(0) == 0)
    def _():
        carry_ref[...] = jnp.zeros_like(carry_ref)

    x = x_ref[...].astype(F32)
    hn = x * lax.rsqrt(jnp.mean(x * x, axis=-1, keepdims=True) + EPS) * g_ref[...]
    hb = hn.astype(BF16)
    logits = jnp.dot(hb, wr_ref[...].astype(BF16), preferred_element_type=F32)
    bits = lax.bitcast_convert_type(hb.astype(F32), jnp.uint32)
    hp_ref[...] = (bits[:, :HALF_D] >> 16) | (bits[:, HALF_D:] & jnp.uint32(0xFFFF0000))

    lane = lax.broadcasted_iota(jnp.int32, (tm, ROUTER_LANES), 1).astype(F32)
    ninf = -jnp.inf
    first = lambda hit: jnp.min(jnp.where(hit, lane, float(ROUTER_LANES)), axis=-1, keepdims=True)
    gl = jnp.where(lane < MOE_GROUPS, logits, ninf)
    gmax = jnp.max(gl, axis=-1, keepdims=True)
    gidx = first(gl == gmax)
    g_gate = 1.0 / jnp.sum(jnp.exp(gl - gmax), axis=-1, keepdims=True)
    off = lane - (MOE_LANE0 + MOE_PER_GROUP * gidx)
    el = jnp.where(jnp.abs(2.0 * off - (MOE_PER_GROUP - 1)) < MOE_PER_GROUP, logits, ninf)
    v1 = jnp.max(el, axis=-1, keepdims=True)
    l1 = first(el == v1)
    el2 = jnp.where(lane == l1, ninf, el)
    v2 = jnp.max(el2, axis=-1, keepdims=True)
    l2 = first(el2 == v2)
    t = jnp.exp(v2 - v1)
    gate1 = g_gate / (1.0 + t)
    gate2 = g_gate * t / (1.0 + t)

    oh = jnp.where(lane == l1, 1.0, 0.0) + jnp.where(lane == l2, 1.0, 0.0)
    ti = lax.broadcasted_iota(jnp.int32, (tm, tm), 0)
    si = lax.broadcasted_iota(jnp.int32, (tm, tm), 1)
    before = jnp.where(si < ti, 1.0, 0.0).astype(BF16)
    base = jnp.dot(before, oh.astype(BF16), preferred_element_type=F32) + carry_ref[0:1, :]
    rank1 = jnp.sum(jnp.where(lane == l1, base, 0.0), axis=-1, keepdims=True)
    rank2 = jnp.sum(jnp.where(lane == l2, base, 0.0), axis=-1, keepdims=True)
    carry_ref[...] = carry_ref[...] + jnp.sum(oh, axis=0, keepdims=True)
    cnt_ref[...] = carry_ref[...]

    meta = jnp.zeros((tm, ROUTER_LANES), F32)
    for ln, val in ((META_E, l1 - MOE_LANE0), (META_E + 1, l2 - MOE_LANE0), (META_RANK, rank1),
                    (META_RANK + 1, rank2), (META_GATE, gate1), (META_GATE + 1, gate2)):
        meta = jnp.where(lane == ln, val, meta)
    meta_ref[...] = meta


def _router_pallas(h, gain, w_rg, w_re, *, tm=512):
    n = h.shape[0]
    wr = jnp.concatenate([w_rg, w_re], axis=1)
    wr = jnp.pad(wr, ((0, 0), (0, ROUTER_LANES - wr.shape[1])))
    return pl.pallas_call(
        _router_kernel,
        name="moe_router",
        grid=(n // tm,),
        in_specs=[pl.BlockSpec((tm, D_MODEL), lambda i: (i, 0)),
                  pl.BlockSpec((1, D_MODEL), lambda i: (0, 0)),
                  pl.BlockSpec((D_MODEL, ROUTER_LANES), lambda i: (0, 0))],
        out_specs=[pl.BlockSpec((tm, HALF_D), lambda i: (i, 0)),
                   pl.BlockSpec((tm, ROUTER_LANES), lambda i: (i, 0)),
                   pl.BlockSpec((8, ROUTER_LANES), lambda i: (0, 0))],
        out_shape=[jax.ShapeDtypeStruct((n, HALF_D), jnp.uint32),
                   jax.ShapeDtypeStruct((n, ROUTER_LANES), F32),
                   jax.ShapeDtypeStruct((8, ROUTER_LANES), F32)],
        scratch_shapes=[pltpu.VMEM((8, ROUTER_LANES), F32)],
        compiler_params=pltpu.CompilerParams(
            dimension_semantics=("arbitrary",),
            vmem_limit_bytes=VMEM_LIMIT_BYTES),
    )(h, gain.reshape(1, D_MODEL).astype(F32), wr)


def _dispatch_kernel(dest_ref, hp_ref, xs_in_ref, xs_ref, sem):
    del xs_in_ref
    tc = hp_ref.shape[0]
    base = pl.program_id(0) * tc * MOE_TOPK

    def row_copy(r, k):
        return pltpu.make_async_copy(hp_ref.at[pl.ds(r, 1)],
                                     xs_ref.at[pl.ds(dest_ref[base + MOE_TOPK * r + k], 1)], sem)

    def start(r, c):
        for k in range(MOE_TOPK):
            row_copy(r, k).start()
        return c

    def wait(r, c):
        for k in range(MOE_TOPK):
            row_copy(r, k).wait()
        return c

    lax.fori_loop(0, tc, start, 0, unroll=ROW_DMA_UNROLL)
    lax.fori_loop(0, tc, wait, 0, unroll=ROW_DMA_UNROLL)


def _dispatch_pallas(hp, dest, n_slots, *, tc=256):
    n = hp.shape[0]
    xs0 = jnp.zeros((n_slots, HALF_D), jnp.uint32)
    return pl.pallas_call(
        _dispatch_kernel,
        name="moe_dispatch",
        grid_spec=pltpu.PrefetchScalarGridSpec(
            num_scalar_prefetch=1,
            grid=(n // tc,),
            in_specs=[pl.BlockSpec((tc, HALF_D), lambda i, d: (i, 0)),
                      pl.BlockSpec(memory_space=pl.ANY)],
            out_specs=pl.BlockSpec(memory_space=pl.ANY),
            scratch_shapes=[pltpu.SemaphoreType.DMA(())]),
        out_shape=jax.ShapeDtypeStruct((n_slots, HALF_D), jnp.uint32),
        input_output_aliases={2: 0},
        compiler_params=pltpu.CompilerParams(
            dimension_semantics=("arbitrary",),
            vmem_limit_bytes=VMEM_LIMIT_BYTES),
    )(dest, hp, xs0)


def _expert_kernel(be_ref, first_ref, slot_ref, nxt_ref, nu_ref, x_ref, w1_hbm, w3_hbm, w2_hbm, y_ref,
                   f1_ref, f3_ref, f2_ref, b1_ref, b3_ref, b2_ref, sem, *, layer):
    i = pl.program_id(0)

    def weight_copies(e, slot):
        return [pltpu.make_async_copy(src.at[layer, e], dst.at[slot], sem.at[slot, n])
                for n, (src, dst) in enumerate(((w1_hbm, f1_ref), (w3_hbm, f3_ref), (w2_hbm, f2_ref)))]

    @pl.when(i < nu_ref[0])
    def _():
        slot = slot_ref[i]

        @pl.when(first_ref[i] == 1)
        def _():
            @pl.when(i == 0)
            def _():
                for c in weight_copies(be_ref[i], slot):
                    c.start()

            for c in weight_copies(be_ref[i], slot):
                c.wait()

            @pl.when(nxt_ref[i] >= 0)
            def _():
                for c in weight_copies(nxt_ref[i], 1 - slot):
                    c.start()

            b1_ref[...] = f1_ref[slot].astype(BF16)
            b3_ref[...] = f3_ref[slot].astype(BF16)
            b2_ref[...] = f2_ref[slot].astype(BF16)

        xp = x_ref[...]
        x_lo = lax.bitcast_convert_type(xp << 16, F32).astype(BF16)
        x_hi = lax.bitcast_convert_type(xp & jnp.uint32(0xFFFF0000), F32).astype(BF16)

        def up(w_ref):
            return (jnp.dot(x_lo, w_ref[:HALF_D, :], preferred_element_type=F32)
                    + jnp.dot(x_hi, w_ref[HALF_D:, :], preferred_element_type=F32))

        a = _silu(up(b1_ref)) * up(b3_ref)
        y_ref[...] = jnp.dot(a.astype(BF16), b2_ref[...], preferred_element_type=F32).astype(y_ref.dtype)

    @pl.when(i >= nu_ref[0])
    def _():
        y_ref[...] = jnp.zeros_like(y_ref)


def _expert_pallas(xs, nblk, w1, w3, w2, layer):
    n_slots = xs.shape[0]
    nb = n_slots // MOE_TB
    bend = jnp.cumsum(nblk)
    blocks = jnp.arange(nb, dtype=jnp.int32)
    block_e = jnp.minimum(jnp.searchsorted(bend, blocks, side='right'), MOE_EXPERTS - 1).astype(jnp.int32)
    first = (blocks == (bend - nblk)[block_e]).astype(jnp.int32)
    nonempty = nblk > 0
    slot_e = (jnp.cumsum(nonempty) - 1) % 2
    ids = jnp.where(nonempty, jnp.arange(MOE_EXPERTS), MOE_EXPERTS)
    after = jnp.concatenate([lax.cummin(ids, reverse=True)[1:], jnp.array([MOE_EXPERTS])])
    next_e = jnp.where(after < MOE_EXPERTS, after, -1)
    n_used = bend[-1:].astype(jnp.int32)
    blk = lambda i, be, fi, sl, nx, nu: jnp.minimum(i, nu[0] - 1)
    hbm = pl.BlockSpec(memory_space=pl.ANY)
    return pl.pallas_call(
        functools.partial(_expert_kernel, layer=layer),
        name="moe_experts",
        grid_spec=pltpu.PrefetchScalarGridSpec(
            num_scalar_prefetch=5,
            grid=(nb,),
            in_specs=[pl.BlockSpec((MOE_TB, HALF_D), lambda i, *s: (blk(i, *s), 0)), hbm, hbm, hbm],
            out_specs=pl.BlockSpec((MOE_TB, D_MODEL), lambda i, *s: (i, 0)),
            scratch_shapes=[pltpu.VMEM((2, D_MODEL, MOE_FF), F32), pltpu.VMEM((2, D_MODEL, MOE_FF), F32),
                            pltpu.VMEM((2, MOE_FF, D_MODEL), F32),
                            pltpu.VMEM((D_MODEL, MOE_FF), BF16), pltpu.VMEM((D_MODEL, MOE_FF), BF16),
                            pltpu.VMEM((MOE_FF, D_MODEL), BF16),
                            pltpu.SemaphoreType.DMA((2, 3))]),
        out_shape=jax.ShapeDtypeStruct((n_slots, D_MODEL), F32),
        compiler_params=pltpu.CompilerParams(
            dimension_semantics=("arbitrary",),
            vmem_limit_bytes=VMEM_LIMIT_BYTES),
    )(block_e, first, slot_e[block_e].astype(jnp.int32), next_e[block_e].astype(jnp.int32), n_used,
      xs, w1, w3, w2)


def _combine_kernel(dest_ref, h_ref, meta_ref, g_ref, yb_ref, o_ref, buf_ref, sem, *, final_norm):
    tc = h_ref.shape[0]
    base = pl.program_id(0) * tc * MOE_TOPK

    def row_copy(r, k):
        return pltpu.make_async_copy(yb_ref.at[pl.ds(dest_ref[base + MOE_TOPK * r + k], 1)],
                                     buf_ref.at[k, pl.ds(r, 1)], sem)

    def start(r, c):
        for k in range(MOE_TOPK):
            row_copy(r, k).start()
        return c

    def wait(r, c):
        for k in range(MOE_TOPK):
            row_copy(r, k).wait()
        return c

    lax.fori_loop(0, tc, start, 0, unroll=ROW_DMA_UNROLL)
    lax.fori_loop(0, tc, wait, 0, unroll=ROW_DMA_UNROLL)
    meta = meta_ref[...]
    out = h_ref[...]
    y = jnp.zeros_like(out)
    for k in range(MOE_TOPK):
        y = y + buf_ref[k] * meta[:, META_GATE + k:META_GATE + k + 1]
    out = out + y
    if final_norm:
        out = out * lax.rsqrt(jnp.mean(out * out, axis=-1, keepdims=True) + EPS) * g_ref[...]
    o_ref[...] = out


def _combine_pallas(h, meta, dest, yb, final_gain, *, final_norm, tc=256):
    n = h.shape[0]
    return pl.pallas_call(
        functools.partial(_combine_kernel, final_norm=final_norm),
        name="moe_combine",
        grid_spec=pltpu.PrefetchScalarGridSpec(
            num_scalar_prefetch=1,
            grid=(n // tc,),
            in_specs=[pl.BlockSpec((tc, D_MODEL), lambda i, d: (i, 0)),
                      pl.BlockSpec((tc, ROUTER_LANES), lambda i, d: (i, 0)),
                      pl.BlockSpec((1, D_MODEL), lambda i, d: (0, 0)),
                      pl.BlockSpec(memory_space=pl.ANY)],
            out_specs=pl.BlockSpec((tc, D_MODEL), lambda i, d: (i, 0)),
            scratch_shapes=[pltpu.VMEM((MOE_TOPK, tc, D_MODEL), F32), pltpu.SemaphoreType.DMA(())]),
        out_shape=jax.ShapeDtypeStruct((n, D_MODEL), F32),
        compiler_params=pltpu.CompilerParams(
            dimension_semantics=("arbitrary",),
            vmem_limit_bytes=VMEM_LIMIT_BYTES),
    )(dest, h, meta, final_gain.reshape(1, D_MODEL).astype(F32), yb)


def _moe_pallas(h, gain, w_rg, w_re, w1, w3, w2, layer, final_gain, *, final_norm):
    n = h.shape[0]
    nb = (n * MOE_TOPK) // MOE_TB + MOE_EXPERTS
    hp, meta, cnt = _router_pallas(h, gain, w_rg, w_re)
    expert = meta[:, META_E:META_E + MOE_TOPK].astype(jnp.int32)
    rank = meta[:, META_RANK:META_RANK + MOE_TOPK].astype(jnp.int32)
    counts = cnt[0, MOE_LANE0:MOE_LANE0 + MOE_EXPERTS].astype(jnp.int32)
    nblk = (counts + MOE_TB - 1) // MOE_TB
    dest = ((jnp.cumsum(nblk) - nblk)[expert] * MOE_TB + rank).reshape(n * MOE_TOPK)
    xs = _dispatch_pallas(hp, dest, nb * MOE_TB)
    yb = _expert_pallas(xs, nblk, w1, w3, w2, layer)
    return _combine_pallas(h, meta, dest, yb, final_gain, final_norm=final_norm)


COL_MLSTM, COL_HGRN, COL_FNET, COL_NA = 0, 4 * GROUP_W, 9 * GROUP_W, 10 * GROUP_W
N_GATES = 4 * ML_HEADS


def kernel(x, mem, norm_mix, norm_cross, norm_ffn, norm_final, norm_mem, w_in, mlstm_conv,
           mlstm_gate_bias, hgrn_lower_bound, na_rpb, group_gain, w_out, xa_wq, xa_wk, xa_wv, xa_wo,
           moe_router_group, moe_router_expert, moe_w1, moe_w3, moe_w2):
    b, s, d = x.shape
    mem_len = mem.shape[1]
    h = x.reshape(b * s, d)
    mem_f = mem.reshape(b * mem_len, d)
    lbs = jnp.cumsum(jax.nn.softmax(hgrn_lower_bound.astype(F32), axis=0), axis=0)
    lbs = lbs - lbs[0]
    fnet_tables = _fnet_tables(s)
    g0 = 4 * GROUP_W
    w_main = jnp.concatenate([w_in[:, :, :g0], w_in[:, :, g0 + N_GATES:]], axis=2).astype(BF16)
    w_gate = jnp.pad(w_in[:, :, g0:g0 + N_GATES], ((0, 0), (0, 0), (0, ML_GATE_LANES - N_GATES)))
    w_out_b, wq_b, wk_b, wv_b, wo_b = (w.astype(BF16) for w in (w_out, xa_wq, xa_wk, xa_wv, xa_wo))
    for l in range(DEPTH):
        proj, gates = _matmul(h, w_main, l, name="in_proj", gain=norm_mix[l], w_side=w_gate, out_dtype=BF16)
        y_ml = _mlstm_pallas(proj, gates[:, :N_GATES], mlstm_conv[l], mlstm_gate_bias[l], batch=b, seq=s,
                             out_dtype=BF16)
        y_hg = _hgrn_pallas(proj, lbs[l], batch=b, seq=s, col0=COL_HGRN, out_dtype=BF16)
        y_fn = _fnet_pallas(proj, fnet_tables, batch=b, seq=s, col0=COL_FNET, out_dtype=BF16)
        y_na = _na_pallas(proj, na_rpb[l], batch=b, seq=s, col0=COL_NA, out_dtype=BF16)
        h = _matmul([y_ml, y_hg, y_fn, y_na], w_out_b, l, name="out_proj", gain=group_gain[l], residual=h)

        q = _matmul(h, wq_b, l, name="xa_q_proj", gain=norm_cross[l], out_dtype=BF16)
        k = _matmul(mem_f, wk_b, l, name="xa_k_proj", gain=norm_mem, out_dtype=BF16)
        v = _matmul(mem_f, wv_b, l, name="xa_v_proj", gain=norm_mem, out_dtype=BF16)
        o = _xattn_pallas(q, k, v, batch=b, seq=s, mem_len=mem_len)
        h = _matmul(o, wo_b, l, name="xa_o_proj", residual=h)

        h = _moe_pallas(h, norm_ffn[l], moe_router_group[l], moe_router_expert[l],
                        moe_w1, moe_w3, moe_w2, l, norm_final, final_norm=(l == DEPTH - 1))
    return h.reshape(b, s, d)
```

```python
import functools

import jax
import jax.numpy as jnp
import numpy as np
from jax import lax
from jax.experimental import pallas as pl
from jax.experimental.pallas import tpu as pltpu

D_MODEL = 2048
DEPTH = 4
GRID_W = 64
N_MIXERS = 4
GROUP_W = D_MODEL // N_MIXERS
ML_HEADS = 4
ML_DH = GROUP_W // ML_HEADS
ML_CHUNK = 64
HG_HEADS = 4
HG_CHUNK = 16
FN_GROUPS = 4
FN_CH = GROUP_W // FN_GROUPS
NA_HEADS = 8
NA_DH = GROUP_W // NA_HEADS
NA_KH = 8
NA_KW = 16
XA_HEADS = 4
XA_DH = D_MODEL // XA_HEADS
MOE_GROUPS = 4
MOE_PER_GROUP = 8
MOE_EXPERTS = MOE_GROUPS * MOE_PER_GROUP
MOE_TOPK = 2
MOE_BLOCK = 128
EPS = 1e-6
F32 = jnp.float32
BF16 = jnp.bfloat16

VMEM_LIMIT_BYTES = 56 * 1024 * 1024


def _mm_kernel(*refs, n_x, norm, residual, side):
    refs = list(refs)
    x_refs = [refs.pop(0) for _ in range(n_x)]
    g_ref = refs.pop(0) if norm else None
    w_ref = refs.pop(0)
    ws_ref = refs.pop(0) if side else None
    r_ref = refs.pop(0) if residual else None
    o_ref = refs.pop(0)
    os_ref = refs.pop(0) if side else None
    xn_ref = refs.pop(0)

    @pl.when(pl.program_id(1) == 0)
    def _():
        col = 0
        for x_ref in x_refs:
            x = x_ref[...].astype(F32)
            kx = x.shape[1]
            if norm:
                ms = jnp.mean(x * x, axis=-1, keepdims=True)
                x = x * lax.rsqrt(ms + EPS) * g_ref[:, col:col + kx]
            xn_ref[:, col:col + kx] = x.astype(BF16)
            col += kx
        if side:
            os_ref[...] = jnp.dot(xn_ref[...], ws_ref[...].astype(BF16),
                                  preferred_element_type=F32).astype(os_ref.dtype)

    acc = jnp.dot(xn_ref[...], w_ref[...].astype(BF16), preferred_element_type=F32)
    if residual:
        acc = acc + r_ref[...]
    o_ref[...] = acc.astype(o_ref.dtype)


def _matmul(xs, w, layer, *, name, gain=None, residual=None, w_side=None, tm=1024, tn=512, out_dtype=F32):
    if not isinstance(xs, (list, tuple)):
        xs = [xs]
    m = xs[0].shape[0]
    k = sum(x.shape[1] for x in xs)
    n = w.shape[2]
    tm = min(tm, m)
    tn = min(tn, n)
    assert m % tm == 0 and n % tn == 0 and w.shape[1] == k, (m, n, k, tm, tn)
    norm = gain is not None
    has_res = residual is not None
    side = w_side is not None
    in_specs = [pl.BlockSpec((tm, x.shape[1]), lambda i, j: (i, 0)) for x in xs]
    args = list(xs)
    if norm:
        in_specs.append(pl.BlockSpec((1, k), lambda i, j: (0, 0)))
        args.append(gain.reshape(1, k).astype(F32))
    in_specs.append(pl.BlockSpec((None, k, tn), lambda i, j: (layer, 0, j)))
    args.append(w)
    if side:
        in_specs.append(pl.BlockSpec((None, k, w_side.shape[2]), lambda i, j: (layer, 0, 0)))
        args.append(w_side)
    if has_res:
        in_specs.append(pl.BlockSpec((tm, tn), lambda i, j: (i, j)))
        args.append(residual)
    out_specs = pl.BlockSpec((tm, tn), lambda i, j: (i, j))
    out_shape = jax.ShapeDtypeStruct((m, n), out_dtype)
    if side:
        out_specs = [out_specs, pl.BlockSpec((tm, w_side.shape[2]), lambda i, j: (i, 0))]
        out_shape = [out_shape, jax.ShapeDtypeStruct((m, w_side.shape[2]), F32)]
    return pl.pallas_call(
        functools.partial(_mm_kernel, n_x=len(xs), norm=norm, residual=has_res, side=side),
        name=name,
        grid=(m // tm, n // tn),
        in_specs=in_specs,
        out_specs=out_specs,
        out_shape=out_shape,
        scratch_shapes=[pltpu.VMEM((tm, k), BF16)],
        compiler_params=pltpu.CompilerParams(
            dimension_semantics=("parallel", "arbitrary"),
            vmem_limit_bytes=VMEM_LIMIT_BYTES),
    )(*args)


NA_QROWS = 4
NA_WROWS = 12


def _na_groups(rows):
    tables, plan = [], []
    for r0 in range(0, rows, NA_QROWS):
        band0 = lambda r: min(max(r - NA_KH // 2, 0), rows - NA_KH)
        kr0 = min(band0(r0), rows - NA_WROWS)
        assert band0(r0 + NA_QROWS - 1) + NA_KH <= kr0 + NA_WROWS
        dr = np.full((NA_QROWS, NA_WROWS), -1, np.int64)
        for i in range(NA_QROWS):
            for j in range(NA_WROWS):
                if 0 <= kr0 + j - band0(r0 + i) < NA_KH:
                    dr[i, j] = kr0 + j - (r0 + i) + NA_KH - 1
        key = dr.tobytes()
        if key not in [t.tobytes() for t in tables]:
            tables.append(dr)
        plan.append((r0, kr0, [t.tobytes() for t in tables].index(key)))
    return plan, np.stack(tables)


def _na_bias_tables(rpb, dr_tables):
    c = np.arange(GRID_W)
    dc = np.clip(c[None, :] - c[:, None] + NA_KW - 1, 0, 2 * NA_KW - 2)
    onehot = (dc[None] == np.arange(2 * NA_KW - 1)[:, None, None]).astype(np.float32)
    col_start = np.clip(c - NA_KW // 2, 0, GRID_W - NA_KW)
    col_ok = (c[None, :] >= col_start[:, None]) & (c[None, :] < col_start[:, None] + NA_KW)
    t = jnp.einsum('hrd,dqk->hrqk', rpb.astype(F32), onehot, precision=lax.Precision.HIGHEST)
    t = jnp.where(col_ok[None, None], t, -jnp.inf)
    t = jnp.concatenate([t, jnp.full_like(t[:, :1], -jnp.inf)], axis=1)
    idx = np.where(dr_tables < 0, t.shape[1] - 1, dr_tables)
    win = t[:, idx]
    n_t = dr_tables.shape[0]
    return win.transpose(0, 1, 2, 4, 3, 5).reshape(NA_HEADS, n_t, NA_QROWS * GRID_W, NA_WROWS * GRID_W)


def _na_kernel(q_ref, k_ref, v_ref, bias_ref, o_ref, *, plan):
    nq = NA_QROWS * GRID_W
    nk = NA_WROWS * GRID_W
    for r0, kr0, typ in plan:
        q = q_ref[r0 * GRID_W:r0 * GRID_W + nq, :].astype(F32) * (NA_DH ** -0.5)
        kb = k_ref[kr0 * GRID_W:kr0 * GRID_W + nk, :]
        vb = v_ref[kr0 * GRID_W:kr0 * GRID_W + nk, :]
        outs = []
        for hh in range(2):
            sl = slice(hh * NA_DH, (hh + 1) * NA_DH)
            s = lax.dot_general(q[:, sl].astype(BF16), kb[:, sl].astype(BF16),
                                (((1,), (1,)), ((), ())), preferred_element_type=F32)
            s = s + bias_ref[hh, typ]
            m = jnp.max(s, axis=-1, keepdims=True)
            p = jnp.exp(s - m)
            l = jnp.sum(p, axis=-1, keepdims=True)
            o = jnp.dot(p.astype(BF16), vb[:, sl].astype(BF16), preferred_element_type=F32)
            outs.append(o / l)
        o_ref[r0 * GRID_W:r0 * GRID_W + nq, :] = jnp.concatenate(outs, axis=-1).astype(o_ref.dtype)


def _na_pallas(proj, rpb, *, batch, seq, col0, out_dtype=F32):
    rows = seq // GRID_W
    pair_w = 2 * NA_DH
    cb = col0 // pair_w
    gb = GROUP_W // pair_w
    plan, dr_tables = _na_groups(rows)
    bias = _na_bias_tables(rpb, dr_tables)
    spec = lambda g: pl.BlockSpec((seq, pair_w), lambda b, p, g=g: (b, cb + g * gb + p))
    return pl.pallas_call(
        functools.partial(_na_kernel, plan=plan),
        name="na_attention",
        grid=(batch, NA_HEADS // 2),
        in_specs=[spec(0), spec(1), spec(2),
                  pl.BlockSpec((2,) + bias.shape[1:], lambda b, p: (p, 0, 0, 0))],
        out_specs=pl.BlockSpec((seq, pair_w), lambda b, p: (b, p)),
        out_shape=jax.ShapeDtypeStruct((batch * seq, GROUP_W), out_dtype),
        compiler_params=pltpu.CompilerParams(
            dimension_semantics=("parallel", "parallel"),
            vmem_limit_bytes=VMEM_LIMIT_BYTES),
    )(proj, proj, proj, bias)


ML_L = 128
ML_GATE_LANES = 128


def _log_sigmoid(x):
    return jnp.minimum(x, 0.0) - jnp.log(1.0 + jnp.exp(-jnp.abs(x)))


def _silu(x):
    return x * jax.nn.sigmoid(x)


_NT = (((1,), (1,)), ((), ()))
_TN = (((0,), (0,)), ((), ()))


def _split3(x):
    hi = x.astype(BF16)
    r1 = x - hi.astype(F32)
    mid = r1.astype(BF16)
    lo = (r1 - mid.astype(F32)).astype(BF16)
    return hi, mid, lo


def _dot_exact_lhs(a, x, dims=None):
    a = a.astype(BF16)
    if dims is None:
        return sum(jnp.dot(a, p, preferred_element_type=F32) for p in _split3(x))
    return sum(lax.dot_general(a, p, dims, preferred_element_type=F32) for p in _split3(x))


def _dot_exact_rhs(x, a, dims):
    a = a.astype(BF16)
    return sum(lax.dot_general(p, a, dims, preferred_element_type=F32) for p in _split3(x))


ML_GROUP = 4
ML_PAD = 16


def _mlstm_local(q, k, v, gcol, tri, mask_w, gi, end_row):
    L = ML_L
    cum_col = _dot_exact_lhs(tri, _log_sigmoid(gcol))
    wide = lambda col: jnp.broadcast_to(col, (L, L))
    b_cols, ig_cols, rows = [], [], []
    for g in range(len(q)):
        lanes = slice(g * ML_GATE_LANES, (g + 1) * ML_GATE_LANES)
        b_cols.append(cum_col[:, g * ML_GATE_LANES + gi + 1:g * ML_GATE_LANES + gi + 2])
        ig_cols.append(gcol[:, g * ML_GATE_LANES + gi:g * ML_GATE_LANES + gi + 1])
        rows.append(gcol[:, lanes].T[gi:gi + 1, :] - cum_col[:, lanes].T[gi + 1:gi + 2, :])
    b_w = jnp.concatenate([wide(b) for b in b_cols], axis=1)
    row_w = jnp.concatenate(rows, axis=1)
    dmat = jnp.where(mask_w != 0.0, b_w + row_w, -jnp.inf)
    m_in_w = jnp.concatenate([wide(jnp.max(dmat[:, g * L:(g + 1) * L], axis=-1, keepdims=True))
                              for g in range(len(q))], axis=1)
    w = jnp.exp(dmat - m_in_w)
    ones = jnp.ones((L, ML_DH), BF16)
    out = []
    for g in range(len(q)):
        lanes = slice(g * L, (g + 1) * L)
        kb = k[g].astype(BF16)
        v_aug = jnp.concatenate([v[g].astype(BF16), ones], axis=1)
        qk = lax.dot_general(q[g].astype(BF16), kb, _NT, preferred_element_type=F32) * w[:, lanes]
        res = jnp.dot(qk.astype(BF16), v_aug, preferred_element_type=F32)
        b_end = b_cols[g][end_row:end_row + 1, :]
        a_col = b_end - b_cols[g] + ig_cols[g]
        m_loc = jnp.max(a_col, axis=0, keepdims=True)
        kw = k[g] * jnp.exp(a_col - m_loc)
        st = lax.dot_general(v_aug, kw.astype(BF16), _TN, preferred_element_type=F32)
        out.append(dict(m_in=m_in_w[:, lanes], num=res[:, :ML_DH], den=res[:, ML_DH:], b=b_w[:, lanes],
                        ct=st[:ML_DH], n=st[ML_DH:ML_DH + 1], m_loc=m_loc, b_end=b_end))
    return out


def _mlstm_carry(q, m_in, num_in, den_in, b, ct, n, m_prev):
    inter = b + m_prev
    m_t = jnp.maximum(inter, m_in)
    s_in = jnp.exp(m_in - m_t)
    s_inter = jnp.exp(inter - m_t)
    rhs = jnp.concatenate([ct.astype(BF16), jnp.broadcast_to(n, (ML_PAD, ML_DH)).astype(BF16)], axis=0)
    both = lax.dot_general(q.astype(BF16), rhs, _NT, preferred_element_type=F32)
    num = s_in * num_in + s_inter * both[:, :ML_DH]
    den = s_in * den_in + s_inter * both[:, ML_DH:ML_DH + 1]
    return num / jnp.maximum(jnp.abs(den), jnp.exp(-m_t))


def _mlstm_kernel(q_ref, k_ref, v_ref, o_ref, cwq_ref, cwk_ref, gcol_ref, bcol_ref,
                  y_ref, qs_ref, ks_ref, num_ref, den_ref, min_ref, b_ref, ctl_ref, stat_ref, *, seq):
    L = ML_L
    nc = seq // L
    t_idx = lax.broadcasted_iota(jnp.int32, (seq, 1), 0)

    def conv_silu(x, w):
        prev = jnp.where(t_idx == 0, 0.0, pltpu.roll(x, 1, axis=0))
        nxt = jnp.where(t_idx == seq - 1, 0.0, pltpu.roll(x, seq - 1, axis=0))
        return _silu(prev * w[0:1, :] + x * w[1:2, :] + nxt * w[2:3, :])

    qs_ref[...] = conv_silu(q_ref[...].astype(F32), cwq_ref[...])
    ks_ref[...] = conv_silu(k_ref[...].astype(F32), cwk_ref[...]) * (ML_DH ** -0.5)

    ti = lax.broadcasted_iota(jnp.int32, (L, L), 0)
    si = lax.broadcasted_iota(jnp.int32, (L, L), 1)
    tris = (jnp.where(si <= ti, 1.0, 0.0), jnp.where(si >= ti, 1.0, 0.0))
    masks_w = [jnp.concatenate([t] * ML_GROUP, axis=1) for t in tris]

    for direction in (0, 1):
        for c0 in range(0, nc, ML_GROUP):
            rows = [pl.ds((c0 + g) * L, L) for g in range(ML_GROUP)]
            gcol = jnp.concatenate([gcol_ref[r, :] + bcol_ref[...] for r in rows], axis=1)
            local = _mlstm_local([qs_ref[r, :] for r in rows], [ks_ref[r, :] for r in rows],
                                 [v_ref[r, :].astype(F32) for r in rows], gcol,
                                 tris[direction], masks_w[direction], 2 * direction,
                                 L - 1 if direction == 0 else 0)
            for g, (r, loc) in enumerate(zip(rows, local)):
                idx = direction * nc + c0 + g
                num_ref[direction, r, :] = loc["num"]
                den_ref[direction, r, :] = loc["den"]
                min_ref[direction, r, :] = loc["m_in"]
                b_ref[direction, r, :] = loc["b"]
                ctl_ref[idx] = loc["ct"]
                stat_ref[idx, 0:1, :] = loc["n"]
                stat_ref[idx, 1:2, :] = jnp.broadcast_to(loc["m_loc"], (1, ML_DH))
                stat_ref[idx, 2:3, :] = jnp.broadcast_to(loc["b_end"], (1, ML_DH))

    state = [(jnp.zeros((ML_DH, ML_DH), F32), jnp.zeros((1, ML_DH), F32), jnp.zeros((1, ML_DH), F32))] * 2
    for i in range(nc):
        for direction in (0, 1):
            c = i if direction == 0 else nc - 1 - i
            r = pl.ds(c * L, L)
            idx = direction * nc + c
            ct, n, m = state[direction]
            num_ref[direction, r, :] = _mlstm_carry(qs_ref[r, :], min_ref[direction, r, :],
                                                    num_ref[direction, r, :], den_ref[direction, r, :],
                                                    b_ref[direction, r, :], ct, n, m)
            n_loc, m_loc, b_end = stat_ref[idx, 0:1, :], stat_ref[idx, 1:2, :], stat_ref[idx, 2:3, :]
            m_new = jnp.maximum(b_end + m, m_loc)
            s_old = jnp.exp(b_end + m - m_new)
            s_new = jnp.exp(m_loc - m_new)
            state[direction] = (s_old * ct + s_new * ctl_ref[idx], s_old * n + s_new * n_loc, m_new)
    y_ref[...] = (jax.nn.sigmoid(o_ref[...].astype(F32)) * (num_ref[0] + num_ref[1])).astype(y_ref.dtype)


def _mlstm_pallas(proj, gates, conv_w, gate_b, *, batch, seq, out_dtype=F32):
    d = ML_DH
    hb = GROUP_W // d
    gb4 = gate_b.astype(F32).reshape(4, ML_HEADS).T
    gb_col = jnp.pad(gb4, ((0, 0), (0, ML_GATE_LANES - 4))).reshape(ML_HEADS, 1, ML_GATE_LANES)
    spec = lambda grp: pl.BlockSpec((seq, d), lambda b, h, grp=grp: (b, grp * hb + h))
    f32 = lambda *shape: pltpu.VMEM(shape, F32)
    return pl.pallas_call(
        functools.partial(_mlstm_kernel, seq=seq),
        name="mlstm",
        grid=(batch, ML_HEADS),
        in_specs=[spec(0), spec(1), spec(2), spec(3),
                  pl.BlockSpec((3, d), lambda b, h: (0, h)),
                  pl.BlockSpec((3, d), lambda b, h: (0, hb + h)),
                  pl.BlockSpec((seq, ML_GATE_LANES), lambda b, h: (b, h)),
                  pl.BlockSpec((None, 1, ML_GATE_LANES), lambda b, h: (h, 0, 0))],
        out_specs=pl.BlockSpec((seq, d), lambda b, h: (b, h)),
        out_shape=jax.ShapeDtypeStruct((batch * seq, GROUP_W), out_dtype),
        scratch_shapes=[f32(seq, d), f32(seq, d), f32(2, seq, d), f32(2, seq, d), f32(2, seq, d),
                        f32(2, seq, d), f32(2 * (seq // ML_L), d, d), f32(2 * (seq // ML_L), 8, d)],
        compiler_params=pltpu.CompilerParams(
            dimension_semantics=("parallel", "parallel"),
            vmem_limit_bytes=VMEM_LIMIT_BYTES),
    )(proj, proj, proj, proj, conv_w.astype(F32), conv_w.astype(F32), gates, gb_col)


HG_L = 128
HG_LEAF = 8


HG_GROUP = 4
HG_NLEAF = HG_L // HG_LEAF
HG_LEVELS = 4


def _hgrn_ref_rows(p_ref, chunks, backward):
    def rows(first, count, stride):
        return jnp.concatenate([p_ref[pl.ds(c * HG_L + first, count, stride=stride), :] for c in chunks], axis=1)

    leaf = rows(HG_LEAF // 2 if backward else HG_LEAF // 2 - 1, HG_NLEAF, HG_LEAF)
    levels = []
    m = HG_L // 2
    for _ in range(HG_LEVELS):
        blocks = HG_L // (2 * m)
        r = rows(m if backward else m - 1, blocks, 2 * m) if blocks > 1 else rows(m if backward else m - 1, 1, 1)
        levels.append(jnp.repeat(r, HG_NLEAF // blocks, axis=0))
        m //= 2
    return leaf, levels


def _hgrn_chunks(q, k, v, p, ref_leaf, ref_levels, masks):
    L = HG_L
    dk = HG_L
    groups = q.shape[1] // dk
    spread = lambda e: jnp.broadcast_to(e[:, None, :], (HG_NLEAF, HG_LEAF, e.shape[1])).reshape(L, e.shape[1])
    d = p - spread(ref_leaf)
    q_leaf = q * jnp.exp(d)
    k_leaf = k * jnp.exp(-d)
    pairs = [(q_leaf.astype(BF16), k_leaf.astype(BF16), masks[HG_LEVELS])]
    for li in range(HG_LEVELS):
        gap = ref_leaf - ref_levels[li]
        qh = q_leaf * spread(jnp.exp(jnp.minimum(gap, 0.0)))
        kh = k_leaf * spread(jnp.exp(jnp.minimum(-gap, 0.0)))
        pairs.append((qh.astype(BF16), kh.astype(BF16), masks[li]))
    vb = v.astype(BF16)
    outs = []
    for g in range(groups):
        lanes = slice(g * dk, (g + 1) * dk)
        attn = jnp.zeros((L, L), F32)
        for qh, kh, mask in pairs:
            s = lax.dot_general(qh[:, lanes], kh[:, lanes], _NT, preferred_element_type=F32)
            attn = attn + jnp.where(mask != 0.0, s, 0.0)
        outs.append(jnp.dot(attn.astype(BF16), vb[:, lanes], preferred_element_type=F32))
    return jnp.concatenate(outs, axis=1)


def _hgrn_carry(q, k, v, p, st, backward):
    L = HG_L
    o = lax.dot_general((q * jnp.exp(p)).astype(BF16), st.astype(BF16), _NT, preferred_element_type=F32)
    p_end = p[0:1, :] if backward else p[L - 1:L, :]
    kd = k * jnp.exp(p_end - p)
    st = st * jnp.exp(p_end) + lax.dot_general(v.astype(BF16), kd.astype(BF16), _TN,
                                               preferred_element_type=F32)
    return o, st


def _hgrn_masks(backward):
    L = HG_L
    ti = lax.broadcasted_iota(jnp.int32, (L, L), 0)
    si = lax.broadcasted_iota(jnp.int32, (L, L), 1)
    if backward:
        ti, si = si, ti
    one = lambda cond: jnp.where(cond, 1.0, 0.0)
    masks = []
    m = L // 2
    while m >= HG_LEAF:
        same = one((ti // (2 * m)) == (si // (2 * m)))
        masks.append(same * one((ti % (2 * m)) >= m) * one((si % (2 * m)) < m))
        m //= 2
    masks.append(one((ti // HG_LEAF) == (si // HG_LEAF)) * one(si <= ti))
    return masks


def _hgrn_kernel(q_ref, ff_ref, fb_ref, i_ref, g_ref, lb_ref, y_ref,
                 qs_ref, lf_ref, kf_ref, lbk_ref, kb_ref, of_ref, ob_ref, *, seq):
    L = HG_L
    nc = seq // L
    lb = lb_ref[...]
    log_lb = jnp.log(lb)
    log1m_lb = jnp.log1p(-lb)

    def forget(fp):
        ls = _log_sigmoid(fp)
        a = log_lb
        c = log1m_lb + ls
        logf = jnp.maximum(a, c) + jnp.log(1.0 + jnp.exp(-jnp.abs(a - c)))
        return logf, (1.0 - lb) * jnp.exp(ls - fp)

    qs_ref[...] = _silu(q_ref[...].astype(F32))
    lf_ref[...], kf_ref[...] = forget(ff_ref[...].astype(F32))
    lbk_ref[...], kb_ref[...] = forget(fb_ref[...].astype(F32))

    ti = lax.broadcasted_iota(jnp.int32, (L, L), 0)
    si = lax.broadcasted_iota(jnp.int32, (L, L), 1)
    tril = (si <= ti).astype(F32)
    triu = (si >= ti).astype(F32)
    masks_f = _hgrn_masks(False)
    masks_b = _hgrn_masks(True)

    def side_by_side(ref, c0):
        return jnp.concatenate([ref[pl.ds((c0 + g) * L, L), :].astype(F32) for g in range(HG_GROUP)], axis=1)

    def scatter_back(ref, c0, wide):
        for g in range(HG_GROUP):
            ref[pl.ds((c0 + g) * L, L), :] = wide[:, g * L:(g + 1) * L]

    directions = ((kf_ref, lf_ref, of_ref, tril, masks_f, False), (kb_ref, lbk_ref, ob_ref, triu, masks_b, True))
    for c0 in range(0, nc, HG_GROUP):
        for _, f_ref, _, tri, _, _ in directions:
            scatter_back(f_ref, c0, _dot_exact_lhs(tri, side_by_side(f_ref, c0)))
    for c0 in range(0, nc, HG_GROUP):
        q = side_by_side(qs_ref, c0)
        v = side_by_side(i_ref, c0)
        for k_ref, f_ref, o_ref, _, masks, backward in directions:
            ref_leaf, ref_levels = _hgrn_ref_rows(f_ref, range(c0, c0 + HG_GROUP), backward)
            scatter_back(o_ref, c0, _hgrn_chunks(q, side_by_side(k_ref, c0), v, side_by_side(f_ref, c0),
                                                 ref_leaf, ref_levels, masks))
    st_f = jnp.zeros((L, L), F32)
    st_b = jnp.zeros((L, L), F32)
    for i in range(nc):
        sl = pl.ds(i * L, L)
        o, st_f = _hgrn_carry(qs_ref[sl, :], kf_ref[sl, :], i_ref[sl, :].astype(F32), lf_ref[sl, :], st_f, False)
        of_ref[sl, :] += o
        sl = pl.ds((nc - 1 - i) * L, L)
        o, st_b = _hgrn_carry(qs_ref[sl, :], kb_ref[sl, :], i_ref[sl, :].astype(F32), lbk_ref[sl, :], st_b, True)
        ob_ref[sl, :] += o
    o = of_ref[...] + ob_ref[...]
    o = o * lax.rsqrt(jnp.mean(o * o, axis=-1, keepdims=True) + EPS)
    y_ref[...] = (o * _silu(g_ref[...].astype(F32))).astype(y_ref.dtype)


def _hgrn_pallas(proj, lb, *, batch, seq, col0, out_dtype=F32):
    d = GROUP_W // HG_HEADS
    hb = GROUP_W // d
    cb = col0 // d
    spec = lambda grp: pl.BlockSpec((seq, d), lambda b, h, grp=grp: (b, cb + grp * hb + h))
    f32 = lambda *shape: pltpu.VMEM(shape, F32)
    return pl.pallas_call(
        functools.partial(_hgrn_kernel, seq=seq),
        name="hgrn2",
        grid=(batch, HG_HEADS),
        in_specs=[spec(0), spec(1), spec(2), spec(3), spec(4),
                  pl.BlockSpec((1, d), lambda b, h: (0, h))],
        out_specs=pl.BlockSpec((seq, d), lambda b, h: (b, h)),
        out_shape=jax.ShapeDtypeStruct((batch * seq, GROUP_W), out_dtype),
        scratch_shapes=[f32(seq, d)] * 7,
        compiler_params=pltpu.CompilerParams(
            dimension_semantics=("parallel", "parallel"),
            vmem_limit_bytes=VMEM_LIMIT_BYTES),
    )(proj, proj, proj, proj, proj, lb.astype(F32).reshape(1, GROUP_W))


def _dft_cos_sin(n):
    k = (jnp.arange(n, dtype=jnp.int32)[:, None] * jnp.arange(n, dtype=jnp.int32)[None, :]) % n
    ang = k.astype(F32) * (2.0 * np.pi / n)
    return jnp.cos(ang), jnp.sin(ang)


def _fnet_tables(seq):
    cc, sc = _dft_cos_sin(FN_CH)
    eye = jnp.eye(FN_GROUPS, dtype=F32)
    chan = jnp.concatenate([jnp.kron(eye, cc), jnp.kron(eye, sc)], axis=1)
    cs, ss = _dft_cos_sin(seq)
    pos = jnp.concatenate([cs, -ss], axis=1) * ((seq * FN_CH) ** -0.5)
    return chan.astype(BF16), pos.astype(BF16)


def _fnet_chan_kernel(u_ref, dft_ref, v_ref):
    r = jnp.dot(u_ref[...].astype(BF16), dft_ref[...], preferred_element_type=F32)
    v_ref[0] = r[:, :GROUP_W].astype(v_ref.dtype)
    v_ref[1] = r[:, GROUP_W:].astype(v_ref.dtype)


def _fnet_pos_kernel(w_ref, v_ref, o_ref):
    o_ref[...] = jnp.dot(w_ref[...], v_ref[...], preferred_element_type=F32).astype(o_ref.dtype)


def _fnet_pallas(proj, tables, *, batch, seq, col0, out_dtype=F32, ts=512):
    chan, pos = tables
    cb = col0 // GROUP_W
    nt = seq // ts
    v = pl.pallas_call(
        _fnet_chan_kernel,
        name="fnet_channel_dft",
        grid=(batch, nt),
        in_specs=[pl.BlockSpec((ts, GROUP_W), lambda b, i: (b * nt + i, cb)),
                  pl.BlockSpec((GROUP_W, 2 * GROUP_W), lambda b, i: (0, 0))],
        out_specs=pl.BlockSpec((2, ts, GROUP_W), lambda b, i: (0, i, b)),
        out_shape=jax.ShapeDtypeStruct((2, seq, batch * GROUP_W), BF16),
        compiler_params=pltpu.CompilerParams(
            dimension_semantics=("parallel", "parallel"),
            vmem_limit_bytes=VMEM_LIMIT_BYTES),
    )(proj, chan)
    v = v.reshape(2 * seq, batch * GROUP_W)
    return pl.pallas_call(
        _fnet_pos_kernel,
        name="fnet_position_dft",
        grid=(nt, batch),
        in_specs=[pl.BlockSpec((ts, 2 * seq), lambda i, b: (i, 0)),
                  pl.BlockSpec((2 * seq, GROUP_W), lambda i, b: (0, b))],
        out_specs=pl.BlockSpec((ts, GROUP_W), lambda i, b: (b * nt + i, 0)),
        out_shape=jax.ShapeDtypeStruct((batch * seq, GROUP_W), out_dtype),
        compiler_params=pltpu.CompilerParams(
            dimension_semantics=("parallel", "parallel"),
            vmem_limit_bytes=VMEM_LIMIT_BYTES),
    )(pos, v)


def _xattn_kernel(q_ref, k_ref, v_ref, o_ref):
    for h in range(XA_HEADS):
        sl = slice(h * XA_DH, (h + 1) * XA_DH)
        s = lax.dot_general(q_ref[:, sl].astype(BF16), k_ref[:, sl].astype(BF16),
                            (((1,), (1,)), ((), ())), preferred_element_type=F32) * (XA_DH ** -0.5)
        p = jnp.exp(s - jnp.max(s, axis=-1, keepdims=True))
        l = jnp.sum(p, axis=-1, keepdims=True)
        o = jnp.dot(p.astype(BF16), v_ref[:, sl].astype(BF16), preferred_element_type=F32)
        o_ref[:, sl] = (o / l).astype(o_ref.dtype)


def _xattn_pallas(q, k, v, *, batch, seq, mem_len, ts=512, out_dtype=BF16):
    nt = seq // ts
    return pl.pallas_call(
        _xattn_kernel,
        name="cross_attention",
        grid=(batch, nt),
        in_specs=[pl.BlockSpec((ts, D_MODEL), lambda b, i: (b * nt + i, 0)),
                  pl.BlockSpec((mem_len, D_MODEL), lambda b, i: (b, 0)),
                  pl.BlockSpec((mem_len, D_MODEL), lambda b, i: (b, 0))],
        out_specs=pl.BlockSpec((ts, D_MODEL), lambda b, i: (b * nt + i, 0)),
        out_shape=jax.ShapeDtypeStruct((batch * seq, D_MODEL), out_dtype),
        compiler_params=pltpu.CompilerParams(
            dimension_semantics=("parallel", "parallel"),
            vmem_limit_bytes=VMEM_LIMIT_BYTES),
    )(q, k, v)


MOE_FF = D_MODEL // 4
MOE_TB = 256
MOE_LANE0 = MOE_GROUPS
ROUTER_LANES = 128
META_E, META_RANK, META_GATE = 0, 2, 4
HALF_D = D_MODEL // 2
ROW_DMA_UNROLL = 8


def _router_kernel(x_ref, g_ref, wr_ref, hp_ref, meta_ref, cnt_ref, carry_ref):
    tm = x_ref.shape[0]

    @pl.when(pl.program_id(0) == 0)
    def _():
        carry_ref[...] = jnp.zeros_like(carry_ref)

    x = x_ref[...].astype(F32)
    hn = x * lax.rsqrt(jnp.mean(x * x, axis=-1, keepdims=True) + EPS) * g_ref[...]
    hb = hn.astype(BF16)
    logits = jnp.dot(hb, wr_ref[...].astype(BF16), preferred_element_type=F32)
    bits = lax.bitcast_convert_type(hb.astype(F32), jnp.uint32)
    hp_ref[...] = (bits[:, :HALF_D] >> 16) | (bits[:, HALF_D:] & jnp.uint32(0xFFFF0000))

    lane = lax.broadcasted_iota(jnp.int32, (tm, ROUTER_LANES), 1).astype(F32)
    ninf = -jnp.inf
    first = lambda hit: jnp.min(jnp.where(hit, lane, float(ROUTER_LANES)), axis=-1, keepdims=True)
    gl = jnp.where(lane < MOE_GROUPS, logits, ninf)
    gmax = jnp.max(gl, axis=-1, keepdims=True)
    gidx = first(gl == gmax)
    g_gate = 1.0 / jnp.sum(jnp.exp(gl - gmax), axis=-1, keepdims=True)
    off = lane - (MOE_LANE0 + MOE_PER_GROUP * gidx)
    el = jnp.where(jnp.abs(2.0 * off - (MOE_PER_GROUP - 1)) < MOE_PER_GROUP, logits, ninf)
    v1 = jnp.max(el, axis=-1, keepdims=True)
    l1 = first(el == v1)
    el2 = jnp.where(lane == l1, ninf, el)
    v2 = jnp.max(el2, axis=-1, keepdims=True)
    l2 = first(el2 == v2)
    t = jnp.exp(v2 - v1)
    gate1 = g_gate / (1.0 + t)
    gate2 = g_gate * t / (1.0 + t)

    oh = jnp.where(lane == l1, 1.0, 0.0) + jnp.where(lane == l2, 1.0, 0.0)
    ti = lax.broadcasted_iota(jnp.int32, (tm, tm), 0)
    si = lax.broadcasted_iota(jnp.int32, (tm, tm), 1)
    before = jnp.where(si < ti, 1.0, 0.0).astype(BF16)
    base = jnp.dot(before, oh.astype(BF16), preferred_element_type=F32) + carry_ref[0:1, :]
    rank1 = jnp.sum(jnp.where(lane == l1, base, 0.0), axis=-1, keepdims=True)
    rank2 = jnp.sum(jnp.where(lane == l2, base, 0.0), axis=-1, keepdims=True)
    carry_ref[...] = carry_ref[...] + jnp.sum(oh, axis=0, keepdims=True)
    cnt_ref[...] = carry_ref[...]

    meta = jnp.zeros((tm, ROUTER_LANES), F32)
    for ln, val in ((META_E, l1 - MOE_LANE0), (META_E + 1, l2 - MOE_LANE0), (META_RANK, rank1),
                    (META_RANK + 1, rank2), (META_GATE, gate1), (META_GATE + 1, gate2)):
        meta = jnp.where(lane == ln, val, meta)
    meta_ref[...] = meta


def _router_pallas(h, gain, w_rg, w_re, *, tm=512):
    n = h.shape[0]
    wr = jnp.concatenate([w_rg, w_re], axis=1)
    wr = jnp.pad(wr, ((0, 0), (0, ROUTER_LANES - wr.shape[1])))
    return pl.pallas_call(
        _router_kernel,
        name="moe_router",
        grid=(n // tm,),
        in_specs=[pl.BlockSpec((tm, D_MODEL), lambda i: (i, 0)),
                  pl.BlockSpec((1, D_MODEL), lambda i: (0, 0)),
                  pl.BlockSpec((D_MODEL, ROUTER_LANES), lambda i: (0, 0))],
        out_specs=[pl.BlockSpec((tm, HALF_D), lambda i: (i, 0)),
                   pl.BlockSpec((tm, ROUTER_LANES), lambda i: (i, 0)),
                   pl.BlockSpec((8, ROUTER_LANES), lambda i: (0, 0))],
        out_shape=[jax.ShapeDtypeStruct((n, HALF_D), jnp.uint32),
                   jax.ShapeDtypeStruct((n, ROUTER_LANES), F32),
                   jax.ShapeDtypeStruct((8, ROUTER_LANES), F32)],
        scratch_shapes=[pltpu.VMEM((8, ROUTER_LANES), F32)],
        compiler_params=pltpu.CompilerParams(
            dimension_semantics=("arbitrary",),
            vmem_limit_bytes=VMEM_LIMIT_BYTES),
    )(h, gain.reshape(1, D_MODEL).astype(F32), wr)


def _dispatch_kernel(dest_ref, hp_ref, xs_in_ref, xs_ref, sem):
    del xs_in_ref
    tc = hp_ref.shape[0]
    base = pl.program_id(0) * tc * MOE_TOPK

    def row_copy(r, k):
        return pltpu.make_async_copy(hp_ref.at[pl.ds(r, 1)],
                                     xs_ref.at[pl.ds(dest_ref[base + MOE_TOPK * r + k], 1)], sem)

    def start(r, c):
        for k in range(MOE_TOPK):
            row_copy(r, k).start()
        return c

    def wait(r, c):
        for k in range(MOE_TOPK):
            row_copy(r, k).wait()
        return c

    lax.fori_loop(0, tc, start, 0, unroll=ROW_DMA_UNROLL)
    lax.fori_loop(0, tc, wait, 0, unroll=ROW_DMA_UNROLL)


def _dispatch_pallas(hp, dest, n_slots, *, tc=256):
    n = hp.shape[0]
    xs0 = jnp.zeros((n_slots, HALF_D), jnp.uint32)
    return pl.pallas_call(
        _dispatch_kernel,
        name="moe_dispatch",
        grid_spec=pltpu.PrefetchScalarGridSpec(
            num_scalar_prefetch=1,
            grid=(n // tc,),
            in_specs=[pl.BlockSpec((tc, HALF_D), lambda i, d: (i, 0)),
                      pl.BlockSpec(memory_space=pl.ANY)],
            out_specs=pl.BlockSpec(memory_space=pl.ANY),
            scratch_shapes=[pltpu.SemaphoreType.DMA(())]),
        out_shape=jax.ShapeDtypeStruct((n_slots, HALF_D), jnp.uint32),
        input_output_aliases={2: 0},
        compiler_params=pltpu.CompilerParams(
            dimension_semantics=("arbitrary",),
            vmem_limit_bytes=VMEM_LIMIT_BYTES),
    )(dest, hp, xs0)


def _expert_kernel(be_ref, first_ref, slot_ref, nxt_ref, nu_ref, x_ref, w1_hbm, w3_hbm, w2_hbm, y_ref,
                   f1_ref, f3_ref, f2_ref, b1_ref, b3_ref, b2_ref, sem, *, layer):
    i = pl.program_id(0)

    def weight_copies(e, slot):
        return [pltpu.make_async_copy(src.at[layer, e], dst.at[slot], sem.at[slot, n])
                for n, (src, dst) in enumerate(((w1_hbm, f1_ref), (w3_hbm, f3_ref), (w2_hbm, f2_ref)))]

    @pl.when(i < nu_ref[0])
    def _():
        slot = slot_ref[i]

        @pl.when(first_ref[i] == 1)
        def _():
            @pl.when(i == 0)
            def _():
                for c in weight_copies(be_ref[i], slot):
                    c.start()

            for c in weight_copies(be_ref[i], slot):
                c.wait()

            @pl.when(nxt_ref[i] >= 0)
            def _():
                for c in weight_copies(nxt_ref[i], 1 - slot):
                    c.start()

            b1_ref[...] = f1_ref[slot].astype(BF16)
            b3_ref[...] = f3_ref[slot].astype(BF16)
            b2_ref[...] = f2_ref[slot].astype(BF16)

        xp = x_ref[...]
        x_lo = lax.bitcast_convert_type(xp << 16, F32).astype(BF16)
        x_hi = lax.bitcast_convert_type(xp & jnp.uint32(0xFFFF0000), F32).astype(BF16)

        def up(w_ref):
            return (jnp.dot(x_lo, w_ref[:HALF_D, :], preferred_element_type=F32)
                    + jnp.dot(x_hi, w_ref[HALF_D:, :], preferred_element_type=F32))

        a = _silu(up(b1_ref)) * up(b3_ref)
        y_ref[...] = jnp.dot(a.astype(BF16), b2_ref[...], preferred_element_type=F32).astype(y_ref.dtype)

    @pl.when(i >= nu_ref[0])
    def _():
        y_ref[...] = jnp.zeros_like(y_ref)


def _expert_pallas(xs, nblk, w1, w3, w2, layer):
    n_slots = xs.shape[0]
    nb = n_slots // MOE_TB
    bend = jnp.cumsum(nblk)
    blocks = jnp.arange(nb, dtype=jnp.int32)
    block_e = jnp.minimum(jnp.searchsorted(bend, blocks, side='right'), MOE_EXPERTS - 1).astype(jnp.int32)
    first = (blocks == (bend - nblk)[block_e]).astype(jnp.int32)
    nonempty = nblk > 0
    slot_e = (jnp.cumsum(nonempty) - 1) % 2
    ids = jnp.where(nonempty, jnp.arange(MOE_EXPERTS), MOE_EXPERTS)
    after = jnp.concatenate([lax.cummin(ids, reverse=True)[1:], jnp.array([MOE_EXPERTS])])
    next_e = jnp.where(after < MOE_EXPERTS, after, -1)
    n_used = bend[-1:].astype(jnp.int32)
    blk = lambda i, be, fi, sl, nx, nu: jnp.minimum(i, nu[0] - 1)
    hbm = pl.BlockSpec(memory_space=pl.ANY)
    return pl.pallas_call(
        functools.partial(_expert_kernel, layer=layer),
        name="moe_experts",
        grid_spec=pltpu.PrefetchScalarGridSpec(
            num_scalar_prefetch=5,
            grid=(nb,),
            in_specs=[pl.BlockSpec((MOE_TB, HALF_D), lambda i, *s: (blk(i, *s), 0)), hbm, hbm, hbm],
            out_specs=pl.BlockSpec((MOE_TB, D_MODEL), lambda i, *s: (i, 0)),
            scratch_shapes=[pltpu.VMEM((2, D_MODEL, MOE_FF), F32), pltpu.VMEM((2, D_MODEL, MOE_FF), F32),
                            pltpu.VMEM((2, MOE_FF, D_MODEL), F32),
                            pltpu.VMEM((D_MODEL, MOE_FF), BF16), pltpu.VMEM((D_MODEL, MOE_FF), BF16),
                            pltpu.VMEM((MOE_FF, D_MODEL), BF16),
                            pltpu.SemaphoreType.DMA((2, 3))]),
        out_shape=jax.ShapeDtypeStruct((n_slots, D_MODEL), F32),
        compiler_params=pltpu.CompilerParams(
            dimension_semantics=("arbitrary",),
            vmem_limit_bytes=VMEM_LIMIT_BYTES),
    )(block_e, first, slot_e[block_e].astype(jnp.int32), next_e[block_e].astype(jnp.int32), n_used,
      xs, w1, w3, w2)


def _combine_kernel(dest_ref, h_ref, meta_ref, g_ref, yb_ref, o_ref, buf_ref, sem, *, final_norm):
    tc = h_ref.shape[0]
    base = pl.program_id(0) * tc * MOE_TOPK

    def row_copy(r, k):
        return pltpu.make_async_copy(yb_ref.at[pl.ds(dest_ref[base + MOE_TOPK * r + k], 1)],
                                     buf_ref.at[k, pl.ds(r, 1)], sem)

    def start(r, c):
        for k in range(MOE_TOPK):
            row_copy(r, k).start()
        return c

    def wait(r, c):
        for k in range(MOE_TOPK):
            row_copy(r, k).wait()
        return c

    lax.fori_loop(0, tc, start, 0, unroll=ROW_DMA_UNROLL)
    lax.fori_loop(0, tc, wait, 0, unroll=ROW_DMA_UNROLL)
    meta = meta_ref[...]
    out = h_ref[...]
    y = jnp.zeros_like(out)
    for k in range(MOE_TOPK):
        y = y + buf_ref[k] * meta[:, META_GATE + k:META_GATE + k + 1]
    out = out + y
    if final_norm:
        out = out * lax.rsqrt(jnp.mean(out * out, axis=-1, keepdims=True) + EPS) * g_ref[...]
    o_ref[...] = out


def _combine_pallas(h, meta, dest, yb, final_gain, *, final_norm, tc=256):
    n = h.shape[0]
    return pl.pallas_call(
        functools.partial(_combine_kernel, final_norm=final_norm),
        name="moe_combine",
        grid_spec=pltpu.PrefetchScalarGridSpec(
            num_scalar_prefetch=1,
            grid=(n // tc,),
            in_specs=[pl.BlockSpec((tc, D_MODEL), lambda i, d: (i, 0)),
                      pl.BlockSpec((tc, ROUTER_LANES), lambda i, d: (i, 0)),
                      pl.BlockSpec((1, D_MODEL), lambda i, d: (0, 0)),
                      pl.BlockSpec(memory_space=pl.ANY)],
            out_specs=pl.BlockSpec((tc, D_MODEL), lambda i, d: (i, 0)),
            scratch_shapes=[pltpu.VMEM((MOE_TOPK, tc, D_MODEL), F32), pltpu.SemaphoreType.DMA(())]),
        out_shape=jax.ShapeDtypeStruct((n, D_MODEL), F32),
        compiler_params=pltpu.CompilerParams(
            dimension_semantics=("arbitrary",),
            vmem_limit_bytes=VMEM_LIMIT_BYTES),
    )(dest, h, meta, final_gain.reshape(1, D_MODEL).astype(F32), yb)


def _moe_pallas(h, gain, w_rg, w_re, w1, w3, w2, layer, final_gain, *, final_norm):
    n = h.shape[0]
    nb = (n * MOE_TOPK) // MOE_TB + MOE_EXPERTS
    hp, meta, cnt = _router_pallas(h, gain, w_rg, w_re)
    expert = meta[:, META_E:META_E + MOE_TOPK].astype(jnp.int32)
    rank = meta[:, META_RANK:META_RANK + MOE_TOPK].astype(jnp.int32)
    counts = cnt[0, MOE_LANE0:MOE_LANE0 + MOE_EXPERTS].astype(jnp.int32)
    nblk = (counts + MOE_TB - 1) // MOE_TB
    dest = ((jnp.cumsum(nblk) - nblk)[expert] * MOE_TB + rank).reshape(n * MOE_TOPK)
    xs = _dispatch_pallas(hp, dest, nb * MOE_TB)
    yb = _expert_pallas(xs, nblk, w1, w3, w2, layer)
    return _combine_pallas(h, meta, dest, yb, final_gain, final_norm=final_norm)


N_GATES = 4 * ML_HEADS
IN_GROUPS_BEFORE_GATES = 4
IN_GROUPS = 13
LANE = 128


def _win_relayout_kernel(*refs):
    parts, main_ref, gate_ref = refs[:-2], refs[-2], refs[-1]
    n = pl.program_id(1)
    wide = jnp.concatenate([p[...] for p in parts], axis=1)

    @pl.when(n < IN_GROUPS_BEFORE_GATES)
    def _():
        main_ref[...] = wide[:, :GROUP_W].astype(main_ref.dtype)

    @pl.when(n >= IN_GROUPS_BEFORE_GATES)
    def _():
        main_ref[...] = wide[:, N_GATES:N_GATES + GROUP_W].astype(main_ref.dtype)

    @pl.when(n == IN_GROUPS_BEFORE_GATES)
    def _():
        src = lax.broadcasted_iota(jnp.int32, (LANE, ML_HEADS * LANE), 0)
        dst = lax.broadcasted_iota(jnp.int32, (LANE, ML_HEADS * LANE), 1)
        perm = (jnp.where(dst % LANE < 4, 1.0, 0.0)
                * jnp.where(src == ML_HEADS * (dst % LANE) + dst // LANE, 1.0, 0.0))
        gate_ref[...] = _dot_exact_rhs(parts[0][...], perm, (((1,), (0,)), ((), ())))


def _win_relayout(w_in):
    depth, k, _ = w_in.shape
    per = GROUP_W // LANE
    part = lambda t: pl.BlockSpec((None, k, LANE), lambda l, n, t=t: (l, 0, per * n + t))
    return pl.pallas_call(
        _win_relayout_kernel,
        name="w_in_relayout",
        grid=(depth, IN_GROUPS),
        in_specs=[part(t) for t in range(per + 1)],
        out_specs=[pl.BlockSpec((None, k, GROUP_W), lambda l, n: (l, 0, n)),
                   pl.BlockSpec((None, k, ML_HEADS * LANE), lambda l, n: (l, 0, 0))],
        out_shape=[jax.ShapeDtypeStruct((depth, k, IN_GROUPS * GROUP_W), BF16),
                   jax.ShapeDtypeStruct((depth, k, ML_HEADS * LANE), F32)],
        compiler_params=pltpu.CompilerParams(
            dimension_semantics=("parallel", "arbitrary"),
            vmem_limit_bytes=VMEM_LIMIT_BYTES),
    )(*([w_in] * (per + 1)))


COL_MLSTM, COL_HGRN, COL_FNET, COL_NA = 0, 4 * GROUP_W, 9 * GROUP_W, 10 * GROUP_W


def kernel(x, mem, norm_mix, norm_cross, norm_ffn, norm_final, norm_mem, w_in, mlstm_conv,
           mlstm_gate_bias, hgrn_lower_bound, na_rpb, group_gain, w_out, xa_wq, xa_wk, xa_wv, xa_wo,
           moe_router_group, moe_router_expert, moe_w1, moe_w3, moe_w2):
    b, s, d = x.shape
    mem_len = mem.shape[1]
    h = x.reshape(b * s, d)
    mem_f = mem.reshape(b * mem_len, d)
    lbs = jnp.cumsum(jax.nn.softmax(hgrn_lower_bound.astype(F32), axis=0), axis=0)
    lbs = lbs - lbs[0]
    fnet_tables = _fnet_tables(s)
    w_main, w_gate = _win_relayout(w_in)
    w_out_b, wq_b, wk_b, wv_b, wo_b = (w.astype(BF16) for w in (w_out, xa_wq, xa_wk, xa_wv, xa_wo))
    for l in range(DEPTH):
        proj, gates = _matmul(h, w_main, l, name="in_proj", gain=norm_mix[l], w_side=w_gate, out_dtype=BF16)
        y_ml = _mlstm_pallas(proj, gates, mlstm_conv[l], mlstm_gate_bias[l], batch=b, seq=s,
                             out_dtype=BF16)
        y_hg = _hgrn_pallas(proj, lbs[l], batch=b, seq=s, col0=COL_HGRN, out_dtype=BF16)
        y_fn = _fnet_pallas(proj, fnet_tables, batch=b, seq=s, col0=COL_FNET, out_dtype=BF16)
        y_na = _na_pallas(proj, na_rpb[l], batch=b, seq=s, col0=COL_NA, out_dtype=BF16)
        h = _matmul([y_ml, y_hg, y_fn, y_na], w_out_b, l, name="out_proj", gain=group_gain[l], residual=h)

        q = _matmul(h, wq_b, l, name="xa_q_proj", gain=norm_cross[l], out_dtype=BF16)
        k = _matmul(mem_f, wk_b, l, name="xa_k_proj", gain=norm_mem, out_dtype=BF16)
        v = _matmul(mem_f, wv_b, l, name="xa_v_proj", gain=norm_mem, out_dtype=BF16)
        o = _xattn_pallas(q, k, v, batch=b, seq=s, mem_len=mem_len)
        h = _matmul(o, wo_b, l, name="xa_o_proj", residual=h)

        h = _moe_pallas(h, norm_ffn[l], moe_router_group[l], moe_router_expert[l],
                        moe_w1, moe_w3, moe_w2, l, norm_final, final_norm=(l == DEPTH - 1))
    return h.reshape(b, s, d)
```

```python
import functools

import jax
import jax.numpy as jnp
import numpy as np
from jax import lax
from jax.experimental import pallas as pl
from jax.experimental.pallas import tpu as pltpu

D_MODEL = 2048
DEPTH = 4
GRID_W = 64
N_MIXERS = 4
GROUP_W = D_MODEL // N_MIXERS
ML_HEADS = 4
ML_DH = GROUP_W // ML_HEADS
ML_CHUNK = 64
HG_HEADS = 4
HG_CHUNK = 16
FN_GROUPS = 4
FN_CH = GROUP_W // FN_GROUPS
NA_HEADS = 8
NA_DH = GROUP_W // NA_HEADS
NA_KH = 8
NA_KW = 16
XA_HEADS = 4
XA_DH = D_MODEL // XA_HEADS
MOE_GROUPS = 4
MOE_PER_GROUP = 8
MOE_EXPERTS = MOE_GROUPS * MOE_PER_GROUP
MOE_TOPK = 2
MOE_BLOCK = 128
EPS = 1e-6
F32 = jnp.float32
BF16 = jnp.bfloat16

VMEM_LIMIT_BYTES = 56 * 1024 * 1024


def _mm_kernel(*refs, n_x, norm, residual, side):
    refs = list(refs)
    x_refs = [refs.pop(0) for _ in range(n_x)]
    g_ref = refs.pop(0) if norm else None
    w_ref = refs.pop(0)
    ws_ref = refs.pop(0) if side else None
    r_ref = refs.pop(0) if residual else None
    o_ref = refs.pop(0)
    os_ref = refs.pop(0) if side else None
    xn_ref = refs.pop(0)

    @pl.when(pl.program_id(1) == 0)
    def _():
        col = 0
        for x_ref in x_refs:
            x = x_ref[...].astype(F32)
            kx = x.shape[1]
            if norm:
                ms = jnp.mean(x * x, axis=-1, keepdims=True)
                x = x * lax.rsqrt(ms + EPS) * g_ref[:, col:col + kx]
            xn_ref[:, col:col + kx] = x.astype(BF16)
            col += kx
        if side:
            os_ref[...] = jnp.dot(xn_ref[...], ws_ref[...].astype(BF16),
                                  preferred_element_type=F32).astype(os_ref.dtype)

    acc = jnp.dot(xn_ref[...], w_ref[...].astype(BF16), preferred_element_type=F32)
    if residual:
        acc = acc + r_ref[...]
    o_ref[...] = acc.astype(o_ref.dtype)


def _matmul(xs, w, layer, *, name, gain=None, residual=None, w_side=None, tm=1024, tn=512, out_dtype=F32):
    if not isinstance(xs, (list, tuple)):
        xs = [xs]
    m = xs[0].shape[0]
    k = sum(x.shape[1] for x in xs)
    n = w.shape[2]
    tm = min(tm, m)
    tn = min(tn, n)
    assert m % tm == 0 and n % tn == 0 and w.shape[1] == k, (m, n, k, tm, tn)
    norm = gain is not None
    has_res = residual is not None
    side = w_side is not None
    in_specs = [pl.BlockSpec((tm, x.shape[1]), lambda i, j: (i, 0)) for x in xs]
    args = list(xs)
    if norm:
        in_specs.append(pl.BlockSpec((1, k), lambda i, j: (0, 0)))
        args.append(gain.reshape(1, k).astype(F32))
    in_specs.append(pl.BlockSpec((None, k, tn), lambda i, j: (layer, 0, j)))
    args.append(w)
    if side:
        in_specs.append(pl.BlockSpec((None, k, w_side.shape[2]), lambda i, j: (layer, 0, 0)))
        args.append(w_side)
    if has_res:
        in_specs.append(pl.BlockSpec((tm, tn), lambda i, j: (i, j)))
        args.append(residual)
    out_specs = pl.BlockSpec((tm, tn), lambda i, j: (i, j))
    out_shape = jax.ShapeDtypeStruct((m, n), out_dtype)
    if side:
        out_specs = [out_specs, pl.BlockSpec((tm, w_side.shape[2]), lambda i, j: (i, 0))]
        out_shape = [out_shape, jax.ShapeDtypeStruct((m, w_side.shape[2]), F32)]
    return pl.pallas_call(
        functools.partial(_mm_kernel, n_x=len(xs), norm=norm, residual=has_res, side=side),
        name=name,
        grid=(m // tm, n // tn),
        in_specs=in_specs,
        out_specs=out_specs,
        out_shape=out_shape,
        scratch_shapes=[pltpu.VMEM((tm, k), BF16)],
        compiler_params=pltpu.CompilerParams(
            dimension_semantics=("parallel", "arbitrary"),
            vmem_limit_bytes=VMEM_LIMIT_BYTES),
    )(*args)


NA_QROWS = 4
NA_WROWS = 12


def _na_groups(rows):
    tables, plan = [], []
    for r0 in range(0, rows, NA_QROWS):
        band0 = lambda r: min(max(r - NA_KH // 2, 0), rows - NA_KH)
        kr0 = min(band0(r0), rows - NA_WROWS)
        assert band0(r0 + NA_QROWS - 1) + NA_KH <= kr0 + NA_WROWS
        dr = np.full((NA_QROWS, NA_WROWS), -1, np.int64)
        for i in range(NA_QROWS):
            for j in range(NA_WROWS):
                if 0 <= kr0 + j - band0(r0 + i) < NA_KH:
                    dr[i, j] = kr0 + j - (r0 + i) + NA_KH - 1
        key = dr.tobytes()
        if key not in [t.tobytes() for t in tables]:
            tables.append(dr)
        plan.append((r0, kr0, [t.tobytes() for t in tables].index(key)))
    return plan, np.stack(tables)


def _na_bias_tables(rpb, dr_tables):
    c = np.arange(GRID_W)
    dc = np.clip(c[None, :] - c[:, None] + NA_KW - 1, 0, 2 * NA_KW - 2)
    onehot = (dc[None] == np.arange(2 * NA_KW - 1)[:, None, None]).astype(np.float32)
    col_start = np.clip(c - NA_KW // 2, 0, GRID_W - NA_KW)
    col_ok = (c[None, :] >= col_start[:, None]) & (c[None, :] < col_start[:, None] + NA_KW)
    t = jnp.einsum('hrd,dqk->hrqk', rpb.astype(F32), onehot, precision=lax.Precision.HIGHEST)
    t = jnp.where(col_ok[None, None], t, -jnp.inf)
    t = jnp.concatenate([t, jnp.full_like(t[:, :1], -jnp.inf)], axis=1)
    idx = np.where(dr_tables < 0, t.shape[1] - 1, dr_tables)
    pairs = idx.reshape(idx.shape[0], idx.shape[1], -1, 2)
    uniq = sorted({(int(a), int(b)) for a, b in pairs.reshape(-1, 2)})
    pair_idx = np.array([[[uniq.index((int(a), int(b))) for a, b in row] for row in typ] for typ in pairs])
    table = jnp.stack([jnp.concatenate([t[:, a], t[:, b]], axis=-1) for a, b in uniq], axis=1)
    return table, pair_idx


def _na_kernel(q_ref, k_ref, v_ref, tab_ref, o_ref, bias_ref, *, plan, pair_idx):
    nq = NA_QROWS * GRID_W
    nk = NA_WROWS * GRID_W
    for hh in range(2):
        for typ in range(pair_idx.shape[0]):
            for i in range(pair_idx.shape[1]):
                for jj in range(pair_idx.shape[2]):
                    bias_ref[hh, typ, i * GRID_W:(i + 1) * GRID_W, jj * 2 * GRID_W:(jj + 1) * 2 * GRID_W] = (
                        tab_ref[hh, int(pair_idx[typ, i, jj])])
    for r0, kr0, typ in plan:
        q = q_ref[r0 * GRID_W:r0 * GRID_W + nq, :].astype(F32) * (NA_DH ** -0.5)
        kb = k_ref[kr0 * GRID_W:kr0 * GRID_W + nk, :]
        vb = v_ref[kr0 * GRID_W:kr0 * GRID_W + nk, :]
        outs = []
        for hh in range(2):
            sl = slice(hh * NA_DH, (hh + 1) * NA_DH)
            s = lax.dot_general(q[:, sl].astype(BF16), kb[:, sl].astype(BF16),
                                (((1,), (1,)), ((), ())), preferred_element_type=F32)
            s = s + bias_ref[hh, typ]
            m = jnp.max(s, axis=-1, keepdims=True)
            p = jnp.exp(s - m)
            l = jnp.sum(p, axis=-1, keepdims=True)
            o = jnp.dot(p.astype(BF16), vb[:, sl].astype(BF16), preferred_element_type=F32)
            outs.append(o / l)
        o_ref[r0 * GRID_W:r0 * GRID_W + nq, :] = jnp.concatenate(outs, axis=-1).astype(o_ref.dtype)


def _na_pallas(proj, rpb, *, batch, seq, col0, out_dtype=F32):
    rows = seq // GRID_W
    pair_w = 2 * NA_DH
    cb = col0 // pair_w
    gb = GROUP_W // pair_w
    plan, dr_tables = _na_groups(rows)
    table, pair_idx = _na_bias_tables(rpb, dr_tables)
    spec = lambda g: pl.BlockSpec((seq, pair_w), lambda b, p, g=g: (b, cb + g * gb + p))
    return pl.pallas_call(
        functools.partial(_na_kernel, plan=plan, pair_idx=pair_idx),
        name="na_attention",
        grid=(batch, NA_HEADS // 2),
        in_specs=[spec(0), spec(1), spec(2),
                  pl.BlockSpec((2,) + table.shape[1:], lambda b, p: (p, 0, 0, 0))],
        out_specs=pl.BlockSpec((seq, pair_w), lambda b, p: (b, p)),
        out_shape=jax.ShapeDtypeStruct((batch * seq, GROUP_W), out_dtype),
        scratch_shapes=[pltpu.VMEM((2, dr_tables.shape[0], NA_QROWS * GRID_W, NA_WROWS * GRID_W), F32)],
        compiler_params=pltpu.CompilerParams(
            dimension_semantics=("parallel", "parallel"),
            vmem_limit_bytes=VMEM_LIMIT_BYTES),
    )(proj, proj, proj, table)


ML_L = 128
ML_GATE_LANES = 128


def _log_sigmoid(x):
    return jnp.minimum(x, 0.0) - jnp.log(1.0 + jnp.exp(-jnp.abs(x)))


def _silu(x):
    return x * jax.nn.sigmoid(x)


_NT = (((1,), (1,)), ((), ()))
_TN = (((0,), (0,)), ((), ()))


def _split3(x):
    hi = x.astype(BF16)
    r1 = x - hi.astype(F32)
    mid = r1.astype(BF16)
    lo = (r1 - mid.astype(F32)).astype(BF16)
    return hi, mid, lo


def _dot_exact_lhs(a, x, dims=None):
    a = a.astype(BF16)
    if dims is None:
        return sum(jnp.dot(a, p, preferred_element_type=F32) for p in _split3(x))
    return sum(lax.dot_general(a, p, dims, preferred_element_type=F32) for p in _split3(x))


def _dot_exact_rhs(x, a, dims):
    a = a.astype(BF16)
    return sum(lax.dot_general(p, a, dims, preferred_element_type=F32) for p in _split3(x))


ML_GROUP = 4
ML_PAD = 16


def _mlstm_local(q, k, v, gcol, tri, mask_w, gi, end_row):
    L = ML_L
    cum_col = _dot_exact_lhs(tri, _log_sigmoid(gcol))
    wide = lambda col: jnp.broadcast_to(col, (L, L))
    b_cols, ig_cols, rows = [], [], []
    for g in range(len(q)):
        lanes = slice(g * ML_GATE_LANES, (g + 1) * ML_GATE_LANES)
        b_cols.append(cum_col[:, g * ML_GATE_LANES + gi + 1:g * ML_GATE_LANES + gi + 2])
        ig_cols.append(gcol[:, g * ML_GATE_LANES + gi:g * ML_GATE_LANES + gi + 1])
        rows.append(gcol[:, lanes].T[gi:gi + 1, :] - cum_col[:, lanes].T[gi + 1:gi + 2, :])
    b_w = jnp.concatenate([wide(b) for b in b_cols], axis=1)
    row_w = jnp.concatenate(rows, axis=1)
    dmat = jnp.where(mask_w != 0.0, b_w + row_w, -jnp.inf)
    m_in_w = jnp.concatenate([wide(jnp.max(dmat[:, g * L:(g + 1) * L], axis=-1, keepdims=True))
                              for g in range(len(q))], axis=1)
    w = jnp.exp(dmat - m_in_w)
    ones = jnp.ones((L, ML_DH), BF16)
    out = []
    for g in range(len(q)):
        lanes = slice(g * L, (g + 1) * L)
        kb = k[g].astype(BF16)
        v_aug = jnp.concatenate([v[g].astype(BF16), ones], axis=1)
        qk = lax.dot_general(q[g].astype(BF16), kb, _NT, preferred_element_type=F32) * w[:, lanes]
        res = jnp.dot(qk.astype(BF16), v_aug, preferred_element_type=F32)
        b_end = b_cols[g][end_row:end_row + 1, :]
        a_col = b_end - b_cols[g] + ig_cols[g]
        m_loc = jnp.max(a_col, axis=0, keepdims=True)
        kw = k[g] * jnp.exp(a_col - m_loc)
        st = lax.dot_general(v_aug, kw.astype(BF16), _TN, preferred_element_type=F32)
        out.append(dict(m_in=m_in_w[:, lanes], num=res[:, :ML_DH], den=res[:, ML_DH:], b=b_w[:, lanes],
                        ct=st[:ML_DH], n=st[ML_DH:ML_DH + 1], m_loc=m_loc, b_end=b_end))
    return out


def _mlstm_carry(q, m_in, num_in, den_in, b, ct, n, m_prev):
    inter = b + m_prev
    m_t = jnp.maximum(inter, m_in)
    s_in = jnp.exp(m_in - m_t)
    s_inter = jnp.exp(inter - m_t)
    rhs = jnp.concatenate([ct.astype(BF16), jnp.broadcast_to(n, (ML_PAD, ML_DH)).astype(BF16)], axis=0)
    both = lax.dot_general(q.astype(BF16), rhs, _NT, preferred_element_type=F32)
    num = s_in * num_in + s_inter * both[:, :ML_DH]
    den = s_in * den_in + s_inter * both[:, ML_DH:ML_DH + 1]
    return num / jnp.maximum(jnp.abs(den), jnp.exp(-m_t))


def _mlstm_kernel(q_ref, k_ref, v_ref, o_ref, cwq_ref, cwk_ref, gcol_ref, bcol_ref,
                  y_ref, qs_ref, ks_ref, num_ref, den_ref, min_ref, b_ref, ctl_ref, stat_ref, *, seq):
    L = ML_L
    nc = seq // L
    t_idx = lax.broadcasted_iota(jnp.int32, (seq, 1), 0)

    def conv_silu(x, w):
        prev = jnp.where(t_idx == 0, 0.0, pltpu.roll(x, 1, axis=0))
        nxt = jnp.where(t_idx == seq - 1, 0.0, pltpu.roll(x, seq - 1, axis=0))
        return _silu(prev * w[0:1, :] + x * w[1:2, :] + nxt * w[2:3, :])

    qs_ref[...] = conv_silu(q_ref[...].astype(F32), cwq_ref[...])
    ks_ref[...] = conv_silu(k_ref[...].astype(F32), cwk_ref[...]) * (ML_DH ** -0.5)

    ti = lax.broadcasted_iota(jnp.int32, (L, L), 0)
    si = lax.broadcasted_iota(jnp.int32, (L, L), 1)
    tris = (jnp.where(si <= ti, 1.0, 0.0), jnp.where(si >= ti, 1.0, 0.0))
    masks_w = [jnp.concatenate([t] * ML_GROUP, axis=1) for t in tris]

    for direction in (0, 1):
        for c0 in range(0, nc, ML_GROUP):
            rows = [pl.ds((c0 + g) * L, L) for g in range(ML_GROUP)]
            gcol = jnp.concatenate([gcol_ref[r, :] + bcol_ref[...] for r in rows], axis=1)
            local = _mlstm_local([qs_ref[r, :] for r in rows], [ks_ref[r, :] for r in rows],
                                 [v_ref[r, :].astype(F32) for r in rows], gcol,
                                 tris[direction], masks_w[direction], 2 * direction,
                                 L - 1 if direction == 0 else 0)
            for g, (r, loc) in enumerate(zip(rows, local)):
                idx = direction * nc + c0 + g
                num_ref[direction, r, :] = loc["num"]
                den_ref[direction, r, :] = loc["den"]
                min_ref[direction, r, :] = loc["m_in"]
                b_ref[direction, r, :] = loc["b"]
                ctl_ref[idx] = loc["ct"]
                stat_ref[idx, 0:1, :] = loc["n"]
                stat_ref[idx, 1:2, :] = jnp.broadcast_to(loc["m_loc"], (1, ML_DH))
                stat_ref[idx, 2:3, :] = jnp.broadcast_to(loc["b_end"], (1, ML_DH))

    state = [(jnp.zeros((ML_DH, ML_DH), F32), jnp.zeros((1, ML_DH), F32), jnp.zeros((1, ML_DH), F32))] * 2
    for i in range(nc):
        for direction in (0, 1):
            c = i if direction == 0 else nc - 1 - i
            r = pl.ds(c * L, L)
            idx = direction * nc + c
            ct, n, m = state[direction]
            num_ref[direction, r, :] = _mlstm_carry(qs_ref[r, :], min_ref[direction, r, :],
                                                    num_ref[direction, r, :], den_ref[direction, r, :],
                                                    b_ref[direction, r, :], ct, n, m)
            n_loc, m_loc, b_end = stat_ref[idx, 0:1, :], stat_ref[idx, 1:2, :], stat_ref[idx, 2:3, :]
            m_new = jnp.maximum(b_end + m, m_loc)
            s_old = jnp.exp(b_end + m - m_new)
            s_new = jnp.exp(m_loc - m_new)
            state[direction] = (s_old * ct + s_new * ctl_ref[idx], s_old * n + s_new * n_loc, m_new)
    y_ref[...] = (jax.nn.sigmoid(o_ref[...].astype(F32)) * (num_ref[0] + num_ref[1])).astype(y_ref.dtype)


def _mlstm_pallas(proj, gates, conv_w, gate_b, *, batch, seq, out_dtype=F32):
    d = ML_DH
    hb = GROUP_W // d
    gb4 = gate_b.astype(F32).reshape(4, ML_HEADS).T
    gb_col = jnp.pad(gb4, ((0, 0), (0, ML_GATE_LANES - 4))).reshape(ML_HEADS, 1, ML_GATE_LANES)
    spec = lambda grp: pl.BlockSpec((seq, d), lambda b, h, grp=grp: (b, grp * hb + h))
    f32 = lambda *shape: pltpu.VMEM(shape, F32)
    return pl.pallas_call(
        functools.partial(_mlstm_kernel, seq=seq),
        name="mlstm",
        grid=(batch, ML_HEADS),
        in_specs=[spec(0), spec(1), spec(2), spec(3),
                  pl.BlockSpec((3, d), lambda b, h: (0, h)),
                  pl.BlockSpec((3, d), lambda b, h: (0, hb + h)),
                  pl.BlockSpec((seq, ML_GATE_LANES), lambda b, h: (b, h)),
                  pl.BlockSpec((None, 1, ML_GATE_LANES), lambda b, h: (h, 0, 0))],
        out_specs=pl.BlockSpec((seq, d), lambda b, h: (b, h)),
        out_shape=jax.ShapeDtypeStruct((batch * seq, GROUP_W), out_dtype),
        scratch_shapes=[f32(seq, d), f32(seq, d), f32(2, seq, d), f32(2, seq, d), f32(2, seq, d),
                        f32(2, seq, d), f32(2 * (seq // ML_L), d, d), f32(2 * (seq // ML_L), 8, d)],
        compiler_params=pltpu.CompilerParams(
            dimension_semantics=("parallel", "parallel"),
            vmem_limit_bytes=VMEM_LIMIT_BYTES),
    )(proj, proj, proj, proj, conv_w.astype(F32), conv_w.astype(F32), gates, gb_col)


HG_L = 128
HG_LEAF = 8


HG_GROUP = 4
HG_NLEAF = HG_L // HG_LEAF
HG_LEVELS = 4


def _hgrn_ref_rows(p_ref, chunks, backward):
    def rows(first, count, stride):
        return jnp.concatenate([p_ref[pl.ds(c * HG_L + first, count, stride=stride), :] for c in chunks], axis=1)

    leaf = rows(HG_LEAF // 2 if backward else HG_LEAF // 2 - 1, HG_NLEAF, HG_LEAF)
    levels = []
    m = HG_L // 2
    for _ in range(HG_LEVELS):
        blocks = HG_L // (2 * m)
        r = rows(m if backward else m - 1, blocks, 2 * m) if blocks > 1 else rows(m if backward else m - 1, 1, 1)
        levels.append(jnp.repeat(r, HG_NLEAF // blocks, axis=0))
        m //= 2
    return leaf, levels


def _hgrn_chunks(q, k, v, p, ref_leaf, ref_levels, masks):
    L = HG_L
    dk = HG_L
    groups = q.shape[1] // dk
    spread = lambda e: jnp.broadcast_to(e[:, None, :], (HG_NLEAF, HG_LEAF, e.shape[1])).reshape(L, e.shape[1])
    d = p - spread(ref_leaf)
    q_leaf = q * jnp.exp(d)
    k_leaf = k * jnp.exp(-d)
    pairs = [(q_leaf.astype(BF16), k_leaf.astype(BF16), masks[HG_LEVELS])]
    for li in range(HG_LEVELS):
        gap = ref_leaf - ref_levels[li]
        qh = q_leaf * spread(jnp.exp(jnp.minimum(gap, 0.0)))
        kh = k_leaf * spread(jnp.exp(jnp.minimum(-gap, 0.0)))
        pairs.append((qh.astype(BF16), kh.astype(BF16), masks[li]))
    vb = v.astype(BF16)
    outs = []
    for g in range(groups):
        lanes = slice(g * dk, (g + 1) * dk)
        attn = jnp.zeros((L, L), F32)
        for qh, kh, mask in pairs:
            s = lax.dot_general(qh[:, lanes], kh[:, lanes], _NT, preferred_element_type=F32)
            attn = attn + jnp.where(mask != 0.0, s, 0.0)
        outs.append(jnp.dot(attn.astype(BF16), vb[:, lanes], preferred_element_type=F32))
    return jnp.concatenate(outs, axis=1)


def _hgrn_carry(q, k, v, p, st, backward):
    L = HG_L
    o = lax.dot_general((q * jnp.exp(p)).astype(BF16), st.astype(BF16), _NT, preferred_element_type=F32)
    p_end = p[0:1, :] if backward else p[L - 1:L, :]
    kd = k * jnp.exp(p_end - p)
    st = st * jnp.exp(p_end) + lax.dot_general(v.astype(BF16), kd.astype(BF16), _TN,
                                               preferred_element_type=F32)
    return o, st


def _hgrn_masks(backward):
    L = HG_L
    ti = lax.broadcasted_iota(jnp.int32, (L, L), 0)
    si = lax.broadcasted_iota(jnp.int32, (L, L), 1)
    if backward:
        ti, si = si, ti
    one = lambda cond: jnp.where(cond, 1.0, 0.0)
    masks = []
    m = L // 2
    while m >= HG_LEAF:
        same = one((ti // (2 * m)) == (si // (2 * m)))
        masks.append(same * one((ti % (2 * m)) >= m) * one((si % (2 * m)) < m))
        m //= 2
    masks.append(one((ti // HG_LEAF) == (si // HG_LEAF)) * one(si <= ti))
    return masks


def _hgrn_kernel(q_ref, ff_ref, fb_ref, i_ref, g_ref, lb_ref, y_ref,
                 qs_ref, lf_ref, kf_ref, lbk_ref, kb_ref, of_ref, ob_ref, *, seq):
    L = HG_L
    nc = seq // L
    lb = lb_ref[...]
    log_lb = jnp.log(lb)
    log1m_lb = jnp.log1p(-lb)

    def forget(fp):
        ls = _log_sigmoid(fp)
        a = log_lb
        c = log1m_lb + ls
        logf = jnp.maximum(a, c) + jnp.log(1.0 + jnp.exp(-jnp.abs(a - c)))
        return logf, (1.0 - lb) * jnp.exp(ls - fp)

    qs_ref[...] = _silu(q_ref[...].astype(F32))
    lf_ref[...], kf_ref[...] = forget(ff_ref[...].astype(F32))
    lbk_ref[...], kb_ref[...] = forget(fb_ref[...].astype(F32))

    ti = lax.broadcasted_iota(jnp.int32, (L, L), 0)
    si = lax.broadcasted_iota(jnp.int32, (L, L), 1)
    tril = (si <= ti).astype(F32)
    triu = (si >= ti).astype(F32)
    masks_f = _hgrn_masks(False)
    masks_b = _hgrn_masks(True)

    def side_by_side(ref, c0):
        return jnp.concatenate([ref[pl.ds((c0 + g) * L, L), :].astype(F32) for g in range(HG_GROUP)], axis=1)

    def scatter_back(ref, c0, wide):
        for g in range(HG_GROUP):
            ref[pl.ds((c0 + g) * L, L), :] = wide[:, g * L:(g + 1) * L]

    directions = ((kf_ref, lf_ref, of_ref, tril, masks_f, False), (kb_ref, lbk_ref, ob_ref, triu, masks_b, True))
    for c0 in range(0, nc, HG_GROUP):
        for _, f_ref, _, tri, _, _ in directions:
            scatter_back(f_ref, c0, _dot_exact_lhs(tri, side_by_side(f_ref, c0)))
    for c0 in range(0, nc, HG_GROUP):
        q = side_by_side(qs_ref, c0)
        v = side_by_side(i_ref, c0)
        for k_ref, f_ref, o_ref, _, masks, backward in directions:
            ref_leaf, ref_levels = _hgrn_ref_rows(f_ref, range(c0, c0 + HG_GROUP), backward)
            scatter_back(o_ref, c0, _hgrn_chunks(q, side_by_side(k_ref, c0), v, side_by_side(f_ref, c0),
                                                 ref_leaf, ref_levels, masks))
    st_f = jnp.zeros((L, L), F32)
    st_b = jnp.zeros((L, L), F32)
    for i in range(nc):
        sl = pl.ds(i * L, L)
        o, st_f = _hgrn_carry(qs_ref[sl, :], kf_ref[sl, :], i_ref[sl, :].astype(F32), lf_ref[sl, :], st_f, False)
        of_ref[sl, :] += o
        sl = pl.ds((nc - 1 - i) * L, L)
        o, st_b = _hgrn_carry(qs_ref[sl, :], kb_ref[sl, :], i_ref[sl, :].astype(F32), lbk_ref[sl, :], st_b, True)
        ob_ref[sl, :] += o
    o = of_ref[...] + ob_ref[...]
    o = o * lax.rsqrt(jnp.mean(o * o, axis=-1, keepdims=True) + EPS)
    y_ref[...] = (o * _silu(g_ref[...].astype(F32))).astype(y_ref.dtype)


def _hgrn_pallas(proj, lb, *, batch, seq, col0, out_dtype=F32):
    d = GROUP_W // HG_HEADS
    hb = GROUP_W // d
    cb = col0 // d
    spec = lambda grp: pl.BlockSpec((seq, d), lambda b, h, grp=grp: (b, cb + grp * hb + h))
    f32 = lambda *shape: pltpu.VMEM(shape, F32)
    return pl.pallas_call(
        functools.partial(_hgrn_kernel, seq=seq),
        name="hgrn2",
        grid=(batch, HG_HEADS),
        in_specs=[spec(0), spec(1), spec(2), spec(3), spec(4),
                  pl.BlockSpec((1, d), lambda b, h: (0, h))],
        out_specs=pl.BlockSpec((seq, d), lambda b, h: (b, h)),
        out_shape=jax.ShapeDtypeStruct((batch * seq, GROUP_W), out_dtype),
        scratch_shapes=[f32(seq, d)] * 7,
        compiler_params=pltpu.CompilerParams(
            dimension_semantics=("parallel", "parallel"),
            vmem_limit_bytes=VMEM_LIMIT_BYTES),
    )(proj, proj, proj, proj, proj, lb.astype(F32).reshape(1, GROUP_W))


def _dft_cos_sin(n):
    k = (jnp.arange(n, dtype=jnp.int32)[:, None] * jnp.arange(n, dtype=jnp.int32)[None, :]) % n
    ang = k.astype(F32) * (2.0 * np.pi / n)
    return jnp.cos(ang), jnp.sin(ang)


def _fnet_tables(seq):
    cc, sc = _dft_cos_sin(FN_CH)
    eye = jnp.eye(FN_GROUPS, dtype=F32)
    chan = jnp.concatenate([jnp.kron(eye, cc), jnp.kron(eye, sc)], axis=1)
    cs, ss = _dft_cos_sin(seq)
    pos = jnp.concatenate([cs, -ss], axis=1) * ((seq * FN_CH) ** -0.5)
    return chan.astype(BF16), pos.astype(BF16)


def _fnet_chan_kernel(u_ref, dft_ref, v_ref):
    r = jnp.dot(u_ref[...].astype(BF16), dft_ref[...], preferred_element_type=F32)
    v_ref[0] = r[:, :GROUP_W].astype(v_ref.dtype)
    v_ref[1] = r[:, GROUP_W:].astype(v_ref.dtype)


def _fnet_pos_kernel(w_ref, v_ref, o_ref):
    o_ref[...] = jnp.dot(w_ref[...], v_ref[...], preferred_element_type=F32).astype(o_ref.dtype)


def _fnet_pallas(proj, tables, *, batch, seq, col0, out_dtype=F32, ts=512):
    chan, pos = tables
    cb = col0 // GROUP_W
    nt = seq // ts
    v = pl.pallas_call(
        _fnet_chan_kernel,
        name="fnet_channel_dft",
        grid=(batch, nt),
        in_specs=[pl.BlockSpec((ts, GROUP_W), lambda b, i: (b * nt + i, cb)),
                  pl.BlockSpec((GROUP_W, 2 * GROUP_W), lambda b, i: (0, 0))],
        out_specs=pl.BlockSpec((2, ts, GROUP_W), lambda b, i: (0, i, b)),
        out_shape=jax.ShapeDtypeStruct((2, seq, batch * GROUP_W), BF16),
        compiler_params=pltpu.CompilerParams(
            dimension_semantics=("parallel", "parallel"),
            vmem_limit_bytes=VMEM_LIMIT_BYTES),
    )(proj, chan)
    v = v.reshape(2 * seq, batch * GROUP_W)
    return pl.pallas_call(
        _fnet_pos_kernel,
        name="fnet_position_dft",
        grid=(nt, batch),
        in_specs=[pl.BlockSpec((ts, 2 * seq), lambda i, b: (i, 0)),
                  pl.BlockSpec((2 * seq, GROUP_W), lambda i, b: (0, b))],
        out_specs=pl.BlockSpec((ts, GROUP_W), lambda i, b: (b * nt + i, 0)),
        out_shape=jax.ShapeDtypeStruct((batch * seq, GROUP_W), out_dtype),
        compiler_params=pltpu.CompilerParams(
            dimension_semantics=("parallel", "parallel"),
            vmem_limit_bytes=VMEM_LIMIT_BYTES),
    )(pos, v)


def _xattn_kernel(q_ref, k_ref, v_ref, o_ref):
    for h in range(XA_HEADS):
        sl = slice(h * XA_DH, (h + 1) * XA_DH)
        s = lax.dot_general(q_ref[:, sl].astype(BF16), k_ref[:, sl].astype(BF16),
                            (((1,), (1,)), ((), ())), preferred_element_type=F32) * (XA_DH ** -0.5)
        p = jnp.exp(s - jnp.max(s, axis=-1, keepdims=True))
        l = jnp.sum(p, axis=-1, keepdims=True)
        o = jnp.dot(p.astype(BF16), v_ref[:, sl].astype(BF16), preferred_element_type=F32)
        o_ref[:, sl] = (o / l).astype(o_ref.dtype)


def _xattn_pallas(q, k, v, *, batch, seq, mem_len, ts=512, out_dtype=BF16):
    nt = seq // ts
    return pl.pallas_call(
        _xattn_kernel,
        name="cross_attention",
        grid=(batch, nt),
        in_specs=[pl.BlockSpec((ts, D_MODEL), lambda b, i: (b * nt + i, 0)),
                  pl.BlockSpec((mem_len, D_MODEL), lambda b, i: (b, 0)),
                  pl.BlockSpec((mem_len, D_MODEL), lambda b, i: (b, 0))],
        out_specs=pl.BlockSpec((ts, D_MODEL), lambda b, i: (b * nt + i, 0)),
        out_shape=jax.ShapeDtypeStruct((batch * seq, D_MODEL), out_dtype),
        compiler_params=pltpu.CompilerParams(
            dimension_semantics=("parallel", "parallel"),
            vmem_limit_bytes=VMEM_LIMIT_BYTES),
    )(q, k, v)


MOE_FF = D_MODEL // 4
MOE_TB = 256
MOE_LANE0 = MOE_GROUPS
ROUTER_LANES = 128
META_E, META_RANK, META_GATE = 0, 2, 4
HALF_D = D_MODEL // 2
ROW_DMA_UNROLL = 8


def _router_kernel(x_ref, g_ref, wr_ref, hp_ref, meta_ref, cnt_ref, carry_ref):
    tm = x_ref.shape[0]

    @pl.when(pl.program_id(0) == 0)
    def _():
        carry_ref[...] = jnp.zeros_like(carry_ref)

    x = x_ref[...].astype(F32)
    hn = x * lax.rsqrt(jnp.mean(x * x, axis=-1, keepdims=True) + EPS) * g_ref[...]
    hb = hn.astype(BF16)
    logits = jnp.dot(hb, wr_ref[...].astype(BF16), preferred_element_type=F32)
    bits = lax.bitcast_convert_type(hb.astype(F32), jnp.uint32)
    hp_ref[...] = (bits[:, :HALF_D] >> 16) | (bits[:, HALF_D:] & jnp.uint32(0xFFFF0000))

    lane = lax.broadcasted_iota(jnp.int32, (tm, ROUTER_LANES), 1).astype(F32)
    ninf = -jnp.inf
    first = lambda hit: jnp.min(jnp.where(hit, lane, float(ROUTER_LANES)), axis=-1, keepdims=True)
    gl = jnp.where(lane < MOE_GROUPS, logits, ninf)
    gmax = jnp.max(gl, axis=-1, keepdims=True)
    gidx = first(gl == gmax)
    g_gate = 1.0 / jnp.sum(jnp.exp(gl - gmax), axis=-1, keepdims=True)
    off = lane - (MOE_LANE0 + MOE_PER_GROUP * gidx)
    el = jnp.where(jnp.abs(2.0 * off - (MOE_PER_GROUP - 1)) < MOE_PER_GROUP, logits, ninf)
    v1 = jnp.max(el, axis=-1, keepdims=True)
    l1 = first(el == v1)
    el2 = jnp.where(lane == l1, ninf, el)
    v2 = jnp.max(el2, axis=-1, keepdims=True)
    l2 = first(el2 == v2)
    t = jnp.exp(v2 - v1)
    gate1 = g_gate / (1.0 + t)
    gate2 = g_gate * t / (1.0 + t)

    oh = jnp.where(lane == l1, 1.0, 0.0) + jnp.where(lane == l2, 1.0, 0.0)
    ti = lax.broadcasted_iota(jnp.int32, (tm, tm), 0)
    si = lax.broadcasted_iota(jnp.int32, (tm, tm), 1)
    before = jnp.where(si < ti, 1.0, 0.0).astype(BF16)
    base = jnp.dot(before, oh.astype(BF16), preferred_element_type=F32) + carry_ref[0:1, :]
    rank1 = jnp.sum(jnp.where(lane == l1, base, 0.0), axis=-1, keepdims=True)
    rank2 = jnp.sum(jnp.where(lane == l2, base, 0.0), axis=-1, keepdims=True)
    carry_ref[...] = carry_ref[...] + jnp.sum(oh, axis=0, keepdims=True)
    cnt_ref[...] = carry_ref[...]

    meta = jnp.zeros((tm, ROUTER_LANES), F32)
    for ln, val in ((META_E, l1 - MOE_LANE0), (META_E + 1, l2 - MOE_LANE0), (META_RANK, rank1),
                    (META_RANK + 1, rank2), (META_GATE, gate1), (META_GATE + 1, gate2)):
        meta = jnp.where(lane == ln, val, meta)
    meta_ref[...] = meta


def _router_pallas(h, gain, w_rg, w_re, *, tm=512):
    n = h.shape[0]
    wr = jnp.concatenate([w_rg, w_re], axis=1)
    wr = jnp.pad(wr, ((0, 0), (0, ROUTER_LANES - wr.shape[1])))
    return pl.pallas_call(
        _router_kernel,
        name="moe_router",
        grid=(n // tm,),
        in_specs=[pl.BlockSpec((tm, D_MODEL), lambda i: (i, 0)),
                  pl.BlockSpec((1, D_MODEL), lambda i: (0, 0)),
                  pl.BlockSpec((D_MODEL, ROUTER_LANES), lambda i: (0, 0))],
        out_specs=[pl.BlockSpec((tm, HALF_D), lambda i: (i, 0)),
                   pl.BlockSpec((tm, ROUTER_LANES), lambda i: (i, 0)),
                   pl.BlockSpec((8, ROUTER_LANES), lambda i: (0, 0))],
        out_shape=[jax.ShapeDtypeStruct((n, HALF_D), jnp.uint32),
                   jax.ShapeDtypeStruct((n, ROUTER_LANES), F32),
                   jax.ShapeDtypeStruct((8, ROUTER_LANES), F32)],
        scratch_shapes=[pltpu.VMEM((8, ROUTER_LANES), F32)],
        compiler_params=pltpu.CompilerParams(
            dimension_semantics=("arbitrary",),
            vmem_limit_bytes=VMEM_LIMIT_BYTES),
    )(h, gain.reshape(1, D_MODEL).astype(F32), wr)


def _dispatch_kernel(dest_ref, hp_ref, xs_in_ref, xs_ref, sem):
    del xs_in_ref
    tc = hp_ref.shape[0]
    base = pl.program_id(0) * tc * MOE_TOPK

    def row_copy(r, k):
        return pltpu.make_async_copy(hp_ref.at[pl.ds(r, 1)],
                                     xs_ref.at[pl.ds(dest_ref[base + MOE_TOPK * r + k], 1)], sem)

    def start(r, c):
        for k in range(MOE_TOPK):
            row_copy(r, k).start()
        return c

    def wait(r, c):
        for k in range(MOE_TOPK):
            row_copy(r, k).wait()
        return c

    lax.fori_loop(0, tc, start, 0, unroll=ROW_DMA_UNROLL)
    lax.fori_loop(0, tc, wait, 0, unroll=ROW_DMA_UNROLL)


def _dispatch_pallas(hp, dest, n_slots, *, tc=256):
    n = hp.shape[0]
    xs0 = jnp.zeros((n_slots, HALF_D), jnp.uint32)
    return pl.pallas_call(
        _dispatch_kernel,
        name="moe_dispatch",
        grid_spec=pltpu.PrefetchScalarGridSpec(
            num_scalar_prefetch=1,
            grid=(n // tc,),
            in_specs=[pl.BlockSpec((tc, HALF_D), lambda i, d: (i, 0)),
                      pl.BlockSpec(memory_space=pl.ANY)],
            out_specs=pl.BlockSpec(memory_space=pl.ANY),
            scratch_shapes=[pltpu.SemaphoreType.DMA(())]),
        out_shape=jax.ShapeDtypeStruct((n_slots, HALF_D), jnp.uint32),
        input_output_aliases={2: 0},
        compiler_params=pltpu.CompilerParams(
            dimension_semantics=("arbitrary",),
            vmem_limit_bytes=VMEM_LIMIT_BYTES),
    )(dest, hp, xs0)


MOE_WSLOTS = 3


def _expert_kernel(be_ref, first_ref, slot_ref, nxt1_ref, nxt2_ref, nu_ref, x_ref, w1_hbm, w3_hbm, w2_hbm,
                   y_ref, f1_ref, f3_ref, f2_ref, b1_ref, b3_ref, b2_ref, sem, *, layer):
    i = pl.program_id(0)

    def weight_copies(e, slot):
        return [pltpu.make_async_copy(src.at[layer, e], dst.at[slot], sem.at[slot, n])
                for n, (src, dst) in enumerate(((w1_hbm, f1_ref), (w3_hbm, f3_ref), (w2_hbm, f2_ref)))]

    def start_into(e, slot):
        @pl.when(e >= 0)
        def _():
            for c in weight_copies(e, slot):
                c.start()

    @pl.when(i < nu_ref[0])
    def _():
        slot = slot_ref[i]

        @pl.when(first_ref[i] == 1)
        def _():
            @pl.when(i == 0)
            def _():
                start_into(be_ref[i], slot)
                start_into(nxt1_ref[i], lax.rem(slot + 1, MOE_WSLOTS))

            for c in weight_copies(be_ref[i], slot):
                c.wait()
            start_into(nxt2_ref[i], lax.rem(slot + 2, MOE_WSLOTS))

            b1_ref[...] = f1_ref[slot].astype(BF16)
            b3_ref[...] = f3_ref[slot].astype(BF16)
            b2_ref[...] = f2_ref[slot].astype(BF16)

        xp = x_ref[...]
        x_lo = lax.bitcast_convert_type(xp << 16, F32).astype(BF16)
        x_hi = lax.bitcast_convert_type(xp & jnp.uint32(0xFFFF0000), F32).astype(BF16)

        def up(w_ref):
            return (jnp.dot(x_lo, w_ref[:HALF_D, :], preferred_element_type=F32)
                    + jnp.dot(x_hi, w_ref[HALF_D:, :], preferred_element_type=F32))

        a = _silu(up(b1_ref)) * up(b3_ref)
        y_ref[...] = jnp.dot(a.astype(BF16), b2_ref[...], preferred_element_type=F32).astype(y_ref.dtype)

    @pl.when(i >= nu_ref[0])
    def _():
        y_ref[...] = jnp.zeros_like(y_ref)


def _expert_pallas(xs, nblk, w1, w3, w2, layer):
    n_slots = xs.shape[0]
    nb = n_slots // MOE_TB
    bend = jnp.cumsum(nblk)
    blocks = jnp.arange(nb, dtype=jnp.int32)
    block_e = jnp.minimum(jnp.searchsorted(bend, blocks, side='right'), MOE_EXPERTS - 1).astype(jnp.int32)
    first = (blocks == (bend - nblk)[block_e]).astype(jnp.int32)
    nonempty = nblk > 0
    slot_e = (jnp.cumsum(nonempty) - 1) % MOE_WSLOTS
    ids = jnp.where(nonempty, jnp.arange(MOE_EXPERTS), MOE_EXPERTS)
    pad = jnp.full((2,), MOE_EXPERTS, ids.dtype)
    after = jnp.concatenate([lax.cummin(ids, reverse=True), pad])
    nxt1 = after[1:MOE_EXPERTS + 1]
    nxt2 = jnp.concatenate([after, pad[:1]])[jnp.minimum(nxt1, MOE_EXPERTS) + 1]
    as_id = lambda e: jnp.where(e < MOE_EXPERTS, e, -1).astype(jnp.int32)
    n_used = bend[-1:].astype(jnp.int32)
    blk = lambda i, *s: jnp.minimum(i, s[-1][0] - 1)
    hbm = pl.BlockSpec(memory_space=pl.ANY)
    return pl.pallas_call(
        functools.partial(_expert_kernel, layer=layer),
        name="moe_experts",
        grid_spec=pltpu.PrefetchScalarGridSpec(
            num_scalar_prefetch=6,
            grid=(nb,),
            in_specs=[pl.BlockSpec((MOE_TB, HALF_D), lambda i, *s: (blk(i, *s), 0)), hbm, hbm, hbm],
            out_specs=pl.BlockSpec((MOE_TB, D_MODEL), lambda i, *s: (i, 0)),
            scratch_shapes=[pltpu.VMEM((MOE_WSLOTS, D_MODEL, MOE_FF), F32),
                            pltpu.VMEM((MOE_WSLOTS, D_MODEL, MOE_FF), F32),
                            pltpu.VMEM((MOE_WSLOTS, MOE_FF, D_MODEL), F32),
                            pltpu.VMEM((D_MODEL, MOE_FF), BF16), pltpu.VMEM((D_MODEL, MOE_FF), BF16),
                            pltpu.VMEM((MOE_FF, D_MODEL), BF16),
                            pltpu.SemaphoreType.DMA((MOE_WSLOTS, 3))]),
        out_shape=jax.ShapeDtypeStruct((n_slots, D_MODEL), F32),
        compiler_params=pltpu.CompilerParams(
            dimension_semantics=("arbitrary",),
            vmem_limit_bytes=VMEM_LIMIT_BYTES),
    )(block_e, first, slot_e[block_e].astype(jnp.int32), as_id(nxt1)[block_e], as_id(nxt2)[block_e], n_used,
      xs, w1, w3, w2)


def _combine_kernel(dest_ref, h_ref, meta_ref, g_ref, yb_ref, o_ref, buf_ref, sem, *, final_norm):
    tc = h_ref.shape[0]
    base = pl.program_id(0) * tc * MOE_TOPK

    def row_copy(r, k):
        return pltpu.make_async_copy(yb_ref.at[pl.ds(dest_ref[base + MOE_TOPK * r + k], 1)],
                                     buf_ref.at[k, pl.ds(r, 1)], sem)

    def start(r, c):
        for k in range(MOE_TOPK):
            row_copy(r, k).start()
        return c

    def wait(r, c):
        for k in range(MOE_TOPK):
            row_copy(r, k).wait()
        return c

    lax.fori_loop(0, tc, start, 0, unroll=ROW_DMA_UNROLL)
    lax.fori_loop(0, tc, wait, 0, unroll=ROW_DMA_UNROLL)
    meta = meta_ref[...]
    out = h_ref[...]
    y = jnp.zeros_like(out)
    for k in range(MOE_TOPK):
        y = y + buf_ref[k] * meta[:, META_GATE + k:META_GATE + k + 1]
    out = out + y
    if final_norm:
        out = out * lax.rsqrt(jnp.mean(out * out, axis=-1, keepdims=True) + EPS) * g_ref[...]
    o_ref[...] = out


def _combine_pallas(h, meta, dest, yb, final_gain, *, final_norm, tc=256):
    n = h.shape[0]
    return pl.pallas_call(
        functools.partial(_combine_kernel, final_norm=final_norm),
        name="moe_combine",
        grid_spec=pltpu.PrefetchScalarGridSpec(
            num_scalar_prefetch=1,
            grid=(n // tc,),
            in_specs=[pl.BlockSpec((tc, D_MODEL), lambda i, d: (i, 0)),
                      pl.BlockSpec((tc, ROUTER_LANES), lambda i, d: (i, 0)),
                      pl.BlockSpec((1, D_MODEL), lambda i, d: (0, 0)),
                      pl.BlockSpec(memory_space=pl.ANY)],
            out_specs=pl.BlockSpec((tc, D_MODEL), lambda i, d: (i, 0)),
            scratch_shapes=[pltpu.VMEM((MOE_TOPK, tc, D_MODEL), F32), pltpu.SemaphoreType.DMA(())]),
        out_shape=jax.ShapeDtypeStruct((n, D_MODEL), F32),
        compiler_params=pltpu.CompilerParams(
            dimension_semantics=("arbitrary",),
            vmem_limit_bytes=VMEM_LIMIT_BYTES),
    )(dest, h, meta, final_gain.reshape(1, D_MODEL).astype(F32), yb)


def _moe_pallas(h, gain, w_rg, w_re, w1, w3, w2, layer, final_gain, *, final_norm):
    n = h.shape[0]
    nb = (n * MOE_TOPK) // MOE_TB + MOE_EXPERTS
    hp, meta, cnt = _router_pallas(h, gain, w_rg, w_re)
    expert = meta[:, META_E:META_E + MOE_TOPK].astype(jnp.int32)
    rank = meta[:, META_RANK:META_RANK + MOE_TOPK].astype(jnp.int32)
    counts = cnt[0, MOE_LANE0:MOE_LANE0 + MOE_EXPERTS].astype(jnp.int32)
    nblk = (counts + MOE_TB - 1) // MOE_TB
    dest = ((jnp.cumsum(nblk) - nblk)[expert] * MOE_TB + rank).reshape(n * MOE_TOPK)
    xs = _dispatch_pallas(hp, dest, nb * MOE_TB)
    yb = _expert_pallas(xs, nblk, w1, w3, w2, layer)
    return _combine_pallas(h, meta, dest, yb, final_gain, final_norm=final_norm)


N_GATES = 4 * ML_HEADS
IN_GROUPS_BEFORE_GATES = 4
IN_GROUPS = 13
LANE = 128


def _win_relayout_kernel(*refs):
    parts, main_ref = refs[:-1], refs[-1]
    n = pl.program_id(1)
    wide = jnp.concatenate([p[...] for p in parts], axis=1)

    @pl.when(n < IN_GROUPS_BEFORE_GATES)
    def _():
        main_ref[...] = wide[:, :GROUP_W]

    @pl.when(n >= IN_GROUPS_BEFORE_GATES)
    def _():
        main_ref[...] = wide[:, N_GATES:N_GATES + GROUP_W]


def _win_relayout(w_in):
    depth, k, _ = w_in.shape
    per = GROUP_W // LANE
    g0 = IN_GROUPS_BEFORE_GATES * GROUP_W
    gate = w_in[:, :, g0:g0 + N_GATES].reshape(depth, k, 4, ML_HEADS).transpose(0, 1, 3, 2)
    gate = jnp.pad(gate, ((0, 0), (0, 0), (0, 0), (0, LANE - 4))).reshape(depth, k, ML_HEADS * LANE)
    w_b = w_in.astype(BF16)
    part = lambda t: pl.BlockSpec((None, k, LANE), lambda l, n, t=t: (l, 0, per * n + t))
    main = pl.pallas_call(
        _win_relayout_kernel,
        name="w_in_relayout",
        grid=(depth, IN_GROUPS),
        in_specs=[part(t) for t in range(per + 1)],
        out_specs=pl.BlockSpec((None, k, GROUP_W), lambda l, n: (l, 0, n)),
        out_shape=jax.ShapeDtypeStruct((depth, k, IN_GROUPS * GROUP_W), BF16),
        compiler_params=pltpu.CompilerParams(
            dimension_semantics=("parallel", "parallel"),
            vmem_limit_bytes=VMEM_LIMIT_BYTES),
    )(*([w_b] * (per + 1)))
    return main, gate


COL_MLSTM, COL_HGRN, COL_FNET, COL_NA = 0, 4 * GROUP_W, 9 * GROUP_W, 10 * GROUP_W


def kernel(x, mem, norm_mix, norm_cross, norm_ffn, norm_final, norm_mem, w_in, mlstm_conv,
           mlstm_gate_bias, hgrn_lower_bound, na_rpb, group_gain, w_out, xa_wq, xa_wk, xa_wv, xa_wo,
           moe_router_group, moe_router_expert, moe_w1, moe_w3, moe_w2):
    b, s, d = x.shape
    mem_len = mem.shape[1]
    h = x.reshape(b * s, d)
    mem_f = mem.reshape(b * mem_len, d)
    lbs = jnp.cumsum(jax.nn.softmax(hgrn_lower_bound.astype(F32), axis=0), axis=0)
    lbs = lbs - lbs[0]
    fnet_tables = _fnet_tables(s)
    w_main, w_gate = _win_relayout(w_in)
    w_out_b, wq_b, wk_b, wv_b, wo_b = (w.astype(BF16) for w in (w_out, xa_wq, xa_wk, xa_wv, xa_wo))
    for l in range(DEPTH):
        proj, gates = _matmul(h, w_main, l, name="in_proj", gain=norm_mix[l], w_side=w_gate, out_dtype=BF16)
        y_ml = _mlstm_pallas(proj, gates, mlstm_conv[l], mlstm_gate_bias[l], batch=b, seq=s,
                             out_dtype=BF16)
        y_hg = _hgrn_pallas(proj, lbs[l], batch=b, seq=s, col0=COL_HGRN, out_dtype=BF16)
        y_fn = _fnet_pallas(proj, fnet_tables, batch=b, seq=s, col0=COL_FNET, out_dtype=BF16)
        y_na = _na_pallas(proj, na_rpb[l], batch=b, seq=s, col0=COL_NA, out_dtype=BF16)
        h = _matmul([y_ml, y_hg, y_fn, y_na], w_out_b, l, name="out_proj", gain=group_gain[l], residual=h)

        q = _matmul(h, wq_b, l, name="xa_q_proj", gain=norm_cross[l], out_dtype=BF16)
        k = _matmul(mem_f, wk_b, l, name="xa_k_proj", gain=norm_mem, out_dtype=BF16)
        v = _matmul(mem_f, wv_b, l, name="xa_v_proj", gain=norm_mem, out_dtype=BF16)
        o = _xattn_pallas(q, k, v, batch=b, seq=s, mem_len=mem_len)
        h = _matmul(o, wo_b, l, name="xa_o_proj", residual=h)

        h = _moe_pallas(h, norm_ffn[l], moe_router_group[l], moe_router_expert[l],
                        moe_w1, moe_w3, moe_w2, l, norm_final, final_norm=(l == DEPTH - 1))
    return h.reshape(b, s, d)
```

```python
import functools

import jax
import jax.numpy as jnp
import numpy as np
from jax import lax
from jax.experimental import pallas as pl
from jax.experimental.pallas import tpu as pltpu

D_MODEL = 2048
DEPTH = 4
GRID_W = 64
N_MIXERS = 4
GROUP_W = D_MODEL // N_MIXERS
ML_HEADS = 4
ML_DH = GROUP_W // ML_HEADS
ML_CHUNK = 64
HG_HEADS = 4
HG_CHUNK = 16
FN_GROUPS = 4
FN_CH = GROUP_W // FN_GROUPS
NA_HEADS = 8
NA_DH = GROUP_W // NA_HEADS
NA_KH = 8
NA_KW = 16
XA_HEADS = 4
XA_DH = D_MODEL // XA_HEADS
MOE_GROUPS = 4
MOE_PER_GROUP = 8
MOE_EXPERTS = MOE_GROUPS * MOE_PER_GROUP
MOE_TOPK = 2
MOE_BLOCK = 128
EPS = 1e-6
F32 = jnp.float32
BF16 = jnp.bfloat16

VMEM_LIMIT_BYTES = 56 * 1024 * 1024


def _mm_kernel(*refs, n_x, norm, residual, side):
    refs = list(refs)
    x_refs = [refs.pop(0) for _ in range(n_x)]
    g_ref = refs.pop(0) if norm else None
    w_ref = refs.pop(0)
    ws_ref = refs.pop(0) if side else None
    r_ref = refs.pop(0) if residual else None
    o_ref = refs.pop(0)
    os_ref = refs.pop(0) if side else None
    xn_ref = refs.pop(0)

    @pl.when(pl.program_id(1) == 0)
    def _():
        col = 0
        for x_ref in x_refs:
            x = x_ref[...].astype(F32)
            kx = x.shape[1]
            if norm:
                ms = jnp.mean(x * x, axis=-1, keepdims=True)
                x = x * lax.rsqrt(ms + EPS) * g_ref[:, col:col + kx]
            xn_ref[:, col:col + kx] = x.astype(BF16)
            col += kx
        if side:
            os_ref[...] = jnp.dot(xn_ref[...], ws_ref[...].astype(BF16),
                                  preferred_element_type=F32).astype(os_ref.dtype)

    acc = jnp.dot(xn_ref[...], w_ref[...].astype(BF16), preferred_element_type=F32)
    if residual:
        acc = acc + r_ref[...]
    o_ref[...] = acc.astype(o_ref.dtype)


def _matmul(xs, w, layer, *, name, gain=None, residual=None, w_side=None, tm=1024, tn=512, out_dtype=F32):
    if not isinstance(xs, (list, tuple)):
        xs = [xs]
    m = xs[0].shape[0]
    k = sum(x.shape[1] for x in xs)
    n = w.shape[2]
    tm = min(tm, m)
    tn = min(tn, n)
    assert m % tm == 0 and n % tn == 0 and w.shape[1] == k, (m, n, k, tm, tn)
    norm = gain is not None
    has_res = residual is not None
    side = w_side is not None
    in_specs = [pl.BlockSpec((tm, x.shape[1]), lambda i, j: (i, 0)) for x in xs]
    args = list(xs)
    if norm:
        in_specs.append(pl.BlockSpec((1, k), lambda i, j: (0, 0)))
        args.append(gain.reshape(1, k).astype(F32))
    in_specs.append(pl.BlockSpec((None, k, tn), lambda i, j: (layer, 0, j)))
    args.append(w)
    if side:
        in_specs.append(pl.BlockSpec((None, k, w_side.shape[2]), lambda i, j: (layer, 0, 0)))
        args.append(w_side)
    if has_res:
        in_specs.append(pl.BlockSpec((tm, tn), lambda i, j: (i, j)))
        args.append(residual)
    out_specs = pl.BlockSpec((tm, tn), lambda i, j: (i, j))
    out_shape = jax.ShapeDtypeStruct((m, n), out_dtype)
    if side:
        out_specs = [out_specs, pl.BlockSpec((tm, w_side.shape[2]), lambda i, j: (i, 0))]
        out_shape = [out_shape, jax.ShapeDtypeStruct((m, w_side.shape[2]), F32)]
    return pl.pallas_call(
        functools.partial(_mm_kernel, n_x=len(xs), norm=norm, residual=has_res, side=side),
        name=name,
        grid=(m // tm, n // tn),
        in_specs=in_specs,
        out_specs=out_specs,
        out_shape=out_shape,
        scratch_shapes=[pltpu.VMEM((tm, k), BF16)],
        compiler_params=pltpu.CompilerParams(
            dimension_semantics=("parallel", "arbitrary"),
            vmem_limit_bytes=VMEM_LIMIT_BYTES),
    )(*args)


NA_QROWS = 4
NA_WROWS = 12


def _na_groups(rows):
    tables, plan = [], []
    for r0 in range(0, rows, NA_QROWS):
        band0 = lambda r: min(max(r - NA_KH // 2, 0), rows - NA_KH)
        kr0 = min(band0(r0), rows - NA_WROWS)
        assert band0(r0 + NA_QROWS - 1) + NA_KH <= kr0 + NA_WROWS
        dr = np.full((NA_QROWS, NA_WROWS), -1, np.int64)
        for i in range(NA_QROWS):
            for j in range(NA_WROWS):
                if 0 <= kr0 + j - band0(r0 + i) < NA_KH:
                    dr[i, j] = kr0 + j - (r0 + i) + NA_KH - 1
        key = dr.tobytes()
        if key not in [t.tobytes() for t in tables]:
            tables.append(dr)
        plan.append((r0, kr0, [t.tobytes() for t in tables].index(key)))
    return plan, np.stack(tables)


def _na_bias_tables(rpb, dr_tables):
    c = np.arange(GRID_W)
    dc = np.clip(c[None, :] - c[:, None] + NA_KW - 1, 0, 2 * NA_KW - 2)
    onehot = (dc[None] == np.arange(2 * NA_KW - 1)[:, None, None]).astype(np.float32)
    col_start = np.clip(c - NA_KW // 2, 0, GRID_W - NA_KW)
    col_ok = (c[None, :] >= col_start[:, None]) & (c[None, :] < col_start[:, None] + NA_KW)
    t = jnp.einsum('hrd,dqk->hrqk', rpb.astype(F32), onehot, precision=lax.Precision.HIGHEST)
    t = jnp.where(col_ok[None, None], t, -jnp.inf)
    t = jnp.concatenate([t, jnp.full_like(t[:, :1], -jnp.inf)], axis=1)
    idx = np.where(dr_tables < 0, t.shape[1] - 1, dr_tables)
    pairs = idx.reshape(idx.shape[0], idx.shape[1], -1, 2)
    uniq = sorted({(int(a), int(b)) for a, b in pairs.reshape(-1, 2)})
    pair_idx = np.array([[[uniq.index((int(a), int(b))) for a, b in row] for row in typ] for typ in pairs])
    table = jnp.stack([jnp.concatenate([t[:, a], t[:, b]], axis=-1) for a, b in uniq], axis=1)
    return table, pair_idx


def _na_kernel(q_ref, k_ref, v_ref, tab_ref, o_ref, bias_ref, *, plan, pair_idx):
    nq = NA_QROWS * GRID_W
    nk = NA_WROWS * GRID_W
    for hh in range(2):
        for typ in range(pair_idx.shape[0]):
            for i in range(pair_idx.shape[1]):
                for jj in range(pair_idx.shape[2]):
                    bias_ref[hh, typ, i * GRID_W:(i + 1) * GRID_W, jj * 2 * GRID_W:(jj + 1) * 2 * GRID_W] = (
                        tab_ref[hh, int(pair_idx[typ, i, jj])])
    for r0, kr0, typ in plan:
        q = q_ref[r0 * GRID_W:r0 * GRID_W + nq, :].astype(F32) * (NA_DH ** -0.5)
        kb = k_ref[kr0 * GRID_W:kr0 * GRID_W + nk, :]
        vb = v_ref[kr0 * GRID_W:kr0 * GRID_W + nk, :]
        outs = []
        for hh in range(2):
            sl = slice(hh * NA_DH, (hh + 1) * NA_DH)
            s = lax.dot_general(q[:, sl].astype(BF16), kb[:, sl].astype(BF16),
                                (((1,), (1,)), ((), ())), preferred_element_type=F32)
            s = s + bias_ref[hh, typ]
            m = jnp.max(s, axis=-1, keepdims=True)
            p = jnp.exp(s - m)
            l = jnp.sum(p, axis=-1, keepdims=True)
            o = jnp.dot(p.astype(BF16), vb[:, sl].astype(BF16), preferred_element_type=F32)
            outs.append(o / l)
        o_ref[r0 * GRID_W:r0 * GRID_W + nq, :] = jnp.concatenate(outs, axis=-1).astype(o_ref.dtype)


def _na_pallas(proj, rpb, *, batch, seq, col0, out_dtype=F32):
    rows = seq // GRID_W
    pair_w = 2 * NA_DH
    cb = col0 // pair_w
    gb = GROUP_W // pair_w
    plan, dr_tables = _na_groups(rows)
    table, pair_idx = _na_bias_tables(rpb, dr_tables)
    spec = lambda g: pl.BlockSpec((seq, pair_w), lambda b, p, g=g: (b, cb + g * gb + p))
    return pl.pallas_call(
        functools.partial(_na_kernel, plan=plan, pair_idx=pair_idx),
        name="na_attention",
        grid=(batch, NA_HEADS // 2),
        in_specs=[spec(0), spec(1), spec(2),
                  pl.BlockSpec((2,) + table.shape[1:], lambda b, p: (p, 0, 0, 0))],
        out_specs=pl.BlockSpec((seq, pair_w), lambda b, p: (b, p)),
        out_shape=jax.ShapeDtypeStruct((batch * seq, GROUP_W), out_dtype),
        scratch_shapes=[pltpu.VMEM((2, dr_tables.shape[0], NA_QROWS * GRID_W, NA_WROWS * GRID_W), F32)],
        compiler_params=pltpu.CompilerParams(
            dimension_semantics=("parallel", "parallel"),
            vmem_limit_bytes=VMEM_LIMIT_BYTES),
    )(proj, proj, proj, table)


ML_L = 128
ML_GATE_LANES = 128


def _log_sigmoid(x):
    return jnp.minimum(x, 0.0) - jnp.log(1.0 + jnp.exp(-jnp.abs(x)))


def _silu(x):
    return x * jax.nn.sigmoid(x)


_NT = (((1,), (1,)), ((), ()))
_TN = (((0,), (0,)), ((), ()))


def _split3(x):
    hi = x.astype(BF16)
    r1 = x - hi.astype(F32)
    mid = r1.astype(BF16)
    lo = (r1 - mid.astype(F32)).astype(BF16)
    return hi, mid, lo


def _dot_exact_lhs(a, x, dims=None):
    a = a.astype(BF16)
    if dims is None:
        return sum(jnp.dot(a, p, preferred_element_type=F32) for p in _split3(x))
    return sum(lax.dot_general(a, p, dims, preferred_element_type=F32) for p in _split3(x))


def _dot_exact_rhs(x, a, dims):
    a = a.astype(BF16)
    return sum(lax.dot_general(p, a, dims, preferred_element_type=F32) for p in _split3(x))


ML_GROUP = 4
ML_PAD = 16


def _mlstm_local(q, k, v, gcol, tri, mask_w, gi, end_row):
    L = ML_L
    cum_col = _dot_exact_lhs(tri, _log_sigmoid(gcol))
    wide = lambda col: jnp.broadcast_to(col, (L, L))
    b_cols, ig_cols, rows = [], [], []
    for g in range(len(q)):
        lanes = slice(g * ML_GATE_LANES, (g + 1) * ML_GATE_LANES)
        b_cols.append(cum_col[:, g * ML_GATE_LANES + gi + 1:g * ML_GATE_LANES + gi + 2])
        ig_cols.append(gcol[:, g * ML_GATE_LANES + gi:g * ML_GATE_LANES + gi + 1])
        rows.append(gcol[:, lanes].T[gi:gi + 1, :] - cum_col[:, lanes].T[gi + 1:gi + 2, :])
    b_w = jnp.concatenate([wide(b) for b in b_cols], axis=1)
    row_w = jnp.concatenate(rows, axis=1)
    dmat = jnp.where(mask_w != 0.0, b_w + row_w, -jnp.inf)
    m_in_w = jnp.concatenate([wide(jnp.max(dmat[:, g * L:(g + 1) * L], axis=-1, keepdims=True))
                              for g in range(len(q))], axis=1)
    w = jnp.exp(dmat - m_in_w)
    ones = jnp.ones((L, ML_DH), BF16)
    out = []
    for g in range(len(q)):
        lanes = slice(g * L, (g + 1) * L)
        kb = k[g].astype(BF16)
        v_aug = jnp.concatenate([v[g].astype(BF16), ones], axis=1)
        qk = lax.dot_general(q[g].astype(BF16), kb, _NT, preferred_element_type=F32) * w[:, lanes]
        res = jnp.dot(qk.astype(BF16), v_aug, preferred_element_type=F32)
        b_end = b_cols[g][end_row:end_row + 1, :]
        a_col = b_end - b_cols[g] + ig_cols[g]
        m_loc = jnp.max(a_col, axis=0, keepdims=True)
        kw = k[g] * jnp.exp(a_col - m_loc)
        st = lax.dot_general(v_aug, kw.astype(BF16), _TN, preferred_element_type=F32)
        out.append(dict(m_in=m_in_w[:, lanes], num=res[:, :ML_DH], den=res[:, ML_DH:], b=b_w[:, lanes],
                        ct=st[:ML_DH], n=st[ML_DH:ML_DH + 1], m_loc=m_loc, b_end=b_end))
    return out


def _mlstm_carry(q, m_in, num_in, den_in, b, ct, n, m_prev):
    inter = b + m_prev
    m_t = jnp.maximum(inter, m_in)
    s_in = jnp.exp(m_in - m_t)
    s_inter = jnp.exp(inter - m_t)
    rhs = jnp.concatenate([ct.astype(BF16), jnp.broadcast_to(n, (ML_PAD, ML_DH)).astype(BF16)], axis=0)
    both = lax.dot_general(q.astype(BF16), rhs, _NT, preferred_element_type=F32)
    num = s_in * num_in + s_inter * both[:, :ML_DH]
    den = s_in * den_in + s_inter * both[:, ML_DH:ML_DH + 1]
    return num / jnp.maximum(jnp.abs(den), jnp.exp(-m_t))


def _mlstm_kernel(q_ref, k_ref, v_ref, o_ref, cwq_ref, cwk_ref, gcol_ref, bcol_ref,
                  y_ref, qs_ref, ks_ref, num_ref, den_ref, min_ref, b_ref, ctl_ref, stat_ref, *, seq):
    L = ML_L
    nc = seq // L
    t_idx = lax.broadcasted_iota(jnp.int32, (seq, 1), 0)

    def conv_silu(x, w):
        prev = jnp.where(t_idx == 0, 0.0, pltpu.roll(x, 1, axis=0))
        nxt = jnp.where(t_idx == seq - 1, 0.0, pltpu.roll(x, seq - 1, axis=0))
        return _silu(prev * w[0:1, :] + x * w[1:2, :] + nxt * w[2:3, :])

    qs_ref[...] = conv_silu(q_ref[...].astype(F32), cwq_ref[...])
    ks_ref[...] = conv_silu(k_ref[...].astype(F32), cwk_ref[...]) * (ML_DH ** -0.5)

    ti = lax.broadcasted_iota(jnp.int32, (L, L), 0)
    si = lax.broadcasted_iota(jnp.int32, (L, L), 1)
    tris = (jnp.where(si <= ti, 1.0, 0.0), jnp.where(si >= ti, 1.0, 0.0))
    masks_w = [jnp.concatenate([t] * ML_GROUP, axis=1) for t in tris]

    for direction in (0, 1):
        for c0 in range(0, nc, ML_GROUP):
            rows = [pl.ds((c0 + g) * L, L) for g in range(ML_GROUP)]
            gcol = jnp.concatenate([gcol_ref[r, :] + bcol_ref[...] for r in rows], axis=1)
            local = _mlstm_local([qs_ref[r, :] for r in rows], [ks_ref[r, :] for r in rows],
                                 [v_ref[r, :].astype(F32) for r in rows], gcol,
                                 tris[direction], masks_w[direction], 2 * direction,
                                 L - 1 if direction == 0 else 0)
            for g, (r, loc) in enumerate(zip(rows, local)):
                idx = direction * nc + c0 + g
                num_ref[direction, r, :] = loc["num"]
                den_ref[direction, r, :] = loc["den"]
                min_ref[direction, r, :] = loc["m_in"]
                b_ref[direction, r, :] = loc["b"]
                ctl_ref[idx] = loc["ct"]
                stat_ref[idx, 0:1, :] = loc["n"]
                stat_ref[idx, 1:2, :] = jnp.broadcast_to(loc["m_loc"], (1, ML_DH))
                stat_ref[idx, 2:3, :] = jnp.broadcast_to(loc["b_end"], (1, ML_DH))

    state = [(jnp.zeros((ML_DH, ML_DH), F32), jnp.zeros((1, ML_DH), F32), jnp.zeros((1, ML_DH), F32))] * 2
    for i in range(nc):
        for direction in (0, 1):
            c = i if direction == 0 else nc - 1 - i
            r = pl.ds(c * L, L)
            idx = direction * nc + c
            ct, n, m = state[direction]
            num_ref[direction, r, :] = _mlstm_carry(qs_ref[r, :], min_ref[direction, r, :],
                                                    num_ref[direction, r, :], den_ref[direction, r, :],
                                                    b_ref[direction, r, :], ct, n, m)
            n_loc, m_loc, b_end = stat_ref[idx, 0:1, :], stat_ref[idx, 1:2, :], stat_ref[idx, 2:3, :]
            m_new = jnp.maximum(b_end + m, m_loc)
            s_old = jnp.exp(b_end + m - m_new)
            s_new = jnp.exp(m_loc - m_new)
            state[direction] = (s_old * ct + s_new * ctl_ref[idx], s_old * n + s_new * n_loc, m_new)
    y_ref[...] = (jax.nn.sigmoid(o_ref[...].astype(F32)) * (num_ref[0] + num_ref[1])).astype(y_ref.dtype)


def _mlstm_pallas(proj, gates, conv_w, gate_b, *, batch, seq, out_dtype=F32):
    d = ML_DH
    hb = GROUP_W // d
    gb4 = gate_b.astype(F32).reshape(4, ML_HEADS).T
    gb_col = jnp.pad(gb4, ((0, 0), (0, ML_GATE_LANES - 4))).reshape(ML_HEADS, 1, ML_GATE_LANES)
    spec = lambda grp: pl.BlockSpec((seq, d), lambda b, h, grp=grp: (b, grp * hb + h))
    f32 = lambda *shape: pltpu.VMEM(shape, F32)
    return pl.pallas_call(
        functools.partial(_mlstm_kernel, seq=seq),
        name="mlstm",
        grid=(batch, ML_HEADS),
        in_specs=[spec(0), spec(1), spec(2), spec(3),
                  pl.BlockSpec((3, d), lambda b, h: (0, h)),
                  pl.BlockSpec((3, d), lambda b, h: (0, hb + h)),
                  pl.BlockSpec((seq, ML_GATE_LANES), lambda b, h: (b, h)),
                  pl.BlockSpec((None, 1, ML_GATE_LANES), lambda b, h: (h, 0, 0))],
        out_specs=pl.BlockSpec((seq, d), lambda b, h: (b, h)),
        out_shape=jax.ShapeDtypeStruct((batch * seq, GROUP_W), out_dtype),
        scratch_shapes=[f32(seq, d), f32(seq, d), f32(2, seq, d), f32(2, seq, d), f32(2, seq, d),
                        f32(2, seq, d), f32(2 * (seq // ML_L), d, d), f32(2 * (seq // ML_L), 8, d)],
        compiler_params=pltpu.CompilerParams(
            dimension_semantics=("parallel", "parallel"),
            vmem_limit_bytes=VMEM_LIMIT_BYTES),
    )(proj, proj, proj, proj, conv_w.astype(F32), conv_w.astype(F32), gates, gb_col)


HG_L = 128
HG_LEAF = 8


HG_GROUP = 4
HG_NLEAF = HG_L // HG_LEAF
HG_LEVELS = 4


def _hgrn_ref_rows(p_ref, chunks, backward):
    def rows(first, count, stride):
        return jnp.concatenate([p_ref[pl.ds(c * HG_L + first, count, stride=stride), :] for c in chunks], axis=1)

    leaf = rows(HG_LEAF // 2 if backward else HG_LEAF // 2 - 1, HG_NLEAF, HG_LEAF)
    levels = []
    m = HG_L // 2
    for _ in range(HG_LEVELS):
        blocks = HG_L // (2 * m)
        r = rows(m if backward else m - 1, blocks, 2 * m) if blocks > 1 else rows(m if backward else m - 1, 1, 1)
        levels.append(jnp.repeat(r, HG_NLEAF // blocks, axis=0))
        m //= 2
    return leaf, levels


def _hgrn_chunks(q, k, v, p, ref_leaf, ref_levels, masks):
    L = HG_L
    dk = HG_L
    groups = q.shape[1] // dk
    spread = lambda e: jnp.broadcast_to(e[:, None, :], (HG_NLEAF, HG_LEAF, e.shape[1])).reshape(L, e.shape[1])
    d = p - spread(ref_leaf)
    q_leaf = q * jnp.exp(d)
    k_leaf = k * jnp.exp(-d)
    pairs = [(q_leaf.astype(BF16), k_leaf.astype(BF16), masks[HG_LEVELS])]
    for li in range(HG_LEVELS):
        gap = ref_leaf - ref_levels[li]
        qh = q_leaf * spread(jnp.exp(jnp.minimum(gap, 0.0)))
        kh = k_leaf * spread(jnp.exp(jnp.minimum(-gap, 0.0)))
        pairs.append((qh.astype(BF16), kh.astype(BF16), masks[li]))
    vb = v.astype(BF16)
    outs = []
    for g in range(groups):
        lanes = slice(g * dk, (g + 1) * dk)
        attn = jnp.zeros((L, L), F32)
        for qh, kh, mask in pairs:
            s = lax.dot_general(qh[:, lanes], kh[:, lanes], _NT, preferred_element_type=F32)
            attn = attn + jnp.where(mask != 0.0, s, 0.0)
        outs.append(jnp.dot(attn.astype(BF16), vb[:, lanes], preferred_element_type=F32))
    return jnp.concatenate(outs, axis=1)


def _hgrn_carry(q, k, v, p, st, backward):
    L = HG_L
    o = lax.dot_general((q * jnp.exp(p)).astype(BF16), st.astype(BF16), _NT, preferred_element_type=F32)
    p_end = p[0:1, :] if backward else p[L - 1:L, :]
    kd = k * jnp.exp(p_end - p)
    st = st * jnp.exp(p_end) + lax.dot_general(v.astype(BF16), kd.astype(BF16), _TN,
                                               preferred_element_type=F32)
    return o, st


def _hgrn_masks(backward):
    L = HG_L
    ti = lax.broadcasted_iota(jnp.int32, (L, L), 0)
    si = lax.broadcasted_iota(jnp.int32, (L, L), 1)
    if backward:
        ti, si = si, ti
    one = lambda cond: jnp.where(cond, 1.0, 0.0)
    masks = []
    m = L // 2
    while m >= HG_LEAF:
        same = one((ti // (2 * m)) == (si // (2 * m)))
        masks.append(same * one((ti % (2 * m)) >= m) * one((si % (2 * m)) < m))
        m //= 2
    masks.append(one((ti // HG_LEAF) == (si // HG_LEAF)) * one(si <= ti))
    return masks


def _hgrn_kernel(q_ref, ff_ref, fb_ref, i_ref, g_ref, lb_ref, y_ref,
                 qs_ref, lf_ref, kf_ref, lbk_ref, kb_ref, of_ref, ob_ref, *, seq):
    L = HG_L
    nc = seq // L
    lb = lb_ref[...]
    log_lb = jnp.log(lb)
    log1m_lb = jnp.log1p(-lb)

    def forget(fp):
        ls = _log_sigmoid(fp)
        a = log_lb
        c = log1m_lb + ls
        logf = jnp.maximum(a, c) + jnp.log(1.0 + jnp.exp(-jnp.abs(a - c)))
        return logf, (1.0 - lb) * jnp.exp(ls - fp)

    qs_ref[...] = _silu(q_ref[...].astype(F32))
    lf_ref[...], kf_ref[...] = forget(ff_ref[...].astype(F32))
    lbk_ref[...], kb_ref[...] = forget(fb_ref[...].astype(F32))

    ti = lax.broadcasted_iota(jnp.int32, (L, L), 0)
    si = lax.broadcasted_iota(jnp.int32, (L, L), 1)
    tril = (si <= ti).astype(F32)
    triu = (si >= ti).astype(F32)
    masks_f = _hgrn_masks(False)
    masks_b = _hgrn_masks(True)

    def side_by_side(ref, c0):
        return jnp.concatenate([ref[pl.ds((c0 + g) * L, L), :].astype(F32) for g in range(HG_GROUP)], axis=1)

    def scatter_back(ref, c0, wide):
        for g in range(HG_GROUP):
            ref[pl.ds((c0 + g) * L, L), :] = wide[:, g * L:(g + 1) * L]

    directions = ((kf_ref, lf_ref, of_ref, tril, masks_f, False), (kb_ref, lbk_ref, ob_ref, triu, masks_b, True))
    for c0 in range(0, nc, HG_GROUP):
        for _, f_ref, _, tri, _, _ in directions:
            scatter_back(f_ref, c0, _dot_exact_lhs(tri, side_by_side(f_ref, c0)))
    for c0 in range(0, nc, HG_GROUP):
        q = side_by_side(qs_ref, c0)
        v = side_by_side(i_ref, c0)
        for k_ref, f_ref, o_ref, _, masks, backward in directions:
            ref_leaf, ref_levels = _hgrn_ref_rows(f_ref, range(c0, c0 + HG_GROUP), backward)
            scatter_back(o_ref, c0, _hgrn_chunks(q, side_by_side(k_ref, c0), v, side_by_side(f_ref, c0),
                                                 ref_leaf, ref_levels, masks))
    st_f = jnp.zeros((L, L), F32)
    st_b = jnp.zeros((L, L), F32)
    for i in range(nc):
        sl = pl.ds(i * L, L)
        o, st_f = _hgrn_carry(qs_ref[sl, :], kf_ref[sl, :], i_ref[sl, :].astype(F32), lf_ref[sl, :], st_f, False)
        of_ref[sl, :] += o
        sl = pl.ds((nc - 1 - i) * L, L)
        o, st_b = _hgrn_carry(qs_ref[sl, :], kb_ref[sl, :], i_ref[sl, :].astype(F32), lbk_ref[sl, :], st_b, True)
        ob_ref[sl, :] += o
    o = of_ref[...] + ob_ref[...]
    o = o * lax.rsqrt(jnp.mean(o * o, axis=-1, keepdims=True) + EPS)
    y_ref[...] = (o * _silu(g_ref[...].astype(F32))).astype(y_ref.dtype)


def _hgrn_pallas(proj, lb, *, batch, seq, col0, out_dtype=F32):
    d = GROUP_W // HG_HEADS
    hb = GROUP_W // d
    cb = col0 // d
    spec = lambda grp: pl.BlockSpec((seq, d), lambda b, h, grp=grp: (b, cb + grp * hb + h))
    f32 = lambda *shape: pltpu.VMEM(shape, F32)
    return pl.pallas_call(
        functools.partial(_hgrn_kernel, seq=seq),
        name="hgrn2",
        grid=(batch, HG_HEADS),
        in_specs=[spec(0), spec(1), spec(2), spec(3), spec(4),
                  pl.BlockSpec((1, d), lambda b, h: (0, h))],
        out_specs=pl.BlockSpec((seq, d), lambda b, h: (b, h)),
        out_shape=jax.ShapeDtypeStruct((batch * seq, GROUP_W), out_dtype),
        scratch_shapes=[f32(seq, d)] * 7,
        compiler_params=pltpu.CompilerParams(
            dimension_semantics=("parallel", "parallel"),
            vmem_limit_bytes=VMEM_LIMIT_BYTES),
    )(proj, proj, proj, proj, proj, lb.astype(F32).reshape(1, GROUP_W))


DFT_SPLIT = 32


def _dft_cos_sin(n):
    def direct(rows, period):
        k = (rows[:, None] * jnp.arange(n, dtype=jnp.int32)[None, :]) % period
        ang = k.astype(F32) * (2.0 * np.pi / period)
        return jnp.cos(ang), jnp.sin(ang)

    if n % DFT_SPLIT or n // DFT_SPLIT < DFT_SPLIT:
        return direct(jnp.arange(n, dtype=jnp.int32), n)
    ca, sa = direct(jnp.arange(n // DFT_SPLIT, dtype=jnp.int32), n // DFT_SPLIT)
    cb, sb = direct(jnp.arange(DFT_SPLIT, dtype=jnp.int32), n)
    cos = ca[:, None, :] * cb[None, :, :] - sa[:, None, :] * sb[None, :, :]
    sin = sa[:, None, :] * cb[None, :, :] + ca[:, None, :] * sb[None, :, :]
    return cos.reshape(n, n), sin.reshape(n, n)


def _fnet_tables(seq):
    cc, sc = _dft_cos_sin(FN_CH)
    eye = jnp.eye(FN_GROUPS, dtype=F32)
    chan = jnp.concatenate([jnp.kron(eye, cc), jnp.kron(eye, sc)], axis=1)
    cs, ss = _dft_cos_sin(seq)
    pos = jnp.concatenate([cs, -ss], axis=1) * ((seq * FN_CH) ** -0.5)
    return chan.astype(BF16), pos.astype(BF16)


def _fnet_chan_kernel(u_ref, dft_ref, v_ref):
    r = jnp.dot(u_ref[...].astype(BF16), dft_ref[...], preferred_element_type=F32)
    v_ref[0] = r[:, :GROUP_W].astype(v_ref.dtype)
    v_ref[1] = r[:, GROUP_W:].astype(v_ref.dtype)


def _fnet_pos_kernel(w_ref, v_ref, o_ref):
    o_ref[...] = jnp.dot(w_ref[...], v_ref[...], preferred_element_type=F32).astype(o_ref.dtype)


def _fnet_pallas(proj, tables, *, batch, seq, col0, out_dtype=F32, ts=512):
    chan, pos = tables
    cb = col0 // GROUP_W
    nt = seq // ts
    v = pl.pallas_call(
        _fnet_chan_kernel,
        name="fnet_channel_dft",
        grid=(batch, nt),
        in_specs=[pl.BlockSpec((ts, GROUP_W), lambda b, i: (b * nt + i, cb)),
                  pl.BlockSpec((GROUP_W, 2 * GROUP_W), lambda b, i: (0, 0))],
        out_specs=pl.BlockSpec((2, ts, GROUP_W), lambda b, i: (0, i, b)),
        out_shape=jax.ShapeDtypeStruct((2, seq, batch * GROUP_W), BF16),
        compiler_params=pltpu.CompilerParams(
            dimension_semantics=("parallel", "parallel"),
            vmem_limit_bytes=VMEM_LIMIT_BYTES),
    )(proj, chan)
    v = v.reshape(2 * seq, batch * GROUP_W)
    return pl.pallas_call(
        _fnet_pos_kernel,
        name="fnet_position_dft",
        grid=(nt, batch),
        in_specs=[pl.BlockSpec((ts, 2 * seq), lambda i, b: (i, 0)),
                  pl.BlockSpec((2 * seq, GROUP_W), lambda i, b: (0, b))],
        out_specs=pl.BlockSpec((ts, GROUP_W), lambda i, b: (b * nt + i, 0)),
        out_shape=jax.ShapeDtypeStruct((batch * seq, GROUP_W), out_dtype),
        compiler_params=pltpu.CompilerParams(
            dimension_semantics=("parallel", "parallel"),
            vmem_limit_bytes=VMEM_LIMIT_BYTES),
    )(pos, v)


def _xattn_kernel(q_ref, k_ref, v_ref, o_ref):
    for h in range(XA_HEADS):
        sl = slice(h * XA_DH, (h + 1) * XA_DH)
        s = lax.dot_general(q_ref[:, sl].astype(BF16), k_ref[:, sl].astype(BF16),
                            (((1,), (1,)), ((), ())), preferred_element_type=F32) * (XA_DH ** -0.5)
        p = jnp.exp(s - jnp.max(s, axis=-1, keepdims=True))
        l = jnp.sum(p, axis=-1, keepdims=True)
        o = jnp.dot(p.astype(BF16), v_ref[:, sl].astype(BF16), preferred_element_type=F32)
        o_ref[:, sl] = (o / l).astype(o_ref.dtype)


def _xattn_pallas(q, k, v, *, batch, seq, mem_len, ts=512, out_dtype=BF16):
    nt = seq // ts
    return pl.pallas_call(
        _xattn_kernel,
        name="cross_attention",
        grid=(batch, nt),
        in_specs=[pl.BlockSpec((ts, D_MODEL), lambda b, i: (b * nt + i, 0)),
                  pl.BlockSpec((mem_len, D_MODEL), lambda b, i: (b, 0)),
                  pl.BlockSpec((mem_len, D_MODEL), lambda b, i: (b, 0))],
        out_specs=pl.BlockSpec((ts, D_MODEL), lambda b, i: (b * nt + i, 0)),
        out_shape=jax.ShapeDtypeStruct((batch * seq, D_MODEL), out_dtype),
        compiler_params=pltpu.CompilerParams(
            dimension_semantics=("parallel", "parallel"),
            vmem_limit_bytes=VMEM_LIMIT_BYTES),
    )(q, k, v)


MOE_FF = D_MODEL // 4
MOE_TB = 256
MOE_LANE0 = MOE_GROUPS
ROUTER_LANES = 128
META_E, META_RANK, META_GATE = 0, 2, 4
HALF_D = D_MODEL // 2
ROW_DMA_UNROLL = 8


def _pack_halves(x):
    bits = lax.bitcast_convert_type(x.astype(BF16).astype(F32), jnp.uint32)
    h = x.shape[1] // 2
    return (bits[:, :h] >> 16) | (bits[:, h:] & jnp.uint32(0xFFFF0000))


def _unpack_halves(p):
    return (lax.bitcast_convert_type(p << 16, F32),
            lax.bitcast_convert_type(p & jnp.uint32(0xFFFF0000), F32))


def _router_kernel(x_ref, g_ref, wr_ref, hp_ref, meta_ref, cnt_ref, carry_ref):
    tm = x_ref.shape[0]

    @pl.when(pl.program_id(0) == 0)
    def _():
        carry_ref[...] = jnp.zeros_like(carry_ref)

    x = x_ref[...].astype(F32)
    hn = x * lax.rsqrt(jnp.mean(x * x, axis=-1, keepdims=True) + EPS) * g_ref[...]
    hb = hn.astype(BF16)
    logits = jnp.dot(hb, wr_ref[...].astype(BF16), preferred_element_type=F32)
    hp_ref[...] = _pack_halves(hn)

    lane = lax.broadcasted_iota(jnp.int32, (tm, ROUTER_LANES), 1).astype(F32)
    ninf = -jnp.inf
    first = lambda hit: jnp.min(jnp.where(hit, lane, float(ROUTER_LANES)), axis=-1, keepdims=True)
    gl = jnp.where(lane < MOE_GROUPS, logits, ninf)
    gmax = jnp.max(gl, axis=-1, keepdims=True)
    gidx = first(gl == gmax)
    g_gate = 1.0 / jnp.sum(jnp.exp(gl - gmax), axis=-1, keepdims=True)
    off = lane - (MOE_LANE0 + MOE_PER_GROUP * gidx)
    el = jnp.where(jnp.abs(2.0 * off - (MOE_PER_GROUP - 1)) < MOE_PER_GROUP, logits, ninf)
    v1 = jnp.max(el, axis=-1, keepdims=True)
    l1 = first(el == v1)
    el2 = jnp.where(lane == l1, ninf, el)
    v2 = jnp.max(el2, axis=-1, keepdims=True)
    l2 = first(el2 == v2)
    t = jnp.exp(v2 - v1)
    gate1 = g_gate / (1.0 + t)
    gate2 = g_gate * t / (1.0 + t)

    oh = jnp.where(lane == l1, 1.0, 0.0) + jnp.where(lane == l2, 1.0, 0.0)
    ti = lax.broadcasted_iota(jnp.int32, (tm, tm), 0)
    si = lax.broadcasted_iota(jnp.int32, (tm, tm), 1)
    before = jnp.where(si < ti, 1.0, 0.0).astype(BF16)
    base = jnp.dot(before, oh.astype(BF16), preferred_element_type=F32) + carry_ref[0:1, :]
    rank1 = jnp.sum(jnp.where(lane == l1, base, 0.0), axis=-1, keepdims=True)
    rank2 = jnp.sum(jnp.where(lane == l2, base, 0.0), axis=-1, keepdims=True)
    carry_ref[...] = carry_ref[...] + jnp.sum(oh, axis=0, keepdims=True)
    cnt_ref[...] = carry_ref[...]

    meta = jnp.zeros((tm, ROUTER_LANES), F32)
    for ln, val in ((META_E, l1 - MOE_LANE0), (META_E + 1, l2 - MOE_LANE0), (META_RANK, rank1),
                    (META_RANK + 1, rank2), (META_GATE, gate1), (META_GATE + 1, gate2)):
        meta = jnp.where(lane == ln, val, meta)
    meta_ref[...] = meta


def _router_pallas(h, gain, w_rg, w_re, *, tm=512):
    n = h.shape[0]
    wr = jnp.concatenate([w_rg, w_re], axis=1)
    wr = jnp.pad(wr, ((0, 0), (0, ROUTER_LANES - wr.shape[1])))
    return pl.pallas_call(
        _router_kernel,
        name="moe_router",
        grid=(n // tm,),
        in_specs=[pl.BlockSpec((tm, D_MODEL), lambda i: (i, 0)),
                  pl.BlockSpec((1, D_MODEL), lambda i: (0, 0)),
                  pl.BlockSpec((D_MODEL, ROUTER_LANES), lambda i: (0, 0))],
        out_specs=[pl.BlockSpec((tm, HALF_D), lambda i: (i, 0)),
                   pl.BlockSpec((tm, ROUTER_LANES), lambda i: (i, 0)),
                   pl.BlockSpec((8, ROUTER_LANES), lambda i: (0, 0))],
        out_shape=[jax.ShapeDtypeStruct((n, HALF_D), jnp.uint32),
                   jax.ShapeDtypeStruct((n, ROUTER_LANES), F32),
                   jax.ShapeDtypeStruct((8, ROUTER_LANES), F32)],
        scratch_shapes=[pltpu.VMEM((8, ROUTER_LANES), F32)],
        compiler_params=pltpu.CompilerParams(
            dimension_semantics=("arbitrary",),
            vmem_limit_bytes=VMEM_LIMIT_BYTES),
    )(h, gain.reshape(1, D_MODEL).astype(F32), wr)


def _dispatch_kernel(dest_ref, hp_ref, xs_in_ref, xs_ref, sem):
    del xs_in_ref
    tc = hp_ref.shape[0]
    base = pl.program_id(0) * tc * MOE_TOPK

    def row_copy(r, k):
        return pltpu.make_async_copy(hp_ref.at[pl.ds(r, 1)],
                                     xs_ref.at[pl.ds(dest_ref[base + MOE_TOPK * r + k], 1)], sem)

    def start(r, c):
        for k in range(MOE_TOPK):
            row_copy(r, k).start()
        return c

    def wait(r, c):
        for k in range(MOE_TOPK):
            row_copy(r, k).wait()
        return c

    lax.fori_loop(0, tc, start, 0, unroll=ROW_DMA_UNROLL)
    lax.fori_loop(0, tc, wait, 0, unroll=ROW_DMA_UNROLL)


def _dispatch_pallas(hp, dest, n_slots, *, tc=256):
    n = hp.shape[0]
    xs0 = jnp.zeros((n_slots, HALF_D), jnp.uint32)
    return pl.pallas_call(
        _dispatch_kernel,
        name="moe_dispatch",
        grid_spec=pltpu.PrefetchScalarGridSpec(
            num_scalar_prefetch=1,
            grid=(n // tc,),
            in_specs=[pl.BlockSpec((tc, HALF_D), lambda i, d: (i, 0)),
                      pl.BlockSpec(memory_space=pl.ANY)],
            out_specs=pl.BlockSpec(memory_space=pl.ANY),
            scratch_shapes=[pltpu.SemaphoreType.DMA(())]),
        out_shape=jax.ShapeDtypeStruct((n_slots, HALF_D), jnp.uint32),
        input_output_aliases={2: 0},
        compiler_params=pltpu.CompilerParams(
            dimension_semantics=("arbitrary",),
            vmem_limit_bytes=VMEM_LIMIT_BYTES),
    )(dest, hp, xs0)


MOE_WSLOTS = 3


def _expert_kernel(be_ref, first_ref, slot_ref, nxt1_ref, nxt2_ref, nu_ref, x_ref, w1_hbm, w3_hbm, w2_hbm,
                   y_ref, f1_ref, f3_ref, f2_ref, b1_ref, b3_ref, b2_ref, sem, *, layer):
    i = pl.program_id(0)

    def weight_copies(e, slot):
        return [pltpu.make_async_copy(src.at[layer, e], dst.at[slot], sem.at[slot, n])
                for n, (src, dst) in enumerate(((w1_hbm, f1_ref), (w3_hbm, f3_ref), (w2_hbm, f2_ref)))]

    def start_into(e, slot):
        @pl.when(e >= 0)
        def _():
            for c in weight_copies(e, slot):
                c.start()

    @pl.when(i < nu_ref[0])
    def _():
        slot = slot_ref[i]

        @pl.when(first_ref[i] == 1)
        def _():
            @pl.when(i == 0)
            def _():
                start_into(be_ref[i], slot)
                start_into(nxt1_ref[i], lax.rem(slot + 1, MOE_WSLOTS))

            for c in weight_copies(be_ref[i], slot):
                c.wait()
            start_into(nxt2_ref[i], lax.rem(slot + 2, MOE_WSLOTS))

            b1_ref[...] = f1_ref[slot].astype(BF16)
            b3_ref[...] = f3_ref[slot].astype(BF16)
            b2_ref[...] = f2_ref[slot].astype(BF16)

        x_lo, x_hi = (half.astype(BF16) for half in _unpack_halves(x_ref[...]))

        def up(w_ref):
            return (jnp.dot(x_lo, w_ref[:HALF_D, :], preferred_element_type=F32)
                    + jnp.dot(x_hi, w_ref[HALF_D:, :], preferred_element_type=F32))

        a = _silu(up(b1_ref)) * up(b3_ref)
        y_ref[...] = _pack_halves(jnp.dot(a.astype(BF16), b2_ref[...], preferred_element_type=F32))

    @pl.when(i >= nu_ref[0])
    def _():
        y_ref[...] = jnp.zeros_like(y_ref)


def _expert_pallas(xs, nblk, w1, w3, w2, layer):
    n_slots = xs.shape[0]
    nb = n_slots // MOE_TB
    bend = jnp.cumsum(nblk)
    blocks = jnp.arange(nb, dtype=jnp.int32)
    block_e = jnp.minimum(jnp.searchsorted(bend, blocks, side='right'), MOE_EXPERTS - 1).astype(jnp.int32)
    first = (blocks == (bend - nblk)[block_e]).astype(jnp.int32)
    nonempty = nblk > 0
    slot_e = (jnp.cumsum(nonempty) - 1) % MOE_WSLOTS
    ids = jnp.where(nonempty, jnp.arange(MOE_EXPERTS), MOE_EXPERTS)
    pad = jnp.full((2,), MOE_EXPERTS, ids.dtype)
    after = jnp.concatenate([lax.cummin(ids, reverse=True), pad])
    nxt1 = after[1:MOE_EXPERTS + 1]
    nxt2 = jnp.concatenate([after, pad[:1]])[jnp.minimum(nxt1, MOE_EXPERTS) + 1]
    as_id = lambda e: jnp.where(e < MOE_EXPERTS, e, -1).astype(jnp.int32)
    n_used = bend[-1:].astype(jnp.int32)
    blk = lambda i, *s: jnp.minimum(i, s[-1][0] - 1)
    hbm = pl.BlockSpec(memory_space=pl.ANY)
    return pl.pallas_call(
        functools.partial(_expert_kernel, layer=layer),
        name="moe_experts",
        grid_spec=pltpu.PrefetchScalarGridSpec(
            num_scalar_prefetch=6,
            grid=(nb,),
            in_specs=[pl.BlockSpec((MOE_TB, HALF_D), lambda i, *s: (blk(i, *s), 0)), hbm, hbm, hbm],
            out_specs=pl.BlockSpec((MOE_TB, HALF_D), lambda i, *s: (i, 0)),
            scratch_shapes=[pltpu.VMEM((MOE_WSLOTS, D_MODEL, MOE_FF), F32),
                            pltpu.VMEM((MOE_WSLOTS, D_MODEL, MOE_FF), F32),
                            pltpu.VMEM((MOE_WSLOTS, MOE_FF, D_MODEL), F32),
                            pltpu.VMEM((D_MODEL, MOE_FF), BF16), pltpu.VMEM((D_MODEL, MOE_FF), BF16),
                            pltpu.VMEM((MOE_FF, D_MODEL), BF16),
                            pltpu.SemaphoreType.DMA((MOE_WSLOTS, 3))]),
        out_shape=jax.ShapeDtypeStruct((n_slots, HALF_D), jnp.uint32),
        compiler_params=pltpu.CompilerParams(
            dimension_semantics=("arbitrary",),
            vmem_limit_bytes=VMEM_LIMIT_BYTES),
    )(block_e, first, slot_e[block_e].astype(jnp.int32), as_id(nxt1)[block_e], as_id(nxt2)[block_e], n_used,
      xs, w1, w3, w2)


def _combine_kernel(dest_ref, h_ref, meta_ref, g_ref, yb_ref, o_ref, buf_ref, sem, *, final_norm):
    tc = h_ref.shape[0]
    base = pl.program_id(0) * tc * MOE_TOPK

    def row_copy(r, k):
        return pltpu.make_async_copy(yb_ref.at[pl.ds(dest_ref[base + MOE_TOPK * r + k], 1)],
                                     buf_ref.at[k, pl.ds(r, 1)], sem)

    def start(r, c):
        for k in range(MOE_TOPK):
            row_copy(r, k).start()
        return c

    def wait(r, c):
        for k in range(MOE_TOPK):
            row_copy(r, k).wait()
        return c

    lax.fori_loop(0, tc, start, 0, unroll=ROW_DMA_UNROLL)
    lax.fori_loop(0, tc, wait, 0, unroll=ROW_DMA_UNROLL)
    meta = meta_ref[...]
    y_lo = jnp.zeros((tc, HALF_D), F32)
    y_hi = jnp.zeros((tc, HALF_D), F32)
    for k in range(MOE_TOPK):
        lo, hi = _unpack_halves(buf_ref[k])
        gate = meta[:, META_GATE + k:META_GATE + k + 1]
        y_lo = y_lo + lo * gate
        y_hi = y_hi + hi * gate
    out = h_ref[...] + jnp.concatenate([y_lo, y_hi], axis=1)
    if final_norm:
        out = out * lax.rsqrt(jnp.mean(out * out, axis=-1, keepdims=True) + EPS) * g_ref[...]
    o_ref[...] = out


def _combine_pallas(h, meta, dest, yb, final_gain, *, final_norm, tc=256):
    n = h.shape[0]
    return pl.pallas_call(
        functools.partial(_combine_kernel, final_norm=final_norm),
        name="moe_combine",
        grid_spec=pltpu.PrefetchScalarGridSpec(
            num_scalar_prefetch=1,
            grid=(n // tc,),
            in_specs=[pl.BlockSpec((tc, D_MODEL), lambda i, d: (i, 0)),
                      pl.BlockSpec((tc, ROUTER_LANES), lambda i, d: (i, 0)),
                      pl.BlockSpec((1, D_MODEL), lambda i, d: (0, 0)),
                      pl.BlockSpec(memory_space=pl.ANY)],
            out_specs=pl.BlockSpec((tc, D_MODEL), lambda i, d: (i, 0)),
            scratch_shapes=[pltpu.VMEM((MOE_TOPK, tc, HALF_D), jnp.uint32), pltpu.SemaphoreType.DMA(())]),
        out_shape=jax.ShapeDtypeStruct((n, D_MODEL), F32),
        compiler_params=pltpu.CompilerParams(
            dimension_semantics=("arbitrary",),
            vmem_limit_bytes=VMEM_LIMIT_BYTES),
    )(dest, h, meta, final_gain.reshape(1, D_MODEL).astype(F32), yb)


def _moe_pallas(h, gain, w_rg, w_re, w1, w3, w2, layer, final_gain, *, final_norm):
    n = h.shape[0]
    nb = (n * MOE_TOPK) // MOE_TB + MOE_EXPERTS
    hp, meta, cnt = _router_pallas(h, gain, w_rg, w_re)
    expert = meta[:, META_E:META_E + MOE_TOPK].astype(jnp.int32)
    rank = meta[:, META_RANK:META_RANK + MOE_TOPK].astype(jnp.int32)
    counts = cnt[0, MOE_LANE0:MOE_LANE0 + MOE_EXPERTS].astype(jnp.int32)
    nblk = (counts + MOE_TB - 1) // MOE_TB
    dest = ((jnp.cumsum(nblk) - nblk)[expert] * MOE_TB + rank).reshape(n * MOE_TOPK)
    xs = _dispatch_pallas(hp, dest, nb * MOE_TB)
    yb = _expert_pallas(xs, nblk, w1, w3, w2, layer)
    return _combine_pallas(h, meta, dest, yb, final_gain, final_norm=final_norm)


N_GATES = 4 * ML_HEADS
IN_GROUPS_BEFORE_GATES = 4
IN_GROUPS = 13
LANE = 128


def _win_relayout_kernel(*refs):
    parts, main_ref = refs[:-1], refs[-1]
    n = pl.program_id(1)
    wide = jnp.concatenate([p[...] for p in parts], axis=1)

    @pl.when(n < IN_GROUPS_BEFORE_GATES)
    def _():
        main_ref[...] = wide[:, :GROUP_W]

    @pl.when(n >= IN_GROUPS_BEFORE_GATES)
    def _():
        main_ref[...] = wide[:, N_GATES:N_GATES + GROUP_W]


def _win_relayout(w_in):
    depth, k, _ = w_in.shape
    per = GROUP_W // LANE
    g0 = IN_GROUPS_BEFORE_GATES * GROUP_W
    gate = w_in[:, :, g0:g0 + N_GATES].reshape(depth, k, 4, ML_HEADS).transpose(0, 1, 3, 2)
    gate = jnp.pad(gate, ((0, 0), (0, 0), (0, 0), (0, LANE - 4))).reshape(depth, k, ML_HEADS * LANE)
    width = w_in.shape[2]
    w_b = jnp.pad(w_in, ((0, 0), (0, 0), (0, -width % LANE))).astype(BF16)
    part = lambda t: pl.BlockSpec((None, k, LANE), lambda l, n, t=t: (l, 0, per * n + t))
    main = pl.pallas_call(
        _win_relayout_kernel,
        name="w_in_relayout",
        grid=(depth, IN_GROUPS),
        in_specs=[part(t) for t in range(per + 1)],
        out_specs=pl.BlockSpec((None, k, GROUP_W), lambda l, n: (l, 0, n)),
        out_shape=jax.ShapeDtypeStruct((depth, k, IN_GROUPS * GROUP_W), BF16),
        compiler_params=pltpu.CompilerParams(
            dimension_semantics=("parallel", "parallel"),
            vmem_limit_bytes=VMEM_LIMIT_BYTES),
    )(*([w_b] * (per + 1)))
    return main, gate


COL_MLSTM, COL_HGRN, COL_FNET, COL_NA = 0, 4 * GROUP_W, 9 * GROUP_W, 10 * GROUP_W


def kernel(x, mem, norm_mix, norm_cross, norm_ffn, norm_final, norm_mem, w_in, mlstm_conv,
           mlstm_gate_bias, hgrn_lower_bound, na_rpb, group_gain, w_out, xa_wq, xa_wk, xa_wv, xa_wo,
           moe_router_group, moe_router_expert, moe_w1, moe_w3, moe_w2):
    b, s, d = x.shape
    mem_len = mem.shape[1]
    h = x.reshape(b * s, d)
    mem_f = mem.reshape(b * mem_len, d)
    lbs = jnp.cumsum(jax.nn.softmax(hgrn_lower_bound.astype(F32), axis=0), axis=0)
    lbs = lbs - lbs[0]
    fnet_tables = _fnet_tables(s)
    w_main, w_gate = _win_relayout(w_in)
    w_out_b, wq_b, wo_b = (w.astype(BF16) for w in (w_out, xa_wq, xa_wo))
    for l in range(DEPTH):
        proj, gates = _matmul(h, w_main, l, name="in_proj", gain=norm_mix[l], w_side=w_gate, out_dtype=BF16)
        y_ml = _mlstm_pallas(proj, gates, mlstm_conv[l], mlstm_gate_bias[l], batch=b, seq=s,
                             out_dtype=BF16)
        y_hg = _hgrn_pallas(proj, lbs[l], batch=b, seq=s, col0=COL_HGRN, out_dtype=BF16)
        y_fn = _fnet_pallas(proj, fnet_tables, batch=b, seq=s, col0=COL_FNET, out_dtype=BF16)
        y_na = _na_pallas(proj, na_rpb[l], batch=b, seq=s, col0=COL_NA, out_dtype=BF16)
        h = _matmul([y_ml, y_hg, y_fn, y_na], w_out_b, l, name="out_proj", gain=group_gain[l], residual=h)

        q = _matmul(h, wq_b, l, name="xa_q_proj", gain=norm_cross[l], out_dtype=BF16)
        k = _matmul(mem_f, xa_wk, l, name="xa_k_proj", gain=norm_mem, out_dtype=BF16)
        v = _matmul(mem_f, xa_wv, l, name="xa_v_proj", gain=norm_mem, out_dtype=BF16)
        o = _xattn_pallas(q, k, v, batch=b, seq=s, mem_len=mem_len)
        h = _matmul(o, wo_b, l, name="xa_o_proj", residual=h)

        h = _moe_pallas(h, norm_ffn[l], moe_router_group[l], moe_router_expert[l],
                        moe_w1, moe_w3, moe_w2, l, norm_final, final_norm=(l == DEPTH - 1))
    return h.reshape(b, s, d)
```

```python
import functools

import jax
import jax.numpy as jnp
import numpy as np
from jax import lax
from jax.experimental import pallas as pl
from jax.experimental.pallas import tpu as pltpu

D_MODEL = 2048
DEPTH = 4
GRID_W = 64
N_MIXERS = 4
GROUP_W = D_MODEL // N_MIXERS
ML_HEADS = 4
ML_DH = GROUP_W // ML_HEADS
ML_CHUNK = 64
HG_HEADS = 4
HG_CHUNK = 16
FN_GROUPS = 4
FN_CH = GROUP_W // FN_GROUPS
NA_HEADS = 8
NA_DH = GROUP_W // NA_HEADS
NA_KH = 8
NA_KW = 16
XA_HEADS = 4
XA_DH = D_MODEL // XA_HEADS
MOE_GROUPS = 4
MOE_PER_GROUP = 8
MOE_EXPERTS = MOE_GROUPS * MOE_PER_GROUP
MOE_TOPK = 2
MOE_BLOCK = 128
EPS = 1e-6
F32 = jnp.float32
BF16 = jnp.bfloat16

VMEM_LIMIT_BYTES = 56 * 1024 * 1024


def _mm_kernel(*refs, n_x, norm, residual, side):
    refs = list(refs)
    x_refs = [refs.pop(0) for _ in range(n_x)]
    g_ref = refs.pop(0) if norm else None
    w_ref = refs.pop(0)
    ws_ref = refs.pop(0) if side else None
    r_ref = refs.pop(0) if residual else None
    o_ref = refs.pop(0)
    os_ref = refs.pop(0) if side else None
    xn_ref = refs.pop(0)

    @pl.when(pl.program_id(1) == 0)
    def _():
        col = 0
        for x_ref in x_refs:
            x = x_ref[...].astype(F32)
            kx = x.shape[1]
            if norm:
                ms = jnp.mean(x * x, axis=-1, keepdims=True)
                x = x * lax.rsqrt(ms + EPS) * g_ref[:, col:col + kx]
            xn_ref[:, col:col + kx] = x.astype(BF16)
            col += kx
        if side:
            os_ref[...] = jnp.dot(xn_ref[...], ws_ref[...].astype(BF16),
                                  preferred_element_type=F32).astype(os_ref.dtype)

    acc = jnp.dot(xn_ref[...], w_ref[...].astype(BF16), preferred_element_type=F32)
    if residual:
        acc = acc + r_ref[...]
    o_ref[...] = acc.astype(o_ref.dtype)


def _matmul(xs, w, layer, *, name, gain=None, residual=None, w_side=None, tm=1024, tn=512, out_dtype=F32):
    if not isinstance(xs, (list, tuple)):
        xs = [xs]
    m = xs[0].shape[0]
    k = sum(x.shape[1] for x in xs)
    n = w.shape[2]
    tm = min(tm, m)
    tn = min(tn, n)
    assert m % tm == 0 and n % tn == 0 and w.shape[1] == k, (m, n, k, tm, tn)
    norm = gain is not None
    has_res = residual is not None
    side = w_side is not None
    in_specs = [pl.BlockSpec((tm, x.shape[1]), lambda i, j: (i, 0)) for x in xs]
    args = list(xs)
    if norm:
        in_specs.append(pl.BlockSpec((1, k), lambda i, j: (0, 0)))
        args.append(gain.reshape(1, k).astype(F32))
    in_specs.append(pl.BlockSpec((None, k, tn), lambda i, j: (layer, 0, j)))
    args.append(w)
    if side:
        in_specs.append(pl.BlockSpec((None, k, w_side.shape[2]), lambda i, j: (layer, 0, 0)))
        args.append(w_side)
    if has_res:
        in_specs.append(pl.BlockSpec((tm, tn), lambda i, j: (i, j)))
        args.append(residual)
    out_specs = pl.BlockSpec((tm, tn), lambda i, j: (i, j))
    out_shape = jax.ShapeDtypeStruct((m, n), out_dtype)
    if side:
        out_specs = [out_specs, pl.BlockSpec((tm, w_side.shape[2]), lambda i, j: (i, 0))]
        out_shape = [out_shape, jax.ShapeDtypeStruct((m, w_side.shape[2]), F32)]
    return pl.pallas_call(
        functools.partial(_mm_kernel, n_x=len(xs), norm=norm, residual=has_res, side=side),
        name=name,
        grid=(m // tm, n // tn),
        in_specs=in_specs,
        out_specs=out_specs,
        out_shape=out_shape,
        scratch_shapes=[pltpu.VMEM((tm, k), BF16)],
        compiler_params=pltpu.CompilerParams(
            dimension_semantics=("parallel", "arbitrary"),
            vmem_limit_bytes=VMEM_LIMIT_BYTES),
    )(*args)


NA_QROWS = 4
NA_WROWS = 12


def _na_groups(rows):
    tables, plan = [], []
    for r0 in range(0, rows, NA_QROWS):
        band0 = lambda r: min(max(r - NA_KH // 2, 0), rows - NA_KH)
        kr0 = min(band0(r0), rows - NA_WROWS)
        assert band0(r0 + NA_QROWS - 1) + NA_KH <= kr0 + NA_WROWS
        dr = np.full((NA_QROWS, NA_WROWS), -1, np.int64)
        for i in range(NA_QROWS):
            for j in range(NA_WROWS):
                if 0 <= kr0 + j - band0(r0 + i) < NA_KH:
                    dr[i, j] = kr0 + j - (r0 + i) + NA_KH - 1
        key = dr.tobytes()
        if key not in [t.tobytes() for t in tables]:
            tables.append(dr)
        plan.append((r0, kr0, [t.tobytes() for t in tables].index(key)))
    return plan, np.stack(tables)


def _na_bias_tables(rpb, dr_tables):
    c = np.arange(GRID_W)
    dc = np.clip(c[None, :] - c[:, None] + NA_KW - 1, 0, 2 * NA_KW - 2)
    onehot = (dc[None] == np.arange(2 * NA_KW - 1)[:, None, None]).astype(np.float32)
    col_start = np.clip(c - NA_KW // 2, 0, GRID_W - NA_KW)
    col_ok = (c[None, :] >= col_start[:, None]) & (c[None, :] < col_start[:, None] + NA_KW)
    t = jnp.einsum('hrd,dqk->hrqk', rpb.astype(F32), onehot, precision=lax.Precision.HIGHEST)
    t = jnp.where(col_ok[None, None], t, -jnp.inf)
    t = jnp.concatenate([t, jnp.full_like(t[:, :1], -jnp.inf)], axis=1)
    idx = np.where(dr_tables < 0, t.shape[1] - 1, dr_tables)
    pairs = idx.reshape(idx.shape[0], idx.shape[1], -1, 2)
    uniq = sorted({(int(a), int(b)) for a, b in pairs.reshape(-1, 2)})
    pair_idx = np.array([[[uniq.index((int(a), int(b))) for a, b in row] for row in typ] for typ in pairs])
    table = jnp.stack([jnp.concatenate([t[:, a], t[:, b]], axis=-1) for a, b in uniq], axis=1)
    return table, pair_idx


def _na_kernel(q_ref, k_ref, v_ref, tab_ref, o_ref, bias_ref, *, plan, pair_idx):
    nq = NA_QROWS * GRID_W
    nk = NA_WROWS * GRID_W
    for hh in range(2):
        for typ in range(pair_idx.shape[0]):
            for i in range(pair_idx.shape[1]):
                for jj in range(pair_idx.shape[2]):
                    bias_ref[hh, typ, i * GRID_W:(i + 1) * GRID_W, jj * 2 * GRID_W:(jj + 1) * 2 * GRID_W] = (
                        tab_ref[hh, int(pair_idx[typ, i, jj])])
    for r0, kr0, typ in plan:
        q = q_ref[r0 * GRID_W:r0 * GRID_W + nq, :].astype(F32) * (NA_DH ** -0.5)
        kb = k_ref[kr0 * GRID_W:kr0 * GRID_W + nk, :]
        vb = v_ref[kr0 * GRID_W:kr0 * GRID_W + nk, :]
        outs = []
        for hh in range(2):
            sl = slice(hh * NA_DH, (hh + 1) * NA_DH)
            s = lax.dot_general(q[:, sl].astype(BF16), kb[:, sl].astype(BF16),
                                (((1,), (1,)), ((), ())), preferred_element_type=F32)
            s = s + bias_ref[hh, typ]
            m = jnp.max(s, axis=-1, keepdims=True)
            p = jnp.exp(s - m)
            l = jnp.sum(p, axis=-1, keepdims=True)
            o = jnp.dot(p.astype(BF16), vb[:, sl].astype(BF16), preferred_element_type=F32)
            outs.append(o / l)
        o_ref[r0 * GRID_W:r0 * GRID_W + nq, :] = jnp.concatenate(outs, axis=-1).astype(o_ref.dtype)


def _na_pallas(proj, rpb, *, batch, seq, col0, out_dtype=F32):
    rows = seq // GRID_W
    pair_w = 2 * NA_DH
    cb = col0 // pair_w
    gb = GROUP_W // pair_w
    plan, dr_tables = _na_groups(rows)
    table, pair_idx = _na_bias_tables(rpb, dr_tables)
    spec = lambda g: pl.BlockSpec((seq, pair_w), lambda b, p, g=g: (b, cb + g * gb + p))
    return pl.pallas_call(
        functools.partial(_na_kernel, plan=plan, pair_idx=pair_idx),
        name="na_attention",
        grid=(batch, NA_HEADS // 2),
        in_specs=[spec(0), spec(1), spec(2),
                  pl.BlockSpec((2,) + table.shape[1:], lambda b, p: (p, 0, 0, 0))],
        out_specs=pl.BlockSpec((seq, pair_w), lambda b, p: (b, p)),
        out_shape=jax.ShapeDtypeStruct((batch * seq, GROUP_W), out_dtype),
        scratch_shapes=[pltpu.VMEM((2, dr_tables.shape[0], NA_QROWS * GRID_W, NA_WROWS * GRID_W), F32)],
        compiler_params=pltpu.CompilerParams(
            dimension_semantics=("parallel", "parallel"),
            vmem_limit_bytes=VMEM_LIMIT_BYTES),
    )(proj, proj, proj, table)


ML_L = 128
ML_GATE_LANES = 128


def _log_sigmoid(x):
    return jnp.minimum(x, 0.0) - jnp.log(1.0 + jnp.exp(-jnp.abs(x)))


def _silu(x):
    return x * jax.nn.sigmoid(x)


_NT = (((1,), (1,)), ((), ()))
_TN = (((0,), (0,)), ((), ()))


def _split3(x):
    hi = x.astype(BF16)
    r1 = x - hi.astype(F32)
    mid = r1.astype(BF16)
    lo = (r1 - mid.astype(F32)).astype(BF16)
    return hi, mid, lo


def _dot_exact_lhs(a, x, dims=None):
    a = a.astype(BF16)
    if dims is None:
        return sum(jnp.dot(a, p, preferred_element_type=F32) for p in _split3(x))
    return sum(lax.dot_general(a, p, dims, preferred_element_type=F32) for p in _split3(x))


def _dot_exact_rhs(x, a, dims):
    a = a.astype(BF16)
    return sum(lax.dot_general(p, a, dims, preferred_element_type=F32) for p in _split3(x))


ML_GROUP = 4
ML_PAD = 16


def _mlstm_local(q, k, v, gcol, tri, mask_w, gi, end_row):
    L = ML_L
    cum_col = _dot_exact_lhs(tri, _log_sigmoid(gcol))
    wide = lambda col: jnp.broadcast_to(col, (L, L))
    b_cols, ig_cols, rows = [], [], []
    for g in range(len(q)):
        lanes = slice(g * ML_GATE_LANES, (g + 1) * ML_GATE_LANES)
        b_cols.append(cum_col[:, g * ML_GATE_LANES + gi + 1:g * ML_GATE_LANES + gi + 2])
        ig_cols.append(gcol[:, g * ML_GATE_LANES + gi:g * ML_GATE_LANES + gi + 1])
        rows.append(gcol[:, lanes].T[gi:gi + 1, :] - cum_col[:, lanes].T[gi + 1:gi + 2, :])
    b_w = jnp.concatenate([wide(b) for b in b_cols], axis=1)
    row_w = jnp.concatenate(rows, axis=1)
    dmat = jnp.where(mask_w != 0.0, b_w + row_w, -jnp.inf)
    m_in_w = jnp.concatenate([wide(jnp.max(dmat[:, g * L:(g + 1) * L], axis=-1, keepdims=True))
                              for g in range(len(q))], axis=1)
    w = jnp.exp(dmat - m_in_w)
    ones = jnp.ones((L, ML_DH), BF16)
    out = []
    for g in range(len(q)):
        lanes = slice(g * L, (g + 1) * L)
        kb = k[g].astype(BF16)
        v_aug = jnp.concatenate([v[g].astype(BF16), ones], axis=1)
        qk = lax.dot_general(q[g].astype(BF16), kb, _NT, preferred_element_type=F32) * w[:, lanes]
        res = jnp.dot(qk.astype(BF16), v_aug, preferred_element_type=F32)
        b_end = b_cols[g][end_row:end_row + 1, :]
        a_col = b_end - b_cols[g] + ig_cols[g]
        m_loc = jnp.max(a_col, axis=0, keepdims=True)
        kw = k[g] * jnp.exp(a_col - m_loc)
        st = lax.dot_general(v_aug, kw.astype(BF16), _TN, preferred_element_type=F32)
        out.append(dict(m_in=m_in_w[:, lanes], num=res[:, :ML_DH], den=res[:, ML_DH:], b=b_w[:, lanes],
                        ct=st[:ML_DH], n=st[ML_DH:ML_DH + 1], m_loc=m_loc, b_end=b_end))
    return out


def _mlstm_carry(q, m_in, num_in, den_in, b, ct, n, m_prev):
    inter = b + m_prev
    m_t = jnp.maximum(inter, m_in)
    s_in = jnp.exp(m_in - m_t)
    s_inter = jnp.exp(inter - m_t)
    rhs = jnp.concatenate([ct.astype(BF16), jnp.broadcast_to(n, (ML_PAD, ML_DH)).astype(BF16)], axis=0)
    both = lax.dot_general(q.astype(BF16), rhs, _NT, preferred_element_type=F32)
    num = s_in * num_in + s_inter * both[:, :ML_DH]
    den = s_in * den_in + s_inter * both[:, ML_DH:ML_DH + 1]
    return num / jnp.maximum(jnp.abs(den), jnp.exp(-m_t))


def _mlstm_kernel(q_ref, k_ref, v_ref, o_ref, cwq_ref, cwk_ref, gcol_ref, bcol_ref,
                  y_ref, qs_ref, ks_ref, num_ref, den_ref, min_ref, b_ref, ctl_ref, stat_ref, *, seq):
    L = ML_L
    nc = seq // L
    t_idx = lax.broadcasted_iota(jnp.int32, (seq, 1), 0)

    def conv_silu(x, w):
        prev = jnp.where(t_idx == 0, 0.0, pltpu.roll(x, 1, axis=0))
        nxt = jnp.where(t_idx == seq - 1, 0.0, pltpu.roll(x, seq - 1, axis=0))
        return _silu(prev * w[0:1, :] + x * w[1:2, :] + nxt * w[2:3, :])

    qs_ref[...] = conv_silu(q_ref[...].astype(F32), cwq_ref[...])
    ks_ref[...] = conv_silu(k_ref[...].astype(F32), cwk_ref[...]) * (ML_DH ** -0.5)

    ti = lax.broadcasted_iota(jnp.int32, (L, L), 0)
    si = lax.broadcasted_iota(jnp.int32, (L, L), 1)
    tris = (jnp.where(si <= ti, 1.0, 0.0), jnp.where(si >= ti, 1.0, 0.0))
    masks_w = [jnp.concatenate([t] * ML_GROUP, axis=1) for t in tris]

    for direction in (0, 1):
        for c0 in range(0, nc, ML_GROUP):
            rows = [pl.ds((c0 + g) * L, L) for g in range(ML_GROUP)]
            gcol = jnp.concatenate([gcol_ref[r, :] + bcol_ref[...] for r in rows], axis=1)
            local = _mlstm_local([qs_ref[r, :] for r in rows], [ks_ref[r, :] for r in rows],
                                 [v_ref[r, :].astype(F32) for r in rows], gcol,
                                 tris[direction], masks_w[direction], 2 * direction,
                                 L - 1 if direction == 0 else 0)
            for g, (r, loc) in enumerate(zip(rows, local)):
                idx = direction * nc + c0 + g
                num_ref[direction, r, :] = loc["num"]
                den_ref[direction, r, :] = loc["den"]
                min_ref[direction, r, :] = loc["m_in"]
                b_ref[direction, r, :] = loc["b"]
                ctl_ref[idx] = loc["ct"]
                stat_ref[idx, 0:1, :] = loc["n"]
                stat_ref[idx, 1:2, :] = jnp.broadcast_to(loc["m_loc"], (1, ML_DH))
                stat_ref[idx, 2:3, :] = jnp.broadcast_to(loc["b_end"], (1, ML_DH))

    state = [(jnp.zeros((ML_DH, ML_DH), F32), jnp.zeros((1, ML_DH), F32), jnp.zeros((1, ML_DH), F32))] * 2
    for i in range(nc):
        for direction in (0, 1):
            c = i if direction == 0 else nc - 1 - i
            r = pl.ds(c * L, L)
            idx = direction * nc + c
            ct, n, m = state[direction]
            num_ref[direction, r, :] = _mlstm_carry(qs_ref[r, :], min_ref[direction, r, :],
                                                    num_ref[direction, r, :], den_ref[direction, r, :],
                                                    b_ref[direction, r, :], ct, n, m)
            n_loc, m_loc, b_end = stat_ref[idx, 0:1, :], stat_ref[idx, 1:2, :], stat_ref[idx, 2:3, :]
            m_new = jnp.maximum(b_end + m, m_loc)
            s_old = jnp.exp(b_end + m - m_new)
            s_new = jnp.exp(m_loc - m_new)
            state[direction] = (s_old * ct + s_new * ctl_ref[idx], s_old * n + s_new * n_loc, m_new)
    y_ref[...] = (jax.nn.sigmoid(o_ref[...].astype(F32)) * (num_ref[0] + num_ref[1])).astype(y_ref.dtype)


def _mlstm_pallas(proj, gates, conv_w, gate_b, *, batch, seq, out_dtype=F32):
    d = ML_DH
    hb = GROUP_W // d
    gb4 = gate_b.astype(F32).reshape(4, ML_HEADS).T
    gb_col = jnp.pad(gb4, ((0, 0), (0, ML_GATE_LANES - 4))).reshape(ML_HEADS, 1, ML_GATE_LANES)
    spec = lambda grp: pl.BlockSpec((seq, d), lambda b, h, grp=grp: (b, grp * hb + h))
    f32 = lambda *shape: pltpu.VMEM(shape, F32)
    return pl.pallas_call(
        functools.partial(_mlstm_kernel, seq=seq),
        name="mlstm",
        grid=(batch, ML_HEADS),
        in_specs=[spec(0), spec(1), spec(2), spec(3),
                  pl.BlockSpec((3, d), lambda b, h: (0, h)),
                  pl.BlockSpec((3, d), lambda b, h: (0, hb + h)),
                  pl.BlockSpec((seq, ML_GATE_LANES), lambda b, h: (b, h)),
                  pl.BlockSpec((None, 1, ML_GATE_LANES), lambda b, h: (h, 0, 0))],
        out_specs=pl.BlockSpec((seq, d), lambda b, h: (b, h)),
        out_shape=jax.ShapeDtypeStruct((batch * seq, GROUP_W), out_dtype),
        scratch_shapes=[f32(seq, d), f32(seq, d), f32(2, seq, d), f32(2, seq, d), f32(2, seq, d),
                        f32(2, seq, d), f32(2 * (seq // ML_L), d, d), f32(2 * (seq // ML_L), 8, d)],
        compiler_params=pltpu.CompilerParams(
            dimension_semantics=("parallel", "parallel"),
            vmem_limit_bytes=VMEM_LIMIT_BYTES),
    )(proj, proj, proj, proj, conv_w.astype(F32), conv_w.astype(F32), gates, gb_col)


HG_L = 128
HG_LEAF = 8


HG_GROUP = 4
HG_NLEAF = HG_L // HG_LEAF
HG_LEVELS = 4


def _hgrn_ref_rows(p_ref, chunks, backward):
    def rows(first, count, stride):
        return jnp.concatenate([p_ref[pl.ds(c * HG_L + first, count, stride=stride), :] for c in chunks], axis=1)

    leaf = rows(HG_LEAF // 2 if backward else HG_LEAF // 2 - 1, HG_NLEAF, HG_LEAF)
    levels = []
    m = HG_L // 2
    for _ in range(HG_LEVELS):
        blocks = HG_L // (2 * m)
        r = rows(m if backward else m - 1, blocks, 2 * m) if blocks > 1 else rows(m if backward else m - 1, 1, 1)
        levels.append(jnp.repeat(r, HG_NLEAF // blocks, axis=0))
        m //= 2
    return leaf, levels


def _hgrn_chunks(q, k, v, p, ref_leaf, ref_levels, masks):
    L = HG_L
    dk = HG_L
    groups = q.shape[1] // dk
    spread = lambda e: jnp.broadcast_to(e[:, None, :], (HG_NLEAF, HG_LEAF, e.shape[1])).reshape(L, e.shape[1])
    d = p - spread(ref_leaf)
    q_leaf = q * jnp.exp(d)
    k_leaf = k * jnp.exp(-d)
    pairs = [(q_leaf.astype(BF16), k_leaf.astype(BF16), masks[HG_LEVELS])]
    for li in range(HG_LEVELS):
        gap = ref_leaf - ref_levels[li]
        qh = q_leaf * spread(jnp.exp(jnp.minimum(gap, 0.0)))
        kh = k_leaf * spread(jnp.exp(jnp.minimum(-gap, 0.0)))
        pairs.append((qh.astype(BF16), kh.astype(BF16), masks[li]))
    vb = v.astype(BF16)
    outs = []
    for g in range(groups):
        lanes = slice(g * dk, (g + 1) * dk)
        attn = jnp.zeros((L, L), F32)
        for qh, kh, mask in pairs:
            s = lax.dot_general(qh[:, lanes], kh[:, lanes], _NT, preferred_element_type=F32)
            attn = attn + jnp.where(mask != 0.0, s, 0.0)
        outs.append(jnp.dot(attn.astype(BF16), vb[:, lanes], preferred_element_type=F32))
    return jnp.concatenate(outs, axis=1)


def _hgrn_carry(q, k, v, p, st, backward):
    L = HG_L
    o = lax.dot_general((q * jnp.exp(p)).astype(BF16), st.astype(BF16), _NT, preferred_element_type=F32)
    p_end = p[0:1, :] if backward else p[L - 1:L, :]
    kd = k * jnp.exp(p_end - p)
    st = st * jnp.exp(p_end) + lax.dot_general(v.astype(BF16), kd.astype(BF16), _TN,
                                               preferred_element_type=F32)
    return o, st


def _hgrn_masks(backward):
    L = HG_L
    ti = lax.broadcasted_iota(jnp.int32, (L, L), 0)
    si = lax.broadcasted_iota(jnp.int32, (L, L), 1)
    if backward:
        ti, si = si, ti
    one = lambda cond: jnp.where(cond, 1.0, 0.0)
    masks = []
    m = L // 2
    while m >= HG_LEAF:
        same = one((ti // (2 * m)) == (si // (2 * m)))
        masks.append(same * one((ti % (2 * m)) >= m) * one((si % (2 * m)) < m))
        m //= 2
    masks.append(one((ti // HG_LEAF) == (si // HG_LEAF)) * one(si <= ti))
    return masks


def _hgrn_kernel(q_ref, ff_ref, fb_ref, i_ref, g_ref, lb_ref, y_ref,
                 qs_ref, lf_ref, kf_ref, lbk_ref, kb_ref, of_ref, ob_ref, *, seq):
    L = HG_L
    nc = seq // L
    lb = lb_ref[...]
    log_lb = jnp.log(lb)
    log1m_lb = jnp.log1p(-lb)

    def forget(fp):
        ls = _log_sigmoid(fp)
        a = log_lb
        c = log1m_lb + ls
        logf = jnp.maximum(a, c) + jnp.log(1.0 + jnp.exp(-jnp.abs(a - c)))
        return logf, (1.0 - lb) * jnp.exp(ls - fp)

    qs_ref[...] = _silu(q_ref[...].astype(F32))
    lf_ref[...], kf_ref[...] = forget(ff_ref[...].astype(F32))
    lbk_ref[...], kb_ref[...] = forget(fb_ref[...].astype(F32))

    ti = lax.broadcasted_iota(jnp.int32, (L, L), 0)
    si = lax.broadcasted_iota(jnp.int32, (L, L), 1)
    tril = (si <= ti).astype(F32)
    triu = (si >= ti).astype(F32)
    masks_f = _hgrn_masks(False)
    masks_b = _hgrn_masks(True)

    def side_by_side(ref, c0):
        return jnp.concatenate([ref[pl.ds((c0 + g) * L, L), :].astype(F32) for g in range(HG_GROUP)], axis=1)

    def scatter_back(ref, c0, wide):
        for g in range(HG_GROUP):
            ref[pl.ds((c0 + g) * L, L), :] = wide[:, g * L:(g + 1) * L]

    directions = ((kf_ref, lf_ref, of_ref, tril, masks_f, False), (kb_ref, lbk_ref, ob_ref, triu, masks_b, True))
    for c0 in range(0, nc, HG_GROUP):
        for _, f_ref, _, tri, _, _ in directions:
            scatter_back(f_ref, c0, _dot_exact_lhs(tri, side_by_side(f_ref, c0)))
    for c0 in range(0, nc, HG_GROUP):
        q = side_by_side(qs_ref, c0)
        v = side_by_side(i_ref, c0)
        for k_ref, f_ref, o_ref, _, masks, backward in directions:
            ref_leaf, ref_levels = _hgrn_ref_rows(f_ref, range(c0, c0 + HG_GROUP), backward)
            scatter_back(o_ref, c0, _hgrn_chunks(q, side_by_side(k_ref, c0), v, side_by_side(f_ref, c0),
                                                 ref_leaf, ref_levels, masks))
    st_f = jnp.zeros((L, L), F32)
    st_b = jnp.zeros((L, L), F32)
    for i in range(nc):
        sl = pl.ds(i * L, L)
        o, st_f = _hgrn_carry(qs_ref[sl, :], kf_ref[sl, :], i_ref[sl, :].astype(F32), lf_ref[sl, :], st_f, False)
        of_ref[sl, :] += o
        sl = pl.ds((nc - 1 - i) * L, L)
        o, st_b = _hgrn_carry(qs_ref[sl, :], kb_ref[sl, :], i_ref[sl, :].astype(F32), lbk_ref[sl, :], st_b, True)
        ob_ref[sl, :] += o
    o = of_ref[...] + ob_ref[...]
    o = o * lax.rsqrt(jnp.mean(o * o, axis=-1, keepdims=True) + EPS)
    y_ref[...] = (o * _silu(g_ref[...].astype(F32))).astype(y_ref.dtype)


def _hgrn_pallas(proj, lb, *, batch, seq, col0, out_dtype=F32):
    d = GROUP_W // HG_HEADS
    hb = GROUP_W // d
    cb = col0 // d
    spec = lambda grp: pl.BlockSpec((seq, d), lambda b, h, grp=grp: (b, cb + grp * hb + h))
    f32 = lambda *shape: pltpu.VMEM(shape, F32)
    return pl.pallas_call(
        functools.partial(_hgrn_kernel, seq=seq),
        name="hgrn2",
        grid=(batch, HG_HEADS),
        in_specs=[spec(0), spec(1), spec(2), spec(3), spec(4),
                  pl.BlockSpec((1, d), lambda b, h: (0, h))],
        out_specs=pl.BlockSpec((seq, d), lambda b, h: (b, h)),
        out_shape=jax.ShapeDtypeStruct((batch * seq, GROUP_W), out_dtype),
        scratch_shapes=[f32(seq, d)] * 7,
        compiler_params=pltpu.CompilerParams(
            dimension_semantics=("parallel", "parallel"),
            vmem_limit_bytes=VMEM_LIMIT_BYTES),
    )(proj, proj, proj, proj, proj, lb.astype(F32).reshape(1, GROUP_W))


DFT_SPLIT = 32


def _dft_cos_sin(n):
    def direct(rows, period):
        k = (rows[:, None] * jnp.arange(n, dtype=jnp.int32)[None, :]) % period
        ang = k.astype(F32) * (2.0 * np.pi / period)
        return jnp.cos(ang), jnp.sin(ang)

    if n % DFT_SPLIT or n // DFT_SPLIT < DFT_SPLIT:
        return direct(jnp.arange(n, dtype=jnp.int32), n)
    ca, sa = direct(jnp.arange(n // DFT_SPLIT, dtype=jnp.int32), n // DFT_SPLIT)
    cb, sb = direct(jnp.arange(DFT_SPLIT, dtype=jnp.int32), n)
    cos = ca[:, None, :] * cb[None, :, :] - sa[:, None, :] * sb[None, :, :]
    sin = sa[:, None, :] * cb[None, :, :] + ca[:, None, :] * sb[None, :, :]
    return cos.reshape(n, n), sin.reshape(n, n)


def _fnet_tables(seq):
    cc, sc = _dft_cos_sin(FN_CH)
    eye = jnp.eye(FN_GROUPS, dtype=F32)
    chan = jnp.concatenate([jnp.kron(eye, cc), jnp.kron(eye, sc)], axis=1)
    cs, ss = _dft_cos_sin(seq)
    pos = jnp.concatenate([cs, -ss], axis=1) * ((seq * FN_CH) ** -0.5)
    return chan.astype(BF16), pos.astype(BF16)


def _fnet_chan_kernel(u_ref, dft_ref, v_ref):
    r = jnp.dot(u_ref[...].astype(BF16), dft_ref[...], preferred_element_type=F32)
    v_ref[0] = r[:, :GROUP_W].astype(v_ref.dtype)
    v_ref[1] = r[:, GROUP_W:].astype(v_ref.dtype)


def _fnet_pos_kernel(w_ref, v_ref, o_ref):
    o_ref[...] = jnp.dot(w_ref[...], v_ref[...], preferred_element_type=F32).astype(o_ref.dtype)


def _fnet_pallas(proj, tables, *, batch, seq, col0, out_dtype=F32, ts=512):
    chan, pos = tables
    cb = col0 // GROUP_W
    nt = seq // ts
    v = pl.pallas_call(
        _fnet_chan_kernel,
        name="fnet_channel_dft",
        grid=(batch, nt),
        in_specs=[pl.BlockSpec((ts, GROUP_W), lambda b, i: (b * nt + i, cb)),
                  pl.BlockSpec((GROUP_W, 2 * GROUP_W), lambda b, i: (0, 0))],
        out_specs=pl.BlockSpec((2, ts, GROUP_W), lambda b, i: (0, i, b)),
        out_shape=jax.ShapeDtypeStruct((2, seq, batch * GROUP_W), BF16),
        compiler_params=pltpu.CompilerParams(
            dimension_semantics=("parallel", "parallel"),
            vmem_limit_bytes=VMEM_LIMIT_BYTES),
    )(proj, chan)
    v = v.reshape(2 * seq, batch * GROUP_W)
    return pl.pallas_call(
        _fnet_pos_kernel,
        name="fnet_position_dft",
        grid=(nt, batch),
        in_specs=[pl.BlockSpec((ts, 2 * seq), lambda i, b: (i, 0)),
                  pl.BlockSpec((2 * seq, GROUP_W), lambda i, b: (0, b))],
        out_specs=pl.BlockSpec((ts, GROUP_W), lambda i, b: (b * nt + i, 0)),
        out_shape=jax.ShapeDtypeStruct((batch * seq, GROUP_W), out_dtype),
        compiler_params=pltpu.CompilerParams(
            dimension_semantics=("parallel", "parallel"),
            vmem_limit_bytes=VMEM_LIMIT_BYTES),
    )(pos, v)


def _xattn_kernel(x_ref, g_ref, wq_ref, k_ref, v_ref, o_ref, xn_ref):
    @pl.when(pl.program_id(1) == 0)
    def _():
        x = x_ref[...].astype(F32)
        ms = jnp.mean(x * x, axis=-1, keepdims=True)
        xn_ref[...] = (x * lax.rsqrt(ms + EPS) * g_ref[...]).astype(BF16)

    q = jnp.dot(xn_ref[...], wq_ref[...].astype(BF16), preferred_element_type=F32)
    s = lax.dot_general(q.astype(BF16), k_ref[...].astype(BF16), _NT,
                        preferred_element_type=F32) * (XA_DH ** -0.5)
    p = jnp.exp(s - jnp.max(s, axis=-1, keepdims=True))
    l = jnp.sum(p, axis=-1, keepdims=True)
    o = jnp.dot(p.astype(BF16), v_ref[...].astype(BF16), preferred_element_type=F32)
    o_ref[...] = (o / l).astype(o_ref.dtype)


def _xattn_pallas(h, gain, wq, layer, k, v, *, batch, seq, mem_len, tm=1024, out_dtype=BF16):
    tiles = seq // tm
    return pl.pallas_call(
        _xattn_kernel,
        name="cross_attention",
        grid=(batch * tiles, XA_HEADS),
        in_specs=[pl.BlockSpec((tm, D_MODEL), lambda i, j: (i, 0)),
                  pl.BlockSpec((1, D_MODEL), lambda i, j: (0, 0)),
                  pl.BlockSpec((None, D_MODEL, XA_DH), lambda i, j: (layer, 0, j)),
                  pl.BlockSpec((mem_len, XA_DH), lambda i, j: (i // tiles, j)),
                  pl.BlockSpec((mem_len, XA_DH), lambda i, j: (i // tiles, j))],
        out_specs=pl.BlockSpec((tm, XA_DH), lambda i, j: (i, j)),
        out_shape=jax.ShapeDtypeStruct((batch * seq, D_MODEL), out_dtype),
        scratch_shapes=[pltpu.VMEM((tm, D_MODEL), BF16)],
        compiler_params=pltpu.CompilerParams(
            dimension_semantics=("parallel", "arbitrary"),
            vmem_limit_bytes=VMEM_LIMIT_BYTES),
    )(h, gain.reshape(1, D_MODEL).astype(F32), wq, k, v)


MOE_FF = D_MODEL // 4
MOE_TB = 256
MOE_LANE0 = MOE_GROUPS
ROUTER_LANES = 128
META_E, META_RANK, META_GATE = 0, 2, 4
HALF_D = D_MODEL // 2
ROW_DMA_UNROLL = 8


def _pack_halves(x):
    bits = lax.bitcast_convert_type(x.astype(BF16).astype(F32), jnp.uint32)
    h = x.shape[1] // 2
    return (bits[:, :h] >> 16) | (bits[:, h:] & jnp.uint32(0xFFFF0000))


def _unpack_halves(p):
    return (lax.bitcast_convert_type(p << 16, F32),
            lax.bitcast_convert_type(p & jnp.uint32(0xFFFF0000), F32))


def _router_kernel(x_ref, g_ref, wr_ref, hp_ref, meta_ref, cnt_ref, carry_ref):
    tm = x_ref.shape[0]

    @pl.when(pl.program_id(0) == 0)
    def _():
        carry_ref[...] = jnp.zeros_like(carry_ref)

    x = x_ref[...].astype(F32)
    hn = x * lax.rsqrt(jnp.mean(x * x, axis=-1, keepdims=True) + EPS) * g_ref[...]
    hb = hn.astype(BF16)
    logits = jnp.dot(hb, wr_ref[...].astype(BF16), preferred_element_type=F32)
    hp_ref[...] = _pack_halves(hn)

    lane = lax.broadcasted_iota(jnp.int32, (tm, ROUTER_LANES), 1).astype(F32)
    ninf = -jnp.inf
    first = lambda hit: jnp.min(jnp.where(hit, lane, float(ROUTER_LANES)), axis=-1, keepdims=True)
    gl = jnp.where(lane < MOE_GROUPS, logits, ninf)
    gmax = jnp.max(gl, axis=-1, keepdims=True)
    gidx = first(gl == gmax)
    g_gate = 1.0 / jnp.sum(jnp.exp(gl - gmax), axis=-1, keepdims=True)
    off = lane - (MOE_LANE0 + MOE_PER_GROUP * gidx)
    el = jnp.where(jnp.abs(2.0 * off - (MOE_PER_GROUP - 1)) < MOE_PER_GROUP, logits, ninf)
    v1 = jnp.max(el, axis=-1, keepdims=True)
    l1 = first(el == v1)
    el2 = jnp.where(lane == l1, ninf, el)
    v2 = jnp.max(el2, axis=-1, keepdims=True)
    l2 = first(el2 == v2)
    t = jnp.exp(v2 - v1)
    gate1 = g_gate / (1.0 + t)
    gate2 = g_gate * t / (1.0 + t)

    oh = jnp.where(lane == l1, 1.0, 0.0) + jnp.where(lane == l2, 1.0, 0.0)
    ti = lax.broadcasted_iota(jnp.int32, (tm, tm), 0)
    si = lax.broadcasted_iota(jnp.int32, (tm, tm), 1)
    before = jnp.where(si < ti, 1.0, 0.0).astype(BF16)
    base = jnp.dot(before, oh.astype(BF16), preferred_element_type=F32) + carry_ref[0:1, :]
    rank1 = jnp.sum(jnp.where(lane == l1, base, 0.0), axis=-1, keepdims=True)
    rank2 = jnp.sum(jnp.where(lane == l2, base, 0.0), axis=-1, keepdims=True)
    carry_ref[...] = carry_ref[...] + jnp.sum(oh, axis=0, keepdims=True)
    cnt_ref[...] = carry_ref[...]

    meta = jnp.zeros((tm, ROUTER_LANES), F32)
    for ln, val in ((META_E, l1 - MOE_LANE0), (META_E + 1, l2 - MOE_LANE0), (META_RANK, rank1),
                    (META_RANK + 1, rank2), (META_GATE, gate1), (META_GATE + 1, gate2)):
        meta = jnp.where(lane == ln, val, meta)
    meta_ref[...] = meta


def _router_pallas(h, gain, w_rg, w_re, *, tm=512):
    n = h.shape[0]
    wr = jnp.concatenate([w_rg, w_re], axis=1)
    wr = jnp.pad(wr, ((0, 0), (0, ROUTER_LANES - wr.shape[1])))
    return pl.pallas_call(
        _router_kernel,
        name="moe_router",
        grid=(n // tm,),
        in_specs=[pl.BlockSpec((tm, D_MODEL), lambda i: (i, 0)),
                  pl.BlockSpec((1, D_MODEL), lambda i: (0, 0)),
                  pl.BlockSpec((D_MODEL, ROUTER_LANES), lambda i: (0, 0))],
        out_specs=[pl.BlockSpec((tm, HALF_D), lambda i: (i, 0)),
                   pl.BlockSpec((tm, ROUTER_LANES), lambda i: (i, 0)),
                   pl.BlockSpec((8, ROUTER_LANES), lambda i: (0, 0))],
        out_shape=[jax.ShapeDtypeStruct((n, HALF_D), jnp.uint32),
                   jax.ShapeDtypeStruct((n, ROUTER_LANES), F32),
                   jax.ShapeDtypeStruct((8, ROUTER_LANES), F32)],
        scratch_shapes=[pltpu.VMEM((8, ROUTER_LANES), F32)],
        compiler_params=pltpu.CompilerParams(
            dimension_semantics=("arbitrary",),
            vmem_limit_bytes=VMEM_LIMIT_BYTES),
    )(h, gain.reshape(1, D_MODEL).astype(F32), wr)


def _dispatch_kernel(dest_ref, hp_ref, xs_in_ref, xs_ref, sem):
    del xs_in_ref
    tc = hp_ref.shape[0]
    base = pl.program_id(0) * tc * MOE_TOPK

    def row_copy(r, k):
        return pltpu.make_async_copy(hp_ref.at[pl.ds(r, 1)],
                                     xs_ref.at[pl.ds(dest_ref[base + MOE_TOPK * r + k], 1)], sem)

    def start(r, c):
        for k in range(MOE_TOPK):
            row_copy(r, k).start()
        return c

    def wait(r, c):
        for k in range(MOE_TOPK):
            row_copy(r, k).wait()
        return c

    lax.fori_loop(0, tc, start, 0, unroll=ROW_DMA_UNROLL)
    lax.fori_loop(0, tc, wait, 0, unroll=ROW_DMA_UNROLL)


def _dispatch_pallas(hp, dest, n_slots, *, tc=256):
    n = hp.shape[0]
    xs0 = jnp.zeros((n_slots, HALF_D), jnp.uint32)
    return pl.pallas_call(
        _dispatch_kernel,
        name="moe_dispatch",
        grid_spec=pltpu.PrefetchScalarGridSpec(
            num_scalar_prefetch=1,
            grid=(n // tc,),
            in_specs=[pl.BlockSpec((tc, HALF_D), lambda i, d: (i, 0)),
                      pl.BlockSpec(memory_space=pl.ANY)],
            out_specs=pl.BlockSpec(memory_space=pl.ANY),
            scratch_shapes=[pltpu.SemaphoreType.DMA(())]),
        out_shape=jax.ShapeDtypeStruct((n_slots, HALF_D), jnp.uint32),
        input_output_aliases={2: 0},
        compiler_params=pltpu.CompilerParams(
            dimension_semantics=("arbitrary",),
            vmem_limit_bytes=VMEM_LIMIT_BYTES),
    )(dest, hp, xs0)


MOE_WSLOTS = 3


def _expert_kernel(be_ref, first_ref, slot_ref, nxt1_ref, nxt2_ref, nu_ref, x_ref, w1_hbm, w3_hbm, w2_hbm,
                   y_ref, f1_ref, f3_ref, f2_ref, b1_ref, b3_ref, b2_ref, sem, *, layer):
    i = pl.program_id(0)

    def weight_copies(e, slot):
        return [pltpu.make_async_copy(src.at[layer, e], dst.at[slot], sem.at[slot, n])
                for n, (src, dst) in enumerate(((w1_hbm, f1_ref), (w3_hbm, f3_ref), (w2_hbm, f2_ref)))]

    def start_into(e, slot):
        @pl.when(e >= 0)
        def _():
            for c in weight_copies(e, slot):
                c.start()

    @pl.when(i < nu_ref[0])
    def _():
        slot = slot_ref[i]

        @pl.when(first_ref[i] == 1)
        def _():
            @pl.when(i == 0)
            def _():
                start_into(be_ref[i], slot)
                start_into(nxt1_ref[i], lax.rem(slot + 1, MOE_WSLOTS))

            for c in weight_copies(be_ref[i], slot):
                c.wait()
            start_into(nxt2_ref[i], lax.rem(slot + 2, MOE_WSLOTS))

            b1_ref[...] = f1_ref[slot].astype(BF16)
            b3_ref[...] = f3_ref[slot].astype(BF16)
            b2_ref[...] = f2_ref[slot].astype(BF16)

        x_lo, x_hi = (half.astype(BF16) for half in _unpack_halves(x_ref[...]))

        def up(w_ref):
            return (jnp.dot(x_lo, w_ref[:HALF_D, :], preferred_element_type=F32)
                    + jnp.dot(x_hi, w_ref[HALF_D:, :], preferred_element_type=F32))

        a = _silu(up(b1_ref)) * up(b3_ref)
        y_ref[...] = _pack_halves(jnp.dot(a.astype(BF16), b2_ref[...], preferred_element_type=F32))

    @pl.when(i >= nu_ref[0])
    def _():
        y_ref[...] = jnp.zeros_like(y_ref)


def _expert_pallas(xs, nblk, w1, w3, w2, layer):
    n_slots = xs.shape[0]
    nb = n_slots // MOE_TB
    bend = jnp.cumsum(nblk)
    blocks = jnp.arange(nb, dtype=jnp.int32)
    block_e = jnp.minimum(jnp.searchsorted(bend, blocks, side='right'), MOE_EXPERTS - 1).astype(jnp.int32)
    first = (blocks == (bend - nblk)[block_e]).astype(jnp.int32)
    nonempty = nblk > 0
    slot_e = (jnp.cumsum(nonempty) - 1) % MOE_WSLOTS
    ids = jnp.where(nonempty, jnp.arange(MOE_EXPERTS), MOE_EXPERTS)
    pad = jnp.full((2,), MOE_EXPERTS, ids.dtype)
    after = jnp.concatenate([lax.cummin(ids, reverse=True), pad])
    nxt1 = after[1:MOE_EXPERTS + 1]
    nxt2 = jnp.concatenate([after, pad[:1]])[jnp.minimum(nxt1, MOE_EXPERTS) + 1]
    as_id = lambda e: jnp.where(e < MOE_EXPERTS, e, -1).astype(jnp.int32)
    n_used = bend[-1:].astype(jnp.int32)
    blk = lambda i, *s: jnp.minimum(i, s[-1][0] - 1)
    hbm = pl.BlockSpec(memory_space=pl.ANY)
    return pl.pallas_call(
        functools.partial(_expert_kernel, layer=layer),
        name="moe_experts",
        grid_spec=pltpu.PrefetchScalarGridSpec(
            num_scalar_prefetch=6,
            grid=(nb,),
            in_specs=[pl.BlockSpec((MOE_TB, HALF_D), lambda i, *s: (blk(i, *s), 0)), hbm, hbm, hbm],
            out_specs=pl.BlockSpec((MOE_TB, HALF_D), lambda i, *s: (i, 0)),
            scratch_shapes=[pltpu.VMEM((MOE_WSLOTS, D_MODEL, MOE_FF), F32),
                            pltpu.VMEM((MOE_WSLOTS, D_MODEL, MOE_FF), F32),
                            pltpu.VMEM((MOE_WSLOTS, MOE_FF, D_MODEL), F32),
                            pltpu.VMEM((D_MODEL, MOE_FF), BF16), pltpu.VMEM((D_MODEL, MOE_FF), BF16),
                            pltpu.VMEM((MOE_FF, D_MODEL), BF16),
                            pltpu.SemaphoreType.DMA((MOE_WSLOTS, 3))]),
        out_shape=jax.ShapeDtypeStruct((n_slots, HALF_D), jnp.uint32),
        compiler_params=pltpu.CompilerParams(
            dimension_semantics=("arbitrary",),
            vmem_limit_bytes=VMEM_LIMIT_BYTES),
    )(block_e, first, slot_e[block_e].astype(jnp.int32), as_id(nxt1)[block_e], as_id(nxt2)[block_e], n_used,
      xs, w1, w3, w2)


def _combine_kernel(dest_ref, h_ref, meta_ref, g_ref, yb_ref, o_ref, buf_ref, sem, *, final_norm):
    tc = h_ref.shape[0]
    base = pl.program_id(0) * tc * MOE_TOPK

    def row_copy(r, k):
        return pltpu.make_async_copy(yb_ref.at[pl.ds(dest_ref[base + MOE_TOPK * r + k], 1)],
                                     buf_ref.at[k, pl.ds(r, 1)], sem)

    def start(r, c):
        for k in range(MOE_TOPK):
            row_copy(r, k).start()
        return c

    def wait(r, c):
        for k in range(MOE_TOPK):
            row_copy(r, k).wait()
        return c

    lax.fori_loop(0, tc, start, 0, unroll=ROW_DMA_UNROLL)
    lax.fori_loop(0, tc, wait, 0, unroll=ROW_DMA_UNROLL)
    meta = meta_ref[...]
    y_lo = jnp.zeros((tc, HALF_D), F32)
    y_hi = jnp.zeros((tc, HALF_D), F32)
    for k in range(MOE_TOPK):
        lo, hi = _unpack_halves(buf_ref[k])
        gate = meta[:, META_GATE + k:META_GATE + k + 1]
        y_lo = y_lo + lo * gate
        y_hi = y_hi + hi * gate
    out = h_ref[...] + jnp.concatenate([y_lo, y_hi], axis=1)
    if final_norm:
        out = out * lax.rsqrt(jnp.mean(out * out, axis=-1, keepdims=True) + EPS) * g_ref[...]
    o_ref[...] = out


def _combine_pallas(h, meta, dest, yb, final_gain, *, final_norm, tc=256):
    n = h.shape[0]
    return pl.pallas_call(
        functools.partial(_combine_kernel, final_norm=final_norm),
        name="moe_combine",
        grid_spec=pltpu.PrefetchScalarGridSpec(
            num_scalar_prefetch=1,
            grid=(n // tc,),
            in_specs=[pl.BlockSpec((tc, D_MODEL), lambda i, d: (i, 0)),
                      pl.BlockSpec((tc, ROUTER_LANES), lambda i, d: (i, 0)),
                      pl.BlockSpec((1, D_MODEL), lambda i, d: (0, 0)),
                      pl.BlockSpec(memory_space=pl.ANY)],
            out_specs=pl.BlockSpec((tc, D_MODEL), lambda i, d: (i, 0)),
            scratch_shapes=[pltpu.VMEM((MOE_TOPK, tc, HALF_D), jnp.uint32), pltpu.SemaphoreType.DMA(())]),
        out_shape=jax.ShapeDtypeStruct((n, D_MODEL), F32),
        compiler_params=pltpu.CompilerParams(
            dimension_semantics=("arbitrary",),
            vmem_limit_bytes=VMEM_LIMIT_BYTES),
    )(dest, h, meta, final_gain.reshape(1, D_MODEL).astype(F32), yb)


def _moe_pallas(h, gain, w_rg, w_re, w1, w3, w2, layer, final_gain, *, final_norm):
    n = h.shape[0]
    nb = (n * MOE_TOPK) // MOE_TB + MOE_EXPERTS
    hp, meta, cnt = _router_pallas(h, gain, w_rg, w_re)
    expert = meta[:, META_E:META_E + MOE_TOPK].astype(jnp.int32)
    rank = meta[:, META_RANK:META_RANK + MOE_TOPK].astype(jnp.int32)
    counts = cnt[0, MOE_LANE0:MOE_LANE0 + MOE_EXPERTS].astype(jnp.int32)
    nblk = (counts + MOE_TB - 1) // MOE_TB
    seg_start = (jnp.cumsum(nblk) - nblk) * MOE_TB
    hit = expert[..., None] == jnp.arange(MOE_EXPERTS, dtype=jnp.int32)
    dest = (jnp.sum(jnp.where(hit, seg_start, 0), axis=-1) + rank).reshape(n * MOE_TOPK)
    xs = _dispatch_pallas(hp, dest, nb * MOE_TB)
    yb = _expert_pallas(xs, nblk, w1, w3, w2, layer)
    return _combine_pallas(h, meta, dest, yb, final_gain, final_norm=final_norm)


N_GATES = 4 * ML_HEADS
IN_GROUPS_BEFORE_GATES = 4
IN_GROUPS = 13
LANE = 128


def _win_relayout_kernel(*refs):
    parts, main_ref = refs[:-1], refs[-1]
    n = pl.program_id(1)
    wide = jnp.concatenate([p[...] for p in parts], axis=1)

    @pl.when(n < IN_GROUPS_BEFORE_GATES)
    def _():
        main_ref[...] = wide[:, :GROUP_W].astype(main_ref.dtype)

    @pl.when(n >= IN_GROUPS_BEFORE_GATES)
    def _():
        main_ref[...] = wide[:, N_GATES:N_GATES + GROUP_W].astype(main_ref.dtype)


def _win_relayout(w_in):
    depth, k, _ = w_in.shape
    per = GROUP_W // LANE
    g0 = IN_GROUPS_BEFORE_GATES * GROUP_W
    gate = w_in[:, :, g0:g0 + N_GATES].reshape(depth, k, 4, ML_HEADS).transpose(0, 1, 3, 2)
    gate = jnp.pad(gate, ((0, 0), (0, 0), (0, 0), (0, LANE - 4))).reshape(depth, k, ML_HEADS * LANE)
    width = w_in.shape[2]
    w_b = jnp.pad(w_in, ((0, 0), (0, 0), (0, -width % LANE)))
    part = lambda t: pl.BlockSpec((None, k, LANE), lambda l, n, t=t: (l, 0, per * n + t))
    main = pl.pallas_call(
        _win_relayout_kernel,
        name="w_in_relayout",
        grid=(depth, IN_GROUPS),
        in_specs=[part(t) for t in range(per + 1)],
        out_specs=pl.BlockSpec((None, k, GROUP_W), lambda l, n: (l, 0, n)),
        out_shape=jax.ShapeDtypeStruct((depth, k, IN_GROUPS * GROUP_W), BF16),
        compiler_params=pltpu.CompilerParams(
            dimension_semantics=("parallel", "parallel"),
            vmem_limit_bytes=VMEM_LIMIT_BYTES),
    )(*([w_b] * (per + 1)))
    return main, gate


COL_MLSTM, COL_HGRN, COL_FNET, COL_NA = 0, 4 * GROUP_W, 9 * GROUP_W, 10 * GROUP_W


def kernel(x, mem, norm_mix, norm_cross, norm_ffn, norm_final, norm_mem, w_in, mlstm_conv,
           mlstm_gate_bias, hgrn_lower_bound, na_rpb, group_gain, w_out, xa_wq, xa_wk, xa_wv, xa_wo,
           moe_router_group, moe_router_expert, moe_w1, moe_w3, moe_w2):
    b, s, d = x.shape
    mem_len = mem.shape[1]
    h = x.reshape(b * s, d)
    mem_f = mem.reshape(b * mem_len, d)
    lbs = jnp.cumsum(jax.nn.softmax(hgrn_lower_bound.astype(F32), axis=0), axis=0)
    lbs = lbs - lbs[0]
    fnet_tables = _fnet_tables(s)
    w_main, w_gate = _win_relayout(w_in)
    w_out_b, wq_b, wo_b = (w.astype(BF16) for w in (w_out, xa_wq, xa_wo))
    for l in range(DEPTH):
        proj, gates = _matmul(h, w_main, l, name="in_proj", gain=norm_mix[l], w_side=w_gate, out_dtype=BF16)
        y_ml = _mlstm_pallas(proj, gates, mlstm_conv[l], mlstm_gate_bias[l], batch=b, seq=s,
                             out_dtype=BF16)
        y_hg = _hgrn_pallas(proj, lbs[l], batch=b, seq=s, col0=COL_HGRN, out_dtype=BF16)
        y_fn = _fnet_pallas(proj, fnet_tables, batch=b, seq=s, col0=COL_FNET, out_dtype=BF16)
        y_na = _na_pallas(proj, na_rpb[l], batch=b, seq=s, col0=COL_NA, out_dtype=BF16)
        h = _matmul([y_ml, y_hg, y_fn, y_na], w_out_b, l, name="out_proj", gain=group_gain[l], residual=h)

        k = _matmul(mem_f, xa_wk, l, name="xa_k_proj", gain=norm_mem, out_dtype=BF16)
        v = _matmul(mem_f, xa_wv, l, name="xa_v_proj", gain=norm_mem, out_dtype=BF16)
        o = _xattn_pallas(h, norm_cross[l], wq_b, l, k, v, batch=b, seq=s, mem_len=mem_len)
        h = _matmul(o, wo_b, l, name="xa_o_proj", residual=h)

        h = _moe_pallas(h, norm_ffn[l], moe_router_group[l], moe_router_expert[l],
                        moe_w1, moe_w3, moe_w2, l, norm_final, final_norm=(l == DEPTH - 1))
    return h.reshape(b, s, d)
```

```python
import functools

import jax
import jax.numpy as jnp
import numpy as np
from jax import lax
from jax.experimental import pallas as pl
from jax.experimental.pallas import tpu as pltpu

D_MODEL = 2048
DEPTH = 4
GRID_W = 64
N_MIXERS = 4
GROUP_W = D_MODEL // N_MIXERS
ML_HEADS = 4
ML_DH = GROUP_W // ML_HEADS
ML_CHUNK = 64
HG_HEADS = 4
HG_CHUNK = 16
FN_GROUPS = 4
FN_CH = GROUP_W // FN_GROUPS
NA_HEADS = 8
NA_DH = GROUP_W // NA_HEADS
NA_KH = 8
NA_KW = 16
XA_HEADS = 4
XA_DH = D_MODEL // XA_HEADS
MOE_GROUPS = 4
MOE_PER_GROUP = 8
MOE_EXPERTS = MOE_GROUPS * MOE_PER_GROUP
MOE_TOPK = 2
MOE_BLOCK = 128
EPS = 1e-6
F32 = jnp.float32
BF16 = jnp.bfloat16

VMEM_LIMIT_BYTES = 56 * 1024 * 1024


def _mm_kernel(*refs, n_x, norm, residual, side):
    refs = list(refs)
    x_refs = [refs.pop(0) for _ in range(n_x)]
    g_ref = refs.pop(0) if norm else None
    w_ref = refs.pop(0)
    ws_ref = refs.pop(0) if side else None
    r_ref = refs.pop(0) if residual else None
    o_ref = refs.pop(0)
    os_ref = refs.pop(0) if side else None
    xn_ref = refs.pop(0)

    @pl.when(pl.program_id(1) == 0)
    def _():
        col = 0
        for x_ref in x_refs:
            x = x_ref[...].astype(F32)
            kx = x.shape[1]
            if norm:
                ms = jnp.mean(x * x, axis=-1, keepdims=True)
                x = x * lax.rsqrt(ms + EPS) * g_ref[:, col:col + kx]
            xn_ref[:, col:col + kx] = x.astype(BF16)
            col += kx
        if side:
            os_ref[...] = jnp.dot(xn_ref[...], ws_ref[...].astype(BF16),
                                  preferred_element_type=F32).astype(os_ref.dtype)

    acc = jnp.dot(xn_ref[...], w_ref[...].astype(BF16), preferred_element_type=F32)
    if residual:
        acc = acc + r_ref[...]
    o_ref[...] = acc.astype(o_ref.dtype)


def _matmul(xs, w, layer, *, name, gain=None, residual=None, w_side=None, tm=1024, tn=512, out_dtype=F32):
    if not isinstance(xs, (list, tuple)):
        xs = [xs]
    m = xs[0].shape[0]
    k = sum(x.shape[1] for x in xs)
    n = w.shape[2]
    tm = min(tm, m)
    tn = min(tn, n)
    assert m % tm == 0 and n % tn == 0 and w.shape[1] == k, (m, n, k, tm, tn)
    norm = gain is not None
    has_res = residual is not None
    side = w_side is not None
    in_specs = [pl.BlockSpec((tm, x.shape[1]), lambda i, j: (i, 0)) for x in xs]
    args = list(xs)
    if norm:
        in_specs.append(pl.BlockSpec((1, k), lambda i, j: (0, 0)))
        args.append(gain.reshape(1, k).astype(F32))
    in_specs.append(pl.BlockSpec((None, k, tn), lambda i, j: (layer, 0, j)))
    args.append(w)
    if side:
        in_specs.append(pl.BlockSpec((None, k, w_side.shape[2]), lambda i, j: (layer, 0, 0)))
        args.append(w_side)
    if has_res:
        in_specs.append(pl.BlockSpec((tm, tn), lambda i, j: (i, j)))
        args.append(residual)
    out_specs = pl.BlockSpec((tm, tn), lambda i, j: (i, j))
    out_shape = jax.ShapeDtypeStruct((m, n), out_dtype)
    if side:
        out_specs = [out_specs, pl.BlockSpec((tm, w_side.shape[2]), lambda i, j: (i, 0))]
        out_shape = [out_shape, jax.ShapeDtypeStruct((m, w_side.shape[2]), F32)]
    return pl.pallas_call(
        functools.partial(_mm_kernel, n_x=len(xs), norm=norm, residual=has_res, side=side),
        name=name,
        grid=(m // tm, n // tn),
        in_specs=in_specs,
        out_specs=out_specs,
        out_shape=out_shape,
        scratch_shapes=[pltpu.VMEM((tm, k), BF16)],
        compiler_params=pltpu.CompilerParams(
            dimension_semantics=("parallel", "arbitrary"),
            vmem_limit_bytes=VMEM_LIMIT_BYTES),
    )(*args)


NA_QROWS = 4
NA_WROWS = 12


def _na_groups(rows):
    tables, plan = [], []
    for r0 in range(0, rows, NA_QROWS):
        band0 = lambda r: min(max(r - NA_KH // 2, 0), rows - NA_KH)
        kr0 = min(band0(r0), rows - NA_WROWS)
        assert band0(r0 + NA_QROWS - 1) + NA_KH <= kr0 + NA_WROWS
        dr = np.full((NA_QROWS, NA_WROWS), -1, np.int64)
        for i in range(NA_QROWS):
            for j in range(NA_WROWS):
                if 0 <= kr0 + j - band0(r0 + i) < NA_KH:
                    dr[i, j] = kr0 + j - (r0 + i) + NA_KH - 1
        key = dr.tobytes()
        if key not in [t.tobytes() for t in tables]:
            tables.append(dr)
        plan.append((r0, kr0, [t.tobytes() for t in tables].index(key)))
    return plan, np.stack(tables)


def _na_bias_tables(rpb, dr_tables):
    c = np.arange(GRID_W)
    dc = np.clip(c[None, :] - c[:, None] + NA_KW - 1, 0, 2 * NA_KW - 2)
    onehot = (dc[None] == np.arange(2 * NA_KW - 1)[:, None, None]).astype(np.float32)
    col_start = np.clip(c - NA_KW // 2, 0, GRID_W - NA_KW)
    col_ok = (c[None, :] >= col_start[:, None]) & (c[None, :] < col_start[:, None] + NA_KW)
    t = jnp.einsum('hrd,dqk->hrqk', rpb.astype(F32), onehot, precision=lax.Precision.HIGHEST)
    t = jnp.where(col_ok[None, None], t, -jnp.inf)
    t = jnp.concatenate([t, jnp.full_like(t[:, :1], -jnp.inf)], axis=1)
    idx = np.where(dr_tables < 0, t.shape[1] - 1, dr_tables)
    pairs = idx.reshape(idx.shape[0], idx.shape[1], -1, 2)
    uniq = sorted({(int(a), int(b)) for a, b in pairs.reshape(-1, 2)})
    pair_idx = np.array([[[uniq.index((int(a), int(b))) for a, b in row] for row in typ] for typ in pairs])
    table = jnp.stack([jnp.concatenate([t[:, a], t[:, b]], axis=-1) for a, b in uniq], axis=1)
    return table, pair_idx


def _na_kernel(q_ref, k_ref, v_ref, tab_ref, o_ref, bias_ref, *, plan, pair_idx):
    nq = NA_QROWS * GRID_W
    nk = NA_WROWS * GRID_W
    for hh in range(2):
        for typ in range(pair_idx.shape[0]):
            for i in range(pair_idx.shape[1]):
                for jj in range(pair_idx.shape[2]):
                    bias_ref[hh, typ, i * GRID_W:(i + 1) * GRID_W, jj * 2 * GRID_W:(jj + 1) * 2 * GRID_W] = (
                        tab_ref[hh, int(pair_idx[typ, i, jj])])
    for r0, kr0, typ in plan:
        q = q_ref[r0 * GRID_W:r0 * GRID_W + nq, :].astype(F32) * (NA_DH ** -0.5)
        kb = k_ref[kr0 * GRID_W:kr0 * GRID_W + nk, :]
        vb = v_ref[kr0 * GRID_W:kr0 * GRID_W + nk, :]
        outs = []
        for hh in range(2):
            sl = slice(hh * NA_DH, (hh + 1) * NA_DH)
            s = lax.dot_general(q[:, sl].astype(BF16), kb[:, sl].astype(BF16),
                                (((1,), (1,)), ((), ())), preferred_element_type=F32)
            s = s + bias_ref[hh, typ]
            m = jnp.max(s, axis=-1, keepdims=True)
            p = jnp.exp(s - m)
            l = jnp.sum(p, axis=-1, keepdims=True)
            o = jnp.dot(p.astype(BF16), vb[:, sl].astype(BF16), preferred_element_type=F32)
            outs.append(o / l)
        o_ref[r0 * GRID_W:r0 * GRID_W + nq, :] = jnp.concatenate(outs, axis=-1).astype(o_ref.dtype)


def _na_pallas(proj, rpb, *, batch, seq, col0, out_dtype=F32):
    rows = seq // GRID_W
    pair_w = 2 * NA_DH
    cb = col0 // pair_w
    gb = GROUP_W // pair_w
    plan, dr_tables = _na_groups(rows)
    table, pair_idx = _na_bias_tables(rpb, dr_tables)
    spec = lambda g: pl.BlockSpec((seq, pair_w), lambda b, p, g=g: (b, cb + g * gb + p))
    return pl.pallas_call(
        functools.partial(_na_kernel, plan=plan, pair_idx=pair_idx),
        name="na_attention",
        grid=(batch, NA_HEADS // 2),
        in_specs=[spec(0), spec(1), spec(2),
                  pl.BlockSpec((2,) + table.shape[1:], lambda b, p: (p, 0, 0, 0))],
        out_specs=pl.BlockSpec((seq, pair_w), lambda b, p: (b, p)),
        out_shape=jax.ShapeDtypeStruct((batch * seq, GROUP_W), out_dtype),
        scratch_shapes=[pltpu.VMEM((2, dr_tables.shape[0], NA_QROWS * GRID_W, NA_WROWS * GRID_W), F32)],
        compiler_params=pltpu.CompilerParams(
            dimension_semantics=("parallel", "parallel"),
            vmem_limit_bytes=VMEM_LIMIT_BYTES),
    )(proj, proj, proj, table)


ML_L = 128
ML_GATE_LANES = 128


def _log_sigmoid(x):
    return jnp.minimum(x, 0.0) - jnp.log(1.0 + jnp.exp(-jnp.abs(x)))


def _silu(x):
    return x * jax.nn.sigmoid(x)


_NT = (((1,), (1,)), ((), ()))
_TN = (((0,), (0,)), ((), ()))


def _split3(x):
    hi = x.astype(BF16)
    r1 = x - hi.astype(F32)
    mid = r1.astype(BF16)
    lo = (r1 - mid.astype(F32)).astype(BF16)
    return hi, mid, lo


def _dot_exact_lhs(a, x, dims=None):
    a = a.astype(BF16)
    if dims is None:
        return sum(jnp.dot(a, p, preferred_element_type=F32) for p in _split3(x))
    return sum(lax.dot_general(a, p, dims, preferred_element_type=F32) for p in _split3(x))


def _dot_exact_rhs(x, a, dims):
    a = a.astype(BF16)
    return sum(lax.dot_general(p, a, dims, preferred_element_type=F32) for p in _split3(x))


ML_GROUP = 4
ML_PAD = 16


def _mlstm_local(q, k, v, gcol, tri, mask_w, gi, end_row):
    L = ML_L
    cum_col = _dot_exact_lhs(tri, _log_sigmoid(gcol))
    wide = lambda col: jnp.broadcast_to(col, (L, L))
    b_cols, ig_cols, rows = [], [], []
    for g in range(len(q)):
        lanes = slice(g * ML_GATE_LANES, (g + 1) * ML_GATE_LANES)
        b_cols.append(cum_col[:, g * ML_GATE_LANES + gi + 1:g * ML_GATE_LANES + gi + 2])
        ig_cols.append(gcol[:, g * ML_GATE_LANES + gi:g * ML_GATE_LANES + gi + 1])
        rows.append(gcol[:, lanes].T[gi:gi + 1, :] - cum_col[:, lanes].T[gi + 1:gi + 2, :])
    b_w = jnp.concatenate([wide(b) for b in b_cols], axis=1)
    row_w = jnp.concatenate(rows, axis=1)
    dmat = jnp.where(mask_w != 0.0, b_w + row_w, -jnp.inf)
    m_in_w = jnp.concatenate([wide(jnp.max(dmat[:, g * L:(g + 1) * L], axis=-1, keepdims=True))
                              for g in range(len(q))], axis=1)
    w = jnp.exp(dmat - m_in_w)
    ones = jnp.ones((L, ML_DH), BF16)
    out = []
    for g in range(len(q)):
        lanes = slice(g * L, (g + 1) * L)
        kb = k[g].astype(BF16)
        v_aug = jnp.concatenate([v[g].astype(BF16), ones], axis=1)
        qk = lax.dot_general(q[g].astype(BF16), kb, _NT, preferred_element_type=F32) * w[:, lanes]
        res = jnp.dot(qk.astype(BF16), v_aug, preferred_element_type=F32)
        b_end = b_cols[g][end_row:end_row + 1, :]
        a_col = b_end - b_cols[g] + ig_cols[g]
        m_loc = jnp.max(a_col, axis=0, keepdims=True)
        kw = k[g] * jnp.exp(a_col - m_loc)
        st = lax.dot_general(v_aug, kw.astype(BF16), _TN, preferred_element_type=F32)
        out.append(dict(m_in=m_in_w[:, lanes], num=res[:, :ML_DH], den=res[:, ML_DH:], b=b_w[:, lanes],
                        ct=st[:ML_DH], n=st[ML_DH:ML_DH + 1], m_loc=m_loc, b_end=b_end))
    return out


def _mlstm_carry(q, m_in, num_in, den_in, b, ct, n, m_prev):
    inter = b + m_prev
    m_t = jnp.maximum(inter, m_in)
    s_in = jnp.exp(m_in - m_t)
    s_inter = jnp.exp(inter - m_t)
    rhs = jnp.concatenate([ct.astype(BF16), jnp.broadcast_to(n, (ML_PAD, ML_DH)).astype(BF16)], axis=0)
    both = lax.dot_general(q.astype(BF16), rhs, _NT, preferred_element_type=F32)
    num = s_in * num_in + s_inter * both[:, :ML_DH]
    den = s_in * den_in + s_inter * both[:, ML_DH:ML_DH + 1]
    return num / jnp.maximum(jnp.abs(den), jnp.exp(-m_t))


def _mlstm_kernel(q_ref, k_ref, v_ref, o_ref, cwq_ref, cwk_ref, gcol_ref, bcol_ref,
                  y_ref, qs_ref, ks_ref, num_ref, den_ref, min_ref, b_ref, ctl_ref, stat_ref, *, seq):
    L = ML_L
    nc = seq // L
    t_idx = lax.broadcasted_iota(jnp.int32, (seq, 1), 0)

    def conv_silu(x, w):
        prev = jnp.where(t_idx == 0, 0.0, pltpu.roll(x, 1, axis=0))
        nxt = jnp.where(t_idx == seq - 1, 0.0, pltpu.roll(x, seq - 1, axis=0))
        return _silu(prev * w[0:1, :] + x * w[1:2, :] + nxt * w[2:3, :])

    qs_ref[...] = conv_silu(q_ref[...].astype(F32), cwq_ref[...])
    ks_ref[...] = conv_silu(k_ref[...].astype(F32), cwk_ref[...]) * (ML_DH ** -0.5)

    ti = lax.broadcasted_iota(jnp.int32, (L, L), 0)
    si = lax.broadcasted_iota(jnp.int32, (L, L), 1)
    tris = (jnp.where(si <= ti, 1.0, 0.0), jnp.where(si >= ti, 1.0, 0.0))
    masks_w = [jnp.concatenate([t] * ML_GROUP, axis=1) for t in tris]

    for direction in (0, 1):
        for c0 in range(0, nc, ML_GROUP):
            rows = [pl.ds((c0 + g) * L, L) for g in range(ML_GROUP)]
            gcol = jnp.concatenate([gcol_ref[r, :] + bcol_ref[...] for r in rows], axis=1)
            local = _mlstm_local([qs_ref[r, :] for r in rows], [ks_ref[r, :] for r in rows],
                                 [v_ref[r, :].astype(F32) for r in rows], gcol,
                                 tris[direction], masks_w[direction], 2 * direction,
                                 L - 1 if direction == 0 else 0)
            for g, (r, loc) in enumerate(zip(rows, local)):
                idx = direction * nc + c0 + g
                num_ref[direction, r, :] = loc["num"]
                den_ref[direction, r, :] = loc["den"]
                min_ref[direction, r, :] = loc["m_in"]
                b_ref[direction, r, :] = loc["b"]
                ctl_ref[idx] = loc["ct"]
                stat_ref[idx, 0:1, :] = loc["n"]
                stat_ref[idx, 1:2, :] = jnp.broadcast_to(loc["m_loc"], (1, ML_DH))
                stat_ref[idx, 2:3, :] = jnp.broadcast_to(loc["b_end"], (1, ML_DH))

    state = [(jnp.zeros((ML_DH, ML_DH), F32), jnp.zeros((1, ML_DH), F32), jnp.zeros((1, ML_DH), F32))] * 2
    for i in range(nc):
        for direction in (0, 1):
            c = i if direction == 0 else nc - 1 - i
            r = pl.ds(c * L, L)
            idx = direction * nc + c
            ct, n, m = state[direction]
            num_ref[direction, r, :] = _mlstm_carry(qs_ref[r, :], min_ref[direction, r, :],
                                                    num_ref[direction, r, :], den_ref[direction, r, :],
                                                    b_ref[direction, r, :], ct, n, m)
            n_loc, m_loc, b_end = stat_ref[idx, 0:1, :], stat_ref[idx, 1:2, :], stat_ref[idx, 2:3, :]
            m_new = jnp.maximum(b_end + m, m_loc)
            s_old = jnp.exp(b_end + m - m_new)
            s_new = jnp.exp(m_loc - m_new)
            state[direction] = (s_old * ct + s_new * ctl_ref[idx], s_old * n + s_new * n_loc, m_new)
    y_ref[...] = (jax.nn.sigmoid(o_ref[...].astype(F32)) * (num_ref[0] + num_ref[1])).astype(y_ref.dtype)


def _mlstm_pallas(proj, gates, conv_w, gate_b, *, batch, seq, out_dtype=F32):
    d = ML_DH
    hb = GROUP_W // d
    gb4 = gate_b.astype(F32).reshape(4, ML_HEADS).T
    gb_col = jnp.pad(gb4, ((0, 0), (0, ML_GATE_LANES - 4))).reshape(ML_HEADS, 1, ML_GATE_LANES)
    spec = lambda grp: pl.BlockSpec((seq, d), lambda b, h, grp=grp: (b, grp * hb + h))
    f32 = lambda *shape: pltpu.VMEM(shape, F32)
    return pl.pallas_call(
        functools.partial(_mlstm_kernel, seq=seq),
        name="mlstm",
        grid=(batch, ML_HEADS),
        in_specs=[spec(0), spec(1), spec(2), spec(3),
                  pl.BlockSpec((3, d), lambda b, h: (0, h)),
                  pl.BlockSpec((3, d), lambda b, h: (0, hb + h)),
                  pl.BlockSpec((seq, ML_GATE_LANES), lambda b, h: (b, h)),
                  pl.BlockSpec((None, 1, ML_GATE_LANES), lambda b, h: (h, 0, 0))],
        out_specs=pl.BlockSpec((seq, d), lambda b, h: (b, h)),
        out_shape=jax.ShapeDtypeStruct((batch * seq, GROUP_W), out_dtype),
        scratch_shapes=[f32(seq, d), f32(seq, d), f32(2, seq, d), f32(2, seq, d), f32(2, seq, d),
                        f32(2, seq, d), f32(2 * (seq // ML_L), d, d), f32(2 * (seq // ML_L), 8, d)],
        compiler_params=pltpu.CompilerParams(
            dimension_semantics=("parallel", "parallel"),
            vmem_limit_bytes=VMEM_LIMIT_BYTES),
    )(proj, proj, proj, proj, conv_w.astype(F32), conv_w.astype(F32), gates, gb_col)


HG_L = 128
HG_LEAF = 8


HG_GROUP = 4
HG_NLEAF = HG_L // HG_LEAF
HG_LEVELS = 4


def _hgrn_ref_rows(p_ref, chunks, backward):
    def rows(first, count, stride):
        return jnp.concatenate([p_ref[pl.ds(c * HG_L + first, count, stride=stride), :] for c in chunks], axis=1)

    leaf = rows(HG_LEAF // 2 if backward else HG_LEAF // 2 - 1, HG_NLEAF, HG_LEAF)
    levels = []
    m = HG_L // 2
    for _ in range(HG_LEVELS):
        blocks = HG_L // (2 * m)
        r = rows(m if backward else m - 1, blocks, 2 * m) if blocks > 1 else rows(m if backward else m - 1, 1, 1)
        levels.append(jnp.repeat(r, HG_NLEAF // blocks, axis=0))
        m //= 2
    return leaf, levels


def _hgrn_chunks(q, k, v, p, ref_leaf, ref_levels, masks):
    L = HG_L
    dk = HG_L
    groups = q.shape[1] // dk
    spread = lambda e: jnp.broadcast_to(e[:, None, :], (HG_NLEAF, HG_LEAF, e.shape[1])).reshape(L, e.shape[1])
    d = p - spread(ref_leaf)
    q_leaf = q * jnp.exp(d)
    k_leaf = k * jnp.exp(-d)
    pairs = [(q_leaf.astype(BF16), k_leaf.astype(BF16), masks[HG_LEVELS])]
    for li in range(HG_LEVELS):
        gap = ref_leaf - ref_levels[li]
        qh = q_leaf * spread(jnp.exp(jnp.minimum(gap, 0.0)))
        kh = k_leaf * spread(jnp.exp(jnp.minimum(-gap, 0.0)))
        pairs.append((qh.astype(BF16), kh.astype(BF16), masks[li]))
    vb = v.astype(BF16)
    outs = []
    for g in range(groups):
        lanes = slice(g * dk, (g + 1) * dk)
        attn = jnp.zeros((L, L), F32)
        for qh, kh, mask in pairs:
            s = lax.dot_general(qh[:, lanes], kh[:, lanes], _NT, preferred_element_type=F32)
            attn = attn + jnp.where(mask != 0.0, s, 0.0)
        outs.append(jnp.dot(attn.astype(BF16), vb[:, lanes], preferred_element_type=F32))
    return jnp.concatenate(outs, axis=1)


def _hgrn_carry(q, k, v, p, st, backward):
    L = HG_L
    o = lax.dot_general((q * jnp.exp(p)).astype(BF16), st.astype(BF16), _NT, preferred_element_type=F32)
    p_end = p[0:1, :] if backward else p[L - 1:L, :]
    kd = k * jnp.exp(p_end - p)
    st = st * jnp.exp(p_end) + lax.dot_general(v.astype(BF16), kd.astype(BF16), _TN,
                                               preferred_element_type=F32)
    return o, st


def _hgrn_masks(backward):
    L = HG_L
    ti = lax.broadcasted_iota(jnp.int32, (L, L), 0)
    si = lax.broadcasted_iota(jnp.int32, (L, L), 1)
    if backward:
        ti, si = si, ti
    one = lambda cond: jnp.where(cond, 1.0, 0.0)
    masks = []
    m = L // 2
    while m >= HG_LEAF:
        same = one((ti // (2 * m)) == (si // (2 * m)))
        masks.append(same * one((ti % (2 * m)) >= m) * one((si % (2 * m)) < m))
        m //= 2
    masks.append(one((ti // HG_LEAF) == (si // HG_LEAF)) * one(si <= ti))
    return masks


def _hgrn_kernel(q_ref, ff_ref, fb_ref, i_ref, g_ref, lb_ref, y_ref,
                 qs_ref, lf_ref, kf_ref, lbk_ref, kb_ref, of_ref, ob_ref, *, seq):
    L = HG_L
    nc = seq // L
    lb = lb_ref[...]
    log_lb = jnp.log(lb)
    log1m_lb = jnp.log1p(-lb)

    def forget(fp):
        ls = _log_sigmoid(fp)
        a = log_lb
        c = log1m_lb + ls
        logf = jnp.maximum(a, c) + jnp.log(1.0 + jnp.exp(-jnp.abs(a - c)))
        return logf, (1.0 - lb) * jnp.exp(ls - fp)

    qs_ref[...] = _silu(q_ref[...].astype(F32))
    lf_ref[...], kf_ref[...] = forget(ff_ref[...].astype(F32))
    lbk_ref[...], kb_ref[...] = forget(fb_ref[...].astype(F32))

    ti = lax.broadcasted_iota(jnp.int32, (L, L), 0)
    si = lax.broadcasted_iota(jnp.int32, (L, L), 1)
    tril = (si <= ti).astype(F32)
    triu = (si >= ti).astype(F32)
    masks_f = _hgrn_masks(False)
    masks_b = _hgrn_masks(True)

    def side_by_side(ref, c0):
        return jnp.concatenate([ref[pl.ds((c0 + g) * L, L), :].astype(F32) for g in range(HG_GROUP)], axis=1)

    def scatter_back(ref, c0, wide):
        for g in range(HG_GROUP):
            ref[pl.ds((c0 + g) * L, L), :] = wide[:, g * L:(g + 1) * L]

    directions = ((kf_ref, lf_ref, of_ref, tril, masks_f, False), (kb_ref, lbk_ref, ob_ref, triu, masks_b, True))
    for c0 in range(0, nc, HG_GROUP):
        for _, f_ref, _, tri, _, _ in directions:
            scatter_back(f_ref, c0, _dot_exact_lhs(tri, side_by_side(f_ref, c0)))
    for c0 in range(0, nc, HG_GROUP):
        q = side_by_side(qs_ref, c0)
        v = side_by_side(i_ref, c0)
        for k_ref, f_ref, o_ref, _, masks, backward in directions:
            ref_leaf, ref_levels = _hgrn_ref_rows(f_ref, range(c0, c0 + HG_GROUP), backward)
            scatter_back(o_ref, c0, _hgrn_chunks(q, side_by_side(k_ref, c0), v, side_by_side(f_ref, c0),
                                                 ref_leaf, ref_levels, masks))
    st_f = jnp.zeros((L, L), F32)
    st_b = jnp.zeros((L, L), F32)
    for i in range(nc):
        sl = pl.ds(i * L, L)
        o, st_f = _hgrn_carry(qs_ref[sl, :], kf_ref[sl, :], i_ref[sl, :].astype(F32), lf_ref[sl, :], st_f, False)
        of_ref[sl, :] += o
        sl = pl.ds((nc - 1 - i) * L, L)
        o, st_b = _hgrn_carry(qs_ref[sl, :], kb_ref[sl, :], i_ref[sl, :].astype(F32), lbk_ref[sl, :], st_b, True)
        ob_ref[sl, :] += o
    o = of_ref[...] + ob_ref[...]
    o = o * lax.rsqrt(jnp.mean(o * o, axis=-1, keepdims=True) + EPS)
    y_ref[...] = (o * _silu(g_ref[...].astype(F32))).astype(y_ref.dtype)


def _hgrn_pallas(proj, lb, *, batch, seq, col0, out_dtype=F32):
    d = GROUP_W // HG_HEADS
    hb = GROUP_W // d
    cb = col0 // d
    spec = lambda grp: pl.BlockSpec((seq, d), lambda b, h, grp=grp: (b, cb + grp * hb + h))
    f32 = lambda *shape: pltpu.VMEM(shape, F32)
    return pl.pallas_call(
        functools.partial(_hgrn_kernel, seq=seq),
        name="hgrn2",
        grid=(batch, HG_HEADS),
        in_specs=[spec(0), spec(1), spec(2), spec(3), spec(4),
                  pl.BlockSpec((1, d), lambda b, h: (0, h))],
        out_specs=pl.BlockSpec((seq, d), lambda b, h: (b, h)),
        out_shape=jax.ShapeDtypeStruct((batch * seq, GROUP_W), out_dtype),
        scratch_shapes=[f32(seq, d)] * 7,
        compiler_params=pltpu.CompilerParams(
            dimension_semantics=("parallel", "parallel"),
            vmem_limit_bytes=VMEM_LIMIT_BYTES),
    )(proj, proj, proj, proj, proj, lb.astype(F32).reshape(1, GROUP_W))


DFT_SPLIT = 32


def _dft_cos_sin(n):
    def direct(rows, period):
        k = (rows[:, None] * jnp.arange(n, dtype=jnp.int32)[None, :]) % period
        ang = k.astype(F32) * (2.0 * np.pi / period)
        return jnp.cos(ang), jnp.sin(ang)

    if n % DFT_SPLIT or n // DFT_SPLIT < DFT_SPLIT:
        return direct(jnp.arange(n, dtype=jnp.int32), n)
    ca, sa = direct(jnp.arange(n // DFT_SPLIT, dtype=jnp.int32), n // DFT_SPLIT)
    cb, sb = direct(jnp.arange(DFT_SPLIT, dtype=jnp.int32), n)
    cos = ca[:, None, :] * cb[None, :, :] - sa[:, None, :] * sb[None, :, :]
    sin = sa[:, None, :] * cb[None, :, :] + ca[:, None, :] * sb[None, :, :]
    return cos.reshape(n, n), sin.reshape(n, n)


def _fnet_tables(seq):
    cc, sc = _dft_cos_sin(FN_CH)
    eye = jnp.eye(FN_GROUPS, dtype=F32)
    chan = jnp.concatenate([jnp.kron(eye, cc), jnp.kron(eye, sc)], axis=1)
    cs, ss = _dft_cos_sin(seq)
    pos = jnp.concatenate([cs, -ss], axis=1) * ((seq * FN_CH) ** -0.5)
    return chan.astype(BF16), pos.astype(BF16)


def _fnet_chan_kernel(u_ref, dft_ref, v_ref):
    r = jnp.dot(u_ref[...].astype(BF16), dft_ref[...], preferred_element_type=F32)
    v_ref[0] = r[:, :GROUP_W].astype(v_ref.dtype)
    v_ref[1] = r[:, GROUP_W:].astype(v_ref.dtype)


def _fnet_pos_kernel(w_ref, v_ref, o_ref):
    o_ref[...] = jnp.dot(w_ref[...], v_ref[...], preferred_element_type=F32).astype(o_ref.dtype)


def _fnet_pallas(proj, tables, *, batch, seq, col0, out_dtype=F32, ts=512):
    chan, pos = tables
    cb = col0 // GROUP_W
    nt = seq // ts
    v = pl.pallas_call(
        _fnet_chan_kernel,
        name="fnet_channel_dft",
        grid=(batch, nt),
        in_specs=[pl.BlockSpec((ts, GROUP_W), lambda b, i: (b * nt + i, cb)),
                  pl.BlockSpec((GROUP_W, 2 * GROUP_W), lambda b, i: (0, 0))],
        out_specs=pl.BlockSpec((2, ts, GROUP_W), lambda b, i: (0, i, b)),
        out_shape=jax.ShapeDtypeStruct((2, seq, batch * GROUP_W), BF16),
        compiler_params=pltpu.CompilerParams(
            dimension_semantics=("parallel", "parallel"),
            vmem_limit_bytes=VMEM_LIMIT_BYTES),
    )(proj, chan)
    v = v.reshape(2 * seq, batch * GROUP_W)
    return pl.pallas_call(
        _fnet_pos_kernel,
        name="fnet_position_dft",
        grid=(nt, batch),
        in_specs=[pl.BlockSpec((ts, 2 * seq), lambda i, b: (i, 0)),
                  pl.BlockSpec((2 * seq, GROUP_W), lambda i, b: (0, b))],
        out_specs=pl.BlockSpec((ts, GROUP_W), lambda i, b: (b * nt + i, 0)),
        out_shape=jax.ShapeDtypeStruct((batch * seq, GROUP_W), out_dtype),
        compiler_params=pltpu.CompilerParams(
            dimension_semantics=("parallel", "parallel"),
            vmem_limit_bytes=VMEM_LIMIT_BYTES),
    )(pos, v)


def _xattn_kernel(x_ref, g_ref, wq_ref, k_ref, v_ref, o_ref, xn_ref):
    @pl.when(pl.program_id(1) == 0)
    def _():
        x = x_ref[...].astype(F32)
        ms = jnp.mean(x * x, axis=-1, keepdims=True)
        xn_ref[...] = (x * lax.rsqrt(ms + EPS) * g_ref[...]).astype(BF16)

    q = jnp.dot(xn_ref[...], wq_ref[...].astype(BF16), preferred_element_type=F32)
    s = lax.dot_general(q.astype(BF16), k_ref[...].astype(BF16), _NT,
                        preferred_element_type=F32) * (XA_DH ** -0.5)
    p = jnp.exp(s - jnp.max(s, axis=-1, keepdims=True))
    l = jnp.sum(p, axis=-1, keepdims=True)
    o = jnp.dot(p.astype(BF16), v_ref[...].astype(BF16), preferred_element_type=F32)
    o_ref[...] = (o / l).astype(o_ref.dtype)


def _xattn_pallas(h, gain, wq, layer, k, v, *, batch, seq, mem_len, tm=1024, out_dtype=BF16):
    tiles = seq // tm
    return pl.pallas_call(
        _xattn_kernel,
        name="cross_attention",
        grid=(batch * tiles, XA_HEADS),
        in_specs=[pl.BlockSpec((tm, D_MODEL), lambda i, j: (i, 0)),
                  pl.BlockSpec((1, D_MODEL), lambda i, j: (0, 0)),
                  pl.BlockSpec((None, D_MODEL, XA_DH), lambda i, j: (layer, 0, j)),
                  pl.BlockSpec((mem_len, XA_DH), lambda i, j: (i // tiles, j)),
                  pl.BlockSpec((mem_len, XA_DH), lambda i, j: (i // tiles, j))],
        out_specs=pl.BlockSpec((tm, XA_DH), lambda i, j: (i, j)),
        out_shape=jax.ShapeDtypeStruct((batch * seq, D_MODEL), out_dtype),
        scratch_shapes=[pltpu.VMEM((tm, D_MODEL), BF16)],
        compiler_params=pltpu.CompilerParams(
            dimension_semantics=("parallel", "arbitrary"),
            vmem_limit_bytes=VMEM_LIMIT_BYTES),
    )(h, gain.reshape(1, D_MODEL).astype(F32), wq, k, v)


MOE_FF = D_MODEL // 4
MOE_TB = 256
MOE_LANE0 = MOE_GROUPS
ROUTER_LANES = 128
META_E, META_RANK, META_GATE = 0, 2, 4
HALF_D = D_MODEL // 2
ROW_DMA_UNROLL = 8


def _pack_halves(x):
    bits = lax.bitcast_convert_type(x.astype(BF16).astype(F32), jnp.uint32)
    h = x.shape[1] // 2
    return (bits[:, :h] >> 16) | (bits[:, h:] & jnp.uint32(0xFFFF0000))


def _unpack_halves(p):
    return (lax.bitcast_convert_type(p << 16, F32),
            lax.bitcast_convert_type(p & jnp.uint32(0xFFFF0000), F32))


def _router_kernel(x_ref, g_ref, wr_ref, hp_ref, meta_ref, cnt_ref, carry_ref):
    tm = x_ref.shape[0]

    @pl.when(pl.program_id(0) == 0)
    def _():
        carry_ref[...] = jnp.zeros_like(carry_ref)

    x = x_ref[...].astype(F32)
    hn = x * lax.rsqrt(jnp.mean(x * x, axis=-1, keepdims=True) + EPS) * g_ref[...]
    hb = hn.astype(BF16)
    logits = jnp.dot(hb, wr_ref[...].astype(BF16), preferred_element_type=F32)
    hp_ref[...] = _pack_halves(hn)

    lane = lax.broadcasted_iota(jnp.int32, (tm, ROUTER_LANES), 1).astype(F32)
    ninf = -jnp.inf
    first = lambda hit: jnp.min(jnp.where(hit, lane, float(ROUTER_LANES)), axis=-1, keepdims=True)
    gl = jnp.where(lane < MOE_GROUPS, logits, ninf)
    gmax = jnp.max(gl, axis=-1, keepdims=True)
    gidx = first(gl == gmax)
    g_gate = 1.0 / jnp.sum(jnp.exp(gl - gmax), axis=-1, keepdims=True)
    off = lane - (MOE_LANE0 + MOE_PER_GROUP * gidx)
    el = jnp.where(jnp.abs(2.0 * off - (MOE_PER_GROUP - 1)) < MOE_PER_GROUP, logits, ninf)
    v1 = jnp.max(el, axis=-1, keepdims=True)
    l1 = first(el == v1)
    el2 = jnp.where(lane == l1, ninf, el)
    v2 = jnp.max(el2, axis=-1, keepdims=True)
    l2 = first(el2 == v2)
    t = jnp.exp(v2 - v1)
    gate1 = g_gate / (1.0 + t)
    gate2 = g_gate * t / (1.0 + t)

    oh = jnp.where(lane == l1, 1.0, 0.0) + jnp.where(lane == l2, 1.0, 0.0)
    ti = lax.broadcasted_iota(jnp.int32, (tm, tm), 0)
    si = lax.broadcasted_iota(jnp.int32, (tm, tm), 1)
    before = jnp.where(si < ti, 1.0, 0.0).astype(BF16)
    base = jnp.dot(before, oh.astype(BF16), preferred_element_type=F32) + carry_ref[0:1, :]
    rank1 = jnp.sum(jnp.where(lane == l1, base, 0.0), axis=-1, keepdims=True)
    rank2 = jnp.sum(jnp.where(lane == l2, base, 0.0), axis=-1, keepdims=True)
    carry_ref[...] = carry_ref[...] + jnp.sum(oh, axis=0, keepdims=True)
    cnt_ref[...] = carry_ref[...]

    meta = jnp.zeros((tm, ROUTER_LANES), F32)
    for ln, val in ((META_E, l1 - MOE_LANE0), (META_E + 1, l2 - MOE_LANE0), (META_RANK, rank1),
                    (META_RANK + 1, rank2), (META_GATE, gate1), (META_GATE + 1, gate2)):
        meta = jnp.where(lane == ln, val, meta)
    meta_ref[...] = meta


def _router_pallas(h, gain, w_rg, w_re, *, tm=512):
    n = h.shape[0]
    wr = jnp.concatenate([w_rg, w_re], axis=1)
    wr = jnp.pad(wr, ((0, 0), (0, ROUTER_LANES - wr.shape[1])))
    return pl.pallas_call(
        _router_kernel,
        name="moe_router",
        grid=(n // tm,),
        in_specs=[pl.BlockSpec((tm, D_MODEL), lambda i: (i, 0)),
                  pl.BlockSpec((1, D_MODEL), lambda i: (0, 0)),
                  pl.BlockSpec((D_MODEL, ROUTER_LANES), lambda i: (0, 0))],
        out_specs=[pl.BlockSpec((tm, HALF_D), lambda i: (i, 0)),
                   pl.BlockSpec((tm, ROUTER_LANES), lambda i: (i, 0)),
                   pl.BlockSpec((8, ROUTER_LANES), lambda i: (0, 0))],
        out_shape=[jax.ShapeDtypeStruct((n, HALF_D), jnp.uint32),
                   jax.ShapeDtypeStruct((n, ROUTER_LANES), F32),
                   jax.ShapeDtypeStruct((8, ROUTER_LANES), F32)],
        scratch_shapes=[pltpu.VMEM((8, ROUTER_LANES), F32)],
        compiler_params=pltpu.CompilerParams(
            dimension_semantics=("arbitrary",),
            vmem_limit_bytes=VMEM_LIMIT_BYTES),
    )(h, gain.reshape(1, D_MODEL).astype(F32), wr)


def _dispatch_kernel(dest_ref, hp_ref, xs_in_ref, xs_ref, sem):
    del xs_in_ref
    tc = hp_ref.shape[0]
    base = pl.program_id(0) * tc * MOE_TOPK

    def row_copy(r, k):
        return pltpu.make_async_copy(hp_ref.at[pl.ds(r, 1)],
                                     xs_ref.at[pl.ds(dest_ref[base + MOE_TOPK * r + k], 1)], sem)

    def start(r, c):
        for k in range(MOE_TOPK):
            row_copy(r, k).start()
        return c

    def wait(r, c):
        for k in range(MOE_TOPK):
            row_copy(r, k).wait()
        return c

    lax.fori_loop(0, tc, start, 0, unroll=ROW_DMA_UNROLL)
    lax.fori_loop(0, tc, wait, 0, unroll=ROW_DMA_UNROLL)


def _dispatch_pallas(hp, dest, n_slots, *, tc=256):
    n = hp.shape[0]
    xs0 = jnp.zeros((n_slots, HALF_D), jnp.uint32)
    return pl.pallas_call(
        _dispatch_kernel,
        name="moe_dispatch",
        grid_spec=pltpu.PrefetchScalarGridSpec(
            num_scalar_prefetch=1,
            grid=(n // tc,),
            in_specs=[pl.BlockSpec((tc, HALF_D), lambda i, d: (i, 0)),
                      pl.BlockSpec(memory_space=pl.ANY)],
            out_specs=pl.BlockSpec(memory_space=pl.ANY),
            scratch_shapes=[pltpu.SemaphoreType.DMA(())]),
        out_shape=jax.ShapeDtypeStruct((n_slots, HALF_D), jnp.uint32),
        input_output_aliases={2: 0},
        compiler_params=pltpu.CompilerParams(
            dimension_semantics=("arbitrary",),
            vmem_limit_bytes=VMEM_LIMIT_BYTES),
    )(dest, hp, xs0)


MOE_WSLOTS = 3


def _expert_kernel(be_ref, first_ref, slot_ref, nxt1_ref, nxt2_ref, nu_ref, x_ref, w1_hbm, w3_hbm, w2_hbm,
                   y_ref, f1_ref, f3_ref, f2_ref, b1_ref, b3_ref, b2_ref, sem, *, layer):
    i = pl.program_id(0)

    def weight_copies(e, slot):
        return [pltpu.make_async_copy(src.at[layer, e], dst.at[slot], sem.at[slot, n])
                for n, (src, dst) in enumerate(((w1_hbm, f1_ref), (w3_hbm, f3_ref), (w2_hbm, f2_ref)))]

    def start_into(e, slot):
        @pl.when(e >= 0)
        def _():
            for c in weight_copies(e, slot):
                c.start()

    @pl.when(i < nu_ref[0])
    def _():
        slot = slot_ref[i]

        @pl.when(first_ref[i] == 1)
        def _():
            @pl.when(i == 0)
            def _():
                start_into(be_ref[i], slot)
                start_into(nxt1_ref[i], lax.rem(slot + 1, MOE_WSLOTS))

            for c in weight_copies(be_ref[i], slot):
                c.wait()
            start_into(nxt2_ref[i], lax.rem(slot + 2, MOE_WSLOTS))

            b1_ref[...] = f1_ref[slot].astype(BF16)
            b3_ref[...] = f3_ref[slot].astype(BF16)
            b2_ref[...] = f2_ref[slot].astype(BF16)

        x_lo, x_hi = (half.astype(BF16) for half in _unpack_halves(x_ref[...]))

        def up(w_ref):
            return (jnp.dot(x_lo, w_ref[:HALF_D, :], preferred_element_type=F32)
                    + jnp.dot(x_hi, w_ref[HALF_D:, :], preferred_element_type=F32))

        a = _silu(up(b1_ref)) * up(b3_ref)
        y_ref[...] = _pack_halves(jnp.dot(a.astype(BF16), b2_ref[...], preferred_element_type=F32))

    @pl.when(i >= nu_ref[0])
    def _():
        y_ref[...] = jnp.zeros_like(y_ref)


def _expert_pallas(xs, nblk, w1, w3, w2, layer):
    n_slots = xs.shape[0]
    nb = n_slots // MOE_TB
    bend = jnp.cumsum(nblk)
    blocks = jnp.arange(nb, dtype=jnp.int32)
    block_e = jnp.minimum(jnp.searchsorted(bend, blocks, side='right'), MOE_EXPERTS - 1).astype(jnp.int32)
    first = (blocks == (bend - nblk)[block_e]).astype(jnp.int32)
    nonempty = nblk > 0
    slot_e = (jnp.cumsum(nonempty) - 1) % MOE_WSLOTS
    ids = jnp.where(nonempty, jnp.arange(MOE_EXPERTS), MOE_EXPERTS)
    pad = jnp.full((2,), MOE_EXPERTS, ids.dtype)
    after = jnp.concatenate([lax.cummin(ids, reverse=True), pad])
    nxt1 = after[1:MOE_EXPERTS + 1]
    nxt2 = jnp.concatenate([after, pad[:1]])[jnp.minimum(nxt1, MOE_EXPERTS) + 1]
    as_id = lambda e: jnp.where(e < MOE_EXPERTS, e, -1).astype(jnp.int32)
    n_used = bend[-1:].astype(jnp.int32)
    blk = lambda i, *s: jnp.minimum(i, s[-1][0] - 1)
    hbm = pl.BlockSpec(memory_space=pl.ANY)
    return pl.pallas_call(
        functools.partial(_expert_kernel, layer=layer),
        name="moe_experts",
        grid_spec=pltpu.PrefetchScalarGridSpec(
            num_scalar_prefetch=6,
            grid=(nb,),
            in_specs=[pl.BlockSpec((MOE_TB, HALF_D), lambda i, *s: (blk(i, *s), 0)), hbm, hbm, hbm],
            out_specs=pl.BlockSpec((MOE_TB, HALF_D), lambda i, *s: (i, 0)),
            scratch_shapes=[pltpu.VMEM((MOE_WSLOTS, D_MODEL, MOE_FF), F32),
                            pltpu.VMEM((MOE_WSLOTS, D_MODEL, MOE_FF), F32),
                            pltpu.VMEM((MOE_WSLOTS, MOE_FF, D_MODEL), F32),
                            pltpu.VMEM((D_MODEL, MOE_FF), BF16), pltpu.VMEM((D_MODEL, MOE_FF), BF16),
                            pltpu.VMEM((MOE_FF, D_MODEL), BF16),
                            pltpu.SemaphoreType.DMA((MOE_WSLOTS, 3))]),
        out_shape=jax.ShapeDtypeStruct((n_slots, HALF_D), jnp.uint32),
        compiler_params=pltpu.CompilerParams(
            dimension_semantics=("arbitrary",),
            vmem_limit_bytes=VMEM_LIMIT_BYTES),
    )(block_e, first, slot_e[block_e].astype(jnp.int32), as_id(nxt1)[block_e], as_id(nxt2)[block_e], n_used,
      xs, w1, w3, w2)


def _combine_kernel(dest_ref, h_ref, meta_ref, g_ref, yb_ref, o_ref, buf_ref, sem, *, final_norm):
    tc = h_ref.shape[0]
    base = pl.program_id(0) * tc * MOE_TOPK

    def row_copy(r, k):
        return pltpu.make_async_copy(yb_ref.at[pl.ds(dest_ref[base + MOE_TOPK * r + k], 1)],
                                     buf_ref.at[k, pl.ds(r, 1)], sem)

    def start(r, c):
        for k in range(MOE_TOPK):
            row_copy(r, k).start()
        return c

    def wait(r, c):
        for k in range(MOE_TOPK):
            row_copy(r, k).wait()
        return c

    lax.fori_loop(0, tc, start, 0, unroll=ROW_DMA_UNROLL)
    lax.fori_loop(0, tc, wait, 0, unroll=ROW_DMA_UNROLL)
    meta = meta_ref[...]
    y_lo = jnp.zeros((tc, HALF_D), F32)
    y_hi = jnp.zeros((tc, HALF_D), F32)
    for k in range(MOE_TOPK):
        lo, hi = _unpack_halves(buf_ref[k])
        gate = meta[:, META_GATE + k:META_GATE + k + 1]
        y_lo = y_lo + lo * gate
        y_hi = y_hi + hi * gate
    out = h_ref[...] + jnp.concatenate([y_lo, y_hi], axis=1)
    if final_norm:
        out = out * lax.rsqrt(jnp.mean(out * out, axis=-1, keepdims=True) + EPS) * g_ref[...]
    o_ref[...] = out


def _combine_pallas(h, meta, dest, yb, final_gain, *, final_norm, tc=256):
    n = h.shape[0]
    return pl.pallas_call(
        functools.partial(_combine_kernel, final_norm=final_norm),
        name="moe_combine",
        grid_spec=pltpu.PrefetchScalarGridSpec(
            num_scalar_prefetch=1,
            grid=(n // tc,),
            in_specs=[pl.BlockSpec((tc, D_MODEL), lambda i, d: (i, 0)),
                      pl.BlockSpec((tc, ROUTER_LANES), lambda i, d: (i, 0)),
                      pl.BlockSpec((1, D_MODEL), lambda i, d: (0, 0)),
                      pl.BlockSpec(memory_space=pl.ANY)],
            out_specs=pl.BlockSpec((tc, D_MODEL), lambda i, d: (i, 0)),
            scratch_shapes=[pltpu.VMEM((MOE_TOPK, tc, HALF_D), jnp.uint32), pltpu.SemaphoreType.DMA(())]),
        out_shape=jax.ShapeDtypeStruct((n, D_MODEL), F32),
        compiler_params=pltpu.CompilerParams(
            dimension_semantics=("arbitrary",),
            vmem_limit_bytes=VMEM_LIMIT_BYTES),
    )(dest, h, meta, final_gain.reshape(1, D_MODEL).astype(F32), yb)


def _moe_pallas(h, gain, w_rg, w_re, w1, w3, w2, layer, final_gain, *, final_norm):
    n = h.shape[0]
    nb = (n * MOE_TOPK) // MOE_TB + MOE_EXPERTS
    hp, meta, cnt = _router_pallas(h, gain, w_rg, w_re)
    expert = meta[:, META_E:META_E + MOE_TOPK].astype(jnp.int32)
    rank = meta[:, META_RANK:META_RANK + MOE_TOPK].astype(jnp.int32)
    counts = cnt[0, MOE_LANE0:MOE_LANE0 + MOE_EXPERTS].astype(jnp.int32)
    nblk = (counts + MOE_TB - 1) // MOE_TB
    seg_start = (jnp.cumsum(nblk) - nblk) * MOE_TB
    hit = expert[..., None] == jnp.arange(MOE_EXPERTS, dtype=jnp.int32)
    dest = (jnp.sum(jnp.where(hit, seg_start, 0), axis=-1) + rank).reshape(n * MOE_TOPK)
    xs = _dispatch_pallas(hp, dest, nb * MOE_TB)
    yb = _expert_pallas(xs, nblk, w1, w3, w2, layer)
    return _combine_pallas(h, meta, dest, yb, final_gain, final_norm=final_norm)


N_GATES = 4 * ML_HEADS
IN_GROUPS_BEFORE_GATES = 4
IN_GROUPS = 13
LANE = 128


def _win_relayout_kernel(*refs):
    parts, main_ref = refs[:-1], refs[-1]
    n = pl.program_id(1)
    wide = jnp.concatenate([p[...] for p in parts], axis=1)

    @pl.when(n < IN_GROUPS_BEFORE_GATES)
    def _():
        main_ref[...] = wide[:, :GROUP_W].astype(main_ref.dtype)

    @pl.when(n >= IN_GROUPS_BEFORE_GATES)
    def _():
        main_ref[...] = wide[:, N_GATES:N_GATES + GROUP_W].astype(main_ref.dtype)


def _win_relayout(w_in):
    depth, k, _ = w_in.shape
    per = GROUP_W // LANE
    g0 = IN_GROUPS_BEFORE_GATES * GROUP_W
    gate = w_in[:, :, g0:g0 + N_GATES].reshape(depth, k, 4, ML_HEADS).transpose(0, 1, 3, 2)
    gate = jnp.pad(gate, ((0, 0), (0, 0), (0, 0), (0, LANE - 4))).reshape(depth, k, ML_HEADS * LANE)
    width = w_in.shape[2]
    w_b = jnp.pad(w_in.astype(BF16), ((0, 0), (0, 0), (0, -width % LANE)))
    part = lambda t: pl.BlockSpec((None, k, LANE), lambda l, n, t=t: (l, 0, per * n + t))
    main = pl.pallas_call(
        _win_relayout_kernel,
        name="w_in_relayout",
        grid=(depth, IN_GROUPS),
        in_specs=[part(t) for t in range(per + 1)],
        out_specs=pl.BlockSpec((None, k, GROUP_W), lambda l, n: (l, 0, n)),
        out_shape=jax.ShapeDtypeStruct((depth, k, IN_GROUPS * GROUP_W), BF16),
        compiler_params=pltpu.CompilerParams(
            dimension_semantics=("parallel", "parallel"),
            vmem_limit_bytes=VMEM_LIMIT_BYTES),
    )(*([w_b] * (per + 1)))
    return main, gate


COL_MLSTM, COL_HGRN, COL_FNET, COL_NA = 0, 4 * GROUP_W, 9 * GROUP_W, 10 * GROUP_W


def kernel(x, mem, norm_mix, norm_cross, norm_ffn, norm_final, norm_mem, w_in, mlstm_conv,
           mlstm_gate_bias, hgrn_lower_bound, na_rpb, group_gain, w_out, xa_wq, xa_wk, xa_wv, xa_wo,
           moe_router_group, moe_router_expert, moe_w1, moe_w3, moe_w2):
    b, s, d = x.shape
    mem_len = mem.shape[1]
    h = x.reshape(b * s, d)
    mem_f = mem.reshape(b * mem_len, d)
    lbs = jnp.cumsum(jax.nn.softmax(hgrn_lower_bound.astype(F32), axis=0), axis=0)
    lbs = lbs - lbs[0]
    fnet_tables = _fnet_tables(s)
    w_main, w_gate = _win_relayout(w_in)
    w_out_b, wq_b, wo_b = (w.astype(BF16) for w in (w_out, xa_wq, xa_wo))
    for l in range(DEPTH):
        proj, gates = _matmul(h, w_main, l, name="in_proj", gain=norm_mix[l], w_side=w_gate, out_dtype=BF16)
        y_ml = _mlstm_pallas(proj, gates, mlstm_conv[l], mlstm_gate_bias[l], batch=b, seq=s,
                             out_dtype=BF16)
        y_hg = _hgrn_pallas(proj, lbs[l], batch=b, seq=s, col0=COL_HGRN, out_dtype=BF16)
        y_fn = _fnet_pallas(proj, fnet_tables, batch=b, seq=s, col0=COL_FNET, out_dtype=BF16)
        y_na = _na_pallas(proj, na_rpb[l], batch=b, seq=s, col0=COL_NA, out_dtype=BF16)
        h = _matmul([y_ml, y_hg, y_fn, y_na], w_out_b, l, name="out_proj", gain=group_gain[l], residual=h)

        k = _matmul(mem_f, xa_wk, l, name="xa_k_proj", gain=norm_mem, out_dtype=BF16)
        v = _matmul(mem_f, xa_wv, l, name="xa_v_proj", gain=norm_mem, out_dtype=BF16)
        o = _xattn_pallas(h, norm_cross[l], wq_b, l, k, v, batch=b, seq=s, mem_len=mem_len)
        h = _matmul(o, wo_b, l, name="xa_o_proj", residual=h)

        h = _moe_pallas(h, norm_ffn[l], moe_router_group[l], moe_router_expert[l],
                        moe_w1, moe_w3, moe_w2, l, norm_final, final_norm=(l == DEPTH - 1))
    return h.reshape(b, s, d)
```

```python
import functools

import jax
import jax.numpy as jnp
import numpy as np
from jax import lax
from jax.experimental import pallas as pl
from jax.experimental.pallas import tpu as pltpu

D_MODEL = 2048
DEPTH = 4
GRID_W = 64
N_MIXERS = 4
GROUP_W = D_MODEL // N_MIXERS
ML_HEADS = 4
ML_DH = GROUP_W // ML_HEADS
ML_CHUNK = 64
HG_HEADS = 4
HG_CHUNK = 16
FN_GROUPS = 4
FN_CH = GROUP_W // FN_GROUPS
NA_HEADS = 8
NA_DH = GROUP_W // NA_HEADS
NA_KH = 8
NA_KW = 16
XA_HEADS = 4
XA_DH = D_MODEL // XA_HEADS
MOE_GROUPS = 4
MOE_PER_GROUP = 8
MOE_EXPERTS = MOE_GROUPS * MOE_PER_GROUP
MOE_TOPK = 2
MOE_BLOCK = 128
EPS = 1e-6
F32 = jnp.float32
BF16 = jnp.bfloat16

VMEM_LIMIT_BYTES = 56 * 1024 * 1024


def _mm_kernel(*refs, n_x, norm, residual, side):
    refs = list(refs)
    x_refs = [refs.pop(0) for _ in range(n_x)]
    g_ref = refs.pop(0) if norm else None
    w_ref = refs.pop(0)
    ws_ref = refs.pop(0) if side else None
    r_ref = refs.pop(0) if residual else None
    o_ref = refs.pop(0)
    os_ref = refs.pop(0) if side else None
    xn_ref = refs.pop(0)

    @pl.when(pl.program_id(1) == 0)
    def _():
        col = 0
        for x_ref in x_refs:
            x = x_ref[...].astype(F32)
            kx = x.shape[1]
            if norm:
                ms = jnp.mean(x * x, axis=-1, keepdims=True)
                x = x * lax.rsqrt(ms + EPS) * g_ref[:, col:col + kx]
            xn_ref[:, col:col + kx] = x.astype(BF16)
            col += kx
        if side:
            os_ref[...] = jnp.dot(xn_ref[...], ws_ref[...].astype(BF16),
                                  preferred_element_type=F32).astype(os_ref.dtype)

    acc = jnp.dot(xn_ref[...], w_ref[...].astype(BF16), preferred_element_type=F32)
    if residual:
        acc = acc + r_ref[...]
    o_ref[...] = acc.astype(o_ref.dtype)


def _matmul(xs, w, layer, *, name, gain=None, residual=None, w_side=None, tm=1024, tn=512, out_dtype=F32):
    if not isinstance(xs, (list, tuple)):
        xs = [xs]
    m = xs[0].shape[0]
    k = sum(x.shape[1] for x in xs)
    n = w.shape[2]
    tm = min(tm, m)
    tn = min(tn, n)
    assert m % tm == 0 and n % tn == 0 and w.shape[1] == k, (m, n, k, tm, tn)
    norm = gain is not None
    has_res = residual is not None
    side = w_side is not None
    in_specs = [pl.BlockSpec((tm, x.shape[1]), lambda i, j: (i, 0)) for x in xs]
    args = list(xs)
    if norm:
        in_specs.append(pl.BlockSpec((1, k), lambda i, j: (0, 0)))
        args.append(gain.reshape(1, k).astype(F32))
    in_specs.append(pl.BlockSpec((None, k, tn), lambda i, j: (layer, 0, j)))
    args.append(w)
    if side:
        in_specs.append(pl.BlockSpec((None, k, w_side.shape[2]), lambda i, j: (layer, 0, 0)))
        args.append(w_side)
    if has_res:
        in_specs.append(pl.BlockSpec((tm, tn), lambda i, j: (i, j)))
        args.append(residual)
    out_specs = pl.BlockSpec((tm, tn), lambda i, j: (i, j))
    out_shape = jax.ShapeDtypeStruct((m, n), out_dtype)
    if side:
        out_specs = [out_specs, pl.BlockSpec((tm, w_side.shape[2]), lambda i, j: (i, 0))]
        out_shape = [out_shape, jax.ShapeDtypeStruct((m, w_side.shape[2]), F32)]
    return pl.pallas_call(
        functools.partial(_mm_kernel, n_x=len(xs), norm=norm, residual=has_res, side=side),
        name=name,
        grid=(m // tm, n // tn),
        in_specs=in_specs,
        out_specs=out_specs,
        out_shape=out_shape,
        scratch_shapes=[pltpu.VMEM((tm, k), BF16)],
        compiler_params=pltpu.CompilerParams(
            dimension_semantics=("parallel", "arbitrary"),
            vmem_limit_bytes=VMEM_LIMIT_BYTES),
    )(*args)


NA_QROWS = 4
NA_WROWS = 12


def _na_groups(rows):
    tables, plan = [], []
    for r0 in range(0, rows, NA_QROWS):
        band0 = lambda r: min(max(r - NA_KH // 2, 0), rows - NA_KH)
        kr0 = min(band0(r0), rows - NA_WROWS)
        assert band0(r0 + NA_QROWS - 1) + NA_KH <= kr0 + NA_WROWS
        dr = np.full((NA_QROWS, NA_WROWS), -1, np.int64)
        for i in range(NA_QROWS):
            for j in range(NA_WROWS):
                if 0 <= kr0 + j - band0(r0 + i) < NA_KH:
                    dr[i, j] = kr0 + j - (r0 + i) + NA_KH - 1
        key = dr.tobytes()
        if key not in [t.tobytes() for t in tables]:
            tables.append(dr)
        plan.append((r0, kr0, [t.tobytes() for t in tables].index(key)))
    return plan, np.stack(tables)


def _na_bias_tables(rpb, dr_tables):
    c = np.arange(GRID_W)
    dc = np.clip(c[None, :] - c[:, None] + NA_KW - 1, 0, 2 * NA_KW - 2)
    onehot = (dc[None] == np.arange(2 * NA_KW - 1)[:, None, None]).astype(np.float32)
    col_start = np.clip(c - NA_KW // 2, 0, GRID_W - NA_KW)
    col_ok = (c[None, :] >= col_start[:, None]) & (c[None, :] < col_start[:, None] + NA_KW)
    t = jnp.einsum('hrd,dqk->hrqk', rpb.astype(F32), onehot, precision=lax.Precision.HIGHEST)
    t = jnp.where(col_ok[None, None], t, -jnp.inf)
    t = jnp.concatenate([t, jnp.full_like(t[:, :1], -jnp.inf)], axis=1)
    idx = np.where(dr_tables < 0, t.shape[1] - 1, dr_tables)
    pairs = idx.reshape(idx.shape[0], idx.shape[1], -1, 2)
    uniq = sorted({(int(a), int(b)) for a, b in pairs.reshape(-1, 2)})
    pair_idx = np.array([[[uniq.index((int(a), int(b))) for a, b in row] for row in typ] for typ in pairs])
    table = jnp.stack([jnp.concatenate([t[:, a], t[:, b]], axis=-1) for a, b in uniq], axis=1)
    return table, pair_idx


def _na_kernel(q_ref, k_ref, v_ref, tab_ref, o_ref, bias_ref, *, plan, pair_idx):
    nq = NA_QROWS * GRID_W
    nk = NA_WROWS * GRID_W
    for hh in range(2):
        for typ in range(pair_idx.shape[0]):
            for i in range(pair_idx.shape[1]):
                for jj in range(pair_idx.shape[2]):
                    bias_ref[hh, typ, i * GRID_W:(i + 1) * GRID_W, jj * 2 * GRID_W:(jj + 1) * 2 * GRID_W] = (
                        tab_ref[hh, int(pair_idx[typ, i, jj])])
    for r0, kr0, typ in plan:
        q = q_ref[r0 * GRID_W:r0 * GRID_W + nq, :].astype(F32) * (NA_DH ** -0.5)
        kb = k_ref[kr0 * GRID_W:kr0 * GRID_W + nk, :]
        vb = v_ref[kr0 * GRID_W:kr0 * GRID_W + nk, :]
        outs = []
        for hh in range(2):
            sl = slice(hh * NA_DH, (hh + 1) * NA_DH)
            s = lax.dot_general(q[:, sl].astype(BF16), kb[:, sl].astype(BF16),
                                (((1,), (1,)), ((), ())), preferred_element_type=F32)
            s = s + bias_ref[hh, typ]
            m = jnp.max(s, axis=-1, keepdims=True)
            p = jnp.exp(s - m)
            l = jnp.sum(p, axis=-1, keepdims=True)
            o = jnp.dot(p.astype(BF16), vb[:, sl].astype(BF16), preferred_element_type=F32)
            outs.append(o / l)
        o_ref[r0 * GRID_W:r0 * GRID_W + nq, :] = jnp.concatenate(outs, axis=-1).astype(o_ref.dtype)


def _na_pallas(proj, rpb, *, batch, seq, col0, out_dtype=F32):
    rows = seq // GRID_W
    pair_w = 2 * NA_DH
    cb = col0 // pair_w
    gb = GROUP_W // pair_w
    plan, dr_tables = _na_groups(rows)
    table, pair_idx = _na_bias_tables(rpb, dr_tables)
    spec = lambda g: pl.BlockSpec((seq, pair_w), lambda b, p, g=g: (b, cb + g * gb + p))
    return pl.pallas_call(
        functools.partial(_na_kernel, plan=plan, pair_idx=pair_idx),
        name="na_attention",
        grid=(batch, NA_HEADS // 2),
        in_specs=[spec(0), spec(1), spec(2),
                  pl.BlockSpec((2,) + table.shape[1:], lambda b, p: (p, 0, 0, 0))],
        out_specs=pl.BlockSpec((seq, pair_w), lambda b, p: (b, p)),
        out_shape=jax.ShapeDtypeStruct((batch * seq, GROUP_W), out_dtype),
        scratch_shapes=[pltpu.VMEM((2, dr_tables.shape[0], NA_QROWS * GRID_W, NA_WROWS * GRID_W), F32)],
        compiler_params=pltpu.CompilerParams(
            dimension_semantics=("parallel", "parallel"),
            vmem_limit_bytes=VMEM_LIMIT_BYTES),
    )(proj, proj, proj, table)


ML_L = 128
ML_GATE_LANES = 128


def _log_sigmoid(x):
    return jnp.minimum(x, 0.0) - jnp.log(1.0 + jnp.exp(-jnp.abs(x)))


def _silu(x):
    return x * jax.nn.sigmoid(x)


_NT = (((1,), (1,)), ((), ()))
_TN = (((0,), (0,)), ((), ()))


def _split3(x):
    hi = x.astype(BF16)
    r1 = x - hi.astype(F32)
    mid = r1.astype(BF16)
    lo = (r1 - mid.astype(F32)).astype(BF16)
    return hi, mid, lo


def _dot_exact_lhs(a, x, dims=None):
    a = a.astype(BF16)
    if dims is None:
        return sum(jnp.dot(a, p, preferred_element_type=F32) for p in _split3(x))
    return sum(lax.dot_general(a, p, dims, preferred_element_type=F32) for p in _split3(x))


def _dot_exact_rhs(x, a, dims):
    a = a.astype(BF16)
    return sum(lax.dot_general(p, a, dims, preferred_element_type=F32) for p in _split3(x))


ML_GROUP = 4
ML_PAD = 16


def _mlstm_local(q, k, v, gcol, tri, mask_w, gi, end_row):
    L = ML_L
    cum_col = _dot_exact_lhs(tri, _log_sigmoid(gcol))
    wide = lambda col: jnp.broadcast_to(col, (L, L))
    b_cols, ig_cols, rows = [], [], []
    for g in range(len(q)):
        lanes = slice(g * ML_GATE_LANES, (g + 1) * ML_GATE_LANES)
        b_cols.append(cum_col[:, g * ML_GATE_LANES + gi + 1:g * ML_GATE_LANES + gi + 2])
        ig_cols.append(gcol[:, g * ML_GATE_LANES + gi:g * ML_GATE_LANES + gi + 1])
        rows.append(gcol[:, lanes].T[gi:gi + 1, :] - cum_col[:, lanes].T[gi + 1:gi + 2, :])
    b_w = jnp.concatenate([wide(b) for b in b_cols], axis=1)
    row_w = jnp.concatenate(rows, axis=1)
    dmat = jnp.where(mask_w != 0.0, b_w + row_w, -jnp.inf)
    m_in_w = jnp.concatenate([wide(jnp.max(dmat[:, g * L:(g + 1) * L], axis=-1, keepdims=True))
                              for g in range(len(q))], axis=1)
    w = jnp.exp(dmat - m_in_w)
    ones = jnp.ones((L, ML_DH), BF16)
    out = []
    for g in range(len(q)):
        lanes = slice(g * L, (g + 1) * L)
        kb = k[g].astype(BF16)
        v_aug = jnp.concatenate([v[g].astype(BF16), ones], axis=1)
        qk = lax.dot_general(q[g].astype(BF16), kb, _NT, preferred_element_type=F32) * w[:, lanes]
        res = jnp.dot(qk.astype(BF16), v_aug, preferred_element_type=F32)
        b_end = b_cols[g][end_row:end_row + 1, :]
        a_col = b_end - b_cols[g] + ig_cols[g]
        m_loc = jnp.max(a_col, axis=0, keepdims=True)
        kw = k[g] * jnp.exp(a_col - m_loc)
        st = lax.dot_general(v_aug, kw.astype(BF16), _TN, preferred_element_type=F32)
        out.append(dict(m_in=m_in_w[:, lanes], num=res[:, :ML_DH], den=res[:, ML_DH:], b=b_w[:, lanes],
                        ct=st[:ML_DH], n=st[ML_DH:ML_DH + 1], m_loc=m_loc, b_end=b_end))
    return out


def _mlstm_carry(q, m_in, num_in, den_in, b, ct, n, m_prev):
    inter = b + m_prev
    m_t = jnp.maximum(inter, m_in)
    s_in = jnp.exp(m_in - m_t)
    s_inter = jnp.exp(inter - m_t)
    rhs = jnp.concatenate([ct.astype(BF16), jnp.broadcast_to(n, (ML_PAD, ML_DH)).astype(BF16)], axis=0)
    both = lax.dot_general(q.astype(BF16), rhs, _NT, preferred_element_type=F32)
    num = s_in * num_in + s_inter * both[:, :ML_DH]
    den = s_in * den_in + s_inter * both[:, ML_DH:ML_DH + 1]
    return num / jnp.maximum(jnp.abs(den), jnp.exp(-m_t))


def _mlstm_kernel(q_ref, k_ref, v_ref, o_ref, cwq_ref, cwk_ref, gcol_ref, bcol_ref,
                  y_ref, qs_ref, ks_ref, num_ref, den_ref, min_ref, b_ref, ctl_ref, stat_ref, *, seq):
    L = ML_L
    nc = seq // L
    t_idx = lax.broadcasted_iota(jnp.int32, (seq, 1), 0)

    def conv_silu(x, w):
        prev = jnp.where(t_idx == 0, 0.0, pltpu.roll(x, 1, axis=0))
        nxt = jnp.where(t_idx == seq - 1, 0.0, pltpu.roll(x, seq - 1, axis=0))
        return _silu(prev * w[0:1, :] + x * w[1:2, :] + nxt * w[2:3, :])

    qs_ref[...] = conv_silu(q_ref[...].astype(F32), cwq_ref[...])
    ks_ref[...] = conv_silu(k_ref[...].astype(F32), cwk_ref[...]) * (ML_DH ** -0.5)

    ti = lax.broadcasted_iota(jnp.int32, (L, L), 0)
    si = lax.broadcasted_iota(jnp.int32, (L, L), 1)
    tris = (jnp.where(si <= ti, 1.0, 0.0), jnp.where(si >= ti, 1.0, 0.0))
    masks_w = [jnp.concatenate([t] * ML_GROUP, axis=1) for t in tris]

    for direction in (0, 1):
        for c0 in range(0, nc, ML_GROUP):
            rows = [pl.ds((c0 + g) * L, L) for g in range(ML_GROUP)]
            gcol = jnp.concatenate([gcol_ref[r, :] + bcol_ref[...] for r in rows], axis=1)
            local = _mlstm_local([qs_ref[r, :] for r in rows], [ks_ref[r, :] for r in rows],
                                 [v_ref[r, :].astype(F32) for r in rows], gcol,
                                 tris[direction], masks_w[direction], 2 * direction,
                                 L - 1 if direction == 0 else 0)
            for g, (r, loc) in enumerate(zip(rows, local)):
                idx = direction * nc + c0 + g
                num_ref[direction, r, :] = loc["num"]
                den_ref[direction, r, :] = loc["den"]
                min_ref[direction, r, :] = loc["m_in"]
                b_ref[direction, r, :] = loc["b"]
                ctl_ref[idx] = loc["ct"]
                stat_ref[idx, 0:1, :] = loc["n"]
                stat_ref[idx, 1:2, :] = jnp.broadcast_to(loc["m_loc"], (1, ML_DH))
                stat_ref[idx, 2:3, :] = jnp.broadcast_to(loc["b_end"], (1, ML_DH))

    state = [(jnp.zeros((ML_DH, ML_DH), F32), jnp.zeros((1, ML_DH), F32), jnp.zeros((1, ML_DH), F32))] * 2
    for i in range(nc):
        for direction in (0, 1):
            c = i if direction == 0 else nc - 1 - i
            r = pl.ds(c * L, L)
            idx = direction * nc + c
            ct, n, m = state[direction]
            num_ref[direction, r, :] = _mlstm_carry(qs_ref[r, :], min_ref[direction, r, :],
                                                    num_ref[direction, r, :], den_ref[direction, r, :],
                                                    b_ref[direction, r, :], ct, n, m)
            n_loc, m_loc, b_end = stat_ref[idx, 0:1, :], stat_ref[idx, 1:2, :], stat_ref[idx, 2:3, :]
            m_new = jnp.maximum(b_end + m, m_loc)
            s_old = jnp.exp(b_end + m - m_new)
            s_new = jnp.exp(m_loc - m_new)
            state[direction] = (s_old * ct + s_new * ctl_ref[idx], s_old * n + s_new * n_loc, m_new)
    y_ref[...] = (jax.nn.sigmoid(o_ref[...].astype(F32)) * (num_ref[0] + num_ref[1])).astype(y_ref.dtype)


def _mlstm_pallas(proj, gates, conv_w, gate_b, *, batch, seq, out_dtype=F32):
    d = ML_DH
    hb = GROUP_W // d
    gb4 = gate_b.astype(F32).reshape(4, ML_HEADS).T
    gb_col = jnp.pad(gb4, ((0, 0), (0, ML_GATE_LANES - 4))).reshape(ML_HEADS, 1, ML_GATE_LANES)
    spec = lambda grp: pl.BlockSpec((seq, d), lambda b, h, grp=grp: (b, grp * hb + h))
    f32 = lambda *shape: pltpu.VMEM(shape, F32)
    return pl.pallas_call(
        functools.partial(_mlstm_kernel, seq=seq),
        name="mlstm",
        grid=(batch, ML_HEADS),
        in_specs=[spec(0), spec(1), spec(2), spec(3),
                  pl.BlockSpec((3, d), lambda b, h: (0, h)),
                  pl.BlockSpec((3, d), lambda b, h: (0, hb + h)),
                  pl.BlockSpec((seq, ML_GATE_LANES), lambda b, h: (b, h)),
                  pl.BlockSpec((None, 1, ML_GATE_LANES), lambda b, h: (h, 0, 0))],
        out_specs=pl.BlockSpec((seq, d), lambda b, h: (b, h)),
        out_shape=jax.ShapeDtypeStruct((batch * seq, GROUP_W), out_dtype),
        scratch_shapes=[f32(seq, d), f32(seq, d), f32(2, seq, d), f32(2, seq, d), f32(2, seq, d),
                        f32(2, seq, d), f32(2 * (seq // ML_L), d, d), f32(2 * (seq // ML_L), 8, d)],
        compiler_params=pltpu.CompilerParams(
            dimension_semantics=("parallel", "parallel"),
            vmem_limit_bytes=VMEM_LIMIT_BYTES),
    )(proj, proj, proj, proj, conv_w.astype(F32), conv_w.astype(F32), gates, gb_col)


HG_L = 128
HG_LEAF = 8


HG_GROUP = 4
HG_NLEAF = HG_L // HG_LEAF
HG_LEVELS = 4


def _hgrn_ref_rows(p_ref, chunks, backward):
    def rows(first, count, stride):
        return jnp.concatenate([p_ref[pl.ds(c * HG_L + first, count, stride=stride), :] for c in chunks], axis=1)

    leaf = rows(HG_LEAF // 2 if backward else HG_LEAF // 2 - 1, HG_NLEAF, HG_LEAF)
    levels = []
    m = HG_L // 2
    for _ in range(HG_LEVELS):
        blocks = HG_L // (2 * m)
        r = rows(m if backward else m - 1, blocks, 2 * m) if blocks > 1 else rows(m if backward else m - 1, 1, 1)
        levels.append(jnp.repeat(r, HG_NLEAF // blocks, axis=0))
        m //= 2
    return leaf, levels


def _hgrn_chunks(q, k, v, p, ref_leaf, ref_levels, masks):
    L = HG_L
    dk = HG_L
    groups = q.shape[1] // dk
    spread = lambda e: jnp.broadcast_to(e[:, None, :], (HG_NLEAF, HG_LEAF, e.shape[1])).reshape(L, e.shape[1])
    d = p - spread(ref_leaf)
    q_leaf = q * jnp.exp(d)
    k_leaf = k * jnp.exp(-d)
    pairs = [(q_leaf.astype(BF16), k_leaf.astype(BF16), masks[HG_LEVELS])]
    for li in range(HG_LEVELS):
        gap = ref_leaf - ref_levels[li]
        qh = q_leaf * spread(jnp.exp(jnp.minimum(gap, 0.0)))
        kh = k_leaf * spread(jnp.exp(jnp.minimum(-gap, 0.0)))
        pairs.append((qh.astype(BF16), kh.astype(BF16), masks[li]))
    vb = v.astype(BF16)
    outs = []
    for g in range(groups):
        lanes = slice(g * dk, (g + 1) * dk)
        attn = jnp.zeros((L, L), F32)
        for qh, kh, mask in pairs:
            s = lax.dot_general(qh[:, lanes], kh[:, lanes], _NT, preferred_element_type=F32)
            attn = attn + jnp.where(mask != 0.0, s, 0.0)
        outs.append(jnp.dot(attn.astype(BF16), vb[:, lanes], preferred_element_type=F32))
    return jnp.concatenate(outs, axis=1)


def _hgrn_carry(q, k, v, p, st, backward):
    L = HG_L
    o = lax.dot_general((q * jnp.exp(p)).astype(BF16), st.astype(BF16), _NT, preferred_element_type=F32)
    p_end = p[0:1, :] if backward else p[L - 1:L, :]
    kd = k * jnp.exp(p_end - p)
    st = st * jnp.exp(p_end) + lax.dot_general(v.astype(BF16), kd.astype(BF16), _TN,
                                               preferred_element_type=F32)
    return o, st


def _hgrn_masks(backward):
    L = HG_L
    ti = lax.broadcasted_iota(jnp.int32, (L, L), 0)
    si = lax.broadcasted_iota(jnp.int32, (L, L), 1)
    if backward:
        ti, si = si, ti
    one = lambda cond: jnp.where(cond, 1.0, 0.0)
    masks = []
    m = L // 2
    while m >= HG_LEAF:
        same = one((ti // (2 * m)) == (si // (2 * m)))
        masks.append(same * one((ti % (2 * m)) >= m) * one((si % (2 * m)) < m))
        m //= 2
    masks.append(one((ti // HG_LEAF) == (si // HG_LEAF)) * one(si <= ti))
    return masks


def _hgrn_kernel(q_ref, ff_ref, fb_ref, i_ref, g_ref, lb_ref, y_ref,
                 qs_ref, lf_ref, kf_ref, lbk_ref, kb_ref, of_ref, ob_ref, *, seq):
    L = HG_L
    nc = seq // L
    lb = lb_ref[...]
    log_lb = jnp.log(lb)
    log1m_lb = jnp.log1p(-lb)

    def forget(fp):
        ls = _log_sigmoid(fp)
        a = log_lb
        c = log1m_lb + ls
        logf = jnp.maximum(a, c) + jnp.log(1.0 + jnp.exp(-jnp.abs(a - c)))
        return logf, (1.0 - lb) * jnp.exp(ls - fp)

    qs_ref[...] = _silu(q_ref[...].astype(F32))
    lf_ref[...], kf_ref[...] = forget(ff_ref[...].astype(F32))
    lbk_ref[...], kb_ref[...] = forget(fb_ref[...].astype(F32))

    ti = lax.broadcasted_iota(jnp.int32, (L, L), 0)
    si = lax.broadcasted_iota(jnp.int32, (L, L), 1)
    tril = (si <= ti).astype(F32)
    triu = (si >= ti).astype(F32)
    masks_f = _hgrn_masks(False)
    masks_b = _hgrn_masks(True)

    def side_by_side(ref, c0):
        return jnp.concatenate([ref[pl.ds((c0 + g) * L, L), :].astype(F32) for g in range(HG_GROUP)], axis=1)

    def scatter_back(ref, c0, wide):
        for g in range(HG_GROUP):
            ref[pl.ds((c0 + g) * L, L), :] = wide[:, g * L:(g + 1) * L]

    directions = ((kf_ref, lf_ref, of_ref, tril, masks_f, False), (kb_ref, lbk_ref, ob_ref, triu, masks_b, True))
    for c0 in range(0, nc, HG_GROUP):
        for _, f_ref, _, tri, _, _ in directions:
            scatter_back(f_ref, c0, _dot_exact_lhs(tri, side_by_side(f_ref, c0)))
    for c0 in range(0, nc, HG_GROUP):
        q = side_by_side(qs_ref, c0)
        v = side_by_side(i_ref, c0)
        for k_ref, f_ref, o_ref, _, masks, backward in directions:
            ref_leaf, ref_levels = _hgrn_ref_rows(f_ref, range(c0, c0 + HG_GROUP), backward)
            scatter_back(o_ref, c0, _hgrn_chunks(q, side_by_side(k_ref, c0), v, side_by_side(f_ref, c0),
                                                 ref_leaf, ref_levels, masks))
    st_f = jnp.zeros((L, L), F32)
    st_b = jnp.zeros((L, L), F32)
    for i in range(nc):
        sl = pl.ds(i * L, L)
        o, st_f = _hgrn_carry(qs_ref[sl, :], kf_ref[sl, :], i_ref[sl, :].astype(F32), lf_ref[sl, :], st_f, False)
        of_ref[sl, :] += o
        sl = pl.ds((nc - 1 - i) * L, L)
        o, st_b = _hgrn_carry(qs_ref[sl, :], kb_ref[sl, :], i_ref[sl, :].astype(F32), lbk_ref[sl, :], st_b, True)
        ob_ref[sl, :] += o
    o = of_ref[...] + ob_ref[...]
    o = o * lax.rsqrt(jnp.mean(o * o, axis=-1, keepdims=True) + EPS)
    y_ref[...] = (o * _silu(g_ref[...].astype(F32))).astype(y_ref.dtype)


def _hgrn_pallas(proj, lb, *, batch, seq, col0, out_dtype=F32):
    d = GROUP_W // HG_HEADS
    hb = GROUP_W // d
    cb = col0 // d
    spec = lambda grp: pl.BlockSpec((seq, d), lambda b, h, grp=grp: (b, cb + grp * hb + h))
    f32 = lambda *shape: pltpu.VMEM(shape, F32)
    return pl.pallas_call(
        functools.partial(_hgrn_kernel, seq=seq),
        name="hgrn2",
        grid=(batch, HG_HEADS),
        in_specs=[spec(0), spec(1), spec(2), spec(3), spec(4),
                  pl.BlockSpec((1, d), lambda b, h: (0, h))],
        out_specs=pl.BlockSpec((seq, d), lambda b, h: (b, h)),
        out_shape=jax.ShapeDtypeStruct((batch * seq, GROUP_W), out_dtype),
        scratch_shapes=[f32(seq, d)] * 7,
        compiler_params=pltpu.CompilerParams(
            dimension_semantics=("parallel", "parallel"),
            vmem_limit_bytes=VMEM_LIMIT_BYTES),
    )(proj, proj, proj, proj, proj, lb.astype(F32).reshape(1, GROUP_W))


DFT_SPLIT = 32


def _dft_cos_sin(n):
    def direct(rows, period):
        k = (rows[:, None] * jnp.arange(n, dtype=jnp.int32)[None, :]) % period
        ang = k.astype(F32) * (2.0 * np.pi / period)
        return jnp.cos(ang), jnp.sin(ang)

    if n % DFT_SPLIT or n // DFT_SPLIT < DFT_SPLIT:
        return direct(jnp.arange(n, dtype=jnp.int32), n)
    ca, sa = direct(jnp.arange(n // DFT_SPLIT, dtype=jnp.int32), n // DFT_SPLIT)
    cb, sb = direct(jnp.arange(DFT_SPLIT, dtype=jnp.int32), n)
    cos = ca[:, None, :] * cb[None, :, :] - sa[:, None, :] * sb[None, :, :]
    sin = sa[:, None, :] * cb[None, :, :] + ca[:, None, :] * sb[None, :, :]
    return cos.reshape(n, n), sin.reshape(n, n)


def _fnet_tables(seq):
    cc, sc = _dft_cos_sin(FN_CH)
    eye = jnp.eye(FN_GROUPS, dtype=F32)
    chan = jnp.concatenate([jnp.kron(eye, cc), jnp.kron(eye, sc)], axis=1)
    cs, ss = _dft_cos_sin(seq)
    pos = jnp.concatenate([cs, -ss], axis=1) * ((seq * FN_CH) ** -0.5)
    return chan.astype(BF16), pos.astype(BF16)


def _fnet_chan_kernel(u_ref, dft_ref, v_ref):
    r = jnp.dot(u_ref[...].astype(BF16), dft_ref[...], preferred_element_type=F32)
    v_ref[0] = r[:, :GROUP_W].astype(v_ref.dtype)
    v_ref[1] = r[:, GROUP_W:].astype(v_ref.dtype)


def _fnet_pos_kernel(w_ref, v_ref, o_ref):
    o_ref[...] = jnp.dot(w_ref[...], v_ref[...], preferred_element_type=F32).astype(o_ref.dtype)


def _fnet_pallas(proj, tables, *, batch, seq, col0, out_dtype=F32, ts=512):
    chan, pos = tables
    cb = col0 // GROUP_W
    nt = seq // ts
    v = pl.pallas_call(
        _fnet_chan_kernel,
        name="fnet_channel_dft",
        grid=(batch, nt),
        in_specs=[pl.BlockSpec((ts, GROUP_W), lambda b, i: (b * nt + i, cb)),
                  pl.BlockSpec((GROUP_W, 2 * GROUP_W), lambda b, i: (0, 0))],
        out_specs=pl.BlockSpec((2, ts, GROUP_W), lambda b, i: (0, i, b)),
        out_shape=jax.ShapeDtypeStruct((2, seq, batch * GROUP_W), BF16),
        compiler_params=pltpu.CompilerParams(
            dimension_semantics=("parallel", "parallel"),
            vmem_limit_bytes=VMEM_LIMIT_BYTES),
    )(proj, chan)
    v = v.reshape(2 * seq, batch * GROUP_W)
    return pl.pallas_call(
        _fnet_pos_kernel,
        name="fnet_position_dft",
        grid=(nt, batch),
        in_specs=[pl.BlockSpec((ts, 2 * seq), lambda i, b: (i, 0)),
                  pl.BlockSpec((2 * seq, GROUP_W), lambda i, b: (0, b))],
        out_specs=pl.BlockSpec((ts, GROUP_W), lambda i, b: (b * nt + i, 0)),
        out_shape=jax.ShapeDtypeStruct((batch * seq, GROUP_W), out_dtype),
        compiler_params=pltpu.CompilerParams(
            dimension_semantics=("parallel", "parallel"),
            vmem_limit_bytes=VMEM_LIMIT_BYTES),
    )(pos, v)


def _xattn_kernel(x_ref, g_ref, wq_ref, k_ref, v_ref, o_ref, xn_ref):
    @pl.when(pl.program_id(1) == 0)
    def _():
        x = x_ref[...].astype(F32)
        ms = jnp.mean(x * x, axis=-1, keepdims=True)
        xn_ref[...] = (x * lax.rsqrt(ms + EPS) * g_ref[...]).astype(BF16)

    q = jnp.dot(xn_ref[...], wq_ref[...].astype(BF16), preferred_element_type=F32)
    s = lax.dot_general(q.astype(BF16), k_ref[...].astype(BF16), _NT,
                        preferred_element_type=F32) * (XA_DH ** -0.5)
    p = jnp.exp(s - jnp.max(s, axis=-1, keepdims=True))
    l = jnp.sum(p, axis=-1, keepdims=True)
    o = jnp.dot(p.astype(BF16), v_ref[...].astype(BF16), preferred_element_type=F32)
    o_ref[...] = (o / l).astype(o_ref.dtype)


def _xattn_pallas(h, gain, wq, layer, k, v, *, batch, seq, mem_len, tm=1024, out_dtype=BF16):
    tiles = seq // tm
    return pl.pallas_call(
        _xattn_kernel,
        name="cross_attention",
        grid=(batch * tiles, XA_HEADS),
        in_specs=[pl.BlockSpec((tm, D_MODEL), lambda i, j: (i, 0)),
                  pl.BlockSpec((1, D_MODEL), lambda i, j: (0, 0)),
                  pl.BlockSpec((None, D_MODEL, XA_DH), lambda i, j: (layer, 0, j)),
                  pl.BlockSpec((mem_len, XA_DH), lambda i, j: (i // tiles, j)),
                  pl.BlockSpec((mem_len, XA_DH), lambda i, j: (i // tiles, j))],
        out_specs=pl.BlockSpec((tm, XA_DH), lambda i, j: (i, j)),
        out_shape=jax.ShapeDtypeStruct((batch * seq, D_MODEL), out_dtype),
        scratch_shapes=[pltpu.VMEM((tm, D_MODEL), BF16)],
        compiler_params=pltpu.CompilerParams(
            dimension_semantics=("parallel", "arbitrary"),
            vmem_limit_bytes=VMEM_LIMIT_BYTES),
    )(h, gain.reshape(1, D_MODEL).astype(F32), wq, k, v)


MOE_FF = D_MODEL // 4
MOE_TB = 256
MOE_LANE0 = MOE_GROUPS
ROUTER_LANES = 128
META_E, META_RANK, META_GATE = 0, 2, 4
HALF_D = D_MODEL // 2
ROW_DMA_UNROLL = 8


def _pack_halves(x):
    bits = lax.bitcast_convert_type(x.astype(BF16).astype(F32), jnp.uint32)
    h = x.shape[1] // 2
    return (bits[:, :h] >> 16) | (bits[:, h:] & jnp.uint32(0xFFFF0000))


def _unpack_halves(p):
    return (lax.bitcast_convert_type(p << 16, F32),
            lax.bitcast_convert_type(p & jnp.uint32(0xFFFF0000), F32))


def _router_kernel(x_ref, g_ref, wr_ref, hp_ref, meta_ref, cnt_ref, carry_ref):
    tm = x_ref.shape[0]

    @pl.when(pl.program_id(0) == 0)
    def _():
        carry_ref[...] = jnp.zeros_like(carry_ref)

    x = x_ref[...].astype(F32)
    hn = x * lax.rsqrt(jnp.mean(x * x, axis=-1, keepdims=True) + EPS) * g_ref[...]
    hb = hn.astype(BF16)
    logits = jnp.dot(hb, wr_ref[...].astype(BF16), preferred_element_type=F32)
    hp_ref[...] = _pack_halves(hn)

    lane = lax.broadcasted_iota(jnp.int32, (tm, ROUTER_LANES), 1).astype(F32)
    ninf = -jnp.inf
    first = lambda hit: jnp.min(jnp.where(hit, lane, float(ROUTER_LANES)), axis=-1, keepdims=True)
    gl = jnp.where(lane < MOE_GROUPS, logits, ninf)
    gmax = jnp.max(gl, axis=-1, keepdims=True)
    gidx = first(gl == gmax)
    g_gate = 1.0 / jnp.sum(jnp.exp(gl - gmax), axis=-1, keepdims=True)
    off = lane - (MOE_LANE0 + MOE_PER_GROUP * gidx)
    el = jnp.where(jnp.abs(2.0 * off - (MOE_PER_GROUP - 1)) < MOE_PER_GROUP, logits, ninf)
    v1 = jnp.max(el, axis=-1, keepdims=True)
    l1 = first(el == v1)
    el2 = jnp.where(lane == l1, ninf, el)
    v2 = jnp.max(el2, axis=-1, keepdims=True)
    l2 = first(el2 == v2)
    t = jnp.exp(v2 - v1)
    gate1 = g_gate / (1.0 + t)
    gate2 = g_gate * t / (1.0 + t)

    oh = jnp.where(lane == l1, 1.0, 0.0) + jnp.where(lane == l2, 1.0, 0.0)
    ti = lax.broadcasted_iota(jnp.int32, (tm, tm), 0)
    si = lax.broadcasted_iota(jnp.int32, (tm, tm), 1)
    before = jnp.where(si < ti, 1.0, 0.0).astype(BF16)
    base = jnp.dot(before, oh.astype(BF16), preferred_element_type=F32) + carry_ref[0:1, :]
    rank1 = jnp.sum(jnp.where(lane == l1, base, 0.0), axis=-1, keepdims=True)
    rank2 = jnp.sum(jnp.where(lane == l2, base, 0.0), axis=-1, keepdims=True)
    carry_ref[...] = carry_ref[...] + jnp.sum(oh, axis=0, keepdims=True)
    cnt_ref[...] = carry_ref[...]

    meta = jnp.zeros((tm, ROUTER_LANES), F32)
    for ln, val in ((META_E, l1 - MOE_LANE0), (META_E + 1, l2 - MOE_LANE0), (META_RANK, rank1),
                    (META_RANK + 1, rank2), (META_GATE, gate1), (META_GATE + 1, gate2)):
        meta = jnp.where(lane == ln, val, meta)
    meta_ref[...] = meta


def _router_pallas(h, gain, w_rg, w_re, *, tm=512):
    n = h.shape[0]
    wr = jnp.concatenate([w_rg, w_re], axis=1)
    wr = jnp.pad(wr, ((0, 0), (0, ROUTER_LANES - wr.shape[1])))
    return pl.pallas_call(
        _router_kernel,
        name="moe_router",
        grid=(n // tm,),
        in_specs=[pl.BlockSpec((tm, D_MODEL), lambda i: (i, 0)),
                  pl.BlockSpec((1, D_MODEL), lambda i: (0, 0)),
                  pl.BlockSpec((D_MODEL, ROUTER_LANES), lambda i: (0, 0))],
        out_specs=[pl.BlockSpec((tm, HALF_D), lambda i: (i, 0)),
                   pl.BlockSpec((tm, ROUTER_LANES), lambda i: (i, 0)),
                   pl.BlockSpec((8, ROUTER_LANES), lambda i: (0, 0))],
        out_shape=[jax.ShapeDtypeStruct((n, HALF_D), jnp.uint32),
                   jax.ShapeDtypeStruct((n, ROUTER_LANES), F32),
                   jax.ShapeDtypeStruct((8, ROUTER_LANES), F32)],
        scratch_shapes=[pltpu.VMEM((8, ROUTER_LANES), F32)],
        compiler_params=pltpu.CompilerParams(
            dimension_semantics=("arbitrary",),
            vmem_limit_bytes=VMEM_LIMIT_BYTES),
    )(h, gain.reshape(1, D_MODEL).astype(F32), wr)


def _dispatch_kernel(dest_ref, hp_ref, xs_in_ref, xs_ref, sem):
    del xs_in_ref
    tc = hp_ref.shape[0]
    base = pl.program_id(0) * tc * MOE_TOPK

    def row_copy(r, k):
        return pltpu.make_async_copy(hp_ref.at[pl.ds(r, 1)],
                                     xs_ref.at[pl.ds(dest_ref[base + MOE_TOPK * r + k], 1)], sem)

    def start(r, c):
        for k in range(MOE_TOPK):
            row_copy(r, k).start()
        return c

    def wait(r, c):
        for k in range(MOE_TOPK):
            row_copy(r, k).wait()
        return c

    lax.fori_loop(0, tc, start, 0, unroll=ROW_DMA_UNROLL)
    lax.fori_loop(0, tc, wait, 0, unroll=ROW_DMA_UNROLL)


def _dispatch_pallas(hp, dest, xs0, *, tc=256):
    n = hp.shape[0]
    n_slots = xs0.shape[0]
    return pl.pallas_call(
        _dispatch_kernel,
        name="moe_dispatch",
        grid_spec=pltpu.PrefetchScalarGridSpec(
            num_scalar_prefetch=1,
            grid=(n // tc,),
            in_specs=[pl.BlockSpec((tc, HALF_D), lambda i, d: (i, 0)),
                      pl.BlockSpec(memory_space=pl.ANY)],
            out_specs=pl.BlockSpec(memory_space=pl.ANY),
            scratch_shapes=[pltpu.SemaphoreType.DMA(())]),
        out_shape=jax.ShapeDtypeStruct((n_slots, HALF_D), jnp.uint32),
        input_output_aliases={2: 0},
        compiler_params=pltpu.CompilerParams(
            dimension_semantics=("arbitrary",),
            vmem_limit_bytes=VMEM_LIMIT_BYTES),
    )(dest, hp, xs0)


MOE_WSLOTS = 3


def _expert_kernel(be_ref, first_ref, slot_ref, nxt1_ref, nxt2_ref, nu_ref, x_ref, w1_hbm, w3_hbm, w2_hbm,
                   y_ref, f1_ref, f3_ref, f2_ref, b1_ref, b3_ref, b2_ref, sem, *, layer):
    i = pl.program_id(0)

    def weight_copies(e, slot):
        return [pltpu.make_async_copy(src.at[layer, e], dst.at[slot], sem.at[slot, n])
                for n, (src, dst) in enumerate(((w1_hbm, f1_ref), (w3_hbm, f3_ref), (w2_hbm, f2_ref)))]

    def start_into(e, slot):
        @pl.when(e >= 0)
        def _():
            for c in weight_copies(e, slot):
                c.start()

    @pl.when(i < nu_ref[0])
    def _():
        slot = slot_ref[i]

        @pl.when(first_ref[i] == 1)
        def _():
            @pl.when(i == 0)
            def _():
                start_into(be_ref[i], slot)
                start_into(nxt1_ref[i], lax.rem(slot + 1, MOE_WSLOTS))

            for c in weight_copies(be_ref[i], slot):
                c.wait()
            start_into(nxt2_ref[i], lax.rem(slot + 2, MOE_WSLOTS))

            b1_ref[...] = f1_ref[slot].astype(BF16)
            b3_ref[...] = f3_ref[slot].astype(BF16)
            b2_ref[...] = f2_ref[slot].astype(BF16)

        x_lo, x_hi = (half.astype(BF16) for half in _unpack_halves(x_ref[...]))

        def up(w_ref):
            return (jnp.dot(x_lo, w_ref[:HALF_D, :], preferred_element_type=F32)
                    + jnp.dot(x_hi, w_ref[HALF_D:, :], preferred_element_type=F32))

        a = _silu(up(b1_ref)) * up(b3_ref)
        y_ref[...] = _pack_halves(jnp.dot(a.astype(BF16), b2_ref[...], preferred_element_type=F32))

    @pl.when(i >= nu_ref[0])
    def _():
        y_ref[...] = jnp.zeros_like(y_ref)


def _expert_pallas(xs, nblk, w1, w3, w2, layer):
    n_slots = xs.shape[0]
    nb = n_slots // MOE_TB
    bend = jnp.cumsum(nblk)
    blocks = jnp.arange(nb, dtype=jnp.int32)
    block_e = jnp.minimum(jnp.searchsorted(bend, blocks, side='right'), MOE_EXPERTS - 1).astype(jnp.int32)
    first = (blocks == (bend - nblk)[block_e]).astype(jnp.int32)
    nonempty = nblk > 0
    slot_e = (jnp.cumsum(nonempty) - 1) % MOE_WSLOTS
    ids = jnp.where(nonempty, jnp.arange(MOE_EXPERTS), MOE_EXPERTS)
    pad = jnp.full((2,), MOE_EXPERTS, ids.dtype)
    after = jnp.concatenate([lax.cummin(ids, reverse=True), pad])
    nxt1 = after[1:MOE_EXPERTS + 1]
    nxt2 = jnp.concatenate([after, pad[:1]])[jnp.minimum(nxt1, MOE_EXPERTS) + 1]
    as_id = lambda e: jnp.where(e < MOE_EXPERTS, e, -1).astype(jnp.int32)
    n_used = bend[-1:].astype(jnp.int32)
    blk = lambda i, *s: jnp.minimum(i, s[-1][0] - 1)
    hbm = pl.BlockSpec(memory_space=pl.ANY)
    return pl.pallas_call(
        functools.partial(_expert_kernel, layer=layer),
        name="moe_experts",
        grid_spec=pltpu.PrefetchScalarGridSpec(
            num_scalar_prefetch=6,
            grid=(nb,),
            in_specs=[pl.BlockSpec((MOE_TB, HALF_D), lambda i, *s: (blk(i, *s), 0)), hbm, hbm, hbm],
            out_specs=pl.BlockSpec((MOE_TB, HALF_D), lambda i, *s: (i, 0)),
            scratch_shapes=[pltpu.VMEM((MOE_WSLOTS, D_MODEL, MOE_FF), F32),
                            pltpu.VMEM((MOE_WSLOTS, D_MODEL, MOE_FF), F32),
                            pltpu.VMEM((MOE_WSLOTS, MOE_FF, D_MODEL), F32),
                            pltpu.VMEM((D_MODEL, MOE_FF), BF16), pltpu.VMEM((D_MODEL, MOE_FF), BF16),
                            pltpu.VMEM((MOE_FF, D_MODEL), BF16),
                            pltpu.SemaphoreType.DMA((MOE_WSLOTS, 3))]),
        out_shape=jax.ShapeDtypeStruct((n_slots, HALF_D), jnp.uint32),
        compiler_params=pltpu.CompilerParams(
            dimension_semantics=("arbitrary",),
            vmem_limit_bytes=VMEM_LIMIT_BYTES),
    )(block_e, first, slot_e[block_e].astype(jnp.int32), as_id(nxt1)[block_e], as_id(nxt2)[block_e], n_used,
      xs, w1, w3, w2)


def _combine_kernel(dest_ref, h_ref, meta_ref, g_ref, yb_ref, o_ref, buf_ref, sem, *, final_norm):
    tc = h_ref.shape[0]
    base = pl.program_id(0) * tc * MOE_TOPK

    def row_copy(r, k):
        return pltpu.make_async_copy(yb_ref.at[pl.ds(dest_ref[base + MOE_TOPK * r + k], 1)],
                                     buf_ref.at[k, pl.ds(r, 1)], sem)

    def start(r, c):
        for k in range(MOE_TOPK):
            row_copy(r, k).start()
        return c

    def wait(r, c):
        for k in range(MOE_TOPK):
            row_copy(r, k).wait()
        return c

    lax.fori_loop(0, tc, start, 0, unroll=ROW_DMA_UNROLL)
    lax.fori_loop(0, tc, wait, 0, unroll=ROW_DMA_UNROLL)
    meta = meta_ref[...]
    y_lo = jnp.zeros((tc, HALF_D), F32)
    y_hi = jnp.zeros((tc, HALF_D), F32)
    for k in range(MOE_TOPK):
        lo, hi = _unpack_halves(buf_ref[k])
        gate = meta[:, META_GATE + k:META_GATE + k + 1]
        y_lo = y_lo + lo * gate
        y_hi = y_hi + hi * gate
    out = h_ref[...] + jnp.concatenate([y_lo, y_hi], axis=1)
    if final_norm:
        out = out * lax.rsqrt(jnp.mean(out * out, axis=-1, keepdims=True) + EPS) * g_ref[...]
    o_ref[...] = out


def _combine_pallas(h, meta, dest, yb, final_gain, *, final_norm, tc=256):
    n = h.shape[0]
    return pl.pallas_call(
        functools.partial(_combine_kernel, final_norm=final_norm),
        name="moe_combine",
        grid_spec=pltpu.PrefetchScalarGridSpec(
            num_scalar_prefetch=1,
            grid=(n // tc,),
            in_specs=[pl.BlockSpec((tc, D_MODEL), lambda i, d: (i, 0)),
                      pl.BlockSpec((tc, ROUTER_LANES), lambda i, d: (i, 0)),
                      pl.BlockSpec((1, D_MODEL), lambda i, d: (0, 0)),
                      pl.BlockSpec(memory_space=pl.ANY)],
            out_specs=pl.BlockSpec((tc, D_MODEL), lambda i, d: (i, 0)),
            scratch_shapes=[pltpu.VMEM((MOE_TOPK, tc, HALF_D), jnp.uint32), pltpu.SemaphoreType.DMA(())]),
        out_shape=jax.ShapeDtypeStruct((n, D_MODEL), F32),
        compiler_params=pltpu.CompilerParams(
            dimension_semantics=("arbitrary",),
            vmem_limit_bytes=VMEM_LIMIT_BYTES),
    )(dest, h, meta, final_gain.reshape(1, D_MODEL).astype(F32), yb)


def _moe_slots(n):
    return ((n * MOE_TOPK) // MOE_TB + MOE_EXPERTS) * MOE_TB


def _moe_pallas(h, gain, w_rg, w_re, w1, w3, w2, layer, final_gain, xs_prev, *, final_norm):
    n = h.shape[0]
    hp, meta, cnt = _router_pallas(h, gain, w_rg, w_re)
    expert = meta[:, META_E:META_E + MOE_TOPK].astype(jnp.int32)
    rank = meta[:, META_RANK:META_RANK + MOE_TOPK].astype(jnp.int32)
    counts = cnt[0, MOE_LANE0:MOE_LANE0 + MOE_EXPERTS].astype(jnp.int32)
    nblk = (counts + MOE_TB - 1) // MOE_TB
    seg_start = (jnp.cumsum(nblk) - nblk) * MOE_TB
    hit = expert[..., None] == jnp.arange(MOE_EXPERTS, dtype=jnp.int32)
    dest = (jnp.sum(jnp.where(hit, seg_start, 0), axis=-1) + rank).reshape(n * MOE_TOPK)
    xs = _dispatch_pallas(hp, dest, xs_prev)
    yb = _expert_pallas(xs, nblk, w1, w3, w2, layer)
    return _combine_pallas(h, meta, dest, yb, final_gain, final_norm=final_norm), xs


N_GATES = 4 * ML_HEADS
IN_GROUPS_BEFORE_GATES = 4
IN_GROUPS = 13
LANE = 128


def _win_relayout_kernel(*refs):
    parts, main_ref = refs[:-1], refs[-1]
    n = pl.program_id(1)
    wide = jnp.concatenate([p[...] for p in parts], axis=1)

    @pl.when(n < IN_GROUPS_BEFORE_GATES)
    def _():
        main_ref[...] = wide[:, :GROUP_W].astype(main_ref.dtype)

    @pl.when(n >= IN_GROUPS_BEFORE_GATES)
    def _():
        main_ref[...] = wide[:, N_GATES:N_GATES + GROUP_W].astype(main_ref.dtype)


def _win_relayout(w_in):
    depth, k, _ = w_in.shape
    per = GROUP_W // LANE
    g0 = IN_GROUPS_BEFORE_GATES * GROUP_W
    gate = w_in[:, :, g0:g0 + N_GATES].reshape(depth, k, 4, ML_HEADS).transpose(0, 1, 3, 2)
    gate = jnp.pad(gate, ((0, 0), (0, 0), (0, 0), (0, LANE - 4))).reshape(depth, k, ML_HEADS * LANE)
    width = w_in.shape[2]
    w_b = jnp.pad(w_in.astype(BF16), ((0, 0), (0, 0), (0, -width % LANE)))
    part = lambda t: pl.BlockSpec((None, k, LANE), lambda l, n, t=t: (l, 0, per * n + t))
    main = pl.pallas_call(
        _win_relayout_kernel,
        name="w_in_relayout",
        grid=(depth, IN_GROUPS),
        in_specs=[part(t) for t in range(per + 1)],
        out_specs=pl.BlockSpec((None, k, GROUP_W), lambda l, n: (l, 0, n)),
        out_shape=jax.ShapeDtypeStruct((depth, k, IN_GROUPS * GROUP_W), BF16),
        compiler_params=pltpu.CompilerParams(
            dimension_semantics=("parallel", "parallel"),
            vmem_limit_bytes=VMEM_LIMIT_BYTES),
    )(*([w_b] * (per + 1)))
    return main, gate


COL_MLSTM, COL_HGRN, COL_FNET, COL_NA = 0, 4 * GROUP_W, 9 * GROUP_W, 10 * GROUP_W


def kernel(x, mem, norm_mix, norm_cross, norm_ffn, norm_final, norm_mem, w_in, mlstm_conv,
           mlstm_gate_bias, hgrn_lower_bound, na_rpb, group_gain, w_out, xa_wq, xa_wk, xa_wv, xa_wo,
           moe_router_group, moe_router_expert, moe_w1, moe_w3, moe_w2):
    b, s, d = x.shape
    mem_len = mem.shape[1]
    h = x.reshape(b * s, d)
    mem_f = mem.reshape(b * mem_len, d)
    lbs = jnp.cumsum(jax.nn.softmax(hgrn_lower_bound.astype(F32), axis=0), axis=0)
    lbs = lbs - lbs[0]
    fnet_tables = _fnet_tables(s)
    w_main, w_gate = _win_relayout(w_in)
    w_out_b, wq_b, wo_b = (w.astype(BF16) for w in (w_out, xa_wq, xa_wo))
    xs = jnp.zeros((_moe_slots(b * s), HALF_D), jnp.uint32)
    for l in range(DEPTH):
        proj, gates = _matmul(h, w_main, l, name="in_proj", gain=norm_mix[l], w_side=w_gate, out_dtype=BF16)
        y_ml = _mlstm_pallas(proj, gates, mlstm_conv[l], mlstm_gate_bias[l], batch=b, seq=s,
                             out_dtype=BF16)
        y_hg = _hgrn_pallas(proj, lbs[l], batch=b, seq=s, col0=COL_HGRN, out_dtype=BF16)
        y_fn = _fnet_pallas(proj, fnet_tables, batch=b, seq=s, col0=COL_FNET, out_dtype=BF16)
        y_na = _na_pallas(proj, na_rpb[l], batch=b, seq=s, col0=COL_NA, out_dtype=BF16)
        h = _matmul([y_ml, y_hg, y_fn, y_na], w_out_b, l, name="out_proj", gain=group_gain[l], residual=h)

        k = _matmul(mem_f, xa_wk, l, name="xa_k_proj", gain=norm_mem, out_dtype=BF16)
        v = _matmul(mem_f, xa_wv, l, name="xa_v_proj", gain=norm_mem, out_dtype=BF16)
        o = _xattn_pallas(h, norm_cross[l], wq_b, l, k, v, batch=b, seq=s, mem_len=mem_len)
        h = _matmul(o, wo_b, l, name="xa_o_proj", residual=h)

        h, xs = _moe_pallas(h, norm_ffn[l], moe_router_group[l], moe_router_expert[l],
                            moe_w1, moe_w3, moe_w2, l, norm_final, xs, final_norm=(l == DEPTH - 1))
    return h.reshape(b, s, d)
```

```python
import functools

import jax
import jax.numpy as jnp
import numpy as np
from jax import lax
from jax.experimental import pallas as pl
from jax.experimental.pallas import tpu as pltpu

D_MODEL = 2048
DEPTH = 4
GRID_W = 64
N_MIXERS = 4
GROUP_W = D_MODEL // N_MIXERS
ML_HEADS = 4
ML_DH = GROUP_W // ML_HEADS
HG_HEADS = 4
FN_GROUPS = 4
FN_CH = GROUP_W // FN_GROUPS
NA_HEADS = 8
NA_DH = GROUP_W // NA_HEADS
NA_KH = 8
NA_KW = 16
XA_HEADS = 4
XA_DH = D_MODEL // XA_HEADS
MOE_GROUPS = 4
MOE_PER_GROUP = 8
MOE_EXPERTS = MOE_GROUPS * MOE_PER_GROUP
MOE_TOPK = 2
EPS = 1e-6
F32 = jnp.float32
BF16 = jnp.bfloat16

VMEM_LIMIT_BYTES = 56 * 1024 * 1024


def _mm_kernel(*refs, n_x, norm, residual, side):
    refs = list(refs)
    x_refs = [refs.pop(0) for _ in range(n_x)]
    g_ref = refs.pop(0) if norm else None
    w_ref = refs.pop(0)
    ws_ref = refs.pop(0) if side else None
    r_ref = refs.pop(0) if residual else None
    o_ref = refs.pop(0)
    os_ref = refs.pop(0) if side else None
    xn_ref = refs.pop(0)

    @pl.when(pl.program_id(1) == 0)
    def _():
        col = 0
        for x_ref in x_refs:
            x = x_ref[...].astype(F32)
            kx = x.shape[1]
            if norm:
                ms = jnp.mean(x * x, axis=-1, keepdims=True)
                x = x * lax.rsqrt(ms + EPS) * g_ref[:, col:col + kx]
            xn_ref[:, col:col + kx] = x.astype(BF16)
            col += kx
        if side:
            os_ref[...] = jnp.dot(xn_ref[...], ws_ref[...].astype(BF16),
                                  preferred_element_type=F32).astype(os_ref.dtype)

    acc = jnp.dot(xn_ref[...], w_ref[...].astype(BF16), preferred_element_type=F32)
    if residual:
        acc = acc + r_ref[...]
    o_ref[...] = acc.astype(o_ref.dtype)


def _matmul(xs, w, layer, *, name, gain=None, residual=None, w_side=None, tm=1024, tn=512, out_dtype=F32):
    if not isinstance(xs, (list, tuple)):
        xs = [xs]
    m = xs[0].shape[0]
    k = sum(x.shape[1] for x in xs)
    n = w.shape[2]
    tm = min(tm, m)
    tn = min(tn, n)
    assert m % tm == 0 and n % tn == 0 and w.shape[1] == k, (m, n, k, tm, tn)
    norm = gain is not None
    has_res = residual is not None
    side = w_side is not None
    in_specs = [pl.BlockSpec((tm, x.shape[1]), lambda i, j: (i, 0)) for x in xs]
    args = list(xs)
    if norm:
        in_specs.append(pl.BlockSpec((1, k), lambda i, j: (0, 0)))
        args.append(gain.reshape(1, k).astype(F32))
    in_specs.append(pl.BlockSpec((None, k, tn), lambda i, j: (layer, 0, j)))
    args.append(w)
    if side:
        in_specs.append(pl.BlockSpec((None, k, w_side.shape[2]), lambda i, j: (layer, 0, 0)))
        args.append(w_side)
    if has_res:
        in_specs.append(pl.BlockSpec((tm, tn), lambda i, j: (i, j)))
        args.append(residual)
    out_specs = pl.BlockSpec((tm, tn), lambda i, j: (i, j))
    out_shape = jax.ShapeDtypeStruct((m, n), out_dtype)
    if side:
        out_specs = [out_specs, pl.BlockSpec((tm, w_side.shape[2]), lambda i, j: (i, 0))]
        out_shape = [out_shape, jax.ShapeDtypeStruct((m, w_side.shape[2]), F32)]
    return pl.pallas_call(
        functools.partial(_mm_kernel, n_x=len(xs), norm=norm, residual=has_res, side=side),
        name=name,
        grid=(m // tm, n // tn),
        in_specs=in_specs,
        out_specs=out_specs,
        out_shape=out_shape,
        scratch_shapes=[pltpu.VMEM((tm, k), BF16)],
        compiler_params=pltpu.CompilerParams(
            dimension_semantics=("parallel", "arbitrary"),
            vmem_limit_bytes=VMEM_LIMIT_BYTES),
    )(*args)


NA_QROWS = 4
NA_WROWS = 12


def _na_groups(rows):
    tables, plan = [], []
    for r0 in range(0, rows, NA_QROWS):
        band0 = lambda r: min(max(r - NA_KH // 2, 0), rows - NA_KH)
        kr0 = min(band0(r0), rows - NA_WROWS)
        assert band0(r0 + NA_QROWS - 1) + NA_KH <= kr0 + NA_WROWS
        dr = np.full((NA_QROWS, NA_WROWS), -1, np.int64)
        for i in range(NA_QROWS):
            for j in range(NA_WROWS):
                if 0 <= kr0 + j - band0(r0 + i) < NA_KH:
                    dr[i, j] = kr0 + j - (r0 + i) + NA_KH - 1
        key = dr.tobytes()
        if key not in [t.tobytes() for t in tables]:
            tables.append(dr)
        plan.append((r0, kr0, [t.tobytes() for t in tables].index(key)))
    return plan, np.stack(tables)


def _na_bias_tables(rpb, dr_tables):
    c = np.arange(GRID_W)
    dc = np.clip(c[None, :] - c[:, None] + NA_KW - 1, 0, 2 * NA_KW - 2)
    onehot = (dc[None] == np.arange(2 * NA_KW - 1)[:, None, None]).astype(np.float32)
    col_start = np.clip(c - NA_KW // 2, 0, GRID_W - NA_KW)
    col_ok = (c[None, :] >= col_start[:, None]) & (c[None, :] < col_start[:, None] + NA_KW)
    t = jnp.einsum('hrd,dqk->hrqk', rpb.astype(F32), onehot, precision=lax.Precision.HIGHEST)
    t = jnp.where(col_ok[None, None], t, -jnp.inf)
    t = jnp.concatenate([t, jnp.full_like(t[:, :1], -jnp.inf)], axis=1)
    idx = np.where(dr_tables < 0, t.shape[1] - 1, dr_tables)
    pairs = idx.reshape(idx.shape[0], idx.shape[1], -1, 2)
    uniq = sorted({(int(a), int(b)) for a, b in pairs.reshape(-1, 2)})
    pair_idx = np.array([[[uniq.index((int(a), int(b))) for a, b in row] for row in typ] for typ in pairs])
    table = jnp.stack([jnp.concatenate([t[:, a], t[:, b]], axis=-1) for a, b in uniq], axis=1)
    return table, pair_idx


def _na_kernel(q_ref, k_ref, v_ref, tab_ref, o_ref, bias_ref, *, plan, pair_idx):
    nq = NA_QROWS * GRID_W
    nk = NA_WROWS * GRID_W
    for hh in range(2):
        for typ in range(pair_idx.shape[0]):
            for i in range(pair_idx.shape[1]):
                for jj in range(pair_idx.shape[2]):
                    bias_ref[hh, typ, i * GRID_W:(i + 1) * GRID_W, jj * 2 * GRID_W:(jj + 1) * 2 * GRID_W] = (
                        tab_ref[hh, int(pair_idx[typ, i, jj])])
    for r0, kr0, typ in plan:
        q = q_ref[r0 * GRID_W:r0 * GRID_W + nq, :].astype(F32) * (NA_DH ** -0.5)
        kb = k_ref[kr0 * GRID_W:kr0 * GRID_W + nk, :]
        vb = v_ref[kr0 * GRID_W:kr0 * GRID_W + nk, :]
        outs = []
        for hh in range(2):
            sl = slice(hh * NA_DH, (hh + 1) * NA_DH)
            s = lax.dot_general(q[:, sl].astype(BF16), kb[:, sl].astype(BF16),
                                (((1,), (1,)), ((), ())), preferred_element_type=F32)
            s = s + bias_ref[hh, typ]
            m = jnp.max(s, axis=-1, keepdims=True)
            p = jnp.exp(s - m)
            l = jnp.sum(p, axis=-1, keepdims=True)
            o = jnp.dot(p.astype(BF16), vb[:, sl].astype(BF16), preferred_element_type=F32)
            outs.append(o / l)
        o_ref[r0 * GRID_W:r0 * GRID_W + nq, :] = jnp.concatenate(outs, axis=-1).astype(o_ref.dtype)


def _na_pallas(proj, rpb, *, batch, seq, col0, out_dtype=F32):
    rows = seq // GRID_W
    pair_w = 2 * NA_DH
    cb = col0 // pair_w
    gb = GROUP_W // pair_w
    plan, dr_tables = _na_groups(rows)
    table, pair_idx = _na_bias_tables(rpb, dr_tables)
    spec = lambda g: pl.BlockSpec((seq, pair_w), lambda b, p, g=g: (b, cb + g * gb + p))
    return pl.pallas_call(
        functools.partial(_na_kernel, plan=plan, pair_idx=pair_idx),
        name="na_attention",
        grid=(batch, NA_HEADS // 2),
        in_specs=[spec(0), spec(1), spec(2),
                  pl.BlockSpec((2,) + table.shape[1:], lambda b, p: (p, 0, 0, 0))],
        out_specs=pl.BlockSpec((seq, pair_w), lambda b, p: (b, p)),
        out_shape=jax.ShapeDtypeStruct((batch * seq, GROUP_W), out_dtype),
        scratch_shapes=[pltpu.VMEM((2, dr_tables.shape[0], NA_QROWS * GRID_W, NA_WROWS * GRID_W), F32)],
        compiler_params=pltpu.CompilerParams(
            dimension_semantics=("parallel", "parallel"),
            vmem_limit_bytes=VMEM_LIMIT_BYTES),
    )(proj, proj, proj, table)


ML_L = 128
ML_GATE_LANES = 128


def _log_sigmoid(x):
    return jnp.minimum(x, 0.0) - jnp.log(1.0 + jnp.exp(-jnp.abs(x)))


def _silu(x):
    return x * jax.nn.sigmoid(x)


_NT = (((1,), (1,)), ((), ()))
_TN = (((0,), (0,)), ((), ()))


def _split3(x):
    hi = x.astype(BF16)
    r1 = x - hi.astype(F32)
    mid = r1.astype(BF16)
    lo = (r1 - mid.astype(F32)).astype(BF16)
    return hi, mid, lo


def _dot_exact_lhs(a, x, dims=None):
    a = a.astype(BF16)
    if dims is None:
        return sum(jnp.dot(a, p, preferred_element_type=F32) for p in _split3(x))
    return sum(lax.dot_general(a, p, dims, preferred_element_type=F32) for p in _split3(x))


ML_GROUP = 4
ML_PAD = 16


def _mlstm_local(q, k, v, gcol, tri, mask_w, gi, end_row):
    L = ML_L
    cum_col = _dot_exact_lhs(tri, _log_sigmoid(gcol))
    wide = lambda col: jnp.broadcast_to(col, (L, L))
    b_cols, ig_cols, rows = [], [], []
    for g in range(len(q)):
        lanes = slice(g * ML_GATE_LANES, (g + 1) * ML_GATE_LANES)
        b_cols.append(cum_col[:, g * ML_GATE_LANES + gi + 1:g * ML_GATE_LANES + gi + 2])
        ig_cols.append(gcol[:, g * ML_GATE_LANES + gi:g * ML_GATE_LANES + gi + 1])
        rows.append(gcol[:, lanes].T[gi:gi + 1, :] - cum_col[:, lanes].T[gi + 1:gi + 2, :])
    b_w = jnp.concatenate([wide(b) for b in b_cols], axis=1)
    row_w = jnp.concatenate(rows, axis=1)
    dmat = jnp.where(mask_w != 0.0, b_w + row_w, -jnp.inf)
    m_in_w = jnp.concatenate([wide(jnp.max(dmat[:, g * L:(g + 1) * L], axis=-1, keepdims=True))
                              for g in range(len(q))], axis=1)
    w = jnp.exp(dmat - m_in_w)
    ones = jnp.ones((L, ML_DH), BF16)
    out = []
    for g in range(len(q)):
        lanes = slice(g * L, (g + 1) * L)
        kb = k[g].astype(BF16)
        v_aug = jnp.concatenate([v[g].astype(BF16), ones], axis=1)
        qk = lax.dot_general(q[g].astype(BF16), kb, _NT, preferred_element_type=F32) * w[:, lanes]
        res = jnp.dot(qk.astype(BF16), v_aug, preferred_element_type=F32)
        b_end = b_cols[g][end_row:end_row + 1, :]
        a_col = b_end - b_cols[g] + ig_cols[g]
        m_loc = jnp.max(a_col, axis=0, keepdims=True)
        kw = k[g] * jnp.exp(a_col - m_loc)
        st = lax.dot_general(v_aug, kw.astype(BF16), _TN, preferred_element_type=F32)
        out.append(dict(m_in=m_in_w[:, lanes], num=res[:, :ML_DH], den=res[:, ML_DH:], b=b_w[:, lanes],
                        ct=st[:ML_DH], n=st[ML_DH:ML_DH + 1], m_loc=m_loc, b_end=b_end))
    return out


def _mlstm_carry(q, m_in, num_in, den_in, b, ct, n, m_prev):
    inter = b + m_prev
    m_t = jnp.maximum(inter, m_in)
    s_in = jnp.exp(m_in - m_t)
    s_inter = jnp.exp(inter - m_t)
    rhs = jnp.concatenate([ct.astype(BF16), jnp.broadcast_to(n, (ML_PAD, ML_DH)).astype(BF16)], axis=0)
    both = lax.dot_general(q.astype(BF16), rhs, _NT, preferred_element_type=F32)
    num = s_in * num_in + s_inter * both[:, :ML_DH]
    den = s_in * den_in + s_inter * both[:, ML_DH:ML_DH + 1]
    return num / jnp.maximum(jnp.abs(den), jnp.exp(-m_t))


def _mlstm_kernel(q_ref, k_ref, v_ref, o_ref, cwq_ref, cwk_ref, gcol_ref, bcol_ref,
                  y_ref, qs_ref, ks_ref, num_ref, den_ref, min_ref, b_ref, ctl_ref, stat_ref, *, seq):
    L = ML_L
    nc = seq // L
    t_idx = lax.broadcasted_iota(jnp.int32, (seq, 1), 0)

    def conv_silu(x, w):
        prev = jnp.where(t_idx == 0, 0.0, pltpu.roll(x, 1, axis=0))
        nxt = jnp.where(t_idx == seq - 1, 0.0, pltpu.roll(x, seq - 1, axis=0))
        return _silu(prev * w[0:1, :] + x * w[1:2, :] + nxt * w[2:3, :])

    qs_ref[...] = conv_silu(q_ref[...].astype(F32), cwq_ref[...])
    ks_ref[...] = conv_silu(k_ref[...].astype(F32), cwk_ref[...]) * (ML_DH ** -0.5)

    ti = lax.broadcasted_iota(jnp.int32, (L, L), 0)
    si = lax.broadcasted_iota(jnp.int32, (L, L), 1)
    tris = (jnp.where(si <= ti, 1.0, 0.0), jnp.where(si >= ti, 1.0, 0.0))
    masks_w = [jnp.concatenate([t] * ML_GROUP, axis=1) for t in tris]

    for direction in (0, 1):
        for c0 in range(0, nc, ML_GROUP):
            rows = [pl.ds((c0 + g) * L, L) for g in range(ML_GROUP)]
            gcol = jnp.concatenate([gcol_ref[r, :] + bcol_ref[...] for r in rows], axis=1)
            local = _mlstm_local([qs_ref[r, :] for r in rows], [ks_ref[r, :] for r in rows],
                                 [v_ref[r, :].astype(F32) for r in rows], gcol,
                                 tris[direction], masks_w[direction], 2 * direction,
                                 L - 1 if direction == 0 else 0)
            for g, (r, loc) in enumerate(zip(rows, local)):
                idx = direction * nc + c0 + g
                num_ref[direction, r, :] = loc["num"]
                den_ref[direction, r, :] = loc["den"]
                min_ref[direction, r, :] = loc["m_in"]
                b_ref[direction, r, :] = loc["b"]
                ctl_ref[idx] = loc["ct"]
                stat_ref[idx, 0:1, :] = loc["n"]
                stat_ref[idx, 1:2, :] = jnp.broadcast_to(loc["m_loc"], (1, ML_DH))
                stat_ref[idx, 2:3, :] = jnp.broadcast_to(loc["b_end"], (1, ML_DH))

    state = [(jnp.zeros((ML_DH, ML_DH), F32), jnp.zeros((1, ML_DH), F32), jnp.zeros((1, ML_DH), F32))] * 2
    for i in range(nc):
        for direction in (0, 1):
            c = i if direction == 0 else nc - 1 - i
            r = pl.ds(c * L, L)
            idx = direction * nc + c
            ct, n, m = state[direction]
            num_ref[direction, r, :] = _mlstm_carry(qs_ref[r, :], min_ref[direction, r, :],
                                                    num_ref[direction, r, :], den_ref[direction, r, :],
                                                    b_ref[direction, r, :], ct, n, m)
            n_loc, m_loc, b_end = stat_ref[idx, 0:1, :], stat_ref[idx, 1:2, :], stat_ref[idx, 2:3, :]
            m_new = jnp.maximum(b_end + m, m_loc)
            s_old = jnp.exp(b_end + m - m_new)
            s_new = jnp.exp(m_loc - m_new)
            state[direction] = (s_old * ct + s_new * ctl_ref[idx], s_old * n + s_new * n_loc, m_new)
    y_ref[...] = (jax.nn.sigmoid(o_ref[...].astype(F32)) * (num_ref[0] + num_ref[1])).astype(y_ref.dtype)


def _mlstm_pallas(proj, gates, conv_w, gate_b, *, batch, seq, out_dtype=F32):
    d = ML_DH
    hb = GROUP_W // d
    gb4 = gate_b.astype(F32).reshape(4, ML_HEADS).T
    gb_col = jnp.pad(gb4, ((0, 0), (0, ML_GATE_LANES - 4))).reshape(ML_HEADS, 1, ML_GATE_LANES)
    spec = lambda grp: pl.BlockSpec((seq, d), lambda b, h, grp=grp: (b, grp * hb + h))
    f32 = lambda *shape: pltpu.VMEM(shape, F32)
    return pl.pallas_call(
        functools.partial(_mlstm_kernel, seq=seq),
        name="mlstm",
        grid=(batch, ML_HEADS),
        in_specs=[spec(0), spec(1), spec(2), spec(3),
                  pl.BlockSpec((3, d), lambda b, h: (0, h)),
                  pl.BlockSpec((3, d), lambda b, h: (0, hb + h)),
                  pl.BlockSpec((seq, ML_GATE_LANES), lambda b, h: (b, h)),
                  pl.BlockSpec((None, 1, ML_GATE_LANES), lambda b, h: (h, 0, 0))],
        out_specs=pl.BlockSpec((seq, d), lambda b, h: (b, h)),
        out_shape=jax.ShapeDtypeStruct((batch * seq, GROUP_W), out_dtype),
        scratch_shapes=[f32(seq, d), f32(seq, d), f32(2, seq, d), f32(2, seq, d), f32(2, seq, d),
                        f32(2, seq, d), f32(2 * (seq // ML_L), d, d), f32(2 * (seq // ML_L), 8, d)],
        compiler_params=pltpu.CompilerParams(
            dimension_semantics=("parallel", "parallel"),
            vmem_limit_bytes=VMEM_LIMIT_BYTES),
    )(proj, proj, proj, proj, conv_w.astype(F32), conv_w.astype(F32), gates, gb_col)


HG_L = 128
HG_LEAF = 8


HG_GROUP = 4
HG_NLEAF = HG_L // HG_LEAF
HG_LEVELS = 4


def _hgrn_ref_rows(p_ref, chunks, backward):
    def rows(first, count, stride):
        return jnp.concatenate([p_ref[pl.ds(c * HG_L + first, count, stride=stride), :] for c in chunks], axis=1)

    leaf = rows(HG_LEAF // 2 if backward else HG_LEAF // 2 - 1, HG_NLEAF, HG_LEAF)
    levels = []
    m = HG_L // 2
    for _ in range(HG_LEVELS):
        blocks = HG_L // (2 * m)
        r = rows(m if backward else m - 1, blocks, 2 * m) if blocks > 1 else rows(m if backward else m - 1, 1, 1)
        levels.append(jnp.repeat(r, HG_NLEAF // blocks, axis=0))
        m //= 2
    return leaf, levels


def _hgrn_chunks(q, k, v, p, ref_leaf, ref_levels, masks):
    L = HG_L
    dk = HG_L
    groups = q.shape[1] // dk
    spread = lambda e: jnp.broadcast_to(e[:, None, :], (HG_NLEAF, HG_LEAF, e.shape[1])).reshape(L, e.shape[1])
    d = p - spread(ref_leaf)
    q_leaf = q * jnp.exp(d)
    k_leaf = k * jnp.exp(-d)
    pairs = [(q_leaf.astype(BF16), k_leaf.astype(BF16), masks[HG_LEVELS])]
    for li in range(HG_LEVELS):
        gap = ref_leaf - ref_levels[li]
        qh = q_leaf * spread(jnp.exp(jnp.minimum(gap, 0.0)))
        kh = k_leaf * spread(jnp.exp(jnp.minimum(-gap, 0.0)))
        pairs.append((qh.astype(BF16), kh.astype(BF16), masks[li]))
    vb = v.astype(BF16)
    outs = []
    for g in range(groups):
        lanes = slice(g * dk, (g + 1) * dk)
        attn = jnp.zeros((L, L), F32)
        for qh, kh, mask in pairs:
            s = lax.dot_general(qh[:, lanes], kh[:, lanes], _NT, preferred_element_type=F32)
            attn = attn + jnp.where(mask != 0.0, s, 0.0)
        outs.append(jnp.dot(attn.astype(BF16), vb[:, lanes], preferred_element_type=F32))
    return jnp.concatenate(outs, axis=1)


def _hgrn_carry(q, k, v, p, st, backward):
    L = HG_L
    o = lax.dot_general((q * jnp.exp(p)).astype(BF16), st.astype(BF16), _NT, preferred_element_type=F32)
    p_end = p[0:1, :] if backward else p[L - 1:L, :]
    kd = k * jnp.exp(p_end - p)
    st = st * jnp.exp(p_end) + lax.dot_general(v.astype(BF16), kd.astype(BF16), _TN,
                                               preferred_element_type=F32)
    return o, st


def _hgrn_masks(backward):
    L = HG_L
    ti = lax.broadcasted_iota(jnp.int32, (L, L), 0)
    si = lax.broadcasted_iota(jnp.int32, (L, L), 1)
    if backward:
        ti, si = si, ti
    one = lambda cond: jnp.where(cond, 1.0, 0.0)
    masks = []
    m = L // 2
    while m >= HG_LEAF:
        same = one((ti // (2 * m)) == (si // (2 * m)))
        masks.append(same * one((ti % (2 * m)) >= m) * one((si % (2 * m)) < m))
        m //= 2
    masks.append(one((ti // HG_LEAF) == (si // HG_LEAF)) * one(si <= ti))
    return masks


def _hgrn_kernel(q_ref, ff_ref, fb_ref, i_ref, g_ref, lb_ref, y_ref,
                 qs_ref, lf_ref, kf_ref, lbk_ref, kb_ref, of_ref, ob_ref, *, seq):
    L = HG_L
    nc = seq // L
    lb = lb_ref[...]
    log_lb = jnp.log(lb)
    log1m_lb = jnp.log1p(-lb)

    def forget(fp):
        ls = _log_sigmoid(fp)
        a = log_lb
        c = log1m_lb + ls
        logf = jnp.maximum(a, c) + jnp.log(1.0 + jnp.exp(-jnp.abs(a - c)))
        return logf, (1.0 - lb) * jnp.exp(ls - fp)

    qs_ref[...] = _silu(q_ref[...].astype(F32))
    lf_ref[...], kf_ref[...] = forget(ff_ref[...].astype(F32))
    lbk_ref[...], kb_ref[...] = forget(fb_ref[...].astype(F32))

    ti = lax.broadcasted_iota(jnp.int32, (L, L), 0)
    si = lax.broadcasted_iota(jnp.int32, (L, L), 1)
    tril = (si <= ti).astype(F32)
    triu = (si >= ti).astype(F32)
    masks_f = _hgrn_masks(False)
    masks_b = _hgrn_masks(True)

    def side_by_side(ref, c0):
        return jnp.concatenate([ref[pl.ds((c0 + g) * L, L), :].astype(F32) for g in range(HG_GROUP)], axis=1)

    def scatter_back(ref, c0, wide):
        for g in range(HG_GROUP):
            ref[pl.ds((c0 + g) * L, L), :] = wide[:, g * L:(g + 1) * L]

    directions = ((kf_ref, lf_ref, of_ref, tril, masks_f, False), (kb_ref, lbk_ref, ob_ref, triu, masks_b, True))
    for c0 in range(0, nc, HG_GROUP):
        for _, f_ref, _, tri, _, _ in directions:
            scatter_back(f_ref, c0, _dot_exact_lhs(tri, side_by_side(f_ref, c0)))
    for c0 in range(0, nc, HG_GROUP):
        q = side_by_side(qs_ref, c0)
        v = side_by_side(i_ref, c0)
        for k_ref, f_ref, o_ref, _, masks, backward in directions:
            ref_leaf, ref_levels = _hgrn_ref_rows(f_ref, range(c0, c0 + HG_GROUP), backward)
            scatter_back(o_ref, c0, _hgrn_chunks(q, side_by_side(k_ref, c0), v, side_by_side(f_ref, c0),
                                                 ref_leaf, ref_levels, masks))
    st_f = jnp.zeros((L, L), F32)
    st_b = jnp.zeros((L, L), F32)
    for i in range(nc):
        sl = pl.ds(i * L, L)
        o, st_f = _hgrn_carry(qs_ref[sl, :], kf_ref[sl, :], i_ref[sl, :].astype(F32), lf_ref[sl, :], st_f, False)
        of_ref[sl, :] += o
        sl = pl.ds((nc - 1 - i) * L, L)
        o, st_b = _hgrn_carry(qs_ref[sl, :], kb_ref[sl, :], i_ref[sl, :].astype(F32), lbk_ref[sl, :], st_b, True)
        ob_ref[sl, :] += o
    o = of_ref[...] + ob_ref[...]
    o = o * lax.rsqrt(jnp.mean(o * o, axis=-1, keepdims=True) + EPS)
    y_ref[...] = (o * _silu(g_ref[...].astype(F32))).astype(y_ref.dtype)


def _hgrn_pallas(proj, lb, *, batch, seq, col0, out_dtype=F32):
    d = GROUP_W // HG_HEADS
    hb = GROUP_W // d
    cb = col0 // d
    spec = lambda grp: pl.BlockSpec((seq, d), lambda b, h, grp=grp: (b, cb + grp * hb + h))
    f32 = lambda *shape: pltpu.VMEM(shape, F32)
    return pl.pallas_call(
        functools.partial(_hgrn_kernel, seq=seq),
        name="hgrn2",
        grid=(batch, HG_HEADS),
        in_specs=[spec(0), spec(1), spec(2), spec(3), spec(4),
                  pl.BlockSpec((1, d), lambda b, h: (0, h))],
        out_specs=pl.BlockSpec((seq, d), lambda b, h: (b, h)),
        out_shape=jax.ShapeDtypeStruct((batch * seq, GROUP_W), out_dtype),
        scratch_shapes=[f32(seq, d)] * 7,
        compiler_params=pltpu.CompilerParams(
            dimension_semantics=("parallel", "parallel"),
            vmem_limit_bytes=VMEM_LIMIT_BYTES),
    )(proj, proj, proj, proj, proj, lb.astype(F32).reshape(1, GROUP_W))


DFT_SPLIT = 32


def _dft_cos_sin(n):
    def direct(rows, period):
        k = (rows[:, None] * jnp.arange(n, dtype=jnp.int32)[None, :]) % period
        ang = k.astype(F32) * (2.0 * np.pi / period)
        return jnp.cos(ang), jnp.sin(ang)

    if n % DFT_SPLIT or n // DFT_SPLIT < DFT_SPLIT:
        return direct(jnp.arange(n, dtype=jnp.int32), n)
    ca, sa = direct(jnp.arange(n // DFT_SPLIT, dtype=jnp.int32), n // DFT_SPLIT)
    cb, sb = direct(jnp.arange(DFT_SPLIT, dtype=jnp.int32), n)
    cos = ca[:, None, :] * cb[None, :, :] - sa[:, None, :] * sb[None, :, :]
    sin = sa[:, None, :] * cb[None, :, :] + ca[:, None, :] * sb[None, :, :]
    return cos.reshape(n, n), sin.reshape(n, n)


def _fnet_tables(seq):
    cc, sc = _dft_cos_sin(FN_CH)
    eye = jnp.eye(FN_GROUPS, dtype=F32)
    chan = jnp.concatenate([jnp.kron(eye, cc), jnp.kron(eye, sc)], axis=1)
    cs, ss = _dft_cos_sin(seq)
    pos = jnp.concatenate([cs, -ss], axis=1) * ((seq * FN_CH) ** -0.5)
    return chan.astype(BF16), pos.astype(BF16)


def _fnet_chan_kernel(u_ref, dft_ref, v_ref):
    r = jnp.dot(u_ref[...].astype(BF16), dft_ref[...], preferred_element_type=F32)
    v_ref[0] = r[:, :GROUP_W].astype(v_ref.dtype)
    v_ref[1] = r[:, GROUP_W:].astype(v_ref.dtype)


def _fnet_pos_kernel(w_ref, v_ref, o_ref):
    o_ref[...] = jnp.dot(w_ref[...], v_ref[...], preferred_element_type=F32).astype(o_ref.dtype)


def _fnet_pallas(proj, tables, *, batch, seq, col0, out_dtype=F32, ts=512):
    chan, pos = tables
    cb = col0 // GROUP_W
    nt = seq // ts
    v = pl.pallas_call(
        _fnet_chan_kernel,
        name="fnet_channel_dft",
        grid=(batch, nt),
        in_specs=[pl.BlockSpec((ts, GROUP_W), lambda b, i: (b * nt + i, cb)),
                  pl.BlockSpec((GROUP_W, 2 * GROUP_W), lambda b, i: (0, 0))],
        out_specs=pl.BlockSpec((2, ts, GROUP_W), lambda b, i: (0, i, b)),
        out_shape=jax.ShapeDtypeStruct((2, seq, batch * GROUP_W), BF16),
        compiler_params=pltpu.CompilerParams(
            dimension_semantics=("parallel", "parallel"),
            vmem_limit_bytes=VMEM_LIMIT_BYTES),
    )(proj, chan)
    v = v.reshape(2 * seq, batch * GROUP_W)
    return pl.pallas_call(
        _fnet_pos_kernel,
        name="fnet_position_dft",
        grid=(nt, batch),
        in_specs=[pl.BlockSpec((ts, 2 * seq), lambda i, b: (i, 0)),
                  pl.BlockSpec((2 * seq, GROUP_W), lambda i, b: (0, b))],
        out_specs=pl.BlockSpec((ts, GROUP_W), lambda i, b: (b * nt + i, 0)),
        out_shape=jax.ShapeDtypeStruct((batch * seq, GROUP_W), out_dtype),
        compiler_params=pltpu.CompilerParams(
            dimension_semantics=("parallel", "parallel"),
            vmem_limit_bytes=VMEM_LIMIT_BYTES),
    )(pos, v)


def _xattn_kernel(x_ref, g_ref, wq_ref, k_ref, v_ref, o_ref, xn_ref):
    @pl.when(pl.program_id(1) == 0)
    def _():
        x = x_ref[...].astype(F32)
        ms = jnp.mean(x * x, axis=-1, keepdims=True)
        xn_ref[...] = (x * lax.rsqrt(ms + EPS) * g_ref[...]).astype(BF16)

    q = jnp.dot(xn_ref[...], wq_ref[...].astype(BF16), preferred_element_type=F32)
    s = lax.dot_general(q.astype(BF16), k_ref[...].astype(BF16), _NT,
                        preferred_element_type=F32) * (XA_DH ** -0.5)
    p = jnp.exp(s - jnp.max(s, axis=-1, keepdims=True))
    l = jnp.sum(p, axis=-1, keepdims=True)
    o = jnp.dot(p.astype(BF16), v_ref[...].astype(BF16), preferred_element_type=F32)
    o_ref[...] = (o / l).astype(o_ref.dtype)


def _xattn_pallas(h, gain, wq, layer, k, v, *, batch, seq, mem_len, tm=1024, out_dtype=BF16):
    tiles = seq // tm
    return pl.pallas_call(
        _xattn_kernel,
        name="cross_attention",
        grid=(batch * tiles, XA_HEADS),
        in_specs=[pl.BlockSpec((tm, D_MODEL), lambda i, j: (i, 0)),
                  pl.BlockSpec((1, D_MODEL), lambda i, j: (0, 0)),
                  pl.BlockSpec((None, D_MODEL, XA_DH), lambda i, j: (layer, 0, j)),
                  pl.BlockSpec((mem_len, XA_DH), lambda i, j: (i // tiles, j)),
                  pl.BlockSpec((mem_len, XA_DH), lambda i, j: (i // tiles, j))],
        out_specs=pl.BlockSpec((tm, XA_DH), lambda i, j: (i, j)),
        out_shape=jax.ShapeDtypeStruct((batch * seq, D_MODEL), out_dtype),
        scratch_shapes=[pltpu.VMEM((tm, D_MODEL), BF16)],
        compiler_params=pltpu.CompilerParams(
            dimension_semantics=("parallel", "arbitrary"),
            vmem_limit_bytes=VMEM_LIMIT_BYTES),
    )(h, gain.reshape(1, D_MODEL).astype(F32), wq, k, v)


MOE_FF = D_MODEL // 4
MOE_TB = 256
MOE_LANE0 = MOE_GROUPS
ROUTER_LANES = 128
META_E, META_RANK, META_GATE = 0, 2, 4
HALF_D = D_MODEL // 2
ROW_DMA_UNROLL = 8


def _pack_halves(x):
    bits = lax.bitcast_convert_type(x.astype(BF16).astype(F32), jnp.uint32)
    h = x.shape[1] // 2
    return (bits[:, :h] >> 16) | (bits[:, h:] & jnp.uint32(0xFFFF0000))


def _unpack_halves(p):
    return (lax.bitcast_convert_type(p << 16, F32),
            lax.bitcast_convert_type(p & jnp.uint32(0xFFFF0000), F32))


def _router_kernel(x_ref, g_ref, wr_ref, hp_ref, meta_ref, cnt_ref, carry_ref):
    tm = x_ref.shape[0]

    @pl.when(pl.program_id(0) == 0)
    def _():
        carry_ref[...] = jnp.zeros_like(carry_ref)

    x = x_ref[...].astype(F32)
    hn = x * lax.rsqrt(jnp.mean(x * x, axis=-1, keepdims=True) + EPS) * g_ref[...]
    hb = hn.astype(BF16)
    logits = jnp.dot(hb, wr_ref[...].astype(BF16), preferred_element_type=F32)
    hp_ref[...] = _pack_halves(hn)

    lane = lax.broadcasted_iota(jnp.int32, (tm, ROUTER_LANES), 1).astype(F32)
    ninf = -jnp.inf
    first = lambda hit: jnp.min(jnp.where(hit, lane, float(ROUTER_LANES)), axis=-1, keepdims=True)
    gl = jnp.where(lane < MOE_GROUPS, logits, ninf)
    gmax = jnp.max(gl, axis=-1, keepdims=True)
    gidx = first(gl == gmax)
    g_gate = 1.0 / jnp.sum(jnp.exp(gl - gmax), axis=-1, keepdims=True)
    off = lane - (MOE_LANE0 + MOE_PER_GROUP * gidx)
    el = jnp.where(jnp.abs(2.0 * off - (MOE_PER_GROUP - 1)) < MOE_PER_GROUP, logits, ninf)
    v1 = jnp.max(el, axis=-1, keepdims=True)
    l1 = first(el == v1)
    el2 = jnp.where(lane == l1, ninf, el)
    v2 = jnp.max(el2, axis=-1, keepdims=True)
    l2 = first(el2 == v2)
    t = jnp.exp(v2 - v1)
    gate1 = g_gate / (1.0 + t)
    gate2 = g_gate * t / (1.0 + t)

    oh = jnp.where(lane == l1, 1.0, 0.0) + jnp.where(lane == l2, 1.0, 0.0)
    ti = lax.broadcasted_iota(jnp.int32, (tm, tm), 0)
    si = lax.broadcasted_iota(jnp.int32, (tm, tm), 1)
    before = jnp.where(si < ti, 1.0, 0.0).astype(BF16)
    base = jnp.dot(before, oh.astype(BF16), preferred_element_type=F32) + carry_ref[0:1, :]
    rank1 = jnp.sum(jnp.where(lane == l1, base, 0.0), axis=-1, keepdims=True)
    rank2 = jnp.sum(jnp.where(lane == l2, base, 0.0), axis=-1, keepdims=True)
    carry_ref[...] = carry_ref[...] + jnp.sum(oh, axis=0, keepdims=True)
    cnt_ref[...] = carry_ref[...]

    meta = jnp.zeros((tm, ROUTER_LANES), F32)
    for ln, val in ((META_E, l1 - MOE_LANE0), (META_E + 1, l2 - MOE_LANE0), (META_RANK, rank1),
                    (META_RANK + 1, rank2), (META_GATE, gate1), (META_GATE + 1, gate2)):
        meta = jnp.where(lane == ln, val, meta)
    meta_ref[...] = meta


def _router_pallas(h, gain, w_rg, w_re, *, tm=512):
    n = h.shape[0]
    wr = jnp.concatenate([w_rg, w_re], axis=1)
    wr = jnp.pad(wr, ((0, 0), (0, ROUTER_LANES - wr.shape[1])))
    return pl.pallas_call(
        _router_kernel,
        name="moe_router",
        grid=(n // tm,),
        in_specs=[pl.BlockSpec((tm, D_MODEL), lambda i: (i, 0)),
                  pl.BlockSpec((1, D_MODEL), lambda i: (0, 0)),
                  pl.BlockSpec((D_MODEL, ROUTER_LANES), lambda i: (0, 0))],
        out_specs=[pl.BlockSpec((tm, HALF_D), lambda i: (i, 0)),
                   pl.BlockSpec((tm, ROUTER_LANES), lambda i: (i, 0)),
                   pl.BlockSpec((8, ROUTER_LANES), lambda i: (0, 0))],
        out_shape=[jax.ShapeDtypeStruct((n, HALF_D), jnp.uint32),
                   jax.ShapeDtypeStruct((n, ROUTER_LANES), F32),
                   jax.ShapeDtypeStruct((8, ROUTER_LANES), F32)],
        scratch_shapes=[pltpu.VMEM((8, ROUTER_LANES), F32)],
        compiler_params=pltpu.CompilerParams(
            dimension_semantics=("arbitrary",),
            vmem_limit_bytes=VMEM_LIMIT_BYTES),
    )(h, gain.reshape(1, D_MODEL).astype(F32), wr)


def _dispatch_kernel(dest_ref, hp_ref, xs_in_ref, xs_ref, sem):
    del xs_in_ref
    tc = hp_ref.shape[0]
    base = pl.program_id(0) * tc * MOE_TOPK

    def row_copy(r, k):
        return pltpu.make_async_copy(hp_ref.at[pl.ds(r, 1)],
                                     xs_ref.at[pl.ds(dest_ref[base + MOE_TOPK * r + k], 1)], sem)

    def start(r, c):
        for k in range(MOE_TOPK):
            row_copy(r, k).start()
        return c

    def wait(r, c):
        for k in range(MOE_TOPK):
            row_copy(r, k).wait()
        return c

    lax.fori_loop(0, tc, start, 0, unroll=ROW_DMA_UNROLL)
    lax.fori_loop(0, tc, wait, 0, unroll=ROW_DMA_UNROLL)


def _dispatch_pallas(hp, dest, xs0, *, tc=256):
    n = hp.shape[0]
    n_slots = xs0.shape[0]
    return pl.pallas_call(
        _dispatch_kernel,
        name="moe_dispatch",
        grid_spec=pltpu.PrefetchScalarGridSpec(
            num_scalar_prefetch=1,
            grid=(n // tc,),
            in_specs=[pl.BlockSpec((tc, HALF_D), lambda i, d: (i, 0)),
                      pl.BlockSpec(memory_space=pl.ANY)],
            out_specs=pl.BlockSpec(memory_space=pl.ANY),
            scratch_shapes=[pltpu.SemaphoreType.DMA(())]),
        out_shape=jax.ShapeDtypeStruct((n_slots, HALF_D), jnp.uint32),
        input_output_aliases={2: 0},
        compiler_params=pltpu.CompilerParams(
            dimension_semantics=("arbitrary",),
            vmem_limit_bytes=VMEM_LIMIT_BYTES),
    )(dest, hp, xs0)


MOE_WSLOTS = 3


def _expert_kernel(be_ref, first_ref, slot_ref, nxt1_ref, nxt2_ref, nu_ref, x_ref, w1_hbm, w3_hbm, w2_hbm,
                   y_ref, f1_ref, f3_ref, f2_ref, b1_ref, b3_ref, b2_ref, sem, *, layer):
    i = pl.program_id(0)

    def weight_copies(e, slot):
        return [pltpu.make_async_copy(src.at[layer, e], dst.at[slot], sem.at[slot, n])
                for n, (src, dst) in enumerate(((w1_hbm, f1_ref), (w3_hbm, f3_ref), (w2_hbm, f2_ref)))]

    def start_into(e, slot):
        @pl.when(e >= 0)
        def _():
            for c in weight_copies(e, slot):
                c.start()

    @pl.when(i < nu_ref[0])
    def _():
        slot = slot_ref[i]

        @pl.when(first_ref[i] == 1)
        def _():
            @pl.when(i == 0)
            def _():
                start_into(be_ref[i], slot)
                start_into(nxt1_ref[i], lax.rem(slot + 1, MOE_WSLOTS))

            for c in weight_copies(be_ref[i], slot):
                c.wait()
            start_into(nxt2_ref[i], lax.rem(slot + 2, MOE_WSLOTS))

            b1_ref[...] = f1_ref[slot].astype(BF16)
            b3_ref[...] = f3_ref[slot].astype(BF16)
            b2_ref[...] = f2_ref[slot].astype(BF16)

        x_lo, x_hi = (half.astype(BF16) for half in _unpack_halves(x_ref[...]))

        def up(w_ref):
            return (jnp.dot(x_lo, w_ref[:HALF_D, :], preferred_element_type=F32)
                    + jnp.dot(x_hi, w_ref[HALF_D:, :], preferred_element_type=F32))

        a = _silu(up(b1_ref)) * up(b3_ref)
        y_ref[...] = _pack_halves(jnp.dot(a.astype(BF16), b2_ref[...], preferred_element_type=F32))

    @pl.when(i >= nu_ref[0])
    def _():
        y_ref[...] = jnp.zeros_like(y_ref)


def _expert_pallas(xs, nblk, w1, w3, w2, layer):
    n_slots = xs.shape[0]
    nb = n_slots // MOE_TB
    bend = jnp.cumsum(nblk)
    blocks = jnp.arange(nb, dtype=jnp.int32)
    block_e = jnp.minimum(jnp.searchsorted(bend, blocks, side='right'), MOE_EXPERTS - 1).astype(jnp.int32)
    first = (blocks == (bend - nblk)[block_e]).astype(jnp.int32)
    nonempty = nblk > 0
    slot_e = (jnp.cumsum(nonempty) - 1) % MOE_WSLOTS
    ids = jnp.where(nonempty, jnp.arange(MOE_EXPERTS), MOE_EXPERTS)
    pad = jnp.full((2,), MOE_EXPERTS, ids.dtype)
    after = jnp.concatenate([lax.cummin(ids, reverse=True), pad])
    nxt1 = after[1:MOE_EXPERTS + 1]
    nxt2 = jnp.concatenate([after, pad[:1]])[jnp.minimum(nxt1, MOE_EXPERTS) + 1]
    as_id = lambda e: jnp.where(e < MOE_EXPERTS, e, -1).astype(jnp.int32)
    n_used = bend[-1:].astype(jnp.int32)
    blk = lambda i, *s: jnp.minimum(i, s[-1][0] - 1)
    hbm = pl.BlockSpec(memory_space=pl.ANY)
    return pl.pallas_call(
        functools.partial(_expert_kernel, layer=layer),
        name="moe_experts",
        grid_spec=pltpu.PrefetchScalarGridSpec(
            num_scalar_prefetch=6,
            grid=(nb,),
            in_specs=[pl.BlockSpec((MOE_TB, HALF_D), lambda i, *s: (blk(i, *s), 0)), hbm, hbm, hbm],
            out_specs=pl.BlockSpec((MOE_TB, HALF_D), lambda i, *s: (i, 0)),
            scratch_shapes=[pltpu.VMEM((MOE_WSLOTS, D_MODEL, MOE_FF), F32),
                            pltpu.VMEM((MOE_WSLOTS, D_MODEL, MOE_FF), F32),
                            pltpu.VMEM((MOE_WSLOTS, MOE_FF, D_MODEL), F32),
                            pltpu.VMEM((D_MODEL, MOE_FF), BF16), pltpu.VMEM((D_MODEL, MOE_FF), BF16),
                            pltpu.VMEM((MOE_FF, D_MODEL), BF16),
                            pltpu.SemaphoreType.DMA((MOE_WSLOTS, 3))]),
        out_shape=jax.ShapeDtypeStruct((n_slots, HALF_D), jnp.uint32),
        compiler_params=pltpu.CompilerParams(
            dimension_semantics=("arbitrary",),
            vmem_limit_bytes=VMEM_LIMIT_BYTES),
    )(block_e, first, slot_e[block_e].astype(jnp.int32), as_id(nxt1)[block_e], as_id(nxt2)[block_e], n_used,
      xs, w1, w3, w2)


def _combine_kernel(dest_ref, h_ref, meta_ref, g_ref, yb_ref, o_ref, buf_ref, sem, *, final_norm):
    tc = h_ref.shape[0]
    base = pl.program_id(0) * tc * MOE_TOPK

    def row_copy(r, k):
        return pltpu.make_async_copy(yb_ref.at[pl.ds(dest_ref[base + MOE_TOPK * r + k], 1)],
                                     buf_ref.at[k, pl.ds(r, 1)], sem)

    def start(r, c):
        for k in range(MOE_TOPK):
            row_copy(r, k).start()
        return c

    def wait(r, c):
        for k in range(MOE_TOPK):
            row_copy(r, k).wait()
        return c

    lax.fori_loop(0, tc, start, 0, unroll=ROW_DMA_UNROLL)
    lax.fori_loop(0, tc, wait, 0, unroll=ROW_DMA_UNROLL)
    meta = meta_ref[...]
    y_lo = jnp.zeros((tc, HALF_D), F32)
    y_hi = jnp.zeros((tc, HALF_D), F32)
    for k in range(MOE_TOPK):
        lo, hi = _unpack_halves(buf_ref[k])
        gate = meta[:, META_GATE + k:META_GATE + k + 1]
        y_lo = y_lo + lo * gate
        y_hi = y_hi + hi * gate
    out = h_ref[...] + jnp.concatenate([y_lo, y_hi], axis=1)
    if final_norm:
        out = out * lax.rsqrt(jnp.mean(out * out, axis=-1, keepdims=True) + EPS) * g_ref[...]
    o_ref[...] = out


def _combine_pallas(h, meta, dest, yb, final_gain, *, final_norm, tc=256):
    n = h.shape[0]
    return pl.pallas_call(
        functools.partial(_combine_kernel, final_norm=final_norm),
        name="moe_combine",
        grid_spec=pltpu.PrefetchScalarGridSpec(
            num_scalar_prefetch=1,
            grid=(n // tc,),
            in_specs=[pl.BlockSpec((tc, D_MODEL), lambda i, d: (i, 0)),
                      pl.BlockSpec((tc, ROUTER_LANES), lambda i, d: (i, 0)),
                      pl.BlockSpec((1, D_MODEL), lambda i, d: (0, 0)),
                      pl.BlockSpec(memory_space=pl.ANY)],
            out_specs=pl.BlockSpec((tc, D_MODEL), lambda i, d: (i, 0)),
            scratch_shapes=[pltpu.VMEM((MOE_TOPK, tc, HALF_D), jnp.uint32), pltpu.SemaphoreType.DMA(())]),
        out_shape=jax.ShapeDtypeStruct((n, D_MODEL), F32),
        compiler_params=pltpu.CompilerParams(
            dimension_semantics=("arbitrary",),
            vmem_limit_bytes=VMEM_LIMIT_BYTES),
    )(dest, h, meta, final_gain.reshape(1, D_MODEL).astype(F32), yb)


def _moe_slots(n):
    return ((n * MOE_TOPK) // MOE_TB + MOE_EXPERTS) * MOE_TB


def _moe_pallas(h, gain, w_rg, w_re, w1, w3, w2, layer, final_gain, xs_prev, *, final_norm):
    n = h.shape[0]
    hp, meta, cnt = _router_pallas(h, gain, w_rg, w_re)
    expert = meta[:, META_E:META_E + MOE_TOPK].astype(jnp.int32)
    rank = meta[:, META_RANK:META_RANK + MOE_TOPK].astype(jnp.int32)
    counts = cnt[0, MOE_LANE0:MOE_LANE0 + MOE_EXPERTS].astype(jnp.int32)
    nblk = (counts + MOE_TB - 1) // MOE_TB
    seg_start = (jnp.cumsum(nblk) - nblk) * MOE_TB
    hit = expert[..., None] == jnp.arange(MOE_EXPERTS, dtype=jnp.int32)
    dest = (jnp.sum(jnp.where(hit, seg_start, 0), axis=-1) + rank).reshape(n * MOE_TOPK)
    xs = _dispatch_pallas(hp, dest, xs_prev)
    yb = _expert_pallas(xs, nblk, w1, w3, w2, layer)
    return _combine_pallas(h, meta, dest, yb, final_gain, final_norm=final_norm), xs


N_GATES = 4 * ML_HEADS
IN_GROUPS_BEFORE_GATES = 4
IN_GROUPS = 13
LANE = 128


def _win_relayout_kernel(*refs):
    parts, main_ref = refs[:-1], refs[-1]
    n = pl.program_id(1)
    wide = jnp.concatenate([p[...] for p in parts], axis=1)

    @pl.when(n < IN_GROUPS_BEFORE_GATES)
    def _():
        main_ref[...] = wide[:, :GROUP_W].astype(main_ref.dtype)

    @pl.when(n >= IN_GROUPS_BEFORE_GATES)
    def _():
        main_ref[...] = wide[:, N_GATES:N_GATES + GROUP_W].astype(main_ref.dtype)


def _win_relayout(w_in):
    depth, k, _ = w_in.shape
    per = GROUP_W // LANE
    g0 = IN_GROUPS_BEFORE_GATES * GROUP_W
    gate = w_in[:, :, g0:g0 + N_GATES].reshape(depth, k, 4, ML_HEADS).transpose(0, 1, 3, 2)
    gate = jnp.pad(gate, ((0, 0), (0, 0), (0, 0), (0, LANE - 4))).reshape(depth, k, ML_HEADS * LANE)
    width = w_in.shape[2]
    w_b = jnp.concatenate([w_in.astype(BF16), jnp.zeros((depth, k, -width % LANE), BF16)], axis=2)
    part = lambda t: pl.BlockSpec((None, k, LANE), lambda l, n, t=t: (l, 0, per * n + t))
    main = pl.pallas_call(
        _win_relayout_kernel,
        name="w_in_relayout",
        grid=(depth, IN_GROUPS),
        in_specs=[part(t) for t in range(per + 1)],
        out_specs=pl.BlockSpec((None, k, GROUP_W), lambda l, n: (l, 0, n)),
        out_shape=jax.ShapeDtypeStruct((depth, k, IN_GROUPS * GROUP_W), BF16),
        compiler_params=pltpu.CompilerParams(
            dimension_semantics=("parallel", "parallel"),
            vmem_limit_bytes=VMEM_LIMIT_BYTES),
    )(*([w_b] * (per + 1)))
    return main, gate


COL_MLSTM, COL_HGRN, COL_FNET, COL_NA = 0, 4 * GROUP_W, 9 * GROUP_W, 10 * GROUP_W


def kernel(x, mem, norm_mix, norm_cross, norm_ffn, norm_final, norm_mem, w_in, mlstm_conv,
           mlstm_gate_bias, hgrn_lower_bound, na_rpb, group_gain, w_out, xa_wq, xa_wk, xa_wv, xa_wo,
           moe_router_group, moe_router_expert, moe_w1, moe_w3, moe_w2):
    b, s, d = x.shape
    mem_len = mem.shape[1]
    h = x.reshape(b * s, d)
    mem_f = mem.reshape(b * mem_len, d)
    lbs = jnp.cumsum(jax.nn.softmax(hgrn_lower_bound.astype(F32), axis=0), axis=0)
    lbs = lbs - lbs[0]
    fnet_tables = _fnet_tables(s)
    w_main, w_gate = _win_relayout(w_in)
    w_out_b, wq_b, wo_b = (w.astype(BF16) for w in (w_out, xa_wq, xa_wo))
    xs = jnp.zeros((_moe_slots(b * s), HALF_D), jnp.uint32)
    for l in range(DEPTH):
        proj, gates = _matmul(h, w_main, l, name="in_proj", gain=norm_mix[l], w_side=w_gate, out_dtype=BF16)
        y_ml = _mlstm_pallas(proj, gates, mlstm_conv[l], mlstm_gate_bias[l], batch=b, seq=s,
                             out_dtype=BF16)
        y_hg = _hgrn_pallas(proj, lbs[l], batch=b, seq=s, col0=COL_HGRN, out_dtype=BF16)
        y_fn = _fnet_pallas(proj, fnet_tables, batch=b, seq=s, col0=COL_FNET, out_dtype=BF16)
        y_na = _na_pallas(proj, na_rpb[l], batch=b, seq=s, col0=COL_NA, out_dtype=BF16)
        h = _matmul([y_ml, y_hg, y_fn, y_na], w_out_b, l, name="out_proj", gain=group_gain[l], residual=h)

        k = _matmul(mem_f, xa_wk, l, name="xa_k_proj", gain=norm_mem, out_dtype=BF16)
        v = _matmul(mem_f, xa_wv, l, name="xa_v_proj", gain=norm_mem, out_dtype=BF16)
        o = _xattn_pallas(h, norm_cross[l], wq_b, l, k, v, batch=b, seq=s, mem_len=mem_len)
        h = _matmul(o, wo_b, l, name="xa_o_proj", residual=h)

        h, xs = _moe_pallas(h, norm_ffn[l], moe_router_group[l], moe_router_expert[l],
                            moe_w1, moe_w3, moe_w2, l, norm_final, xs, final_norm=(l == DEPTH - 1))
    return h.reshape(b, s, d)
```

```python
import functools

import jax
import jax.numpy as jnp
import numpy as np
from jax import lax
from jax.experimental import pallas as pl
from jax.experimental.pallas import tpu as pltpu

D_MODEL = 2048
DEPTH = 4
GRID_W = 64
N_MIXERS = 4
GROUP_W = D_MODEL // N_MIXERS
ML_HEADS = 4
ML_DH = GROUP_W // ML_HEADS
HG_HEADS = 4
FN_GROUPS = 4
FN_CH = GROUP_W // FN_GROUPS
NA_HEADS = 8
NA_DH = GROUP_W // NA_HEADS
NA_KH = 8
NA_KW = 16
XA_HEADS = 4
XA_DH = D_MODEL // XA_HEADS
MOE_GROUPS = 4
MOE_PER_GROUP = 8
MOE_EXPERTS = MOE_GROUPS * MOE_PER_GROUP
MOE_TOPK = 2
EPS = 1e-6
F32 = jnp.float32
BF16 = jnp.bfloat16

VMEM_LIMIT_BYTES = 56 * 1024 * 1024


def _mm_kernel(*refs, n_x, norm, residual, side):
    refs = list(refs)
    x_refs = [refs.pop(0) for _ in range(n_x)]
    g_ref = refs.pop(0) if norm else None
    w_ref = refs.pop(0)
    ws_ref = refs.pop(0) if side else None
    r_ref = refs.pop(0) if residual else None
    o_ref = refs.pop(0)
    os_ref = refs.pop(0) if side else None
    xn_ref = refs.pop(0)

    @pl.when(pl.program_id(1) == 0)
    def _():
        col = 0
        for x_ref in x_refs:
            x = x_ref[...].astype(F32)
            kx = x.shape[1]
            if norm:
                ms = jnp.mean(x * x, axis=-1, keepdims=True)
                x = x * lax.rsqrt(ms + EPS) * g_ref[:, col:col + kx]
            xn_ref[:, col:col + kx] = x.astype(BF16)
            col += kx
        if side:
            os_ref[...] = jnp.dot(xn_ref[...], ws_ref[...].astype(BF16),
                                  preferred_element_type=F32).astype(os_ref.dtype)

    acc = jnp.dot(xn_ref[...], w_ref[...].astype(BF16), preferred_element_type=F32)
    if residual:
        acc = acc + r_ref[...]
    o_ref[...] = acc.astype(o_ref.dtype)


def _matmul(xs, w, layer, *, name, gain=None, residual=None, w_side=None, tm=1024, tn=512, out_dtype=F32):
    if not isinstance(xs, (list, tuple)):
        xs = [xs]
    m = xs[0].shape[0]
    k = sum(x.shape[1] for x in xs)
    n = w.shape[2]
    tm = min(tm, m)
    tn = min(tn, n)
    assert m % tm == 0 and n % tn == 0 and w.shape[1] == k, (m, n, k, tm, tn)
    norm = gain is not None
    has_res = residual is not None
    side = w_side is not None
    in_specs = [pl.BlockSpec((tm, x.shape[1]), lambda i, j: (i, 0)) for x in xs]
    args = list(xs)
    if norm:
        in_specs.append(pl.BlockSpec((1, k), lambda i, j: (0, 0)))
        args.append(gain.reshape(1, k).astype(F32))
    in_specs.append(pl.BlockSpec((None, k, tn), lambda i, j: (layer, 0, j)))
    args.append(w)
    if side:
        in_specs.append(pl.BlockSpec((None, k, w_side.shape[2]), lambda i, j: (layer, 0, 0)))
        args.append(w_side)
    if has_res:
        in_specs.append(pl.BlockSpec((tm, tn), lambda i, j: (i, j)))
        args.append(residual)
    out_specs = pl.BlockSpec((tm, tn), lambda i, j: (i, j))
    out_shape = jax.ShapeDtypeStruct((m, n), out_dtype)
    if side:
        out_specs = [out_specs, pl.BlockSpec((tm, w_side.shape[2]), lambda i, j: (i, 0))]
        out_shape = [out_shape, jax.ShapeDtypeStruct((m, w_side.shape[2]), F32)]
    return pl.pallas_call(
        functools.partial(_mm_kernel, n_x=len(xs), norm=norm, residual=has_res, side=side),
        name=name,
        grid=(m // tm, n // tn),
        in_specs=in_specs,
        out_specs=out_specs,
        out_shape=out_shape,
        scratch_shapes=[pltpu.VMEM((tm, k), BF16)],
        compiler_params=pltpu.CompilerParams(
            dimension_semantics=("parallel", "arbitrary"),
            vmem_limit_bytes=VMEM_LIMIT_BYTES),
    )(*args)


NA_QROWS = 4
NA_WROWS = 12


def _na_groups(rows):
    tables, plan = [], []
    for r0 in range(0, rows, NA_QROWS):
        band0 = lambda r: min(max(r - NA_KH // 2, 0), rows - NA_KH)
        kr0 = min(band0(r0), rows - NA_WROWS)
        assert band0(r0 + NA_QROWS - 1) + NA_KH <= kr0 + NA_WROWS
        dr = np.full((NA_QROWS, NA_WROWS), -1, np.int64)
        for i in range(NA_QROWS):
            for j in range(NA_WROWS):
                if 0 <= kr0 + j - band0(r0 + i) < NA_KH:
                    dr[i, j] = kr0 + j - (r0 + i) + NA_KH - 1
        key = dr.tobytes()
        if key not in [t.tobytes() for t in tables]:
            tables.append(dr)
        plan.append((r0, kr0, [t.tobytes() for t in tables].index(key)))
    return plan, np.stack(tables)


def _na_bias_tables(rpb, dr_tables):
    c = np.arange(GRID_W)
    dc = np.clip(c[None, :] - c[:, None] + NA_KW - 1, 0, 2 * NA_KW - 2)
    onehot = (dc[None] == np.arange(2 * NA_KW - 1)[:, None, None]).astype(np.float32)
    col_start = np.clip(c - NA_KW // 2, 0, GRID_W - NA_KW)
    col_ok = (c[None, :] >= col_start[:, None]) & (c[None, :] < col_start[:, None] + NA_KW)
    t = jnp.einsum('hrd,dqk->hrqk', rpb.astype(F32), onehot, precision=lax.Precision.HIGHEST)
    t = jnp.where(col_ok[None, None], t, -jnp.inf)
    t = jnp.concatenate([t, jnp.full_like(t[:, :1], -jnp.inf)], axis=1)
    idx = np.where(dr_tables < 0, t.shape[1] - 1, dr_tables)
    pairs = idx.reshape(idx.shape[0], idx.shape[1], -1, 2)
    uniq = sorted({(int(a), int(b)) for a, b in pairs.reshape(-1, 2)})
    pair_idx = np.array([[[uniq.index((int(a), int(b))) for a, b in row] for row in typ] for typ in pairs])
    table = jnp.stack([jnp.concatenate([t[:, a], t[:, b]], axis=-1) for a, b in uniq], axis=1)
    return table, pair_idx


def _na_kernel(q_ref, k_ref, v_ref, tab_ref, o_ref, bias_ref, *, plan, pair_idx):
    nq = NA_QROWS * GRID_W
    nk = NA_WROWS * GRID_W
    for hh in range(2):
        for typ in range(pair_idx.shape[0]):
            for i in range(pair_idx.shape[1]):
                for jj in range(pair_idx.shape[2]):
                    bias_ref[hh, typ, i * GRID_W:(i + 1) * GRID_W, jj * 2 * GRID_W:(jj + 1) * 2 * GRID_W] = (
                        tab_ref[hh, int(pair_idx[typ, i, jj])])
    for r0, kr0, typ in plan:
        q = q_ref[r0 * GRID_W:r0 * GRID_W + nq, :].astype(F32) * (NA_DH ** -0.5)
        kb = k_ref[kr0 * GRID_W:kr0 * GRID_W + nk, :]
        vb = v_ref[kr0 * GRID_W:kr0 * GRID_W + nk, :]
        outs = []
        for hh in range(2):
            sl = slice(hh * NA_DH, (hh + 1) * NA_DH)
            s = lax.dot_general(q[:, sl].astype(BF16), kb[:, sl].astype(BF16),
                                (((1,), (1,)), ((), ())), preferred_element_type=F32)
            s = s + bias_ref[hh, typ]
            m = jnp.max(s, axis=-1, keepdims=True)
            p = jnp.exp(s - m)
            l = jnp.sum(p, axis=-1, keepdims=True)
            o = jnp.dot(p.astype(BF16), vb[:, sl].astype(BF16), preferred_element_type=F32)
            outs.append(o / l)
        o_ref[r0 * GRID_W:r0 * GRID_W + nq, :] = jnp.concatenate(outs, axis=-1).astype(o_ref.dtype)


def _na_pallas(proj, rpb, *, batch, seq, col0, out_dtype=F32):
    rows = seq // GRID_W
    pair_w = 2 * NA_DH
    cb = col0 // pair_w
    gb = GROUP_W // pair_w
    plan, dr_tables = _na_groups(rows)
    table, pair_idx = _na_bias_tables(rpb, dr_tables)
    spec = lambda g: pl.BlockSpec((seq, pair_w), lambda b, p, g=g: (b, cb + g * gb + p))
    return pl.pallas_call(
        functools.partial(_na_kernel, plan=plan, pair_idx=pair_idx),
        name="na_attention",
        grid=(batch, NA_HEADS // 2),
        in_specs=[spec(0), spec(1), spec(2),
                  pl.BlockSpec((2,) + table.shape[1:], lambda b, p: (p, 0, 0, 0))],
        out_specs=pl.BlockSpec((seq, pair_w), lambda b, p: (b, p)),
        out_shape=jax.ShapeDtypeStruct((batch * seq, GROUP_W), out_dtype),
        scratch_shapes=[pltpu.VMEM((2, dr_tables.shape[0], NA_QROWS * GRID_W, NA_WROWS * GRID_W), F32)],
        compiler_params=pltpu.CompilerParams(
            dimension_semantics=("parallel", "parallel"),
            vmem_limit_bytes=VMEM_LIMIT_BYTES),
    )(proj, proj, proj, table)


ML_L = 128
ML_GATE_LANES = 128


def _log_sigmoid(x):
    return jnp.minimum(x, 0.0) - jnp.log(1.0 + jnp.exp(-jnp.abs(x)))


def _silu(x):
    return x * jax.nn.sigmoid(x)


_NT = (((1,), (1,)), ((), ()))
_TN = (((0,), (0,)), ((), ()))


def _split3(x):
    hi = x.astype(BF16)
    r1 = x - hi.astype(F32)
    mid = r1.astype(BF16)
    lo = (r1 - mid.astype(F32)).astype(BF16)
    return hi, mid, lo


def _dot_exact_lhs(a, x, dims=None):
    a = a.astype(BF16)
    if dims is None:
        return sum(jnp.dot(a, p, preferred_element_type=F32) for p in _split3(x))
    return sum(lax.dot_general(a, p, dims, preferred_element_type=F32) for p in _split3(x))


ML_GROUP = 4
ML_PAD = 16


def _mlstm_local(q, k, v, gcol, tri, mask_w, gi, end_row):
    L = ML_L
    cum_col = _dot_exact_lhs(tri, _log_sigmoid(gcol))
    wide = lambda col: jnp.broadcast_to(col, (L, L))
    b_cols, ig_cols, rows = [], [], []
    for g in range(len(q)):
        lanes = slice(g * ML_GATE_LANES, (g + 1) * ML_GATE_LANES)
        b_cols.append(cum_col[:, g * ML_GATE_LANES + gi + 1:g * ML_GATE_LANES + gi + 2])
        ig_cols.append(gcol[:, g * ML_GATE_LANES + gi:g * ML_GATE_LANES + gi + 1])
        rows.append(gcol[:, lanes].T[gi:gi + 1, :] - cum_col[:, lanes].T[gi + 1:gi + 2, :])
    b_w = jnp.concatenate([wide(b) for b in b_cols], axis=1)
    row_w = jnp.concatenate(rows, axis=1)
    dmat = jnp.where(mask_w != 0.0, b_w + row_w, -jnp.inf)
    m_in_w = jnp.concatenate([wide(jnp.max(dmat[:, g * L:(g + 1) * L], axis=-1, keepdims=True))
                              for g in range(len(q))], axis=1)
    w = jnp.exp(dmat - m_in_w)
    ones = jnp.ones((L, ML_DH), BF16)
    out = []
    for g in range(len(q)):
        lanes = slice(g * L, (g + 1) * L)
        kb = k[g].astype(BF16)
        v_aug = jnp.concatenate([v[g].astype(BF16), ones], axis=1)
        qk = lax.dot_general(q[g].astype(BF16), kb, _NT, preferred_element_type=F32) * w[:, lanes]
        res = jnp.dot(qk.astype(BF16), v_aug, preferred_element_type=F32)
        b_end = b_cols[g][end_row:end_row + 1, :]
        a_col = b_end - b_cols[g] + ig_cols[g]
        m_loc = jnp.max(a_col, axis=0, keepdims=True)
        kw = k[g] * jnp.exp(a_col - m_loc)
        st = lax.dot_general(v_aug, kw.astype(BF16), _TN, preferred_element_type=F32)
        out.append(dict(m_in=m_in_w[:, lanes], num=res[:, :ML_DH], den=res[:, ML_DH:], b=b_w[:, lanes],
                        ct=st[:ML_DH], n=st[ML_DH:ML_DH + 1], m_loc=m_loc, b_end=b_end))
    return out


def _mlstm_carry(q, m_in, num_in, den_in, b, ct, n, m_prev):
    inter = b + m_prev
    m_t = jnp.maximum(inter, m_in)
    s_in = jnp.exp(m_in - m_t)
    s_inter = jnp.exp(inter - m_t)
    rhs = jnp.concatenate([ct.astype(BF16), jnp.broadcast_to(n, (ML_PAD, ML_DH)).astype(BF16)], axis=0)
    both = lax.dot_general(q.astype(BF16), rhs, _NT, preferred_element_type=F32)
    num = s_in * num_in + s_inter * both[:, :ML_DH]
    den = s_in * den_in + s_inter * both[:, ML_DH:ML_DH + 1]
    return num / jnp.maximum(jnp.abs(den), jnp.exp(-m_t))


def _mlstm_kernel(q_ref, k_ref, v_ref, o_ref, cwq_ref, cwk_ref, gcol_ref, bcol_ref,
                  y_ref, qs_ref, ks_ref, num_ref, den_ref, min_ref, b_ref, ctl_ref, stat_ref, *, seq):
    L = ML_L
    nc = seq // L
    t_idx = lax.broadcasted_iota(jnp.int32, (seq, 1), 0)

    def conv_silu(x, w):
        prev = jnp.where(t_idx == 0, 0.0, pltpu.roll(x, 1, axis=0))
        nxt = jnp.where(t_idx == seq - 1, 0.0, pltpu.roll(x, seq - 1, axis=0))
        return _silu(prev * w[0:1, :] + x * w[1:2, :] + nxt * w[2:3, :])

    qs_ref[...] = conv_silu(q_ref[...].astype(F32), cwq_ref[...])
    ks_ref[...] = conv_silu(k_ref[...].astype(F32), cwk_ref[...]) * (ML_DH ** -0.5)

    ti = lax.broadcasted_iota(jnp.int32, (L, L), 0)
    si = lax.broadcasted_iota(jnp.int32, (L, L), 1)
    tris = (jnp.where(si <= ti, 1.0, 0.0), jnp.where(si >= ti, 1.0, 0.0))
    masks_w = [jnp.concatenate([t] * ML_GROUP, axis=1) for t in tris]

    for direction in (0, 1):
        for c0 in range(0, nc, ML_GROUP):
            rows = [pl.ds((c0 + g) * L, L) for g in range(ML_GROUP)]
            gcol = jnp.concatenate([gcol_ref[r, :] + bcol_ref[...] for r in rows], axis=1)
            local = _mlstm_local([qs_ref[r, :] for r in rows], [ks_ref[r, :] for r in rows],
                                 [v_ref[r, :].astype(F32) for r in rows], gcol,
                                 tris[direction], masks_w[direction], 2 * direction,
                                 L - 1 if direction == 0 else 0)
            for g, (r, loc) in enumerate(zip(rows, local)):
                idx = direction * nc + c0 + g
                num_ref[direction, r, :] = loc["num"]
                den_ref[direction, r, :] = loc["den"]
                min_ref[direction, r, :] = loc["m_in"]
                b_ref[direction, r, :] = loc["b"]
                ctl_ref[idx] = loc["ct"]
                stat_ref[idx, 0:1, :] = loc["n"]
                stat_ref[idx, 1:2, :] = jnp.broadcast_to(loc["m_loc"], (1, ML_DH))
                stat_ref[idx, 2:3, :] = jnp.broadcast_to(loc["b_end"], (1, ML_DH))

    state = [(jnp.zeros((ML_DH, ML_DH), F32), jnp.zeros((1, ML_DH), F32), jnp.zeros((1, ML_DH), F32))] * 2
    for i in range(nc):
        for direction in (0, 1):
            c = i if direction == 0 else nc - 1 - i
            r = pl.ds(c * L, L)
            idx = direction * nc + c
            ct, n, m = state[direction]
            num_ref[direction, r, :] = _mlstm_carry(qs_ref[r, :], min_ref[direction, r, :],
                                                    num_ref[direction, r, :], den_ref[direction, r, :],
                                                    b_ref[direction, r, :], ct, n, m)
            n_loc, m_loc, b_end = stat_ref[idx, 0:1, :], stat_ref[idx, 1:2, :], stat_ref[idx, 2:3, :]
            m_new = jnp.maximum(b_end + m, m_loc)
            s_old = jnp.exp(b_end + m - m_new)
            s_new = jnp.exp(m_loc - m_new)
            state[direction] = (s_old * ct + s_new * ctl_ref[idx], s_old * n + s_new * n_loc, m_new)
    y_ref[...] = (jax.nn.sigmoid(o_ref[...].astype(F32)) * (num_ref[0] + num_ref[1])).astype(y_ref.dtype)


def _mlstm_pallas(proj, gates, conv_w, gate_b, *, batch, seq, out_dtype=F32):
    d = ML_DH
    hb = GROUP_W // d
    gb4 = gate_b.astype(F32).reshape(4, ML_HEADS).T
    gb_col = jnp.pad(gb4, ((0, 0), (0, ML_GATE_LANES - 4))).reshape(ML_HEADS, 1, ML_GATE_LANES)
    spec = lambda grp: pl.BlockSpec((seq, d), lambda b, h, grp=grp: (b, grp * hb + h))
    f32 = lambda *shape: pltpu.VMEM(shape, F32)
    return pl.pallas_call(
        functools.partial(_mlstm_kernel, seq=seq),
        name="mlstm",
        grid=(batch, ML_HEADS),
        in_specs=[spec(0), spec(1), spec(2), spec(3),
                  pl.BlockSpec((3, d), lambda b, h: (0, h)),
                  pl.BlockSpec((3, d), lambda b, h: (0, hb + h)),
                  pl.BlockSpec((seq, ML_GATE_LANES), lambda b, h: (b, h)),
                  pl.BlockSpec((None, 1, ML_GATE_LANES), lambda b, h: (h, 0, 0))],
        out_specs=pl.BlockSpec((seq, d), lambda b, h: (b, h)),
        out_shape=jax.ShapeDtypeStruct((batch * seq, GROUP_W), out_dtype),
        scratch_shapes=[f32(seq, d), f32(seq, d), f32(2, seq, d), f32(2, seq, d), f32(2, seq, d),
                        f32(2, seq, d), f32(2 * (seq // ML_L), d, d), f32(2 * (seq // ML_L), 8, d)],
        compiler_params=pltpu.CompilerParams(
            dimension_semantics=("parallel", "parallel"),
            vmem_limit_bytes=VMEM_LIMIT_BYTES),
    )(proj, proj, proj, proj, conv_w.astype(F32), conv_w.astype(F32), gates, gb_col)


HG_L = 128
HG_LEAF = 8


HG_GROUP = 4
HG_NLEAF = HG_L // HG_LEAF
HG_LEVELS = 4


def _hgrn_ref_rows(p_ref, chunks, backward):
    def rows(first, count, stride):
        return jnp.concatenate([p_ref[pl.ds(c * HG_L + first, count, stride=stride), :] for c in chunks], axis=1)

    leaf = rows(HG_LEAF // 2 if backward else HG_LEAF // 2 - 1, HG_NLEAF, HG_LEAF)
    levels = []
    m = HG_L // 2
    for _ in range(HG_LEVELS):
        blocks = HG_L // (2 * m)
        r = rows(m if backward else m - 1, blocks, 2 * m) if blocks > 1 else rows(m if backward else m - 1, 1, 1)
        levels.append(jnp.repeat(r, HG_NLEAF // blocks, axis=0))
        m //= 2
    return leaf, levels


def _hgrn_chunks(q, k, v, p, ref_leaf, ref_levels, masks):
    L = HG_L
    dk = HG_L
    groups = q.shape[1] // dk
    spread = lambda e: jnp.broadcast_to(e[:, None, :], (HG_NLEAF, HG_LEAF, e.shape[1])).reshape(L, e.shape[1])
    d = p - spread(ref_leaf)
    q_leaf = q * jnp.exp(d)
    k_leaf = k * jnp.exp(-d)
    pairs = [(q_leaf.astype(BF16), k_leaf.astype(BF16), masks[HG_LEVELS])]
    for li in range(HG_LEVELS):
        gap = ref_leaf - ref_levels[li]
        qh = q_leaf * spread(jnp.exp(jnp.minimum(gap, 0.0)))
        kh = k_leaf * spread(jnp.exp(jnp.minimum(-gap, 0.0)))
        pairs.append((qh.astype(BF16), kh.astype(BF16), masks[li]))
    vb = v.astype(BF16)
    outs = []
    for g in range(groups):
        lanes = slice(g * dk, (g + 1) * dk)
        attn = jnp.zeros((L, L), F32)
        for qh, kh, mask in pairs:
            s = lax.dot_general(qh[:, lanes], kh[:, lanes], _NT, preferred_element_type=F32)
            attn = attn + jnp.where(mask != 0.0, s, 0.0)
        outs.append(jnp.dot(attn.astype(BF16), vb[:, lanes], preferred_element_type=F32))
    return jnp.concatenate(outs, axis=1)


def _hgrn_carry(q, k, v, p, st, backward):
    L = HG_L
    o = lax.dot_general((q * jnp.exp(p)).astype(BF16), st.astype(BF16), _NT, preferred_element_type=F32)
    p_end = p[0:1, :] if backward else p[L - 1:L, :]
    kd = k * jnp.exp(p_end - p)
    st = st * jnp.exp(p_end) + lax.dot_general(v.astype(BF16), kd.astype(BF16), _TN,
                                               preferred_element_type=F32)
    return o, st


def _hgrn_masks(backward):
    L = HG_L
    ti = lax.broadcasted_iota(jnp.int32, (L, L), 0)
    si = lax.broadcasted_iota(jnp.int32, (L, L), 1)
    if backward:
        ti, si = si, ti
    one = lambda cond: jnp.where(cond, 1.0, 0.0)
    masks = []
    m = L // 2
    while m >= HG_LEAF:
        same = one((ti // (2 * m)) == (si // (2 * m)))
        masks.append(same * one((ti % (2 * m)) >= m) * one((si % (2 * m)) < m))
        m //= 2
    masks.append(one((ti // HG_LEAF) == (si // HG_LEAF)) * one(si <= ti))
    return masks


def _hgrn_kernel(q_ref, ff_ref, fb_ref, i_ref, g_ref, lb_ref, y_ref,
                 qs_ref, lf_ref, kf_ref, lbk_ref, kb_ref, of_ref, ob_ref, *, seq):
    L = HG_L
    nc = seq // L
    lb = lb_ref[...]
    log_lb = jnp.log(lb)
    log1m_lb = jnp.log1p(-lb)

    def forget(fp):
        ls = _log_sigmoid(fp)
        a = log_lb
        c = log1m_lb + ls
        logf = jnp.maximum(a, c) + jnp.log(1.0 + jnp.exp(-jnp.abs(a - c)))
        return logf, (1.0 - lb) * jnp.exp(ls - fp)

    qs_ref[...] = _silu(q_ref[...].astype(F32))
    lf_ref[...], kf_ref[...] = forget(ff_ref[...].astype(F32))
    lbk_ref[...], kb_ref[...] = forget(fb_ref[...].astype(F32))

    ti = lax.broadcasted_iota(jnp.int32, (L, L), 0)
    si = lax.broadcasted_iota(jnp.int32, (L, L), 1)
    tril = (si <= ti).astype(F32)
    triu = (si >= ti).astype(F32)
    masks_f = _hgrn_masks(False)
    masks_b = _hgrn_masks(True)

    def side_by_side(ref, c0):
        return jnp.concatenate([ref[pl.ds((c0 + g) * L, L), :].astype(F32) for g in range(HG_GROUP)], axis=1)

    def scatter_back(ref, c0, wide):
        for g in range(HG_GROUP):
            ref[pl.ds((c0 + g) * L, L), :] = wide[:, g * L:(g + 1) * L]

    directions = ((kf_ref, lf_ref, of_ref, tril, masks_f, False), (kb_ref, lbk_ref, ob_ref, triu, masks_b, True))
    for c0 in range(0, nc, HG_GROUP):
        for _, f_ref, _, tri, _, _ in directions:
            scatter_back(f_ref, c0, _dot_exact_lhs(tri, side_by_side(f_ref, c0)))
    for c0 in range(0, nc, HG_GROUP):
        q = side_by_side(qs_ref, c0)
        v = side_by_side(i_ref, c0)
        for k_ref, f_ref, o_ref, _, masks, backward in directions:
            ref_leaf, ref_levels = _hgrn_ref_rows(f_ref, range(c0, c0 + HG_GROUP), backward)
            scatter_back(o_ref, c0, _hgrn_chunks(q, side_by_side(k_ref, c0), v, side_by_side(f_ref, c0),
                                                 ref_leaf, ref_levels, masks))
    st_f = jnp.zeros((L, L), F32)
    st_b = jnp.zeros((L, L), F32)
    for i in range(nc):
        sl = pl.ds(i * L, L)
        o, st_f = _hgrn_carry(qs_ref[sl, :], kf_ref[sl, :], i_ref[sl, :].astype(F32), lf_ref[sl, :], st_f, False)
        of_ref[sl, :] += o
        sl = pl.ds((nc - 1 - i) * L, L)
        o, st_b = _hgrn_carry(qs_ref[sl, :], kb_ref[sl, :], i_ref[sl, :].astype(F32), lbk_ref[sl, :], st_b, True)
        ob_ref[sl, :] += o
    o = of_ref[...] + ob_ref[...]
    o = o * lax.rsqrt(jnp.mean(o * o, axis=-1, keepdims=True) + EPS)
    y_ref[...] = (o * _silu(g_ref[...].astype(F32))).astype(y_ref.dtype)


def _hgrn_pallas(proj, lb, *, batch, seq, col0, out_dtype=F32):
    d = GROUP_W // HG_HEADS
    hb = GROUP_W // d
    cb = col0 // d
    spec = lambda grp: pl.BlockSpec((seq, d), lambda b, h, grp=grp: (b, cb + grp * hb + h))
    f32 = lambda *shape: pltpu.VMEM(shape, F32)
    return pl.pallas_call(
        functools.partial(_hgrn_kernel, seq=seq),
        name="hgrn2",
        grid=(batch, HG_HEADS),
        in_specs=[spec(0), spec(1), spec(2), spec(3), spec(4),
                  pl.BlockSpec((1, d), lambda b, h: (0, h))],
        out_specs=pl.BlockSpec((seq, d), lambda b, h: (b, h)),
        out_shape=jax.ShapeDtypeStruct((batch * seq, GROUP_W), out_dtype),
        scratch_shapes=[f32(seq, d)] * 7,
        compiler_params=pltpu.CompilerParams(
            dimension_semantics=("parallel", "parallel"),
            vmem_limit_bytes=VMEM_LIMIT_BYTES),
    )(proj, proj, proj, proj, proj, lb.astype(F32).reshape(1, GROUP_W))


DFT_SPLIT = 32


def _dft_cos_sin(n):
    def direct(rows, period):
        k = (rows[:, None] * jnp.arange(n, dtype=jnp.int32)[None, :]) % period
        ang = k.astype(F32) * (2.0 * np.pi / period)
        return jnp.cos(ang), jnp.sin(ang)

    if n % DFT_SPLIT or n // DFT_SPLIT < DFT_SPLIT:
        return direct(jnp.arange(n, dtype=jnp.int32), n)
    ca, sa = direct(jnp.arange(n // DFT_SPLIT, dtype=jnp.int32), n // DFT_SPLIT)
    cb, sb = direct(jnp.arange(DFT_SPLIT, dtype=jnp.int32), n)
    cos = ca[:, None, :] * cb[None, :, :] - sa[:, None, :] * sb[None, :, :]
    sin = sa[:, None, :] * cb[None, :, :] + ca[:, None, :] * sb[None, :, :]
    return cos.reshape(n, n), sin.reshape(n, n)


def _fnet_tables(seq):
    cc, sc = _dft_cos_sin(FN_CH)
    eye = jnp.eye(FN_GROUPS, dtype=F32)
    chan = jnp.concatenate([jnp.kron(eye, cc), jnp.kron(eye, sc)], axis=1)
    cs, ss = _dft_cos_sin(seq)
    pos = jnp.concatenate([cs, -ss], axis=1) * ((seq * FN_CH) ** -0.5)
    return chan.astype(BF16), pos.astype(BF16)


def _fnet_chan_kernel(u_ref, dft_ref, v_ref):
    r = jnp.dot(u_ref[...].astype(BF16), dft_ref[...], preferred_element_type=F32)
    v_ref[0] = r[:, :GROUP_W].astype(v_ref.dtype)
    v_ref[1] = r[:, GROUP_W:].astype(v_ref.dtype)


def _fnet_pos_kernel(w_ref, v_ref, o_ref):
    o_ref[...] = jnp.dot(w_ref[...], v_ref[...], preferred_element_type=F32).astype(o_ref.dtype)


def _fnet_pallas(proj, tables, *, batch, seq, col0, out_dtype=F32, ts=512):
    chan, pos = tables
    cb = col0 // GROUP_W
    nt = seq // ts
    v = pl.pallas_call(
        _fnet_chan_kernel,
        name="fnet_channel_dft",
        grid=(batch, nt),
        in_specs=[pl.BlockSpec((ts, GROUP_W), lambda b, i: (b * nt + i, cb)),
                  pl.BlockSpec((GROUP_W, 2 * GROUP_W), lambda b, i: (0, 0))],
        out_specs=pl.BlockSpec((2, ts, GROUP_W), lambda b, i: (0, i, b)),
        out_shape=jax.ShapeDtypeStruct((2, seq, batch * GROUP_W), BF16),
        compiler_params=pltpu.CompilerParams(
            dimension_semantics=("parallel", "parallel"),
            vmem_limit_bytes=VMEM_LIMIT_BYTES),
    )(proj, chan)
    v = v.reshape(2 * seq, batch * GROUP_W)
    return pl.pallas_call(
        _fnet_pos_kernel,
        name="fnet_position_dft",
        grid=(nt, batch),
        in_specs=[pl.BlockSpec((ts, 2 * seq), lambda i, b: (i, 0)),
                  pl.BlockSpec((2 * seq, GROUP_W), lambda i, b: (0, b))],
        out_specs=pl.BlockSpec((ts, GROUP_W), lambda i, b: (b * nt + i, 0)),
        out_shape=jax.ShapeDtypeStruct((batch * seq, GROUP_W), out_dtype),
        compiler_params=pltpu.CompilerParams(
            dimension_semantics=("parallel", "parallel"),
            vmem_limit_bytes=VMEM_LIMIT_BYTES),
    )(pos, v)


def _xattn_kernel(x_ref, g_ref, wq_ref, k_ref, v_ref, o_ref, xn_ref):
    @pl.when(pl.program_id(1) == 0)
    def _():
        x = x_ref[...].astype(F32)
        ms = jnp.mean(x * x, axis=-1, keepdims=True)
        xn_ref[...] = (x * lax.rsqrt(ms + EPS) * g_ref[...]).astype(BF16)

    q = jnp.dot(xn_ref[...], wq_ref[...].astype(BF16), preferred_element_type=F32)
    s = lax.dot_general(q.astype(BF16), k_ref[...].astype(BF16), _NT,
                        preferred_element_type=F32) * (XA_DH ** -0.5)
    p = jnp.exp(s - jnp.max(s, axis=-1, keepdims=True))
    l = jnp.sum(p, axis=-1, keepdims=True)
    o = jnp.dot(p.astype(BF16), v_ref[...].astype(BF16), preferred_element_type=F32)
    o_ref[...] = (o / l).astype(o_ref.dtype)


def _xattn_pallas(h, gain, wq, layer, k, v, *, batch, seq, mem_len, tm=1024, out_dtype=BF16):
    tiles = seq // tm
    return pl.pallas_call(
        _xattn_kernel,
        name="cross_attention",
        grid=(batch * tiles, XA_HEADS),
        in_specs=[pl.BlockSpec((tm, D_MODEL), lambda i, j: (i, 0)),
                  pl.BlockSpec((1, D_MODEL), lambda i, j: (0, 0)),
                  pl.BlockSpec((None, D_MODEL, XA_DH), lambda i, j: (layer, 0, j)),
                  pl.BlockSpec((mem_len, XA_DH), lambda i, j: (i // tiles, j)),
                  pl.BlockSpec((mem_len, XA_DH), lambda i, j: (i // tiles, j))],
        out_specs=pl.BlockSpec((tm, XA_DH), lambda i, j: (i, j)),
        out_shape=jax.ShapeDtypeStruct((batch * seq, D_MODEL), out_dtype),
        scratch_shapes=[pltpu.VMEM((tm, D_MODEL), BF16)],
        compiler_params=pltpu.CompilerParams(
            dimension_semantics=("parallel", "arbitrary"),
            vmem_limit_bytes=VMEM_LIMIT_BYTES),
    )(h, gain.reshape(1, D_MODEL).astype(F32), wq, k, v)


MOE_FF = D_MODEL // 4
MOE_TB = 256
MOE_LANE0 = MOE_GROUPS
ROUTER_LANES = 128
META_E, META_RANK, META_GATE = 0, 2, 4
HALF_D = D_MODEL // 2
ROW_DMA_UNROLL = 8


def _pack_halves(x):
    bits = lax.bitcast_convert_type(x.astype(BF16).astype(F32), jnp.uint32)
    h = x.shape[1] // 2
    return (bits[:, :h] >> 16) | (bits[:, h:] & jnp.uint32(0xFFFF0000))


def _unpack_halves(p):
    return (lax.bitcast_convert_type(p << 16, F32),
            lax.bitcast_convert_type(p & jnp.uint32(0xFFFF0000), F32))


def _router_kernel(x_ref, g_ref, wr_ref, hp_ref, meta_ref, cnt_ref, carry_ref):
    tm = x_ref.shape[0]

    @pl.when(pl.program_id(0) == 0)
    def _():
        carry_ref[...] = jnp.zeros_like(carry_ref)

    x = x_ref[...].astype(F32)
    hn = x * lax.rsqrt(jnp.mean(x * x, axis=-1, keepdims=True) + EPS) * g_ref[...]
    hb = hn.astype(BF16)
    logits = jnp.dot(hb, wr_ref[...].astype(BF16), preferred_element_type=F32)
    hp_ref[...] = _pack_halves(hn)

    lane = lax.broadcasted_iota(jnp.int32, (tm, ROUTER_LANES), 1).astype(F32)
    ninf = -jnp.inf
    first = lambda hit: jnp.min(jnp.where(hit, lane, float(ROUTER_LANES)), axis=-1, keepdims=True)
    gl = jnp.where(lane < MOE_GROUPS, logits, ninf)
    gmax = jnp.max(gl, axis=-1, keepdims=True)
    gidx = first(gl == gmax)
    g_gate = 1.0 / jnp.sum(jnp.exp(gl - gmax), axis=-1, keepdims=True)
    off = lane - (MOE_LANE0 + MOE_PER_GROUP * gidx)
    el = jnp.where(jnp.abs(2.0 * off - (MOE_PER_GROUP - 1)) < MOE_PER_GROUP, logits, ninf)
    v1 = jnp.max(el, axis=-1, keepdims=True)
    l1 = first(el == v1)
    el2 = jnp.where(lane == l1, ninf, el)
    v2 = jnp.max(el2, axis=-1, keepdims=True)
    l2 = first(el2 == v2)
    t = jnp.exp(v2 - v1)
    gate1 = g_gate / (1.0 + t)
    gate2 = g_gate * t / (1.0 + t)

    oh = jnp.where(lane == l1, 1.0, 0.0) + jnp.where(lane == l2, 1.0, 0.0)
    ti = lax.broadcasted_iota(jnp.int32, (tm, tm), 0)
    si = lax.broadcasted_iota(jnp.int32, (tm, tm), 1)
    before = jnp.where(si < ti, 1.0, 0.0).astype(BF16)
    base = jnp.dot(before, oh.astype(BF16), preferred_element_type=F32) + carry_ref[0:1, :]
    rank1 = jnp.sum(jnp.where(lane == l1, base, 0.0), axis=-1, keepdims=True)
    rank2 = jnp.sum(jnp.where(lane == l2, base, 0.0), axis=-1, keepdims=True)
    carry_ref[...] = carry_ref[...] + jnp.sum(oh, axis=0, keepdims=True)
    cnt_ref[...] = carry_ref[...]

    meta = jnp.zeros((tm, ROUTER_LANES), F32)
    for ln, val in ((META_E, l1 - MOE_LANE0), (META_E + 1, l2 - MOE_LANE0), (META_RANK, rank1),
                    (META_RANK + 1, rank2), (META_GATE, gate1), (META_GATE + 1, gate2)):
        meta = jnp.where(lane == ln, val, meta)
    meta_ref[...] = meta


def _router_pallas(h, gain, w_rg, w_re, *, tm=512):
    n = h.shape[0]
    wr = jnp.concatenate([w_rg, w_re], axis=1)
    wr = jnp.pad(wr, ((0, 0), (0, ROUTER_LANES - wr.shape[1])))
    return pl.pallas_call(
        _router_kernel,
        name="moe_router",
        grid=(n // tm,),
        in_specs=[pl.BlockSpec((tm, D_MODEL), lambda i: (i, 0)),
                  pl.BlockSpec((1, D_MODEL), lambda i: (0, 0)),
                  pl.BlockSpec((D_MODEL, ROUTER_LANES), lambda i: (0, 0))],
        out_specs=[pl.BlockSpec((tm, HALF_D), lambda i: (i, 0)),
                   pl.BlockSpec((tm, ROUTER_LANES), lambda i: (i, 0)),
                   pl.BlockSpec((8, ROUTER_LANES), lambda i: (0, 0))],
        out_shape=[jax.ShapeDtypeStruct((n, HALF_D), jnp.uint32),
                   jax.ShapeDtypeStruct((n, ROUTER_LANES), F32),
                   jax.ShapeDtypeStruct((8, ROUTER_LANES), F32)],
        scratch_shapes=[pltpu.VMEM((8, ROUTER_LANES), F32)],
        compiler_params=pltpu.CompilerParams(
            dimension_semantics=("arbitrary",),
            vmem_limit_bytes=VMEM_LIMIT_BYTES),
    )(h, gain.reshape(1, D_MODEL).astype(F32), wr)


def _dispatch_kernel(dest_ref, hp_ref, xs_in_ref, xs_ref, sem):
    del xs_in_ref
    tc = hp_ref.shape[0]
    base = pl.program_id(0) * tc * MOE_TOPK

    def row_copy(r, k):
        return pltpu.make_async_copy(hp_ref.at[pl.ds(r, 1)],
                                     xs_ref.at[pl.ds(dest_ref[base + MOE_TOPK * r + k], 1)], sem)

    def start(r, c):
        for k in range(MOE_TOPK):
            row_copy(r, k).start(priority=k % 2)
        return c

    def wait(r, c):
        for k in range(MOE_TOPK):
            row_copy(r, k).wait()
        return c

    lax.fori_loop(0, tc, start, 0, unroll=ROW_DMA_UNROLL)
    lax.fori_loop(0, tc, wait, 0, unroll=ROW_DMA_UNROLL)


def _dispatch_pallas(hp, dest, xs0, *, tc=256):
    n = hp.shape[0]
    n_slots = xs0.shape[0]
    return pl.pallas_call(
        _dispatch_kernel,
        name="moe_dispatch",
        grid_spec=pltpu.PrefetchScalarGridSpec(
            num_scalar_prefetch=1,
            grid=(n // tc,),
            in_specs=[pl.BlockSpec((tc, HALF_D), lambda i, d: (i, 0)),
                      pl.BlockSpec(memory_space=pl.ANY)],
            out_specs=pl.BlockSpec(memory_space=pl.ANY),
            scratch_shapes=[pltpu.SemaphoreType.DMA(())]),
        out_shape=jax.ShapeDtypeStruct((n_slots, HALF_D), jnp.uint32),
        input_output_aliases={2: 0},
        compiler_params=pltpu.CompilerParams(
            dimension_semantics=("arbitrary",),
            vmem_limit_bytes=VMEM_LIMIT_BYTES),
    )(dest, hp, xs0)


MOE_WSLOTS = 3


def _expert_kernel(be_ref, first_ref, slot_ref, nxt1_ref, nxt2_ref, nu_ref, x_ref, w1_hbm, w3_hbm, w2_hbm,
                   y_ref, f1_ref, f3_ref, f2_ref, b1_ref, b3_ref, b2_ref, sem, *, layer):
    i = pl.program_id(0)

    def weight_copies(e, slot):
        return [pltpu.make_async_copy(src.at[layer, e], dst.at[slot], sem.at[slot, n])
                for n, (src, dst) in enumerate(((w1_hbm, f1_ref), (w3_hbm, f3_ref), (w2_hbm, f2_ref)))]

    def start_into(e, slot):
        @pl.when(e >= 0)
        def _():
            for c in weight_copies(e, slot):
                c.start()

    @pl.when(i < nu_ref[0])
    def _():
        slot = slot_ref[i]

        @pl.when(first_ref[i] == 1)
        def _():
            @pl.when(i == 0)
            def _():
                start_into(be_ref[i], slot)
                start_into(nxt1_ref[i], lax.rem(slot + 1, MOE_WSLOTS))

            for c in weight_copies(be_ref[i], slot):
                c.wait()
            start_into(nxt2_ref[i], lax.rem(slot + 2, MOE_WSLOTS))

            b1_ref[...] = f1_ref[slot].astype(BF16)
            b3_ref[...] = f3_ref[slot].astype(BF16)
            b2_ref[...] = f2_ref[slot].astype(BF16)

        x_lo, x_hi = (half.astype(BF16) for half in _unpack_halves(x_ref[...]))

        def up(w_ref):
            return (jnp.dot(x_lo, w_ref[:HALF_D, :], preferred_element_type=F32)
                    + jnp.dot(x_hi, w_ref[HALF_D:, :], preferred_element_type=F32))

        a = _silu(up(b1_ref)) * up(b3_ref)
        y_ref[...] = _pack_halves(jnp.dot(a.astype(BF16), b2_ref[...], preferred_element_type=F32))

    @pl.when(i >= nu_ref[0])
    def _():
        y_ref[...] = jnp.zeros_like(y_ref)


def _expert_pallas(xs, nblk, w1, w3, w2, layer):
    n_slots = xs.shape[0]
    nb = n_slots // MOE_TB
    bend = jnp.cumsum(nblk)
    blocks = jnp.arange(nb, dtype=jnp.int32)
    block_e = jnp.minimum(jnp.searchsorted(bend, blocks, side='right'), MOE_EXPERTS - 1).astype(jnp.int32)
    first = (blocks == (bend - nblk)[block_e]).astype(jnp.int32)
    nonempty = nblk > 0
    slot_e = (jnp.cumsum(nonempty) - 1) % MOE_WSLOTS
    ids = jnp.where(nonempty, jnp.arange(MOE_EXPERTS), MOE_EXPERTS)
    pad = jnp.full((2,), MOE_EXPERTS, ids.dtype)
    after = jnp.concatenate([lax.cummin(ids, reverse=True), pad])
    nxt1 = after[1:MOE_EXPERTS + 1]
    nxt2 = jnp.concatenate([after, pad[:1]])[jnp.minimum(nxt1, MOE_EXPERTS) + 1]
    as_id = lambda e: jnp.where(e < MOE_EXPERTS, e, -1).astype(jnp.int32)
    n_used = bend[-1:].astype(jnp.int32)
    blk = lambda i, *s: jnp.minimum(i, s[-1][0] - 1)
    hbm = pl.BlockSpec(memory_space=pl.ANY)
    return pl.pallas_call(
        functools.partial(_expert_kernel, layer=layer),
        name="moe_experts",
        grid_spec=pltpu.PrefetchScalarGridSpec(
            num_scalar_prefetch=6,
            grid=(nb,),
            in_specs=[pl.BlockSpec((MOE_TB, HALF_D), lambda i, *s: (blk(i, *s), 0)), hbm, hbm, hbm],
            out_specs=pl.BlockSpec((MOE_TB, HALF_D), lambda i, *s: (i, 0)),
            scratch_shapes=[pltpu.VMEM((MOE_WSLOTS, D_MODEL, MOE_FF), F32),
                            pltpu.VMEM((MOE_WSLOTS, D_MODEL, MOE_FF), F32),
                            pltpu.VMEM((MOE_WSLOTS, MOE_FF, D_MODEL), F32),
                            pltpu.VMEM((D_MODEL, MOE_FF), BF16), pltpu.VMEM((D_MODEL, MOE_FF), BF16),
                            pltpu.VMEM((MOE_FF, D_MODEL), BF16),
                            pltpu.SemaphoreType.DMA((MOE_WSLOTS, 3))]),
        out_shape=jax.ShapeDtypeStruct((n_slots, HALF_D), jnp.uint32),
        compiler_params=pltpu.CompilerParams(
            dimension_semantics=("arbitrary",),
            vmem_limit_bytes=VMEM_LIMIT_BYTES),
    )(block_e, first, slot_e[block_e].astype(jnp.int32), as_id(nxt1)[block_e], as_id(nxt2)[block_e], n_used,
      xs, w1, w3, w2)


def _combine_kernel(dest_ref, h_ref, meta_ref, g_ref, yb_ref, o_ref, buf_ref, sem, *, final_norm):
    tc = h_ref.shape[0]
    base = pl.program_id(0) * tc * MOE_TOPK

    def row_copy(r, k):
        return pltpu.make_async_copy(yb_ref.at[pl.ds(dest_ref[base + MOE_TOPK * r + k], 1)],
                                     buf_ref.at[k, pl.ds(r, 1)], sem)

    def start(r, c):
        for k in range(MOE_TOPK):
            row_copy(r, k).start(priority=k % 2)
        return c

    def wait(r, c):
        for k in range(MOE_TOPK):
            row_copy(r, k).wait()
        return c

    lax.fori_loop(0, tc, start, 0, unroll=ROW_DMA_UNROLL)
    lax.fori_loop(0, tc, wait, 0, unroll=ROW_DMA_UNROLL)
    meta = meta_ref[...]
    y_lo = jnp.zeros((tc, HALF_D), F32)
    y_hi = jnp.zeros((tc, HALF_D), F32)
    for k in range(MOE_TOPK):
        lo, hi = _unpack_halves(buf_ref[k])
        gate = meta[:, META_GATE + k:META_GATE + k + 1]
        y_lo = y_lo + lo * gate
        y_hi = y_hi + hi * gate
    out = h_ref[...] + jnp.concatenate([y_lo, y_hi], axis=1)
    if final_norm:
        out = out * lax.rsqrt(jnp.mean(out * out, axis=-1, keepdims=True) + EPS) * g_ref[...]
    o_ref[...] = out


def _combine_pallas(h, meta, dest, yb, final_gain, *, final_norm, tc=256):
    n = h.shape[0]
    return pl.pallas_call(
        functools.partial(_combine_kernel, final_norm=final_norm),
        name="moe_combine",
        grid_spec=pltpu.PrefetchScalarGridSpec(
            num_scalar_prefetch=1,
            grid=(n // tc,),
            in_specs=[pl.BlockSpec((tc, D_MODEL), lambda i, d: (i, 0)),
                      pl.BlockSpec((tc, ROUTER_LANES), lambda i, d: (i, 0)),
                      pl.BlockSpec((1, D_MODEL), lambda i, d: (0, 0)),
                      pl.BlockSpec(memory_space=pl.ANY)],
            out_specs=pl.BlockSpec((tc, D_MODEL), lambda i, d: (i, 0)),
            scratch_shapes=[pltpu.VMEM((MOE_TOPK, tc, HALF_D), jnp.uint32), pltpu.SemaphoreType.DMA(())]),
        out_shape=jax.ShapeDtypeStruct((n, D_MODEL), F32),
        compiler_params=pltpu.CompilerParams(
            dimension_semantics=("arbitrary",),
            vmem_limit_bytes=VMEM_LIMIT_BYTES),
    )(dest, h, meta, final_gain.reshape(1, D_MODEL).astype(F32), yb)


def _moe_slots(n):
    return ((n * MOE_TOPK) // MOE_TB + MOE_EXPERTS) * MOE_TB


def _moe_pallas(h, gain, w_rg, w_re, w1, w3, w2, layer, final_gain, xs_prev, *, final_norm):
    n = h.shape[0]
    hp, meta, cnt = _router_pallas(h, gain, w_rg, w_re)
    expert = meta[:, META_E:META_E + MOE_TOPK].astype(jnp.int32)
    rank = meta[:, META_RANK:META_RANK + MOE_TOPK].astype(jnp.int32)
    counts = cnt[0, MOE_LANE0:MOE_LANE0 + MOE_EXPERTS].astype(jnp.int32)
    nblk = (counts + MOE_TB - 1) // MOE_TB
    seg_start = (jnp.cumsum(nblk) - nblk) * MOE_TB
    hit = expert[..., None] == jnp.arange(MOE_EXPERTS, dtype=jnp.int32)
    dest = (jnp.sum(jnp.where(hit, seg_start, 0), axis=-1) + rank).reshape(n * MOE_TOPK)
    xs = _dispatch_pallas(hp, dest, xs_prev)
    yb = _expert_pallas(xs, nblk, w1, w3, w2, layer)
    return _combine_pallas(h, meta, dest, yb, final_gain, final_norm=final_norm), xs


N_GATES = 4 * ML_HEADS
IN_GROUPS_BEFORE_GATES = 4
IN_GROUPS = 13
LANE = 128


def _win_relayout_kernel(*refs):
    parts, main_ref = refs[:-1], refs[-1]
    n = pl.program_id(1)
    wide = jnp.concatenate([p[...] for p in parts], axis=1)

    @pl.when(n < IN_GROUPS_BEFORE_GATES)
    def _():
        main_ref[...] = wide[:, :GROUP_W].astype(main_ref.dtype)

    @pl.when(n >= IN_GROUPS_BEFORE_GATES)
    def _():
        main_ref[...] = wide[:, N_GATES:N_GATES + GROUP_W].astype(main_ref.dtype)


def _win_relayout(w_in):
    depth, k, _ = w_in.shape
    per = GROUP_W // LANE
    g0 = IN_GROUPS_BEFORE_GATES * GROUP_W
    gate = w_in[:, :, g0:g0 + N_GATES].reshape(depth, k, 4, ML_HEADS).transpose(0, 1, 3, 2)
    gate = jnp.pad(gate, ((0, 0), (0, 0), (0, 0), (0, LANE - 4))).reshape(depth, k, ML_HEADS * LANE)
    width = w_in.shape[2]
    w_b = jnp.concatenate([w_in.astype(BF16), jnp.zeros((depth, k, -width % LANE), BF16)], axis=2)
    part = lambda t: pl.BlockSpec((None, k, LANE), lambda l, n, t=t: (l, 0, per * n + t))
    main = pl.pallas_call(
        _win_relayout_kernel,
        name="w_in_relayout",
        grid=(depth, IN_GROUPS),
        in_specs=[part(t) for t in range(per + 1)],
        out_specs=pl.BlockSpec((None, k, GROUP_W), lambda l, n: (l, 0, n)),
        out_shape=jax.ShapeDtypeStruct((depth, k, IN_GROUPS * GROUP_W), BF16),
        compiler_params=pltpu.CompilerParams(
            dimension_semantics=("parallel", "parallel"),
            vmem_limit_bytes=VMEM_LIMIT_BYTES),
    )(*([w_b] * (per + 1)))
    return main, gate


COL_MLSTM, COL_HGRN, COL_FNET, COL_NA = 0, 4 * GROUP_W, 9 * GROUP_W, 10 * GROUP_W


def kernel(x, mem, norm_mix, norm_cross, norm_ffn, norm_final, norm_mem, w_in, mlstm_conv,
           mlstm_gate_bias, hgrn_lower_bound, na_rpb, group_gain, w_out, xa_wq, xa_wk, xa_wv, xa_wo,
           moe_router_group, moe_router_expert, moe_w1, moe_w3, moe_w2):
    b, s, d = x.shape
    mem_len = mem.shape[1]
    h = x.reshape(b * s, d)
    mem_f = mem.reshape(b * mem_len, d)
    lbs = jnp.cumsum(jax.nn.softmax(hgrn_lower_bound.astype(F32), axis=0), axis=0)
    lbs = lbs - lbs[0]
    fnet_tables = _fnet_tables(s)
    w_main, w_gate = _win_relayout(w_in)
    w_out_b, wq_b, wo_b = (w.astype(BF16) for w in (w_out, xa_wq, xa_wo))
    xs = jnp.zeros((_moe_slots(b * s), HALF_D), jnp.uint32)
    for l in range(DEPTH):
        proj, gates = _matmul(h, w_main, l, name="in_proj", gain=norm_mix[l], w_side=w_gate, out_dtype=BF16)
        y_ml = _mlstm_pallas(proj, gates, mlstm_conv[l], mlstm_gate_bias[l], batch=b, seq=s,
                             out_dtype=BF16)
        y_hg = _hgrn_pallas(proj, lbs[l], batch=b, seq=s, col0=COL_HGRN, out_dtype=BF16)
        y_fn = _fnet_pallas(proj, fnet_tables, batch=b, seq=s, col0=COL_FNET, out_dtype=BF16)
        y_na = _na_pallas(proj, na_rpb[l], batch=b, seq=s, col0=COL_NA, out_dtype=BF16)
        h = _matmul([y_ml, y_hg, y_fn, y_na], w_out_b, l, name="out_proj", gain=group_gain[l], residual=h)

        k = _matmul(mem_f, xa_wk, l, name="xa_k_proj", gain=norm_mem, out_dtype=BF16)
        v = _matmul(mem_f, xa_wv, l, name="xa_v_proj", gain=norm_mem, out_dtype=BF16)
        o = _xattn_pallas(h, norm_cross[l], wq_b, l, k, v, batch=b, seq=s, mem_len=mem_len)
        h = _matmul(o, wo_b, l, name="xa_o_proj", residual=h)

        h, xs = _moe_pallas(h, norm_ffn[l], moe_router_group[l], moe_router_expert[l],
                            moe_w1, moe_w3, moe_w2, l, norm_final, xs, final_norm=(l == DEPTH - 1))
    return h.reshape(b, s, d)
```
